```python
import jax, jax.numpy as jnp
from jax import lax
import numpy as np

D_MODEL = 1024
BATCH = 4
SEQ = 8192
DEPTH = 2
DEC_BATCH = 32
DEC_SEQ = 16
PAST_LEN = 1024

CHUNK = 64
N_MIXERS = 2
N_RET_LAYERS = (DEPTH + 1) // 2
N_ATT_LAYERS = DEPTH // 2
D_HEAD = 128
H_RET = 8
DK_RET = 128
DV_RET = 128
RET_W = H_RET * DK_RET
RET_V = H_RET * DV_RET
H_ATT = 8
ATT_W = H_ATT * D_HEAD
N_PREV_CHUNKS = 8
BAND_PAST = N_PREV_CHUNKS * CHUNK
BAND = BAND_PAST + CHUNK
REL_CLIP = 128
H_MEM = 4
N_MEM = 256
MEM_W = H_MEM * D_HEAD
RET_SPLITS = [RET_W, 2 * RET_W, 2 * RET_W + RET_V, 2 * RET_W + 2 * RET_V]
RET_IN = 2 * RET_W + 2 * RET_V + MEM_W
ATT_SPLITS = [ATT_W, 2 * ATT_W, 3 * ATT_W]
ATT_IN = 3 * ATT_W + MEM_W
MIX_OUT_RET = RET_V + MEM_W
MIX_OUT_ATT = ATT_W + MEM_W
N_GROUPS = 4
EXPERTS_PER_GROUP = 8
N_EXPERTS = N_GROUPS * EXPERTS_PER_GROUP
TOP_K_IN_GROUP = 2
D_FF_EXPERT = 512
MOE_BLOCK = 128
ROPE_BASE = 10000.0
EPS = 1e-6
NEG_INF = -1e30

kernel_name = "hybrid_retention_chunkband_hmoe_step"


def rms_norm(x, g):
    xf = x.astype(jnp.float32)
    y = xf * lax.rsqrt(jnp.mean(xf * xf, axis=-1, keepdims=True) + EPS)
    return (y * g.astype(jnp.float32)).astype(x.dtype)


def group_norm_heads(o, g):
    mu = jnp.mean(o, axis=-1, keepdims=True)
    var = jnp.mean(jnp.square(o - mu), axis=-1, keepdims=True)
    y = (o - mu) * lax.rsqrt(var + EPS)
    B, L, H, DV = o.shape
    return y.reshape(B, L, H * DV) * g.astype(jnp.float32)


def rotary(x, pos):
    half = x.shape[-1] // 2
    inv = ROPE_BASE ** (-jnp.arange(half, dtype=jnp.float32) / half)
    ang = pos.astype(jnp.float32)[:, None] * inv[None, :]
    cos = jnp.cos(ang)[None, :, None, :]
    sin = jnp.sin(ang)[None, :, None, :]
    xf = x.astype(jnp.float32)
    x1, x2 = xf[..., :half], xf[..., half:]
    return jnp.concatenate([x1 * cos - x2 * sin, x1 * sin + x2 * cos], axis=-1).astype(x.dtype)


def retention_log_gamma():
    return jnp.log1p(-jnp.exp2(-5.0 - jnp.arange(H_RET, dtype=jnp.float32)))


def retention_chunk(q, k, v, S, log_gamma):
    L = q.shape[1]
    idx = jnp.arange(L, dtype=jnp.float32)
    diff = idx[:, None] - idx[None, :]
    decay = jnp.where(diff[None] >= 0,
                      jnp.exp(log_gamma[:, None, None] * jnp.maximum(diff, 0.0)[None]), 0.0)
    qf, kf, vf = q.astype(jnp.float32), k.astype(jnp.float32), v.astype(jnp.float32)
    scores = jnp.einsum("blhd,bmhd->bhlm", qf, kf) * decay[None]
    intra = jnp.einsum("bhlm,bmhe->blhe", scores, vf)
    q_dec = jnp.exp(log_gamma[None, :] * (idx + 1.0)[:, None])
    cross = jnp.einsum("blhd,bhde->blhe", qf, S) * q_dec[None, :, :, None]
    k_dec = jnp.exp(log_gamma[None, :] * (L - 1.0 - idx)[:, None])
    S_new = (jnp.exp(log_gamma * L)[None, :, None, None] * S
             + jnp.einsum("blhd,blhe->bhde", kf * k_dec[None, :, :, None], vf))
    return intra + cross, S_new


def retention_scan(q, k, v, S0, log_gamma):
    B, L, H, _ = q.shape
    S0 = S0.astype(jnp.float32)
    if L <= CHUNK:
        return retention_chunk(q, k, v, S0, log_gamma)
    nc = L // CHUNK

    def to_blocks(t):
        return t.reshape(B, nc, CHUNK, H, t.shape[-1]).swapaxes(0, 1)

    def body(S, blk):
        qb, kb, vb = blk
        o, S = retention_chunk(qb, kb, vb, S, log_gamma)
        return S, o

    S, o = lax.scan(body, S0, (to_blocks(q), to_blocks(k), to_blocks(v)))
    return o.swapaxes(0, 1).reshape(B, L, H, DV_RET), S


def band_attend(q, k, v, qpos, kpos, bias_table):
    s = jnp.einsum("bqhd,bkhd->bhqk", q.astype(jnp.float32), k.astype(jnp.float32)) * (D_HEAD ** -0.5)
    rel = jnp.clip(qpos[:, None] - kpos[None, :], -REL_CLIP, REL_CLIP) + REL_CLIP
    s = s + bias_table.astype(jnp.float32)[:, rel][None]
    qc = jnp.floor_divide(qpos, CHUNK)[:, None]
    kc = jnp.floor_divide(kpos, CHUNK)[None, :]
    allowed = (kpos[None, :] >= 0) & (kc <= qc) & (kc >= qc - N_PREV_CHUNKS)
    s = jnp.where(allowed[None, None], s, NEG_INF)
    p = jax.nn.softmax(s, axis=-1)
    return jnp.einsum("bhqk,bkhd->bqhd", p, v.astype(jnp.float32)).astype(v.dtype)


def band_prompt(q, k, v, bias_table):
    B, S, H, D = q.shape
    nc = S // CHUNK
    pad = jnp.zeros((B, BAND_PAST, H, D), k.dtype)
    kp = jnp.concatenate([pad, k], axis=1)
    vp = jnp.concatenate([pad, v], axis=1)
    qb = q.reshape(B, nc, CHUNK, H, D).swapaxes(0, 1)

    def one_chunk(args):
        c, q_blk = args
        start = c * CHUNK
        k_blk = lax.dynamic_slice_in_dim(kp, start, BAND, axis=1)
        v_blk = lax.dynamic_slice_in_dim(vp, start, BAND, axis=1)
        qpos = start + jnp.arange(CHUNK, dtype=jnp.int32)
        kpos = start - BAND_PAST + jnp.arange(BAND, dtype=jnp.int32)
        return band_attend(q_blk, k_blk, v_blk, qpos, kpos, bias_table)

    out = lax.map(one_chunk, (jnp.arange(nc, dtype=jnp.int32), qb))
    return out.swapaxes(0, 1).reshape(B, S, H, D)


def memory_kv(mem, norm_g, w_kv, k_norm_g):
    B = mem.shape[0]
    kv = rms_norm(mem, norm_g) @ w_kv
    mk, mv = jnp.split(kv, [MEM_W], axis=-1)
    mk = rms_norm(mk.reshape(B, N_MEM, H_MEM, D_HEAD), k_norm_g)
    return mk, mv.reshape(B, N_MEM, H_MEM, D_HEAD)


def mem_attend(q, mk, mv):
    s = jnp.einsum("blhd,bmhd->bhlm", q.astype(jnp.float32), mk.astype(jnp.float32)) * (D_HEAD ** -0.5)
    p = jax.nn.softmax(s, axis=-1)
    return jnp.einsum("bhlm,bmhd->blhd", p, mv.astype(jnp.float32)).astype(q.dtype)


def hier_moe(h, w_group, b_group, w_router, b_router, w_gate, w_up, w_down):
    T, D = h.shape
    hf = h.astype(jnp.float32)
    g_logits = hf @ w_group.astype(jnp.float32) + b_group.astype(jnp.float32)
    g_prob = jax.nn.softmax(g_logits, axis=-1)
    g_sel = jnp.argmax(g_logits, axis=-1)
    g_gate = jnp.take_along_axis(g_prob, g_sel[:, None], axis=-1)
    e_logits = (hf @ w_router.astype(jnp.float32) + b_router.astype(jnp.float32)).reshape(
        T, N_GROUPS, EXPERTS_PER_GROUP)
    e_sel_logits = jnp.take_along_axis(e_logits, g_sel[:, None, None], axis=1)[:, 0]
    e_prob = jax.nn.softmax(e_sel_logits, axis=-1)
    top_p, top_i = lax.top_k(e_prob, TOP_K_IN_GROUP)
    top_p = top_p / jnp.sum(top_p, axis=-1, keepdims=True)
    gates = (g_gate * top_p).reshape(-1)
    flat_e = (g_sel[:, None] * EXPERTS_PER_GROUP + top_i).reshape(-1)
    n_assign = T * TOP_K_IN_GROUP
    flat_tok = jnp.arange(n_assign, dtype=jnp.int32) // TOP_K_IN_GROUP
    order = jnp.argsort(flat_e)
    sorted_e = flat_e[order]
    sorted_tok = flat_tok[order]
    counts = jnp.bincount(flat_e, length=N_EXPERTS)
    starts = jnp.cumsum(counts) - counts
    padded = ((counts + MOE_BLOCK - 1) // MOE_BLOCK) * MOE_BLOCK
    pad_ends = jnp.cumsum(padded)
    pad_starts = pad_ends - padded
    dest = pad_starts[sorted_e] + (jnp.arange(n_assign) - starts[sorted_e])
    n_blocks = (n_assign + N_EXPERTS * (MOE_BLOCK - 1) + MOE_BLOCK - 1) // MOE_BLOCK
    n_rows = n_blocks * MOE_BLOCK
    x_pad = jnp.zeros((n_rows, D), h.dtype).at[dest].set(h[sorted_tok])
    block_e = jnp.minimum(jnp.searchsorted(pad_ends, jnp.arange(n_blocks) * MOE_BLOCK, side="right"),
                          N_EXPERTS - 1)

    def run_block(args):
        xb, e = args
        return (jax.nn.silu(xb @ w_gate[e]) * (xb @ w_up[e])) @ w_down[e]

    y_pad = lax.map(run_block, (x_pad.reshape(n_blocks, MOE_BLOCK, D), block_e)).reshape(n_rows, D)
    contrib = y_pad[dest].astype(jnp.float32) * gates[order][:, None]
    out = jax.ops.segment_sum(contrib, sorted_tok, num_segments=T)
    return out.astype(h.dtype)


def run_trunk(x, pos, mem_k, mem_v, ret_state0, band_past, band_past_pos, P):
    B, L, _ = x.shape
    log_gamma = retention_log_gamma()
    ret_states, band_rows = [], []
    for i in range(DEPTH):
        h = rms_norm(x, P["norm_mix"][i])
        if i % N_MIXERS == 0:
            r = i // N_MIXERS
            q, k, v, g, qm = jnp.split(h @ P["w_in_ret"][r], RET_SPLITS, axis=-1)
            q = rotary(q.reshape(B, L, H_RET, DK_RET), pos)
            k = rotary(k.reshape(B, L, H_RET, DK_RET), pos) * (DK_RET ** -0.5)
            v = v.reshape(B, L, H_RET, DV_RET)
            o, S = retention_scan(q, k, v, ret_state0[r], log_gamma)
            ret_states.append(S)
            o = (group_norm_heads(o, P["gn_ret"][r]) * jax.nn.silu(g.astype(jnp.float32))).astype(x.dtype)
            w_out = P["w_out_ret"][r]
        else:
            a = i // N_MIXERS
            q, k, v, qm = jnp.split(h @ P["w_in_att"][a], ATT_SPLITS, axis=-1)
            q = rms_norm(q.reshape(B, L, H_ATT, D_HEAD), P["q_norm_att"][a])
            k = rms_norm(k.reshape(B, L, H_ATT, D_HEAD), P["k_norm_att"][a])
            v = v.reshape(B, L, H_ATT, D_HEAD)
            bias = P["rel_bias_att"][a]
            if band_past is None:
                o = band_prompt(q, k, v, bias)
            else:
                pk, pv = band_past[a]
                o = band_attend(q, jnp.concatenate([pk.astype(k.dtype), k], axis=1),
                                jnp.concatenate([pv.astype(v.dtype), v], axis=1),
                                pos, jnp.concatenate([band_past_pos, pos]), bias)
            keep = min(BAND_PAST, L)
            band_rows.append((k[:, L - keep:], v[:, L - keep:]))
            o = o.reshape(B, L, ATT_W)
            w_out = P["w_out_att"][a]
        qm = rms_norm(qm.reshape(B, L, H_MEM, D_HEAD), P["q_norm_mem"][i])
        om = mem_attend(qm, mem_k[i], mem_v[i]).reshape(B, L, MEM_W)
        x = x + jnp.concatenate([o, om], axis=-1) @ w_out
        h = rms_norm(x, P["norm_ffn"][i])
        x = x + hier_moe(h.reshape(B * L, D_MODEL), P["w_group"][i], P["b_group"][i],
                         P["w_router"][i], P["b_router"][i], P["w_e_gate"][i],
                         P["w_e_up"][i], P["w_e_down"][i]).reshape(B, L, D_MODEL)
    return x, ret_states, band_rows


def setup_inputs(seed: int = 0) -> dict:
    key = jax.random.key(seed)
    ks = jax.random.split(key, 32)
    f32 = jnp.float32

    def nrm(k, shape, scale):
        return jax.random.normal(k, shape, f32) * scale

    def gain(k, shape):
        return 1.0 + 0.05 * jax.random.normal(k, shape, f32)

    lb = min(BAND_PAST, PAST_LEN)
    return {
        "x_prompt": nrm(ks[0], (BATCH, SEQ, D_MODEL), 1.0),
        "x_sample": nrm(ks[1], (DEC_BATCH, DEC_SEQ, D_MODEL), 1.0),
        "mem_prompt": nrm(ks[2], (BATCH, N_MEM, D_MODEL), 1.0),
        "state_ret": nrm(ks[3], (N_RET_LAYERS, DEC_BATCH, H_RET, DK_RET, DV_RET), 0.5),
        "cache_band_k": nrm(ks[4], (N_ATT_LAYERS, DEC_BATCH, lb, H_ATT, D_HEAD), 1.0),
        "cache_band_v": nrm(ks[5], (N_ATT_LAYERS, DEC_BATCH, lb, H_ATT, D_HEAD), 1.0),
        "cache_mem_k": nrm(ks[6], (DEPTH, DEC_BATCH, N_MEM, H_MEM, D_HEAD), 1.0),
        "cache_mem_v": nrm(ks[7], (DEPTH, DEC_BATCH, N_MEM, H_MEM, D_HEAD), 1.0),
        "norm_mix": gain(ks[8], (DEPTH, D_MODEL)),
        "norm_ffn": gain(ks[9], (DEPTH, D_MODEL)),
        "norm_mem": gain(ks[10], (DEPTH, D_MODEL)),
        "w_in_ret": nrm(ks[11], (N_RET_LAYERS, D_MODEL, RET_IN), D_MODEL ** -0.5),
        "gn_ret": gain(ks[12], (N_RET_LAYERS, RET_V)),
        "w_out_ret": nrm(ks[13], (N_RET_LAYERS, MIX_OUT_RET, D_MODEL), MIX_OUT_RET ** -0.5),
        "w_in_att": nrm(ks[14], (N_ATT_LAYERS, D_MODEL, ATT_IN), D_MODEL ** -0.5),
        "q_norm_att": gain(ks[15], (N_ATT_LAYERS, D_HEAD)),
        "k_norm_att": gain(ks[16], (N_ATT_LAYERS, D_HEAD)),
        "rel_bias_att": nrm(ks[17], (N_ATT_LAYERS, H_ATT, 2 * REL_CLIP + 1), 0.1),
        "w_out_att": nrm(ks[18], (N_ATT_LAYERS, MIX_OUT_ATT, D_MODEL), MIX_OUT_ATT ** -0.5),
        "w_mem_kv": nrm(ks[19], (DEPTH, D_MODEL, 2 * MEM_W), D_MODEL ** -0.5),
        "q_norm_mem": gain(ks[20], (DEPTH, D_HEAD)),
        "k_norm_mem": gain(ks[21], (DEPTH, D_HEAD)),
        "w_group": nrm(ks[22], (DEPTH, D_MODEL, N_GROUPS), D_MODEL ** -0.5),
        "b_group": nrm(ks[23], (DEPTH, N_GROUPS), 0.01),
        "w_router": nrm(ks[24], (DEPTH, D_MODEL, N_EXPERTS), D_MODEL ** -0.5),
        "b_router": nrm(ks[25], (DEPTH, N_EXPERTS), 0.01),
        "w_e_gate": nrm(ks[26], (DEPTH, N_EXPERTS, D_MODEL, D_FF_EXPERT), D_MODEL ** -0.5),
        "w_e_up": nrm(ks[27], (DEPTH, N_EXPERTS, D_MODEL, D_FF_EXPERT), D_MODEL ** -0.5),
        "w_e_down": nrm(ks[28], (DEPTH, N_EXPERTS, D_FF_EXPERT, D_MODEL), D_FF_EXPERT ** -0.5),
    }


def reference(x_prompt, x_sample, mem_prompt, state_ret, cache_band_k, cache_band_v, cache_mem_k,
              cache_mem_v, norm_mix, norm_ffn, norm_mem, w_in_ret, gn_ret, w_out_ret, w_in_att,
              q_norm_att, k_norm_att, rel_bias_att, w_out_att, w_mem_kv, q_norm_mem, k_norm_mem,
              w_group, b_group, w_router, b_router, w_e_gate, w_e_up, w_e_down):
    P = {"norm_mix": norm_mix, "norm_ffn": norm_ffn, "w_in_ret": w_in_ret, "gn_ret": gn_ret,
         "w_out_ret": w_out_ret, "w_in_att": w_in_att, "q_norm_att": q_norm_att,
         "k_norm_att": k_norm_att, "rel_bias_att": rel_bias_att, "w_out_att": w_out_att,
         "q_norm_mem": q_norm_mem, "w_group": w_group, "b_group": b_group, "w_router": w_router,
         "b_router": b_router, "w_e_gate": w_e_gate, "w_e_up": w_e_up, "w_e_down": w_e_down}
    B, L = x_prompt.shape[0], x_prompt.shape[1]
    Bd, Ld = x_sample.shape[0], x_sample.shape[1]

    mem_k_p, mem_v_p = [], []
    for i in range(DEPTH):
        mk, mv = memory_kv(mem_prompt, norm_mem[i], w_mem_kv[i], k_norm_mem[i])
        mem_k_p.append(mk)
        mem_v_p.append(mv)
    pos_p = jnp.arange(L, dtype=jnp.int32)
    ret0_p = [jnp.zeros((B, H_RET, DK_RET, DV_RET), jnp.float32) for _ in range(N_RET_LAYERS)]
    y_prompt, ret_p, band_p = run_trunk(x_prompt, pos_p, mem_k_p, mem_v_p, ret0_p, None, None, P)

    pos_s = PAST_LEN + jnp.arange(Ld, dtype=jnp.int32)
    lb = cache_band_k.shape[2]
    band_past_pos = PAST_LEN - lb + jnp.arange(lb, dtype=jnp.int32)
    mem_k_s = [cache_mem_k[i] for i in range(DEPTH)]
    mem_v_s = [cache_mem_v[i] for i in range(DEPTH)]
    ret0_s = [state_ret[r] for r in range(N_RET_LAYERS)]
    band_past = [(cache_band_k[a], cache_band_v[a]) for a in range(N_ATT_LAYERS)]
    y_sample, ret_s, band_s = run_trunk(x_sample, pos_s, mem_k_s, mem_v_s, ret0_s, band_past,
                                        band_past_pos, P)

    state_ret_prompt = jnp.stack(ret_p)
    state_ret_sample = jnp.stack(ret_s)
    band_k_prompt = jnp.stack([kv[0] for kv in band_p])
    band_v_prompt = jnp.stack([kv[1] for kv in band_p])
    band_k_sample = jnp.stack([kv[0] for kv in band_s])
    band_v_sample = jnp.stack([kv[1] for kv in band_s])
    mem_k_prompt = jnp.stack(mem_k_p)
    mem_v_prompt = jnp.stack(mem_v_p)
    return (y_prompt, y_sample, state_ret_prompt, state_ret_sample, band_k_prompt, band_v_prompt,
            band_k_sample, band_v_sample, mem_k_prompt, mem_v_prompt)
```

```python
import functools

import jax
import jax.numpy as jnp
from jax import lax
from jax.experimental import pallas as pl
from jax.experimental.pallas import tpu as pltpu

F32 = jnp.float32
BF16 = jnp.bfloat16

D_MODEL = 1024
D_HEAD = 128
CHUNK = 64
N_HEADS = 8
H_MEM = 4
N_MEM = 256
MEM_W = H_MEM * D_HEAD
MIX_W = N_HEADS * D_HEAD
N_PREV_CHUNKS = 8
BAND_PAST = N_PREV_CHUNKS * CHUNK
REL_CLIP = 128
N_GROUPS = 4
EXPERTS_PER_GROUP = 8
N_EXPERTS = N_GROUPS * EXPERTS_PER_GROUP
D_FF_EXPERT = 512
ROPE_BASE = 10000.0
EPS = 1e-6
NEG_INF = -1e30
PAST_LEN = 1024

RET_CHUNK = 256
BAND_TQ = 512
BAND_SUB = 128
BAND_WIN = BAND_PAST + BAND_SUB
MOE_BM = 256
LANES = 128
ROUTER_GROUP_LANE0 = N_EXPERTS
VMEM_LIMIT = 56 * 1024 * 1024


def _cparams(sem):
    return pltpu.CompilerParams(dimension_semantics=sem, vmem_limit_bytes=VMEM_LIMIT)


def _dot(a, b):
    return jnp.dot(a, b, preferred_element_type=F32)


def _dot_nt(a, b):
    return lax.dot_general(a, b, (((1,), (1,)), ((), ())), preferred_element_type=F32)


def _sigmoid(x):
    return 1.0 / (1.0 + jnp.exp(-x))


def _norm_proj_kernel(segs, has_rope, x_ref, g_ref, w_ref, gain_ref, *rest):
    if has_rope:
        cos_ref, sin_ref, o_ref = rest
    else:
        (o_ref,) = rest
    x = x_ref[...]
    ms = jnp.mean(x * x, axis=-1, keepdims=True)
    h = (x * lax.rsqrt(ms + EPS) * g_ref[...]).astype(BF16)
    col = 0
    for kind, n_heads in segs:
        width = n_heads * D_HEAD
        acc = _dot(h, w_ref[:, col:col + width])
        for j in range(n_heads):
            a = acc[:, j * D_HEAD:(j + 1) * D_HEAD]
            c0 = col + j * D_HEAD
            gain = gain_ref[:, c0:c0 + D_HEAD]
            if kind == "rope":
                rot = pltpu.roll(a, D_HEAD // 2, 1)
                out = (a * cos_ref[...] + rot * sin_ref[...]) * gain
            elif kind == "norm":
                out = a * lax.rsqrt(jnp.mean(a * a, axis=-1, keepdims=True) + EPS) * gain
            elif kind == "silu":
                out = a * _sigmoid(a)
            else:
                out = a
            o_ref[:, c0:c0 + D_HEAD] = out.astype(o_ref.dtype)
        col += width


def norm_proj(x, g, w, gain, segs, *, tm, out_dtype, n_tiles=None, row_map=None, rope=None):
    T, D = x.shape
    N = w.shape[1]
    assert sum(n for _, n in segs) * D_HEAD == N
    if n_tiles is None:
        n_tiles = T // tm
    if row_map is None:
        row_map = lambda i: i
    in_specs = [
        pl.BlockSpec((tm, D), lambda i: (row_map(i), 0)),
        pl.BlockSpec((1, D), lambda i: (0, 0)),
        pl.BlockSpec((D, N), lambda i: (0, 0)),
        pl.BlockSpec((1, N), lambda i: (0, 0)),
    ]
    args = [x, g.reshape(1, D), w, gain.reshape(1, N)]
    if rope is not None:
        cos2, sin2, tab_map = rope
        in_specs += [pl.BlockSpec((tm, D_HEAD), lambda i: (tab_map(i), 0)),
                     pl.BlockSpec((tm, D_HEAD), lambda i: (tab_map(i), 0))]
        args += [cos2, sin2]
    return pl.pallas_call(
        functools.partial(_norm_proj_kernel, segs, rope is not None),
        grid=(n_tiles,),
        in_specs=in_specs,
        out_specs=pl.BlockSpec((tm, N), lambda i: (i, 0)),
        out_shape=jax.ShapeDtypeStruct((n_tiles * tm, N), out_dtype),
        compiler_params=_cparams(("arbitrary",)),
        name="norm_proj",
    )(*args)


def _retention_kernel(q_ref, k_ref, v_ref, g_ref, s0_ref, dmask_ref, qdec_ref, kdec_ref, gc_ref,
                      gn_ref, o_ref, s_ref):
    @pl.when(pl.program_id(1) == 0)
    def _():
        s_ref[...] = s0_ref[...]

    for h in range(N_HEADS):
        sl = slice(h * D_HEAD, (h + 1) * D_HEAD)
        q = q_ref[:, sl]
        k = k_ref[:, sl]
        v = v_ref[:, sl]
        S = s_ref[0, h]
        scores = _dot_nt(q, k) * dmask_ref[h]
        intra = _dot(scores.astype(BF16), v)
        cross = _dot(q, S.astype(BF16)) * qdec_ref[:, sl]
        o = intra + cross
        kd_t = (k.astype(F32) * kdec_ref[:, sl]).T.astype(BF16)
        s_ref[0, h] = gc_ref[h] * S + _dot(kd_t, v)
        mu = jnp.mean(o, axis=-1, keepdims=True)
        d = o - mu
        var = jnp.mean(d * d, axis=-1, keepdims=True)
        y = d * lax.rsqrt(var + EPS) * gn_ref[:, sl] * g_ref[:, sl].astype(F32)
        o_ref[:, sl] = y.astype(o_ref.dtype)


def _retention_tables(C):
    log_gamma = jnp.log1p(-jnp.exp2(-5.0 - jnp.arange(N_HEADS, dtype=F32)))
    idx = jnp.arange(C, dtype=F32)
    diff = idx[:, None] - idx[None, :]
    dmask = jnp.where(diff[None] >= 0,
                      jnp.exp(log_gamma[:, None, None] * jnp.maximum(diff, 0.0)[None]), 0.0)
    q_dec = jnp.exp(log_gamma[None, :] * (idx + 1.0)[:, None])
    k_dec = jnp.exp(log_gamma[None, :] * (C - 1.0 - idx)[:, None])
    g_c = jnp.exp(log_gamma * C)
    rep = lambda t: jnp.repeat(t, D_HEAD, axis=-1)
    return dmask, rep(q_dec), rep(k_dec), rep(g_c[:, None])[:, None, :]


def retention(proj, s0, gn, *, B, L, C):
    nc = L // C
    dmask, qdec, kdec, gc = _retention_tables(C)
    row = lambda col: pl.BlockSpec((C, MIX_W), lambda b, c: (b * nc + c, col))
    const = lambda shape: pl.BlockSpec(shape, lambda b, c: (0,) * len(shape))
    state_spec = pl.BlockSpec((1, N_HEADS, D_HEAD, D_HEAD), lambda b, c: (b, 0, 0, 0))
    return pl.pallas_call(
        _retention_kernel,
        grid=(B, nc),
        in_specs=[row(0), row(1), row(2), row(3), state_spec,
                  const((N_HEADS, C, C)), const((C, MIX_W)), const((C, MIX_W)),
                  const((N_HEADS, 1, D_HEAD)), const((1, MIX_W))],
        out_specs=[pl.BlockSpec((C, MIX_W), lambda b, c: (b * nc + c, 0)), state_spec],
        out_shape=[jax.ShapeDtypeStruct((B * L, MIX_W), BF16),
                   jax.ShapeDtypeStruct((B, N_HEADS, D_HEAD, D_HEAD), F32)],
        compiler_params=_cparams(("arbitrary", "arbitrary")),
        name="retention",
    )(proj, proj, proj, proj, s0, dmask, qdec, kdec, gc, gn.reshape(1, MIX_W))


def _attend_two_piece(q, k_a, k_b, v_a, v_b, bias_a, bias_b, a_valid):
    s_a = _dot_nt(q, k_a) + bias_a
    if a_valid is not None:
        s_a = jnp.where(a_valid, s_a, NEG_INF)
    s_b = _dot_nt(q, k_b) + bias_b
    m = jnp.maximum(jnp.max(s_a, axis=-1, keepdims=True), jnp.max(s_b, axis=-1, keepdims=True))
    p_a = jnp.exp(s_a - m)
    p_b = jnp.exp(s_b - m)
    denom = jnp.sum(p_a, axis=-1, keepdims=True) + jnp.sum(p_b, axis=-1, keepdims=True)
    o = _dot(p_a.astype(BF16), v_a) + _dot(p_b.astype(BF16), v_b)
    return o / denom


def _band_prompt_kernel(q_ref, kp_ref, kc_ref, vp_ref, vc_ref, bias_ref, o_ref):
    has_prev = pl.program_id(1) > 0
    for h in range(N_HEADS):
        sl = slice(h * D_HEAD, (h + 1) * D_HEAD)
        for s in range(BAND_TQ // BAND_SUB):
            r0 = s * BAND_SUB
            n_prev = BAND_TQ - r0
            n_cur = r0 + BAND_SUB
            o = _attend_two_piece(
                q_ref[r0:r0 + BAND_SUB, sl],
                kp_ref[r0:, sl], kc_ref[:n_cur, sl], vp_ref[r0:, sl], vc_ref[:n_cur, sl],
                bias_ref[h, :, :n_prev], bias_ref[h, :, n_prev:], has_prev)
            o_ref[r0:r0 + BAND_SUB, sl] = o.astype(o_ref.dtype)


def _band_prompt_bias(bias_table):
    r = jnp.arange(BAND_SUB, dtype=jnp.int32)[:, None]
    j = jnp.arange(BAND_WIN, dtype=jnp.int32)[None, :]
    rel = jnp.clip(BAND_PAST + r - j, -REL_CLIP, REL_CLIP) + REL_CLIP
    q_chunk = r // CHUNK + N_PREV_CHUNKS
    k_chunk = j // CHUNK
    allowed = (k_chunk <= q_chunk) & (k_chunk >= q_chunk - N_PREV_CHUNKS)
    return jnp.where(allowed[None], bias_table.astype(F32)[:, rel], NEG_INF)


def band_prompt(proj, bias_table, *, B, L):
    nq = L // BAND_TQ
    cur = lambda col: pl.BlockSpec((BAND_TQ, MIX_W), lambda b, i: (b * nq + i, col))
    prev = lambda col: pl.BlockSpec((BAND_TQ, MIX_W), lambda b, i: (b * nq + jnp.maximum(i - 1, 0), col))
    return pl.pallas_call(
        _band_prompt_kernel,
        grid=(B, nq),
        in_specs=[cur(0), prev(1), cur(1), prev(2), cur(2),
                  pl.BlockSpec((N_HEADS, BAND_SUB, BAND_WIN), lambda b, i: (0, 0, 0))],
        out_specs=pl.BlockSpec((BAND_TQ, MIX_W), lambda b, i: (b * nq + i, 0)),
        out_shape=jax.ShapeDtypeStruct((B * L, MIX_W), BF16),
        compiler_params=_cparams(("arbitrary", "arbitrary")),
        name="band_prompt",
    )(proj, proj, proj, proj, proj, _band_prompt_bias(bias_table))


def _band_sample_kernel(q_ref, kp_ref, kc_ref, vp_ref, vc_ref, bias_p_ref, bias_c_ref, o_ref):
    for h in range(N_HEADS):
        sl = slice(h * D_HEAD, (h + 1) * D_HEAD)
        o = _attend_two_piece(q_ref[:, sl], kp_ref[0, :, sl], kc_ref[:, sl], vp_ref[0, :, sl],
                              vc_ref[:, sl], bias_p_ref[h], bias_c_ref[h], None)
        o_ref[:, sl] = o.astype(o_ref.dtype)


def band_sample(proj, past_k, past_v, bias_table, *, B, L):
    P = past_k.shape[1]
    qpos = PAST_LEN + jnp.arange(L, dtype=jnp.int32)
    kpos = jnp.concatenate([PAST_LEN - P + jnp.arange(P, dtype=jnp.int32), qpos])
    rel = jnp.clip(qpos[:, None] - kpos[None, :], -REL_CLIP, REL_CLIP) + REL_CLIP
    qc = (qpos // CHUNK)[:, None]
    kc = (kpos // CHUNK)[None, :]
    allowed = (kpos[None, :] >= 0) & (kc <= qc) & (kc >= qc - N_PREV_CHUNKS)
    bias = jnp.where(allowed[None], bias_table.astype(F32)[:, rel], NEG_INF)
    cur = lambda col: pl.BlockSpec((L, MIX_W), lambda b: (b, col))
    past = pl.BlockSpec((1, P, MIX_W), lambda b: (b, 0, 0))
    return pl.pallas_call(
        _band_sample_kernel,
        grid=(B,),
        in_specs=[cur(0), past, cur(1), past, cur(2),
                  pl.BlockSpec((N_HEADS, L, P), lambda b: (0, 0, 0)),
                  pl.BlockSpec((N_HEADS, L, L), lambda b: (0, 0, 0))],
        out_specs=pl.BlockSpec((L, MIX_W), lambda b: (b, 0)),
        out_shape=jax.ShapeDtypeStruct((B * L, MIX_W), BF16),
        compiler_params=_cparams(("arbitrary",)),
        name="band_sample",
    )(proj, past_k, proj, past_v, proj, bias[:, :, :P], bias[:, :, P:])


def _out_proj_kernel(o_ref, qm_ref, mk_ref, mv_ref, w_ref, x_ref, gf_ref, wr_ref, br_ref,
                     x1_ref, h2_ref, route_ref):
    parts = [o_ref[...]]
    for h in range(H_MEM):
        sl = slice(h * D_HEAD, (h + 1) * D_HEAD)
        s = _dot_nt(qm_ref[:, sl], mk_ref[0, :, sl])
        p = jnp.exp(s - jnp.max(s, axis=-1, keepdims=True))
        om = _dot(p.astype(BF16), mv_ref[0, :, sl]) / jnp.sum(p, axis=-1, keepdims=True)
        parts.append(om.astype(BF16))
    x1 = x_ref[...] + _dot(jnp.concatenate(parts, axis=1), w_ref[...])
    x1_ref[...] = x1
    hn = x1 * lax.rsqrt(jnp.mean(x1 * x1, axis=-1, keepdims=True) + EPS) * gf_ref[...]
    hb = hn.astype(BF16)
    h2_ref[...] = hb

    logits = _dot(hb, wr_ref[...]) + br_ref[...]
    lane = lax.broadcasted_iota(jnp.int32, logits.shape, 1)
    lane_f = lane.astype(F32)
    far = float(LANES)
    is_group = (lane >= ROUTER_GROUP_LANE0) & (lane < ROUTER_GROUP_LANE0 + N_GROUPS)
    gl = jnp.where(is_group, logits, NEG_INF)
    g_max = jnp.max(gl, axis=-1, keepdims=True)
    g_sel = jnp.min(jnp.where(gl == g_max, lane_f, far), axis=-1, keepdims=True) - float(ROUTER_GROUP_LANE0)
    g_gate = 1.0 / jnp.sum(jnp.exp(gl - g_max), axis=-1, keepdims=True)
    in_group = jnp.right_shift(lane, 3).astype(F32) == g_sel
    el = jnp.where(in_group, logits, NEG_INF)
    m1 = jnp.max(el, axis=-1, keepdims=True)
    i1 = jnp.min(jnp.where(el == m1, lane_f, far), axis=-1, keepdims=True)
    el2 = jnp.where(lane_f == i1, NEG_INF, el)
    m2 = jnp.max(el2, axis=-1, keepdims=True)
    i2 = jnp.min(jnp.where(el2 == m2, lane_f, far), axis=-1, keepdims=True)
    e2 = jnp.exp(m2 - m1)
    p1 = 1.0 / (1.0 + e2)
    route = jnp.where(lane == 0, i1, jnp.where(lane == 1, i2, jnp.where(
        lane == 2, g_gate * p1, jnp.where(lane == 3, g_gate * (e2 * p1), 0.0))))
    route_ref[...] = route


def out_proj(mixed, proj, qm_block, mem_k, mem_v, w_out, x, g_ffn, w_router, b_router, *, tm, rows_per_stream):
    T = x.shape[0]
    tiles_per_stream = rows_per_stream // tm
    row = lambda width, col: pl.BlockSpec((tm, width), lambda i: (i, col))
    const = lambda shape: pl.BlockSpec(shape, lambda i: (0,) * len(shape))
    mem = pl.BlockSpec((1, N_MEM, MEM_W), lambda i: (i // tiles_per_stream, 0, 0))
    return pl.pallas_call(
        _out_proj_kernel,
        grid=(T // tm,),
        in_specs=[row(MIX_W, 0), row(MEM_W, qm_block), mem, mem, const((MIX_W + MEM_W, D_MODEL)),
                  row(D_MODEL, 0), const((1, D_MODEL)), const((D_MODEL, LANES)), const((1, LANES))],
        out_specs=[row(D_MODEL, 0), row(D_MODEL, 0), row(LANES, 0)],
        out_shape=[jax.ShapeDtypeStruct((T, D_MODEL), F32), jax.ShapeDtypeStruct((T, D_MODEL), BF16),
                   jax.ShapeDtypeStruct((T, LANES), F32)],
        compiler_params=_cparams(("arbitrary",)),
        name="out_proj",
    )(mixed, proj, mem_k, mem_v, w_out, x, g_ffn.reshape(1, D_MODEL), w_router, b_router)


def _moe_kernel(be_ref, nu_ref, x_ref, wg_ref, wu_ref, wd_ref, y_ref):
    used = pl.program_id(0) < nu_ref[0]

    @pl.when(used)
    def _():
        x = x_ref[...]
        g = _dot(x, wg_ref[0])
        u = _dot(x, wu_ref[0])
        a = (g * _sigmoid(g) * u).astype(BF16)
        y_ref[...] = _dot(a, wd_ref[0])

    @pl.when(jnp.logical_not(used))
    def _():
        y_ref[...] = jnp.zeros_like(y_ref)


def moe_experts(x_pad, block_e, n_used, wg, wu, wd):
    n_rows = x_pad.shape[0]
    n_blocks = n_rows // MOE_BM
    grid_spec = pltpu.PrefetchScalarGridSpec(
        num_scalar_prefetch=2,
        grid=(n_blocks,),
        in_specs=[
            pl.BlockSpec((MOE_BM, D_MODEL), lambda i, be, nu: (i, 0)),
            pl.BlockSpec((1, D_MODEL, D_FF_EXPERT), lambda i, be, nu: (be[i], 0, 0)),
            pl.BlockSpec((1, D_MODEL, D_FF_EXPERT), lambda i, be, nu: (be[i], 0, 0)),
            pl.BlockSpec((1, D_FF_EXPERT, D_MODEL), lambda i, be, nu: (be[i], 0, 0)),
        ],
        out_specs=pl.BlockSpec((MOE_BM, D_MODEL), lambda i, be, nu: (i, 0)),
    )
    return pl.pallas_call(
        _moe_kernel,
        grid_spec=grid_spec,
        out_shape=jax.ShapeDtypeStruct((n_rows, D_MODEL), F32),
        compiler_params=_cparams(("arbitrary",)),
        name="moe_experts",
    )(block_e, n_used, x_pad, wg, wu, wd)


def hier_moe(x1, h2, route, wg, wu, wd):
    T = x1.shape[0]
    n_assign = 2 * T
    flat_e = route[:, :2].astype(jnp.int32).reshape(-1)
    gates = route[:, 2:4]
    order = jnp.argsort(flat_e, stable=True)
    sorted_e = flat_e[order]
    counts = jnp.bincount(flat_e, length=N_EXPERTS)
    starts = jnp.cumsum(counts) - counts
    padded = ((counts + MOE_BM - 1) // MOE_BM) * MOE_BM
    pad_ends = jnp.cumsum(padded)
    pad_starts = pad_ends - padded
    dest = (pad_starts[sorted_e] + (jnp.arange(n_assign) - starts[sorted_e])).astype(jnp.int32)
    n_blocks = (n_assign + N_EXPERTS * (MOE_BM - 1) + MOE_BM - 1) // MOE_BM
    n_rows = n_blocks * MOE_BM
    src_tok = jnp.zeros((n_rows,), jnp.int32).at[dest].set((order // 2).astype(jnp.int32))
    pos = jnp.zeros((n_assign,), jnp.int32).at[order].set(dest)
    block_e = jnp.minimum(jnp.searchsorted(pad_ends, jnp.arange(n_blocks) * MOE_BM, side="right"),
                          N_EXPERTS - 1).astype(jnp.int32)
    n_used = (pad_ends[-1] // MOE_BM).astype(jnp.int32).reshape(1)
    y_pad = moe_experts(h2[src_tok], block_e, n_used, wg, wu, wd)
    pos = pos.reshape(T, 2)
    return x1 + gates[:, 0:1] * y_pad[pos[:, 0]] + gates[:, 1:2] * y_pad[pos[:, 1]]


def _rope_tables(pos):
    half = D_HEAD // 2
    inv = ROPE_BASE ** (-jnp.arange(half, dtype=F32) / half)
    ang = pos.astype(F32)[:, None] * inv[None, :]
    cos, sin = jnp.cos(ang), jnp.sin(ang)
    return jnp.concatenate([cos, cos], axis=-1), jnp.concatenate([-sin, sin], axis=-1)


def _router_weights(w_group, b_group, w_router, b_router):
    w = jnp.zeros((D_MODEL, LANES), F32)
    w = w.at[:, :N_EXPERTS].set(w_router).at[:, ROUTER_GROUP_LANE0:ROUTER_GROUP_LANE0 + N_GROUPS].set(w_group)
    b = jnp.zeros((1, LANES), F32)
    b = b.at[0, :N_EXPERTS].set(b_router).at[0, ROUTER_GROUP_LANE0:ROUTER_GROUP_LANE0 + N_GROUPS].set(b_group)
    return w.astype(BF16), b


def _tile(v, n):
    return jnp.tile(v.astype(F32), n)


def _run_trunk(x, B, L, mem_k, mem_v, ret_state0, band_past, P):
    T = B * L
    prompt = band_past is None
    tm = 512 if prompt else L
    proj_tm = 512
    scale = D_HEAD ** -0.5
    ones = lambda n: jnp.ones((n * D_HEAD,), F32)
    if prompt:
        cos2, sin2 = _rope_tables(jnp.arange(L, dtype=jnp.int32))
        tab_map = lambda i: i % (L // proj_tm)
    else:
        cos2, sin2 = _rope_tables(PAST_LEN + jnp.arange(L, dtype=jnp.int32))
        cos2, sin2 = jnp.tile(cos2, (B, 1)), jnp.tile(sin2, (B, 1))
        tab_map = lambda i: i
    out = {}

    segs0 = (("rope", N_HEADS), ("rope", N_HEADS), ("plain", N_HEADS), ("silu", N_HEADS), ("norm", H_MEM))
    gain0 = jnp.concatenate([ones(N_HEADS), ones(N_HEADS) * scale, ones(2 * N_HEADS),
                             _tile(P["q_norm_mem"][0], H_MEM) * scale])
    proj = norm_proj(x, P["norm_mix"][0], P["w_in_ret"][0].astype(BF16), gain0, segs0, tm=proj_tm,
                     out_dtype=BF16, rope=(cos2, sin2, tab_map))
    mixed, S = retention(proj, ret_state0, P["gn_ret"][0], B=B, L=L, C=RET_CHUNK if prompt else L)
    out["ret_state"] = S
    x1, h2, route = out_proj(mixed, proj, (4 * MIX_W) // MEM_W, mem_k[0], mem_v[0],
                             P["w_out_ret"][0].astype(BF16), x, P["norm_ffn"][0],
                             *_router_weights(P["w_group"][0], P["b_group"][0], P["w_router"][0], P["b_router"][0]),
                             tm=tm, rows_per_stream=L)
    x = hier_moe(x1, h2, route, P["w_e_gate"][0].astype(BF16), P["w_e_up"][0].astype(BF16),
                 P["w_e_down"][0].astype(BF16))

    segs1 = (("norm", N_HEADS), ("norm", N_HEADS), ("plain", N_HEADS), ("norm", H_MEM))
    gain1 = jnp.concatenate([_tile(P["q_norm_att"][0], N_HEADS) * scale, _tile(P["k_norm_att"][0], N_HEADS),
                             ones(N_HEADS), _tile(P["q_norm_mem"][1], H_MEM) * scale])
    w_in = P["w_in_att"][0].astype(BF16)
    if prompt:
        proj = norm_proj(x, P["norm_mix"][1], w_in, gain1, segs1, tm=proj_tm, out_dtype=BF16)
        keep = min(BAND_PAST, L)
        tiles = L // keep
        kv = norm_proj(x, P["norm_mix"][1], w_in[:, MIX_W:3 * MIX_W], gain1[MIX_W:3 * MIX_W], segs1[1:3],
                       tm=keep, out_dtype=F32, n_tiles=B, row_map=lambda i: i * tiles + tiles - 1)
        out["band_k"] = kv[:, :MIX_W].reshape(B, keep, N_HEADS, D_HEAD)
        out["band_v"] = kv[:, MIX_W:].reshape(B, keep, N_HEADS, D_HEAD)
        mixed = band_prompt(proj, P["rel_bias_att"][0], B=B, L=L)
    else:
        proj_f = norm_proj(x, P["norm_mix"][1], w_in, gain1, segs1, tm=proj_tm, out_dtype=F32)
        out["band_k"] = proj_f[:, MIX_W:2 * MIX_W].reshape(B, L, N_HEADS, D_HEAD)
        out["band_v"] = proj_f[:, 2 * MIX_W:3 * MIX_W].reshape(B, L, N_HEADS, D_HEAD)
        proj = proj_f.astype(BF16)
        pk, pv = band_past
        mixed = band_sample(proj, pk, pv, P["rel_bias_att"][0], B=B, L=L)
    x1, h2, route = out_proj(mixed, proj, (3 * MIX_W) // MEM_W, mem_k[1], mem_v[1],
                             P["w_out_att"][0].astype(BF16), x, P["norm_ffn"][1],
                             *_router_weights(P["w_group"][1], P["b_group"][1], P["w_router"][1], P["b_router"][1]),
                             tm=tm, rows_per_stream=L)
    x = hier_moe(x1, h2, route, P["w_e_gate"][1].astype(BF16), P["w_e_up"][1].astype(BF16),
                 P["w_e_down"][1].astype(BF16))
    out["y"] = x
    return out


def kernel(x_prompt, x_sample, mem_prompt, state_ret, cache_band_k, cache_band_v, cache_mem_k, cache_mem_v, norm_mix, norm_ffn, norm_mem, w_in_ret, gn_ret, w_out_ret, w_in_att, q_norm_att, k_norm_att, rel_bias_att, w_out_att, w_mem_kv, q_norm_mem, k_norm_mem, w_group, b_group, w_router, b_router, w_e_gate, w_e_up, w_e_down):
    P = {"norm_mix": norm_mix, "norm_ffn": norm_ffn, "w_in_ret": w_in_ret, "gn_ret": gn_ret,
         "w_out_ret": w_out_ret, "w_in_att": w_in_att, "q_norm_att": q_norm_att,
         "k_norm_att": k_norm_att, "rel_bias_att": rel_bias_att, "w_out_att": w_out_att,
         "q_norm_mem": q_norm_mem, "w_group": w_group, "b_group": b_group, "w_router": w_router,
         "b_router": b_router, "w_e_gate": w_e_gate, "w_e_up": w_e_up, "w_e_down": w_e_down}
    B, L, D = x_prompt.shape
    Bd, Ld, _ = x_sample.shape
    depth = norm_mix.shape[0]

    mem_k_p, mem_v_p = [], []
    for i in range(depth):
        gain = jnp.concatenate([_tile(k_norm_mem[i], H_MEM), jnp.ones((MEM_W,), F32)])
        kv = norm_proj(mem_prompt.reshape(B * N_MEM, D), norm_mem[i], w_mem_kv[i].astype(BF16), gain,
                       (("norm", H_MEM), ("plain", H_MEM)), tm=N_MEM, out_dtype=F32)
        mem_k_p.append(kv[:, :MEM_W].reshape(B, N_MEM, MEM_W))
        mem_v_p.append(kv[:, MEM_W:].reshape(B, N_MEM, MEM_W))

    zeros_state = jnp.zeros((B, N_HEADS, D_HEAD, D_HEAD), F32)
    res_p = _run_trunk(x_prompt.reshape(B * L, D), B, L, [m.astype(BF16) for m in mem_k_p],
                       [m.astype(BF16) for m in mem_v_p], zeros_state, None, P)

    lb = cache_band_k.shape[2]
    band_past = (cache_band_k[0].reshape(Bd, lb, MIX_W).astype(BF16),
                 cache_band_v[0].reshape(Bd, lb, MIX_W).astype(BF16))
    res_s = _run_trunk(x_sample.reshape(Bd * Ld, D), Bd, Ld,
                       [cache_mem_k[i].reshape(Bd, N_MEM, MEM_W).astype(BF16) for i in range(depth)],
                       [cache_mem_v[i].reshape(Bd, N_MEM, MEM_W).astype(BF16) for i in range(depth)],
                       state_ret[0], band_past, P)

    as_heads = lambda ms: jnp.stack(ms).reshape(depth, B, N_MEM, H_MEM, D_HEAD)
    return (res_p["y"].reshape(B, L, D), res_s["y"].reshape(Bd, Ld, D),
            res_p["ret_state"][None], res_s["ret_state"][None],
            res_p["band_k"][None], res_p["band_v"][None], res_s["band_k"][None], res_s["band_v"][None],
            as_heads(mem_k_p), as_heads(mem_v_p))
```

```python
import functools

import jax
import jax.numpy as jnp
from jax import lax
from jax.experimental import pallas as pl
from jax.experimental.pallas import tpu as pltpu

F32 = jnp.float32
BF16 = jnp.bfloat16

D_MODEL = 1024
D_HEAD = 128
CHUNK = 64
N_HEADS = 8
H_MEM = 4
N_MEM = 256
MEM_W = H_MEM * D_HEAD
MIX_W = N_HEADS * D_HEAD
N_PREV_CHUNKS = 8
BAND_PAST = N_PREV_CHUNKS * CHUNK
REL_CLIP = 128
N_GROUPS = 4
EXPERTS_PER_GROUP = 8
N_EXPERTS = N_GROUPS * EXPERTS_PER_GROUP
D_FF_EXPERT = 512
ROPE_BASE = 10000.0
EPS = 1e-6
NEG_INF = -1e30
PAST_LEN = 1024

LANES = 128
SUBLANES = 8
SLABS = D_MODEL // LANES
assert SLABS == SUBLANES

RET_CHUNK = 256
BAND_TQ = 512
BAND_SUB = 128
BAND_WIN = BAND_PAST + BAND_SUB
MOE_BM = 256
DISPATCH_TM = 2048
COMBINE_TM = 1024
ROUTER_GROUP_LANE0 = N_EXPERTS
VMEM_LIMIT = 56 * 1024 * 1024


def _cparams(sem):
    return pltpu.CompilerParams(dimension_semantics=sem, vmem_limit_bytes=VMEM_LIMIT)


def _dot(a, b):
    return jnp.dot(a, b, preferred_element_type=F32)


def _dot_nt(a, b):
    return lax.dot_general(a, b, (((1,), (1,)), ((), ())), preferred_element_type=F32)


def _sigmoid(x):
    return 1.0 / (1.0 + jnp.exp(-x))


def _store_slabs(ref, val):
    for j in range(SLABS):
        ref[:, j, :] = val[:, j * LANES:(j + 1) * LANES]


def _load_slabs(ref, r0, rows):
    return [ref[r0:r0 + rows, j, :] for j in range(SLABS)]


def _norm_proj_kernel(segs, has_rope, x_ref, g_ref, w_ref, gain_ref, *rest):
    if has_rope:
        cos_ref, sin_ref, o_ref = rest
    else:
        (o_ref,) = rest
    x = x_ref[...]
    ms = jnp.mean(x * x, axis=-1, keepdims=True)
    h = (x * lax.rsqrt(ms + EPS) * g_ref[...]).astype(BF16)
    col = 0
    for kind, n_heads in segs:
        width = n_heads * D_HEAD
        acc = _dot(h, w_ref[:, col:col + width])
        for j in range(n_heads):
            a = acc[:, j * D_HEAD:(j + 1) * D_HEAD]
            c0 = col + j * D_HEAD
            gain = gain_ref[:, c0:c0 + D_HEAD]
            if kind == "rope":
                rot = pltpu.roll(a, D_HEAD // 2, 1)
                out = (a * cos_ref[...] + rot * sin_ref[...]) * gain
            elif kind == "norm":
                out = a * lax.rsqrt(jnp.mean(a * a, axis=-1, keepdims=True) + EPS) * gain
            elif kind == "silu":
                out = a * _sigmoid(a)
            else:
                out = a
            o_ref[:, c0:c0 + D_HEAD] = out.astype(o_ref.dtype)
        col += width


def norm_proj(x, g, w, gain, segs, *, tm, out_dtype, n_tiles=None, row_map=None, rope=None):
    T, D = x.shape
    N = w.shape[1]
    assert sum(n for _, n in segs) * D_HEAD == N
    if n_tiles is None:
        n_tiles = T // tm
    if row_map is None:
        row_map = lambda i: i
    in_specs = [
        pl.BlockSpec((tm, D), lambda i: (row_map(i), 0)),
        pl.BlockSpec((1, D), lambda i: (0, 0)),
        pl.BlockSpec((D, N), lambda i: (0, 0)),
        pl.BlockSpec((1, N), lambda i: (0, 0)),
    ]
    args = [x, g.reshape(1, D), w, gain.reshape(1, N)]
    if rope is not None:
        cos2, sin2, tab_map = rope
        in_specs += [pl.BlockSpec((tm, D_HEAD), lambda i: (tab_map(i), 0)),
                     pl.BlockSpec((tm, D_HEAD), lambda i: (tab_map(i), 0))]
        args += [cos2, sin2]
    return pl.pallas_call(
        functools.partial(_norm_proj_kernel, segs, rope is not None),
        grid=(n_tiles,),
        in_specs=in_specs,
        out_specs=pl.BlockSpec((tm, N), lambda i: (i, 0)),
        out_shape=jax.ShapeDtypeStruct((n_tiles * tm, N), out_dtype),
        compiler_params=_cparams(("arbitrary",)),
        name="norm_proj",
    )(*args)


def _retention_kernel(q_ref, k_ref, v_ref, g_ref, s0_ref, dmask_ref, qdec_ref, kdec_ref, gc_ref,
                      gn_ref, o_ref, s_ref):
    @pl.when(pl.program_id(1) == 0)
    def _():
        s_ref[...] = s0_ref[...]

    for h in range(N_HEADS):
        sl = slice(h * D_HEAD, (h + 1) * D_HEAD)
        q = q_ref[:, sl]
        k = k_ref[:, sl]
        v = v_ref[:, sl]
        S = s_ref[0, h]
        scores = _dot_nt(q, k) * dmask_ref[h]
        intra = _dot(scores.astype(BF16), v)
        cross = _dot(q, S.astype(BF16)) * qdec_ref[:, sl]
        o = intra + cross
        kd_t = (k.astype(F32) * kdec_ref[:, sl]).T.astype(BF16)
        s_ref[0, h] = gc_ref[h] * S + _dot(kd_t, v)
        mu = jnp.mean(o, axis=-1, keepdims=True)
        d = o - mu
        var = jnp.mean(d * d, axis=-1, keepdims=True)
        y = d * lax.rsqrt(var + EPS) * gn_ref[:, sl] * g_ref[:, sl].astype(F32)
        o_ref[:, sl] = y.astype(o_ref.dtype)


def _retention_tables(C):
    log_gamma = jnp.log1p(-jnp.exp2(-5.0 - jnp.arange(N_HEADS, dtype=F32)))
    idx = jnp.arange(C, dtype=F32)
    diff = idx[:, None] - idx[None, :]
    dmask = jnp.where(diff[None] >= 0,
                      jnp.exp(log_gamma[:, None, None] * jnp.maximum(diff, 0.0)[None]), 0.0)
    q_dec = jnp.exp(log_gamma[None, :] * (idx + 1.0)[:, None])
    k_dec = jnp.exp(log_gamma[None, :] * (C - 1.0 - idx)[:, None])
    g_c = jnp.exp(log_gamma * C)
    rep = lambda t: jnp.repeat(t, D_HEAD, axis=-1)
    return dmask, rep(q_dec), rep(k_dec), rep(g_c[:, None])[:, None, :]


def retention(proj, s0, gn, *, B, L, C):
    nc = L // C
    dmask, qdec, kdec, gc = _retention_tables(C)
    row = lambda col: pl.BlockSpec((C, MIX_W), lambda b, c: (b * nc + c, col))
    const = lambda shape: pl.BlockSpec(shape, lambda b, c: (0,) * len(shape))
    state_spec = pl.BlockSpec((1, N_HEADS, D_HEAD, D_HEAD), lambda b, c: (b, 0, 0, 0))
    return pl.pallas_call(
        _retention_kernel,
        grid=(B, nc),
        in_specs=[row(0), row(1), row(2), row(3), state_spec,
                  const((N_HEADS, C, C)), const((C, MIX_W)), const((C, MIX_W)),
                  const((N_HEADS, 1, D_HEAD)), const((1, MIX_W))],
        out_specs=[pl.BlockSpec((C, MIX_W), lambda b, c: (b * nc + c, 0)), state_spec],
        out_shape=[jax.ShapeDtypeStruct((B * L, MIX_W), BF16),
                   jax.ShapeDtypeStruct((B, N_HEADS, D_HEAD, D_HEAD), F32)],
        compiler_params=_cparams(("arbitrary", "arbitrary")),
        name="retention",
    )(proj, proj, proj, proj, s0, dmask, qdec, kdec, gc, gn.reshape(1, MIX_W))


def _attend_two_piece(q, k_a, k_b, v_a, v_b, bias_a, bias_b, a_valid):
    s_a = _dot_nt(q, k_a) + bias_a
    if a_valid is not None:
        s_a = jnp.where(a_valid, s_a, NEG_INF)
    s_b = _dot_nt(q, k_b) + bias_b
    m = jnp.maximum(jnp.max(s_a, axis=-1, keepdims=True), jnp.max(s_b, axis=-1, keepdims=True))
    p_a = jnp.exp(s_a - m)
    p_b = jnp.exp(s_b - m)
    denom = jnp.sum(p_a, axis=-1, keepdims=True) + jnp.sum(p_b, axis=-1, keepdims=True)
    o = _dot(p_a.astype(BF16), v_a) + _dot(p_b.astype(BF16), v_b)
    return o / denom


def _band_prompt_kernel(q_ref, kp_ref, kc_ref, vp_ref, vc_ref, bias_ref, o_ref):
    has_prev = pl.program_id(1) > 0
    for h in range(N_HEADS):
        sl = slice(h * D_HEAD, (h + 1) * D_HEAD)
        for s in range(BAND_TQ // BAND_SUB):
            r0 = s * BAND_SUB
            n_prev = BAND_TQ - r0
            n_cur = r0 + BAND_SUB
            o = _attend_two_piece(
                q_ref[r0:r0 + BAND_SUB, sl],
                kp_ref[r0:, sl], kc_ref[:n_cur, sl], vp_ref[r0:, sl], vc_ref[:n_cur, sl],
                bias_ref[h, :, :n_prev], bias_ref[h, :, n_prev:], has_prev)
            o_ref[r0:r0 + BAND_SUB, sl] = o.astype(o_ref.dtype)


def _band_prompt_bias(bias_table):
    r = jnp.arange(BAND_SUB, dtype=jnp.int32)[:, None]
    j = jnp.arange(BAND_WIN, dtype=jnp.int32)[None, :]
    rel = jnp.clip(BAND_PAST + r - j, -REL_CLIP, REL_CLIP) + REL_CLIP
    q_chunk = r // CHUNK + N_PREV_CHUNKS
    k_chunk = j // CHUNK
    allowed = (k_chunk <= q_chunk) & (k_chunk >= q_chunk - N_PREV_CHUNKS)
    return jnp.where(allowed[None], bias_table.astype(F32)[:, rel], NEG_INF)


def band_prompt(proj, bias_table, *, B, L):
    nq = L // BAND_TQ
    cur = lambda col: pl.BlockSpec((BAND_TQ, MIX_W), lambda b, i: (b * nq + i, col))
    prev = lambda col: pl.BlockSpec((BAND_TQ, MIX_W), lambda b, i: (b * nq + jnp.maximum(i - 1, 0), col))
    return pl.pallas_call(
        _band_prompt_kernel,
        grid=(B, nq),
        in_specs=[cur(0), prev(1), cur(1), prev(2), cur(2),
                  pl.BlockSpec((N_HEADS, BAND_SUB, BAND_WIN), lambda b, i: (0, 0, 0))],
        out_specs=pl.BlockSpec((BAND_TQ, MIX_W), lambda b, i: (b * nq + i, 0)),
        out_shape=jax.ShapeDtypeStruct((B * L, MIX_W), BF16),
        compiler_params=_cparams(("arbitrary", "arbitrary")),
        name="band_prompt",
    )(proj, proj, proj, proj, proj, _band_prompt_bias(bias_table))


def _band_sample_kernel(q_ref, kp_ref, kc_ref, vp_ref, vc_ref, bias_p_ref, bias_c_ref, o_ref):
    for h in range(N_HEADS):
        sl = slice(h * D_HEAD, (h + 1) * D_HEAD)
        o = _attend_two_piece(q_ref[:, sl], kp_ref[0, :, h, :].astype(BF16), kc_ref[:, sl],
                              vp_ref[0, :, h, :].astype(BF16), vc_ref[:, sl], bias_p_ref[h], bias_c_ref[h], None)
        o_ref[:, sl] = o.astype(o_ref.dtype)


def band_sample(proj, past_k, past_v, bias_table, *, B, L):
    P = past_k.shape[1]
    qpos = PAST_LEN + jnp.arange(L, dtype=jnp.int32)
    kpos = jnp.concatenate([PAST_LEN - P + jnp.arange(P, dtype=jnp.int32), qpos])
    rel = jnp.clip(qpos[:, None] - kpos[None, :], -REL_CLIP, REL_CLIP) + REL_CLIP
    qc = (qpos // CHUNK)[:, None]
    kc = (kpos // CHUNK)[None, :]
    allowed = (kpos[None, :] >= 0) & (kc <= qc) & (kc >= qc - N_PREV_CHUNKS)
    bias = jnp.where(allowed[None], bias_table.astype(F32)[:, rel], NEG_INF)
    cur = lambda col: pl.BlockSpec((L, MIX_W), lambda b: (b, col))
    past = pl.BlockSpec((1, P, N_HEADS, D_HEAD), lambda b: (b, 0, 0, 0))
    return pl.pallas_call(
        _band_sample_kernel,
        grid=(B,),
        in_specs=[cur(0), past, cur(1), past, cur(2),
                  pl.BlockSpec((N_HEADS, L, P), lambda b: (0, 0, 0)),
                  pl.BlockSpec((N_HEADS, L, L), lambda b: (0, 0, 0))],
        out_specs=pl.BlockSpec((L, MIX_W), lambda b: (b, 0)),
        out_shape=jax.ShapeDtypeStruct((B * L, MIX_W), BF16),
        compiler_params=_cparams(("arbitrary",)),
        name="band_sample",
    )(proj, past_k, proj, past_v, proj, bias[:, :, :P], bias[:, :, P:])


def _out_proj_kernel(o_ref, qm_ref, mk_ref, mv_ref, w_ref, x_ref, gf_ref, wr_ref, br_ref, tri_ref,
                     x1_ref, h2_ref, route_ref, cnt_ref):
    @pl.when(pl.program_id(0) == 0)
    def _():
        cnt_ref[...] = jnp.zeros_like(cnt_ref)

    parts = [o_ref[...]]
    for h in range(H_MEM):
        sl = slice(h * D_HEAD, (h + 1) * D_HEAD)
        s = _dot_nt(qm_ref[:, sl], mk_ref[0, 0, :, h, :].astype(BF16))
        p = jnp.exp(s - jnp.max(s, axis=-1, keepdims=True))
        om = _dot(p.astype(BF16), mv_ref[0, 0, :, h, :].astype(BF16)) / jnp.sum(p, axis=-1, keepdims=True)
        parts.append(om.astype(BF16))
    x1 = x_ref[...] + _dot(jnp.concatenate(parts, axis=1), w_ref[...])
    x1_ref[...] = x1
    hn = x1 * lax.rsqrt(jnp.mean(x1 * x1, axis=-1, keepdims=True) + EPS) * gf_ref[...]
    _store_slabs(h2_ref, hn)

    logits = _dot(hn.astype(BF16), wr_ref[...]) + br_ref[...]
    lane = lax.broadcasted_iota(jnp.int32, logits.shape, 1)
    lane_f = lane.astype(F32)
    far = float(LANES)
    is_group = (lane >= ROUTER_GROUP_LANE0) & (lane < ROUTER_GROUP_LANE0 + N_GROUPS)
    gl = jnp.where(is_group, logits, NEG_INF)
    g_max = jnp.max(gl, axis=-1, keepdims=True)
    g_sel = jnp.min(jnp.where(gl == g_max, lane_f, far), axis=-1, keepdims=True) - float(ROUTER_GROUP_LANE0)
    g_gate = 1.0 / jnp.sum(jnp.exp(gl - g_max), axis=-1, keepdims=True)
    in_group = jnp.right_shift(lane, 3).astype(F32) == g_sel
    el = jnp.where(in_group, logits, NEG_INF)
    m1 = jnp.max(el, axis=-1, keepdims=True)
    i1 = jnp.min(jnp.where(el == m1, lane_f, far), axis=-1, keepdims=True)
    el2 = jnp.where(lane_f == i1, NEG_INF, el)
    m2 = jnp.max(el2, axis=-1, keepdims=True)
    i2 = jnp.min(jnp.where(el2 == m2, lane_f, far), axis=-1, keepdims=True)
    e2 = jnp.exp(m2 - m1)
    p1 = 1.0 / (1.0 + e2)

    oh1 = lane_f == i1
    oh2 = lane_f == i2
    onehot = jnp.where(oh1 | oh2, 1.0, 0.0)
    before = cnt_ref[0:1, :] + _dot(tri_ref[...], onehot.astype(BF16))
    r1 = jnp.sum(jnp.where(oh1, before, 0.0), axis=-1, keepdims=True)
    r2 = jnp.sum(jnp.where(oh2, before, 0.0), axis=-1, keepdims=True)
    cnt_ref[0:1, :] = cnt_ref[0:1, :] + jnp.sum(onehot, axis=0, keepdims=True)

    route = jnp.where(lane == 0, i1, jnp.where(lane == 1, i2, jnp.where(
        lane == 2, g_gate * p1, jnp.where(lane == 3, g_gate * (e2 * p1), jnp.where(
            lane == 4, r1, jnp.where(lane == 5, r2, 0.0))))))
    route_ref[...] = route


def out_proj(mixed, proj, qm_block, mem_k, mem_v, layer, w_out, x, g_ffn, w_router, b_router, *, tm,
             rows_per_stream):
    T = x.shape[0]
    tiles_per_stream = rows_per_stream // tm
    row = lambda width, col: pl.BlockSpec((tm, width), lambda i: (i, col))
    const = lambda shape: pl.BlockSpec(shape, lambda i: (0,) * len(shape))
    mem = pl.BlockSpec((1, 1, N_MEM, H_MEM, D_HEAD), lambda i: (layer, i // tiles_per_stream, 0, 0, 0))
    tri = (jnp.arange(tm)[None, :] < jnp.arange(tm)[:, None]).astype(BF16)
    return pl.pallas_call(
        _out_proj_kernel,
        grid=(T // tm,),
        in_specs=[row(MIX_W, 0), row(MEM_W, qm_block), mem, mem, const((MIX_W + MEM_W, D_MODEL)),
                  row(D_MODEL, 0), const((1, D_MODEL)), const((D_MODEL, LANES)), const((1, LANES)),
                  const((tm, tm))],
        out_specs=[row(D_MODEL, 0), pl.BlockSpec((tm, SLABS, LANES), lambda i: (i, 0, 0)), row(LANES, 0),
                   const((SUBLANES, LANES))],
        out_shape=[jax.ShapeDtypeStruct((T, D_MODEL), F32), jax.ShapeDtypeStruct((T, SLABS, LANES), F32),
                   jax.ShapeDtypeStruct((T, LANES), F32), jax.ShapeDtypeStruct((SUBLANES, LANES), F32)],
        compiler_params=_cparams(("arbitrary",)),
        name="out_proj",
    )(mixed, proj, mem_k, mem_v, w_out, x, g_ffn.reshape(1, D_MODEL), w_router, b_router, tri)


def _dispatch_kernel(tm, pe_ref, pos_ref, h2_hbm, xpad_hbm, zbuf, sem, zsem):
    i = pl.program_id(0)

    @pl.when(i == 0)
    def _():
        zbuf[...] = jnp.zeros_like(zbuf)

        def tail_copy(e):
            return pltpu.make_async_copy(zbuf, xpad_hbm.at[pl.ds(pe_ref[e] - MOE_BM, MOE_BM)], zsem)

        def nonempty(e):
            return pe_ref[e] > (pe_ref[e - 1] if e > 0 else 0)

        for e in range(N_EXPERTS):
            pl.when(nonempty(e))(lambda e=e: tail_copy(e).start())
        for e in range(N_EXPERTS):
            pl.when(nonempty(e))(lambda e=e: tail_copy(e).wait())

        def fill(b, carry):
            cp = pltpu.make_async_copy(zbuf, xpad_hbm.at[pl.ds(b * MOE_BM, MOE_BM)], zsem)
            cp.start()
            cp.wait()
            return carry

        lax.fori_loop(pe_ref[N_EXPERTS - 1] // MOE_BM, xpad_hbm.shape[0] // MOE_BM, fill, 0)

    base = i * tm

    def body(r, carry):
        src = h2_hbm.at[base + r]
        pltpu.make_async_copy(src, xpad_hbm.at[pos_ref[2 * r]], sem).start()
        pltpu.make_async_copy(src, xpad_hbm.at[pos_ref[2 * r + 1]], sem).start()
        return carry

    lax.fori_loop(0, tm, body, 0, unroll=8)
    pltpu.make_async_copy(xpad_hbm.at[pl.ds(0, 2 * tm)], xpad_hbm.at[pl.ds(0, 2 * tm)], sem).wait()


def moe_dispatch(pad_ends, pos, h2, n_rows):
    T = h2.shape[0]
    tm = min(T, DISPATCH_TM)
    grid_spec = pltpu.PrefetchScalarGridSpec(
        num_scalar_prefetch=1,
        grid=(T // tm,),
        in_specs=[pl.BlockSpec((2 * tm,), lambda i, pe: (i,), memory_space=pltpu.SMEM),
                  pl.BlockSpec(memory_space=pl.ANY)],
        out_specs=pl.BlockSpec(memory_space=pl.ANY),
        scratch_shapes=[pltpu.VMEM((MOE_BM, SLABS, LANES), F32), pltpu.SemaphoreType.DMA(()),
                        pltpu.SemaphoreType.DMA(())],
    )
    return pl.pallas_call(
        functools.partial(_dispatch_kernel, tm),
        grid_spec=grid_spec,
        out_shape=jax.ShapeDtypeStruct((n_rows, SLABS, LANES), F32),
        compiler_params=_cparams(("arbitrary",)),
        name="moe_dispatch",
    )(pad_ends, pos, h2)


def _moe_kernel(be_ref, nu_ref, x_ref, wg_ref, wu_ref, wd_ref, y_ref, wg_s, wu_s, wd_s):
    i = pl.program_id(0)
    used = i < nu_ref[0]
    new_expert = jnp.logical_or(i == 0, be_ref[i] != be_ref[jnp.maximum(i - 1, 0)])

    @pl.when(jnp.logical_and(used, new_expert))
    def _():
        wg_s[...] = wg_ref[0, 0].astype(BF16)
        wu_s[...] = wu_ref[0, 0].astype(BF16)
        wd_s[...] = wd_ref[0, 0].astype(BF16)

    @pl.when(used)
    def _():
        x = jnp.concatenate([c.astype(BF16) for c in _load_slabs(x_ref, 0, MOE_BM)], axis=1)
        g = _dot(x, wg_s[...])
        u = _dot(x, wu_s[...])
        a = (g * _sigmoid(g) * u).astype(BF16)
        _store_slabs(y_ref, _dot(a, wd_s[...]))

    @pl.when(jnp.logical_not(used))
    def _():
        y_ref[...] = jnp.zeros_like(y_ref)


def moe_experts(x_pad, block_e, n_used, layer, wg, wu, wd):
    n_rows = x_pad.shape[0]
    rows = pl.BlockSpec((MOE_BM, SLABS, LANES), lambda i, be, nu: (i, 0, 0))
    w_in = pl.BlockSpec((1, 1, D_MODEL, D_FF_EXPERT), lambda i, be, nu: (layer, be[i], 0, 0))
    w_dn = pl.BlockSpec((1, 1, D_FF_EXPERT, D_MODEL), lambda i, be, nu: (layer, be[i], 0, 0))
    grid_spec = pltpu.PrefetchScalarGridSpec(
        num_scalar_prefetch=2,
        grid=(n_rows // MOE_BM,),
        in_specs=[rows, w_in, w_in, w_dn],
        out_specs=rows,
        scratch_shapes=[pltpu.VMEM((D_MODEL, D_FF_EXPERT), BF16), pltpu.VMEM((D_MODEL, D_FF_EXPERT), BF16),
                        pltpu.VMEM((D_FF_EXPERT, D_MODEL), BF16)],
    )
    return pl.pallas_call(
        _moe_kernel,
        grid_spec=grid_spec,
        out_shape=jax.ShapeDtypeStruct((n_rows, SLABS, LANES), F32),
        compiler_params=_cparams(("arbitrary",)),
        name="moe_experts",
    )(block_e, n_used, x_pad, wg, wu, wd)


def _combine_kernel(tm, pos_ref, route_ref, x1_ref, ypad_hbm, o_ref, ybuf, sem):
    def body(r, carry):
        pltpu.make_async_copy(ypad_hbm.at[pos_ref[2 * r]], ybuf.at[r], sem).start()
        pltpu.make_async_copy(ypad_hbm.at[pos_ref[2 * r + 1]], ybuf.at[tm + r], sem).start()
        return carry

    lax.fori_loop(0, tm, body, 0, unroll=8)
    pltpu.make_async_copy(ypad_hbm.at[pl.ds(0, 2 * tm)], ybuf, sem).wait()
    g1 = route_ref[:, 2:3]
    g2 = route_ref[:, 3:4]
    y1 = _load_slabs(ybuf, 0, tm)
    y2 = _load_slabs(ybuf, tm, tm)
    for j in range(SLABS):
        sl = slice(j * LANES, (j + 1) * LANES)
        o_ref[:, sl] = x1_ref[:, sl] + (g1 * y1[j] + g2 * y2[j])


def moe_combine(pos, route, x1, y_pad):
    T = x1.shape[0]
    tm = min(T, COMBINE_TM)
    return pl.pallas_call(
        functools.partial(_combine_kernel, tm),
        grid=(T // tm,),
        in_specs=[pl.BlockSpec((2 * tm,), lambda i: (i,), memory_space=pltpu.SMEM),
                  pl.BlockSpec((tm, LANES), lambda i: (i, 0)),
                  pl.BlockSpec((tm, D_MODEL), lambda i: (i, 0)),
                  pl.BlockSpec(memory_space=pl.ANY)],
        out_specs=pl.BlockSpec((tm, D_MODEL), lambda i: (i, 0)),
        out_shape=jax.ShapeDtypeStruct((T, D_MODEL), F32),
        scratch_shapes=[pltpu.VMEM((2 * tm, SLABS, LANES), F32), pltpu.SemaphoreType.DMA(())],
        compiler_params=_cparams(("arbitrary",)),
        name="moe_combine",
    )(pos, route, x1, y_pad)


def hier_moe(x1, h2, route, counts, layer, wg, wu, wd):
    T = x1.shape[0]
    n_assign = 2 * T
    n_blocks = (n_assign + N_EXPERTS * (MOE_BM - 1) + MOE_BM - 1) // MOE_BM
    experts = route[:, 0:2].astype(jnp.int32)
    ranks = route[:, 4:6].astype(jnp.int32)
    cnt = counts[0, :N_EXPERTS].astype(jnp.int32)
    padded = ((cnt + MOE_BM - 1) // MOE_BM) * MOE_BM
    pad_ends = jnp.cumsum(padded).astype(jnp.int32)
    pad_starts = pad_ends - padded
    is_e = experts[:, :, None] == jnp.arange(N_EXPERTS, dtype=jnp.int32)
    pos = (ranks + jnp.sum(jnp.where(is_e, pad_starts, 0), axis=-1)).reshape(-1)
    block_start = jnp.arange(n_blocks, dtype=jnp.int32) * MOE_BM
    block_e = jnp.minimum(jnp.sum(block_start[:, None] >= pad_ends[None, :], axis=1), N_EXPERTS - 1).astype(jnp.int32)
    n_used = (pad_ends[-1:] // MOE_BM).astype(jnp.int32)
    x_pad = moe_dispatch(pad_ends, pos, h2, n_blocks * MOE_BM)
    y_pad = moe_experts(x_pad, block_e, n_used, layer, wg, wu, wd)
    return moe_combine(pos, route, x1, y_pad)


def _rope_tables(pos):
    half = D_HEAD // 2
    inv = ROPE_BASE ** (-jnp.arange(half, dtype=F32) / half)
    ang = pos.astype(F32)[:, None] * inv[None, :]
    cos, sin = jnp.cos(ang), jnp.sin(ang)
    return jnp.concatenate([cos, cos], axis=-1), jnp.concatenate([-sin, sin], axis=-1)


def _router_weights(w_group, b_group, w_router, b_router):
    w = jnp.zeros((D_MODEL, LANES), F32)
    w = w.at[:, :N_EXPERTS].set(w_router).at[:, ROUTER_GROUP_LANE0:ROUTER_GROUP_LANE0 + N_GROUPS].set(w_group)
    b = jnp.zeros((1, LANES), F32)
    b = b.at[0, :N_EXPERTS].set(b_router).at[0, ROUTER_GROUP_LANE0:ROUTER_GROUP_LANE0 + N_GROUPS].set(b_group)
    return w.astype(BF16), b


def _tile(v, n):
    return jnp.tile(v.astype(F32), n)


def _run_trunk(x, B, L, mem_k, mem_v, ret_state0, band_past, P):
    prompt = band_past is None
    tm = 512 if prompt else L
    proj_tm = 512
    scale = D_HEAD ** -0.5
    ones = lambda n: jnp.ones((n * D_HEAD,), F32)
    if prompt:
        cos2, sin2 = _rope_tables(jnp.arange(L, dtype=jnp.int32))
        tab_map = lambda i: i % (L // proj_tm)
    else:
        cos2, sin2 = _rope_tables(PAST_LEN + jnp.arange(L, dtype=jnp.int32))
        cos2, sin2 = jnp.tile(cos2, (B, 1)), jnp.tile(sin2, (B, 1))
        tab_map = lambda i: i
    out = {}

    def ffn(layer, mixed, proj, qm_block, w_out, x):
        x1, h2, route, counts = out_proj(
            mixed, proj, qm_block, mem_k, mem_v, layer, w_out.astype(BF16), x, P["norm_ffn"][layer],
            *_router_weights(P["w_group"][layer], P["b_group"][layer], P["w_router"][layer], P["b_router"][layer]),
            tm=tm, rows_per_stream=L)
        return hier_moe(x1, h2, route, counts, layer, P["w_e_gate"], P["w_e_up"], P["w_e_down"])

    segs0 = (("rope", N_HEADS), ("rope", N_HEADS), ("plain", N_HEADS), ("silu", N_HEADS), ("norm", H_MEM))
    gain0 = jnp.concatenate([ones(N_HEADS), ones(N_HEADS) * scale, ones(2 * N_HEADS),
                             _tile(P["q_norm_mem"][0], H_MEM) * scale])
    proj = norm_proj(x, P["norm_mix"][0], P["w_in_ret"][0].astype(BF16), gain0, segs0, tm=proj_tm,
                     out_dtype=BF16, rope=(cos2, sin2, tab_map))
    mixed, S = retention(proj, ret_state0, P["gn_ret"][0], B=B, L=L, C=RET_CHUNK if prompt else L)
    out["ret_state"] = S
    x = ffn(0, mixed, proj, (4 * MIX_W) // MEM_W, P["w_out_ret"][0], x)

    segs1 = (("norm", N_HEADS), ("norm", N_HEADS), ("plain", N_HEADS), ("norm", H_MEM))
    gain1 = jnp.concatenate([_tile(P["q_norm_att"][0], N_HEADS) * scale, _tile(P["k_norm_att"][0], N_HEADS),
                             ones(N_HEADS), _tile(P["q_norm_mem"][1], H_MEM) * scale])
    w_in = P["w_in_att"][0].astype(BF16)
    if prompt:
        proj = norm_proj(x, P["norm_mix"][1], w_in, gain1, segs1, tm=proj_tm, out_dtype=BF16)
        keep = min(BAND_PAST, L)
        tiles = L // keep
        kv = norm_proj(x, P["norm_mix"][1], w_in[:, MIX_W:3 * MIX_W], gain1[MIX_W:3 * MIX_W], segs1[1:3],
                       tm=keep, out_dtype=F32, n_tiles=B, row_map=lambda i: i * tiles + tiles - 1)
        out["band_k"] = kv[:, :MIX_W].reshape(B, keep, N_HEADS, D_HEAD)
        out["band_v"] = kv[:, MIX_W:].reshape(B, keep, N_HEADS, D_HEAD)
        mixed = band_prompt(proj, P["rel_bias_att"][0], B=B, L=L)
    else:
        proj_f = norm_proj(x, P["norm_mix"][1], w_in, gain1, segs1, tm=proj_tm, out_dtype=F32)
        out["band_k"] = proj_f[:, MIX_W:2 * MIX_W].reshape(B, L, N_HEADS, D_HEAD)
        out["band_v"] = proj_f[:, 2 * MIX_W:3 * MIX_W].reshape(B, L, N_HEADS, D_HEAD)
        proj = proj_f.astype(BF16)
        mixed = band_sample(proj, band_past[0], band_past[1], P["rel_bias_att"][0], B=B, L=L)
    out["y"] = ffn(1, mixed, proj, (3 * MIX_W) // MEM_W, P["w_out_att"][0], x)
    return out


def kernel(x_prompt, x_sample, mem_prompt, state_ret, cache_band_k, cache_band_v, cache_mem_k, cache_mem_v, norm_mix, norm_ffn, norm_mem, w_in_ret, gn_ret, w_out_ret, w_in_att, q_norm_att, k_norm_att, rel_bias_att, w_out_att, w_mem_kv, q_norm_mem, k_norm_mem, w_group, b_group, w_router, b_router, w_e_gate, w_e_up, w_e_down):
    P = {"norm_mix": norm_mix, "norm_ffn": norm_ffn, "w_in_ret": w_in_ret, "gn_ret": gn_ret,
         "w_out_ret": w_out_ret, "w_in_att": w_in_att, "q_norm_att": q_norm_att,
         "k_norm_att": k_norm_att, "rel_bias_att": rel_bias_att, "w_out_att": w_out_att,
         "q_norm_mem": q_norm_mem, "w_group": w_group, "b_group": b_group, "w_router": w_router,
         "b_router": b_router, "w_e_gate": w_e_gate, "w_e_up": w_e_up, "w_e_down": w_e_down}
    B, L, D = x_prompt.shape
    Bd, Ld, _ = x_sample.shape
    depth = norm_mix.shape[0]

    mem_k_p, mem_v_p = [], []
    for i in range(depth):
        gain = jnp.concatenate([_tile(k_norm_mem[i], H_MEM), jnp.ones((MEM_W,), F32)])
        kv = norm_proj(mem_prompt.reshape(B * N_MEM, D), norm_mem[i], w_mem_kv[i].astype(BF16), gain,
                       (("norm", H_MEM), ("plain", H_MEM)), tm=N_MEM, out_dtype=F32)
        mem_k_p.append(kv[:, :MEM_W].reshape(B, N_MEM, H_MEM, D_HEAD))
        mem_v_p.append(kv[:, MEM_W:].reshape(B, N_MEM, H_MEM, D_HEAD))
    mem_k_p = jnp.stack(mem_k_p)
    mem_v_p = jnp.stack(mem_v_p)

    zeros_state = jnp.zeros((B, N_HEADS, D_HEAD, D_HEAD), F32)
    res_p = _run_trunk(x_prompt.reshape(B * L, D), B, L, mem_k_p, mem_v_p, zeros_state, None, P)
    res_s = _run_trunk(x_sample.reshape(Bd * Ld, D), Bd, Ld, cache_mem_k, cache_mem_v, state_ret[0],
                       (cache_band_k[0], cache_band_v[0]), P)

    return (res_p["y"].reshape(B, L, D), res_s["y"].reshape(Bd, Ld, D),
            res_p["ret_state"][None], res_s["ret_state"][None],
            res_p["band_k"][None], res_p["band_v"][None], res_s["band_k"][None], res_s["band_v"][None],
            mem_k_p, mem_v_p)
```

```python
import functools

import jax
import jax.numpy as jnp
from jax import lax
from jax.experimental import pallas as pl
from jax.experimental.pallas import tpu as pltpu

F32 = jnp.float32
BF16 = jnp.bfloat16

D_MODEL = 1024
D_HEAD = 128
CHUNK = 64
N_HEADS = 8
H_MEM = 4
N_MEM = 256
MEM_W = H_MEM * D_HEAD
MIX_W = N_HEADS * D_HEAD
N_PREV_CHUNKS = 8
BAND_PAST = N_PREV_CHUNKS * CHUNK
REL_CLIP = 128
N_GROUPS = 4
EXPERTS_PER_GROUP = 8
N_EXPERTS = N_GROUPS * EXPERTS_PER_GROUP
D_FF_EXPERT = 512
ROPE_BASE = 10000.0
EPS = 1e-6
NEG_INF = -1e30
PAST_LEN = 1024

LANES = 128
SUBLANES = 8

RET_CHUNK = 256
BAND_TQ = 512
BAND_SUB = 128
BAND_WIN = BAND_PAST + BAND_SUB
MOE_BM = 256
DISPATCH_TM = 1024
COMBINE_TM = 1024
ROUTER_GROUP_LANE0 = N_EXPERTS
VMEM_LIMIT = 56 * 1024 * 1024


def _cparams(sem):
    return pltpu.CompilerParams(dimension_semantics=sem, vmem_limit_bytes=VMEM_LIMIT)


def _dot(a, b):
    return jnp.dot(a, b, preferred_element_type=F32)


def _dot_nt(a, b):
    return lax.dot_general(a, b, (((1,), (1,)), ((), ())), preferred_element_type=F32)


def _sigmoid(x):
    return 1.0 / (1.0 + jnp.exp(-x))


ROW_TILE = D_MODEL // LANES
assert ROW_TILE == SUBLANES


def _load_token_tiles(ref, t0, n):
    return jnp.concatenate([ref[pl.ds(t0 * ROW_TILE + j, n, stride=ROW_TILE), :] for j in range(ROW_TILE)], axis=1)


def _store_token_tiles(ref, t0, val):
    n = val.shape[0]
    for j in range(ROW_TILE):
        ref[pl.ds(t0 * ROW_TILE + j, n, stride=ROW_TILE), :] = val[:, j * LANES:(j + 1) * LANES]


def _token_tile(ref, t):
    return ref.at[pl.ds(pl.multiple_of(t * ROW_TILE, ROW_TILE), ROW_TILE)]


def _norm_proj_kernel(segs, has_rope, x_ref, g_ref, w_ref, gain_ref, *rest):
    if has_rope:
        cos_ref, sin_ref, o_ref = rest
    else:
        (o_ref,) = rest
    x = x_ref[...]
    ms = jnp.mean(x * x, axis=-1, keepdims=True)
    h = (x * lax.rsqrt(ms + EPS) * g_ref[...]).astype(BF16)
    col = 0
    for kind, n_heads in segs:
        width = n_heads * D_HEAD
        acc = _dot(h, w_ref[:, col:col + width])
        for j in range(n_heads):
            a = acc[:, j * D_HEAD:(j + 1) * D_HEAD]
            c0 = col + j * D_HEAD
            gain = gain_ref[:, c0:c0 + D_HEAD]
            if kind == "rope":
                rot = pltpu.roll(a, D_HEAD // 2, 1)
                out = (a * cos_ref[...] + rot * sin_ref[...]) * gain
            elif kind == "norm":
                out = a * lax.rsqrt(jnp.mean(a * a, axis=-1, keepdims=True) + EPS) * gain
            elif kind == "silu":
                out = a * _sigmoid(a)
            else:
                out = a
            o_ref[:, c0:c0 + D_HEAD] = out.astype(o_ref.dtype)
        col += width


def norm_proj(x, g, w, gain, segs, *, tm, out_dtype, n_tiles=None, row_map=None, rope=None):
    T, D = x.shape
    N = w.shape[1]
    assert sum(n for _, n in segs) * D_HEAD == N
    if n_tiles is None:
        n_tiles = T // tm
    if row_map is None:
        row_map = lambda i: i
    in_specs = [
        pl.BlockSpec((tm, D), lambda i: (row_map(i), 0)),
        pl.BlockSpec((1, D), lambda i: (0, 0)),
        pl.BlockSpec((D, N), lambda i: (0, 0)),
        pl.BlockSpec((1, N), lambda i: (0, 0)),
    ]
    args = [x, g.reshape(1, D), w, gain.reshape(1, N)]
    if rope is not None:
        cos2, sin2, tab_map = rope
        in_specs += [pl.BlockSpec((tm, D_HEAD), lambda i: (tab_map(i), 0)),
                     pl.BlockSpec((tm, D_HEAD), lambda i: (tab_map(i), 0))]
        args += [cos2, sin2]
    return pl.pallas_call(
        functools.partial(_norm_proj_kernel, segs, rope is not None),
        grid=(n_tiles,),
        in_specs=in_specs,
        out_specs=pl.BlockSpec((tm, N), lambda i: (i, 0)),
        out_shape=jax.ShapeDtypeStruct((n_tiles * tm, N), out_dtype),
        compiler_params=_cparams(("arbitrary",)),
        name="norm_proj",
    )(*args)


def _retention_kernel(q_ref, k_ref, v_ref, g_ref, s0_ref, dmask_ref, qdec_ref, kdec_ref, gc_ref,
                      gn_ref, o_ref, s_ref):
    @pl.when(pl.program_id(1) == 0)
    def _():
        s_ref[...] = s0_ref[...]

    for h in range(N_HEADS):
        sl = slice(h * D_HEAD, (h + 1) * D_HEAD)
        q = q_ref[:, sl]
        k = k_ref[:, sl]
        v = v_ref[:, sl]
        S = s_ref[0, h]
        scores = _dot_nt(q, k) * dmask_ref[h]
        intra = _dot(scores.astype(BF16), v)
        cross = _dot(q, S.astype(BF16)) * qdec_ref[:, sl]
        o = intra + cross
        kd_t = (k.astype(F32) * kdec_ref[:, sl]).T.astype(BF16)
        s_ref[0, h] = gc_ref[h] * S + _dot(kd_t, v)
        mu = jnp.mean(o, axis=-1, keepdims=True)
        d = o - mu
        var = jnp.mean(d * d, axis=-1, keepdims=True)
        y = d * lax.rsqrt(var + EPS) * gn_ref[:, sl] * g_ref[:, sl].astype(F32)
        o_ref[:, sl] = y.astype(o_ref.dtype)


def _retention_tables(C):
    log_gamma = jnp.log1p(-jnp.exp2(-5.0 - jnp.arange(N_HEADS, dtype=F32)))
    idx = jnp.arange(C, dtype=F32)
    diff = idx[:, None] - idx[None, :]
    dmask = jnp.where(diff[None] >= 0,
                      jnp.exp(log_gamma[:, None, None] * jnp.maximum(diff, 0.0)[None]), 0.0)
    q_dec = jnp.exp(log_gamma[None, :] * (idx + 1.0)[:, None])
    k_dec = jnp.exp(log_gamma[None, :] * (C - 1.0 - idx)[:, None])
    g_c = jnp.exp(log_gamma * C)
    rep = lambda t: jnp.repeat(t, D_HEAD, axis=-1)
    return dmask, rep(q_dec), rep(k_dec), rep(g_c[:, None])[:, None, :]


def retention(proj, s0, gn, *, B, L, C):
    nc = L // C
    dmask, qdec, kdec, gc = _retention_tables(C)
    row = lambda col: pl.BlockSpec((C, MIX_W), lambda b, c: (b * nc + c, col))
    const = lambda shape: pl.BlockSpec(shape, lambda b, c: (0,) * len(shape))
    state_spec = pl.BlockSpec((1, N_HEADS, D_HEAD, D_HEAD), lambda b, c: (b, 0, 0, 0))
    return pl.pallas_call(
        _retention_kernel,
        grid=(B, nc),
        in_specs=[row(0), row(1), row(2), row(3), state_spec,
                  const((N_HEADS, C, C)), const((C, MIX_W)), const((C, MIX_W)),
                  const((N_HEADS, 1, D_HEAD)), const((1, MIX_W))],
        out_specs=[pl.BlockSpec((C, MIX_W), lambda b, c: (b * nc + c, 0)), state_spec],
        out_shape=[jax.ShapeDtypeStruct((B * L, MIX_W), BF16),
                   jax.ShapeDtypeStruct((B, N_HEADS, D_HEAD, D_HEAD), F32)],
        compiler_params=_cparams(("arbitrary", "arbitrary")),
        name="retention",
    )(proj, proj, proj, proj, s0, dmask, qdec, kdec, gc, gn.reshape(1, MIX_W))


def _attend_two_piece(q, k_a, k_b, v_a, v_b, bias_a, bias_b, a_valid):
    s_a = _dot_nt(q, k_a) + bias_a
    if a_valid is not None:
        s_a = jnp.where(a_valid, s_a, NEG_INF)
    s_b = _dot_nt(q, k_b) + bias_b
    m = jnp.maximum(jnp.max(s_a, axis=-1, keepdims=True), jnp.max(s_b, axis=-1, keepdims=True))
    p_a = jnp.exp(s_a - m)
    p_b = jnp.exp(s_b - m)
    denom = jnp.sum(p_a, axis=-1, keepdims=True) + jnp.sum(p_b, axis=-1, keepdims=True)
    o = _dot(p_a.astype(BF16), v_a) + _dot(p_b.astype(BF16), v_b)
    return o / denom


def _band_prompt_kernel(q_ref, kp_ref, kc_ref, vp_ref, vc_ref, bias_ref, o_ref):
    has_prev = pl.program_id(1) > 0
    for h in range(N_HEADS):
        sl = slice(h * D_HEAD, (h + 1) * D_HEAD)
        for s in range(BAND_TQ // BAND_SUB):
            r0 = s * BAND_SUB
            n_prev = BAND_TQ - r0
            n_cur = r0 + BAND_SUB
            o = _attend_two_piece(
                q_ref[r0:r0 + BAND_SUB, sl],
                kp_ref[r0:, sl], kc_ref[:n_cur, sl], vp_ref[r0:, sl], vc_ref[:n_cur, sl],
                bias_ref[h, :, :n_prev], bias_ref[h, :, n_prev:], has_prev)
            o_ref[r0:r0 + BAND_SUB, sl] = o.astype(o_ref.dtype)


def _band_prompt_bias(bias_table):
    r = jnp.arange(BAND_SUB, dtype=jnp.int32)[:, None]
    j = jnp.arange(BAND_WIN, dtype=jnp.int32)[None, :]
    q_chunk = r // CHUNK + N_PREV_CHUNKS
    k_chunk = j // CHUNK
    allowed = (k_chunk <= q_chunk) & (k_chunk >= q_chunk - N_PREV_CHUNKS)
    n_diag = BAND_SUB + BAND_WIN - 1
    d = (BAND_PAST + BAND_SUB - 1) - jnp.arange(n_diag, dtype=jnp.int32)
    diag = bias_table.astype(F32)[:, jnp.clip(d, -REL_CLIP, REL_CLIP) + REL_CLIP]
    starts = (BAND_SUB - 1) - jnp.arange(BAND_SUB, dtype=jnp.int32)
    rows = jax.vmap(lambda s: lax.dynamic_slice_in_dim(diag, s, BAND_WIN, axis=1))(starts)
    return jnp.where(allowed[None], jnp.swapaxes(rows, 0, 1), NEG_INF)


def band_prompt(proj, bias_table, *, B, L):
    nq = L // BAND_TQ
    cur = lambda col: pl.BlockSpec((BAND_TQ, MIX_W), lambda b, i: (b * nq + i, col))
    prev = lambda col: pl.BlockSpec((BAND_TQ, MIX_W), lambda b, i: (b * nq + jnp.maximum(i - 1, 0), col))
    return pl.pallas_call(
        _band_prompt_kernel,
        grid=(B, nq),
        in_specs=[cur(0), prev(1), cur(1), prev(2), cur(2),
                  pl.BlockSpec((N_HEADS, BAND_SUB, BAND_WIN), lambda b, i: (0, 0, 0))],
        out_specs=pl.BlockSpec((BAND_TQ, MIX_W), lambda b, i: (b * nq + i, 0)),
        out_shape=jax.ShapeDtypeStruct((B * L, MIX_W), BF16),
        compiler_params=_cparams(("arbitrary", "arbitrary")),
        name="band_prompt",
    )(proj, proj, proj, proj, proj, _band_prompt_bias(bias_table))


def _band_sample_kernel(q_ref, kp_ref, kc_ref, vp_ref, vc_ref, bias_p_ref, bias_c_ref, o_ref):
    for h in range(N_HEADS):
        sl = slice(h * D_HEAD, (h + 1) * D_HEAD)
        o = _attend_two_piece(q_ref[:, sl], kp_ref[0, :, h, :].astype(BF16), kc_ref[:, sl],
                              vp_ref[0, :, h, :].astype(BF16), vc_ref[:, sl], bias_p_ref[h], bias_c_ref[h], None)
        o_ref[:, sl] = o.astype(o_ref.dtype)


def band_sample(proj, past_k, past_v, bias_table, *, B, L):
    P = past_k.shape[1]
    qpos = PAST_LEN + jnp.arange(L, dtype=jnp.int32)
    kpos = jnp.concatenate([PAST_LEN - P + jnp.arange(P, dtype=jnp.int32), qpos])
    rel = jnp.clip(qpos[:, None] - kpos[None, :], -REL_CLIP, REL_CLIP) + REL_CLIP
    qc = (qpos // CHUNK)[:, None]
    kc = (kpos // CHUNK)[None, :]
    allowed = (kpos[None, :] >= 0) & (kc <= qc) & (kc >= qc - N_PREV_CHUNKS)
    bias = jnp.where(allowed[None], bias_table.astype(F32)[:, rel], NEG_INF)
    cur = lambda col: pl.BlockSpec((L, MIX_W), lambda b: (b, col))
    past = pl.BlockSpec((1, P, N_HEADS, D_HEAD), lambda b: (b, 0, 0, 0))
    return pl.pallas_call(
        _band_sample_kernel,
        grid=(B,),
        in_specs=[cur(0), past, cur(1), past, cur(2),
                  pl.BlockSpec((N_HEADS, L, P), lambda b: (0, 0, 0)),
                  pl.BlockSpec((N_HEADS, L, L), lambda b: (0, 0, 0))],
        out_specs=pl.BlockSpec((L, MIX_W), lambda b: (b, 0)),
        out_shape=jax.ShapeDtypeStruct((B * L, MIX_W), BF16),
        compiler_params=_cparams(("arbitrary",)),
        name="band_sample",
    )(proj, past_k, proj, past_v, proj, bias[:, :, :P], bias[:, :, P:])


def _out_proj_kernel(o_ref, qm_ref, mk_ref, mv_ref, w_ref, x_ref, gf_ref, wr_ref, br_ref, tri_ref,
                     x1_ref, h2_ref, route_ref, cnt_ref):
    @pl.when(pl.program_id(0) == 0)
    def _():
        cnt_ref[...] = jnp.zeros_like(cnt_ref)

    parts = [o_ref[...]]
    for h in range(H_MEM):
        sl = slice(h * D_HEAD, (h + 1) * D_HEAD)
        s = _dot_nt(qm_ref[:, sl], mk_ref[0, 0, :, h, :].astype(BF16))
        p = jnp.exp(s - jnp.max(s, axis=-1, keepdims=True))
        om = _dot(p.astype(BF16), mv_ref[0, 0, :, h, :].astype(BF16)) / jnp.sum(p, axis=-1, keepdims=True)
        parts.append(om.astype(BF16))
    x1 = x_ref[...] + _dot(jnp.concatenate(parts, axis=1), w_ref[...])
    x1_ref[...] = x1
    hn = x1 * lax.rsqrt(jnp.mean(x1 * x1, axis=-1, keepdims=True) + EPS) * gf_ref[...]
    _store_token_tiles(h2_ref, 0, hn)

    logits = _dot(hn.astype(BF16), wr_ref[...]) + br_ref[...]
    lane = lax.broadcasted_iota(jnp.int32, logits.shape, 1)
    lane_f = lane.astype(F32)
    far = float(LANES)
    is_group = (lane >= ROUTER_GROUP_LANE0) & (lane < ROUTER_GROUP_LANE0 + N_GROUPS)
    gl = jnp.where(is_group, logits, NEG_INF)
    g_max = jnp.max(gl, axis=-1, keepdims=True)
    g_sel = jnp.min(jnp.where(gl == g_max, lane_f, far), axis=-1, keepdims=True) - float(ROUTER_GROUP_LANE0)
    g_gate = 1.0 / jnp.sum(jnp.exp(gl - g_max), axis=-1, keepdims=True)
    in_group = jnp.right_shift(lane, 3).astype(F32) == g_sel
    el = jnp.where(in_group, logits, NEG_INF)
    m1 = jnp.max(el, axis=-1, keepdims=True)
    i1 = jnp.min(jnp.where(el == m1, lane_f, far), axis=-1, keepdims=True)
    el2 = jnp.where(lane_f == i1, NEG_INF, el)
    m2 = jnp.max(el2, axis=-1, keepdims=True)
    i2 = jnp.min(jnp.where(el2 == m2, lane_f, far), axis=-1, keepdims=True)
    e2 = jnp.exp(m2 - m1)
    p1 = 1.0 / (1.0 + e2)

    oh1 = lane_f == i1
    oh2 = lane_f == i2
    onehot = jnp.where(oh1 | oh2, 1.0, 0.0)
    before = cnt_ref[0:1, :] + _dot(tri_ref[...], onehot.astype(BF16))
    r1 = jnp.sum(jnp.where(oh1, before, 0.0), axis=-1, keepdims=True)
    r2 = jnp.sum(jnp.where(oh2, before, 0.0), axis=-1, keepdims=True)
    cnt_ref[0:1, :] = cnt_ref[0:1, :] + jnp.sum(onehot, axis=0, keepdims=True)

    route = jnp.where(lane == 0, i1, jnp.where(lane == 1, i2, jnp.where(
        lane == 2, g_gate * p1, jnp.where(lane == 3, g_gate * (e2 * p1), jnp.where(
            lane == 4, r1, jnp.where(lane == 5, r2, 0.0))))))
    route_ref[...] = route


def out_proj(mixed, proj, qm_block, mem_k, mem_v, layer, w_out, x, g_ffn, w_router, b_router, *, tm,
             rows_per_stream):
    T = x.shape[0]
    tiles_per_stream = rows_per_stream // tm
    row = lambda width, col: pl.BlockSpec((tm, width), lambda i: (i, col))
    const = lambda shape: pl.BlockSpec(shape, lambda i: (0,) * len(shape))
    mem = pl.BlockSpec((1, 1, N_MEM, H_MEM, D_HEAD), lambda i: (layer, i // tiles_per_stream, 0, 0, 0))
    tri = (jnp.arange(tm)[None, :] < jnp.arange(tm)[:, None]).astype(BF16)
    return pl.pallas_call(
        _out_proj_kernel,
        grid=(T // tm,),
        in_specs=[row(MIX_W, 0), row(MEM_W, qm_block), mem, mem, const((MIX_W + MEM_W, D_MODEL)),
                  row(D_MODEL, 0), const((1, D_MODEL)), const((D_MODEL, LANES)), const((1, LANES)),
                  const((tm, tm))],
        out_specs=[row(D_MODEL, 0), pl.BlockSpec((tm * ROW_TILE, LANES), lambda i: (i, 0)), row(LANES, 0),
                   const((SUBLANES, LANES))],
        out_shape=[jax.ShapeDtypeStruct((T, D_MODEL), F32), jax.ShapeDtypeStruct((T * ROW_TILE, LANES), F32),
                   jax.ShapeDtypeStruct((T, LANES), F32), jax.ShapeDtypeStruct((SUBLANES, LANES), F32)],
        compiler_params=_cparams(("arbitrary",)),
        name="out_proj",
    )(mixed, proj, mem_k, mem_v, w_out, x, g_ffn.reshape(1, D_MODEL), w_router, b_router, tri)


def _dispatch_kernel(tm, pe_ref, pos_ref, h2_ref, xpad_hbm, zbuf, sem, zsem):
    i = pl.program_id(0)

    @pl.when(i == 0)
    def _():
        zbuf[...] = jnp.zeros_like(zbuf)

        def block_copy(first_row):
            start = pl.multiple_of(first_row * ROW_TILE, MOE_BM * ROW_TILE)
            return pltpu.make_async_copy(zbuf, xpad_hbm.at[pl.ds(start, MOE_BM * ROW_TILE)], zsem)

        def tail_copy(e):
            return block_copy(pe_ref[e] - MOE_BM)

        def nonempty(e):
            return pe_ref[e] > (pe_ref[e - 1] if e > 0 else 0)

        for e in range(N_EXPERTS):
            pl.when(nonempty(e))(lambda e=e: tail_copy(e).start())
        for e in range(N_EXPERTS):
            pl.when(nonempty(e))(lambda e=e: tail_copy(e).wait())

        def fill(b, carry):
            cp = block_copy(b * MOE_BM)
            cp.start()
            cp.wait()
            return carry

        lax.fori_loop(pe_ref[N_EXPERTS - 1] // MOE_BM, xpad_hbm.shape[0] // (MOE_BM * ROW_TILE), fill, 0)

    def body(r, carry):
        src = _token_tile(h2_ref, r)
        pltpu.make_async_copy(src, _token_tile(xpad_hbm, pos_ref[2 * r]), sem).start()
        pltpu.make_async_copy(src, _token_tile(xpad_hbm, pos_ref[2 * r + 1]), sem).start()
        return carry

    lax.fori_loop(0, tm, body, 0, unroll=8)
    for _ in range(2):
        pltpu.make_async_copy(h2_ref, xpad_hbm.at[pl.ds(0, tm * ROW_TILE)], sem).wait()


def moe_dispatch(pad_ends, pos, h2, n_rows):
    T = h2.shape[0] // ROW_TILE
    tm = min(T, DISPATCH_TM)
    grid_spec = pltpu.PrefetchScalarGridSpec(
        num_scalar_prefetch=1,
        grid=(T // tm,),
        in_specs=[pl.BlockSpec((2 * tm,), lambda i, pe: (i,), memory_space=pltpu.SMEM),
                  pl.BlockSpec((tm * ROW_TILE, LANES), lambda i, pe: (i, 0))],
        out_specs=pl.BlockSpec(memory_space=pl.ANY),
        scratch_shapes=[pltpu.VMEM((MOE_BM * ROW_TILE, LANES), F32), pltpu.SemaphoreType.DMA(()),
                        pltpu.SemaphoreType.DMA(())],
    )
    return pl.pallas_call(
        functools.partial(_dispatch_kernel, tm),
        grid_spec=grid_spec,
        out_shape=jax.ShapeDtypeStruct((n_rows * ROW_TILE, LANES), F32),
        compiler_params=_cparams(("arbitrary",)),
        name="moe_dispatch",
    )(pad_ends, pos, h2)


def _moe_kernel(be_ref, nu_ref, x_ref, wg_ref, wu_ref, wd_ref, y_ref, wg_s, wu_s, wd_s):
    i = pl.program_id(0)
    used = i < nu_ref[0]
    new_expert = jnp.logical_or(i == 0, be_ref[i] != be_ref[jnp.maximum(i - 1, 0)])

    @pl.when(jnp.logical_and(used, new_expert))
    def _():
        wg_s[...] = wg_ref[0, 0].astype(BF16)
        wu_s[...] = wu_ref[0, 0].astype(BF16)
        wd_s[...] = wd_ref[0, 0].astype(BF16)

    @pl.when(used)
    def _():
        x = _load_token_tiles(x_ref, 0, MOE_BM).astype(BF16)
        g = _dot(x, wg_s[...])
        u = _dot(x, wu_s[...])
        a = (g * _sigmoid(g) * u).astype(BF16)
        _store_token_tiles(y_ref, 0, _dot(a, wd_s[...]))

    @pl.when(jnp.logical_not(used))
    def _():
        y_ref[...] = jnp.zeros_like(y_ref)


def moe_experts(x_pad, block_e, n_used, layer, wg, wu, wd):
    n_rows = x_pad.shape[0] // ROW_TILE
    rows = pl.BlockSpec((MOE_BM * ROW_TILE, LANES), lambda i, be, nu: (i, 0))
    w_in = pl.BlockSpec((1, 1, D_MODEL, D_FF_EXPERT), lambda i, be, nu: (layer, be[i], 0, 0))
    w_dn = pl.BlockSpec((1, 1, D_FF_EXPERT, D_MODEL), lambda i, be, nu: (layer, be[i], 0, 0))
    grid_spec = pltpu.PrefetchScalarGridSpec(
        num_scalar_prefetch=2,
        grid=(n_rows // MOE_BM,),
        in_specs=[rows, w_in, w_in, w_dn],
        out_specs=rows,
        scratch_shapes=[pltpu.VMEM((D_MODEL, D_FF_EXPERT), BF16), pltpu.VMEM((D_MODEL, D_FF_EXPERT), BF16),
                        pltpu.VMEM((D_FF_EXPERT, D_MODEL), BF16)],
    )
    return pl.pallas_call(
        _moe_kernel,
        grid_spec=grid_spec,
        out_shape=jax.ShapeDtypeStruct((n_rows * ROW_TILE, LANES), F32),
        compiler_params=_cparams(("arbitrary",)),
        name="moe_experts",
    )(block_e, n_used, x_pad, wg, wu, wd)


def _combine_kernel(tm, pos_ref, route_ref, x1_ref, ypad_hbm, o_ref, ybuf, sem):
    def body(r, carry):
        pltpu.make_async_copy(_token_tile(ypad_hbm, pos_ref[2 * r]), _token_tile(ybuf, r), sem).start()
        pltpu.make_async_copy(_token_tile(ypad_hbm, pos_ref[2 * r + 1]), _token_tile(ybuf, tm + r), sem).start()
        return carry

    lax.fori_loop(0, tm, body, 0, unroll=8)
    pltpu.make_async_copy(ypad_hbm.at[pl.ds(0, 2 * tm * ROW_TILE)], ybuf, sem).wait()
    g1 = route_ref[:, 2:3]
    g2 = route_ref[:, 3:4]
    o_ref[...] = x1_ref[...] + (g1 * _load_token_tiles(ybuf, 0, tm) + g2 * _load_token_tiles(ybuf, tm, tm))


def moe_combine(pos, route, x1, y_pad):
    T = x1.shape[0]
    tm = min(T, COMBINE_TM)
    return pl.pallas_call(
        functools.partial(_combine_kernel, tm),
        grid=(T // tm,),
        in_specs=[pl.BlockSpec((2 * tm,), lambda i: (i,), memory_space=pltpu.SMEM),
                  pl.BlockSpec((tm, LANES), lambda i: (i, 0)),
                  pl.BlockSpec((tm, D_MODEL), lambda i: (i, 0)),
                  pl.BlockSpec(memory_space=pl.ANY)],
        out_specs=pl.BlockSpec((tm, D_MODEL), lambda i: (i, 0)),
        out_shape=jax.ShapeDtypeStruct((T, D_MODEL), F32),
        scratch_shapes=[pltpu.VMEM((2 * tm * ROW_TILE, LANES), F32), pltpu.SemaphoreType.DMA(())],
        compiler_params=_cparams(("arbitrary",)),
        name="moe_combine",
    )(pos, route, x1, y_pad)


def hier_moe(x1, h2, route, counts, layer, wg, wu, wd):
    T = x1.shape[0]
    n_assign = 2 * T
    n_blocks = (n_assign + N_EXPERTS * (MOE_BM - 1) + MOE_BM - 1) // MOE_BM
    experts = route[:, 0:2].astype(jnp.int32)
    ranks = route[:, 4:6].astype(jnp.int32)
    cnt = counts[0, :N_EXPERTS].astype(jnp.int32)
    padded = ((cnt + MOE_BM - 1) // MOE_BM) * MOE_BM
    pad_ends = jnp.cumsum(padded).astype(jnp.int32)
    pad_starts = pad_ends - padded
    is_e = experts[:, :, None] == jnp.arange(N_EXPERTS, dtype=jnp.int32)
    pos = (ranks + jnp.sum(jnp.where(is_e, pad_starts, 0), axis=-1)).reshape(-1)
    block_start = jnp.arange(n_blocks, dtype=jnp.int32) * MOE_BM
    block_e = jnp.minimum(jnp.sum(block_start[:, None] >= pad_ends[None, :], axis=1), N_EXPERTS - 1).astype(jnp.int32)
    n_used = (pad_ends[-1:] // MOE_BM).astype(jnp.int32)
    x_pad = moe_dispatch(pad_ends, pos, h2, n_blocks * MOE_BM)
    y_pad = moe_experts(x_pad, block_e, n_used, layer, wg, wu, wd)
    return moe_combine(pos, route, x1, y_pad)


def _rope_tables(pos):
    half = D_HEAD // 2
    inv = ROPE_BASE ** (-jnp.arange(half, dtype=F32) / half)
    ang = pos.astype(F32)[:, None] * inv[None, :]
    cos, sin = jnp.cos(ang), jnp.sin(ang)
    return jnp.concatenate([cos, cos], axis=-1), jnp.concatenate([-sin, sin], axis=-1)


def _router_weights(w_group, b_group, w_router, b_router):
    w = jnp.zeros((D_MODEL, LANES), F32)
    w = w.at[:, :N_EXPERTS].set(w_router).at[:, ROUTER_GROUP_LANE0:ROUTER_GROUP_LANE0 + N_GROUPS].set(w_group)
    b = jnp.zeros((1, LANES), F32)
    b = b.at[0, :N_EXPERTS].set(b_router).at[0, ROUTER_GROUP_LANE0:ROUTER_GROUP_LANE0 + N_GROUPS].set(b_group)
    return w.astype(BF16), b


def _tile(v, n):
    return jnp.tile(v.astype(F32), n)


def _run_trunk(x, B, L, mem_k, mem_v, ret_state0, band_past, P):
    prompt = band_past is None
    tm = 512 if prompt else L
    proj_tm = 512
    scale = D_HEAD ** -0.5
    ones = lambda n: jnp.ones((n * D_HEAD,), F32)
    if prompt:
        cos2, sin2 = _rope_tables(jnp.arange(L, dtype=jnp.int32))
        tab_map = lambda i: i % (L // proj_tm)
    else:
        cos2, sin2 = _rope_tables(PAST_LEN + jnp.arange(L, dtype=jnp.int32))
        cos2, sin2 = jnp.tile(cos2, (B, 1)), jnp.tile(sin2, (B, 1))
        tab_map = lambda i: i
    out = {}

    def ffn(layer, mixed, proj, qm_block, w_out, x):
        x1, h2, route, counts = out_proj(
            mixed, proj, qm_block, mem_k, mem_v, layer, w_out.astype(BF16), x, P["norm_ffn"][layer],
            *_router_weights(P["w_group"][layer], P["b_group"][layer], P["w_router"][layer], P["b_router"][layer]),
            tm=tm, rows_per_stream=L)
        return hier_moe(x1, h2, route, counts, layer, P["w_e_gate"], P["w_e_up"], P["w_e_down"])

    segs0 = (("rope", N_HEADS), ("rope", N_HEADS), ("plain", N_HEADS), ("silu", N_HEADS), ("norm", H_MEM))
    gain0 = jnp.concatenate([ones(N_HEADS), ones(N_HEADS) * scale, ones(2 * N_HEADS),
                             _tile(P["q_norm_mem"][0], H_MEM) * scale])
    proj = norm_proj(x, P["norm_mix"][0], P["w_in_ret"][0].astype(BF16), gain0, segs0, tm=proj_tm,
                     out_dtype=BF16, rope=(cos2, sin2, tab_map))
    mixed, S = retention(proj, ret_state0, P["gn_ret"][0], B=B, L=L, C=RET_CHUNK if prompt else L)
    out["ret_state"] = S
    x = ffn(0, mixed, proj, (4 * MIX_W) // MEM_W, P["w_out_ret"][0], x)

    segs1 = (("norm", N_HEADS), ("norm", N_HEADS), ("plain", N_HEADS), ("norm", H_MEM))
    gain1 = jnp.concatenate([_tile(P["q_norm_att"][0], N_HEADS) * scale, _tile(P["k_norm_att"][0], N_HEADS),
                             ones(N_HEADS), _tile(P["q_norm_mem"][1], H_MEM) * scale])
    w_in = P["w_in_att"][0].astype(BF16)
    if prompt:
        proj = norm_proj(x, P["norm_mix"][1], w_in, gain1, segs1, tm=proj_tm, out_dtype=BF16)
        keep = min(BAND_PAST, L)
        tiles = L // keep
        kv = norm_proj(x, P["norm_mix"][1], w_in[:, MIX_W:3 * MIX_W], gain1[MIX_W:3 * MIX_W], segs1[1:3],
                       tm=keep, out_dtype=F32, n_tiles=B, row_map=lambda i: i * tiles + tiles - 1)
        out["band_k"] = kv[:, :MIX_W].reshape(B, keep, N_HEADS, D_HEAD)
        out["band_v"] = kv[:, MIX_W:].reshape(B, keep, N_HEADS, D_HEAD)
        mixed = band_prompt(proj, P["rel_bias_att"][0], B=B, L=L)
    else:
        proj_f = norm_proj(x, P["norm_mix"][1], w_in, gain1, segs1, tm=proj_tm, out_dtype=F32)
        out["band_k"] = proj_f[:, MIX_W:2 * MIX_W].reshape(B, L, N_HEADS, D_HEAD)
        out["band_v"] = proj_f[:, 2 * MIX_W:3 * MIX_W].reshape(B, L, N_HEADS, D_HEAD)
        proj = proj_f.astype(BF16)
        mixed = band_sample(proj, band_past[0], band_past[1], P["rel_bias_att"][0], B=B, L=L)
    out["y"] = ffn(1, mixed, proj, (3 * MIX_W) // MEM_W, P["w_out_att"][0], x)
    return out


def kernel(x_prompt, x_sample, mem_prompt, state_ret, cache_band_k, cache_band_v, cache_mem_k, cache_mem_v, norm_mix, norm_ffn, norm_mem, w_in_ret, gn_ret, w_out_ret, w_in_att, q_norm_att, k_norm_att, rel_bias_att, w_out_att, w_mem_kv, q_norm_mem, k_norm_mem, w_group, b_group, w_router, b_router, w_e_gate, w_e_up, w_e_down):
    P = {"norm_mix": norm_mix, "norm_ffn": norm_ffn, "w_in_ret": w_in_ret, "gn_ret": gn_ret,
         "w_out_ret": w_out_ret, "w_in_att": w_in_att, "q_norm_att": q_norm_att,
         "k_norm_att": k_norm_att, "rel_bias_att": rel_bias_att, "w_out_att": w_out_att,
         "q_norm_mem": q_norm_mem, "w_group": w_group, "b_group": b_group, "w_router": w_router,
         "b_router": b_router, "w_e_gate": w_e_gate, "w_e_up": w_e_up, "w_e_down": w_e_down}
    B, L, D = x_prompt.shape
    Bd, Ld, _ = x_sample.shape
    depth = norm_mix.shape[0]

    mem_k_p, mem_v_p = [], []
    for i in range(depth):
        gain = jnp.concatenate([_tile(k_norm_mem[i], H_MEM), jnp.ones((MEM_W,), F32)])
        kv = norm_proj(mem_prompt.reshape(B * N_MEM, D), norm_mem[i], w_mem_kv[i].astype(BF16), gain,
                       (("norm", H_MEM), ("plain", H_MEM)), tm=N_MEM, out_dtype=F32)
        mem_k_p.append(kv[:, :MEM_W].reshape(B, N_MEM, H_MEM, D_HEAD))
        mem_v_p.append(kv[:, MEM_W:].reshape(B, N_MEM, H_MEM, D_HEAD))
    mem_k_p = jnp.stack(mem_k_p)
    mem_v_p = jnp.stack(mem_v_p)

    zeros_state = jnp.zeros((B, N_HEADS, D_HEAD, D_HEAD), F32)
    res_p = _run_trunk(x_prompt.reshape(B * L, D), B, L, mem_k_p, mem_v_p, zeros_state, None, P)
    res_s = _run_trunk(x_sample.reshape(Bd * Ld, D), Bd, Ld, cache_mem_k, cache_mem_v, state_ret[0],
                       (cache_band_k[0], cache_band_v[0]), P)

    return (res_p["y"].reshape(B, L, D), res_s["y"].reshape(Bd, Ld, D),
            res_p["ret_state"][None], res_s["ret_state"][None],
            res_p["band_k"][None], res_p["band_v"][None], res_s["band_k"][None], res_s["band_v"][None],
            mem_k_p, mem_v_p)
```

```python
import functools

import jax
import jax.numpy as jnp
from jax import lax
from jax.experimental import pallas as pl
from jax.experimental.pallas import tpu as pltpu

F32 = jnp.float32
BF16 = jnp.bfloat16

D_MODEL = 1024
D_HEAD = 128
CHUNK = 64
N_HEADS = 8
H_MEM = 4
N_MEM = 256
MEM_W = H_MEM * D_HEAD
MIX_W = N_HEADS * D_HEAD
N_PREV_CHUNKS = 8
BAND_PAST = N_PREV_CHUNKS * CHUNK
REL_CLIP = 128
N_GROUPS = 4
EXPERTS_PER_GROUP = 8
N_EXPERTS = N_GROUPS * EXPERTS_PER_GROUP
D_FF_EXPERT = 512
ROPE_BASE = 10000.0
EPS = 1e-6
NEG_INF = -1e30
PAST_LEN = 1024

LANES = 128
SUBLANES = 8

RET_CHUNK = 256
BAND_TQ = 512
BAND_SUB = 128
BAND_WIN = BAND_PAST + BAND_SUB
MOE_BM = 256
DISPATCH_TM = 1024
COMBINE_TM = 1024
ROUTER_GROUP_LANE0 = N_EXPERTS
VMEM_LIMIT = 56 * 1024 * 1024


def _cparams(sem):
    return pltpu.CompilerParams(dimension_semantics=sem, vmem_limit_bytes=VMEM_LIMIT)


def _dot(a, b):
    return jnp.dot(a, b, preferred_element_type=F32)


def _dot_nt(a, b):
    return lax.dot_general(a, b, (((1,), (1,)), ((), ())), preferred_element_type=F32)


def _sigmoid(x):
    return 1.0 / (1.0 + jnp.exp(-x))


ROW_TILE = D_MODEL // LANES
assert ROW_TILE == SUBLANES


def _load_token_tiles(ref, t0, n):
    return jnp.concatenate([ref[pl.ds(t0 * ROW_TILE + j, n, stride=ROW_TILE), :] for j in range(ROW_TILE)], axis=1)


def _store_token_tiles(ref, t0, val):
    n = val.shape[0]
    for j in range(ROW_TILE):
        ref[pl.ds(t0 * ROW_TILE + j, n, stride=ROW_TILE), :] = val[:, j * LANES:(j + 1) * LANES]


def _token_tile(ref, t):
    return ref.at[pl.ds(pl.multiple_of(t * ROW_TILE, ROW_TILE), ROW_TILE)]


def _norm_proj_kernel(segs, has_rope, x_ref, g_ref, w_ref, gain_ref, *rest):
    if has_rope:
        cos_ref, sin_ref, o_ref = rest
    else:
        (o_ref,) = rest
    x = x_ref[...]
    ms = jnp.mean(x * x, axis=-1, keepdims=True)
    h = (x * lax.rsqrt(ms + EPS) * g_ref[...]).astype(BF16)
    col = 0
    for kind, n_heads in segs:
        width = n_heads * D_HEAD
        acc = _dot(h, w_ref[:, col:col + width])
        for j in range(n_heads):
            a = acc[:, j * D_HEAD:(j + 1) * D_HEAD]
            c0 = col + j * D_HEAD
            gain = gain_ref[:, c0:c0 + D_HEAD]
            if kind == "rope":
                rot = pltpu.roll(a, D_HEAD // 2, 1)
                out = (a * cos_ref[...] + rot * sin_ref[...]) * gain
            elif kind == "norm":
                out = a * lax.rsqrt(jnp.mean(a * a, axis=-1, keepdims=True) + EPS) * gain
            elif kind == "silu":
                out = a * _sigmoid(a)
            else:
                out = a
            o_ref[:, c0:c0 + D_HEAD] = out.astype(o_ref.dtype)
        col += width


def norm_proj(x, g, w, gain, segs, *, tm, out_dtype, n_tiles=None, row_map=None, rope=None):
    T, D = x.shape
    N = w.shape[1]
    assert sum(n for _, n in segs) * D_HEAD == N
    if n_tiles is None:
        n_tiles = T // tm
    if row_map is None:
        row_map = lambda i: i
    in_specs = [
        pl.BlockSpec((tm, D), lambda i: (row_map(i), 0)),
        pl.BlockSpec((1, D), lambda i: (0, 0)),
        pl.BlockSpec((D, N), lambda i: (0, 0)),
        pl.BlockSpec((1, N), lambda i: (0, 0)),
    ]
    args = [x, g.reshape(1, D), w, gain.reshape(1, N)]
    if rope is not None:
        cos2, sin2, tab_map = rope
        in_specs += [pl.BlockSpec((tm, D_HEAD), lambda i: (tab_map(i), 0)),
                     pl.BlockSpec((tm, D_HEAD), lambda i: (tab_map(i), 0))]
        args += [cos2, sin2]
    return pl.pallas_call(
        functools.partial(_norm_proj_kernel, segs, rope is not None),
        grid=(n_tiles,),
        in_specs=in_specs,
        out_specs=pl.BlockSpec((tm, N), lambda i: (i, 0)),
        out_shape=jax.ShapeDtypeStruct((n_tiles * tm, N), out_dtype),
        compiler_params=_cparams(("arbitrary",)),
        name="norm_proj",
    )(*args)


def _retention_kernel(q_ref, k_ref, v_ref, g_ref, s0_ref, dmask_ref, qdec_ref, kdec_ref, gc_ref,
                      gn_ref, o_ref, s_ref):
    @pl.when(pl.program_id(1) == 0)
    def _():
        s_ref[...] = s0_ref[...]

    for h in range(N_HEADS):
        sl = slice(h * D_HEAD, (h + 1) * D_HEAD)
        q = q_ref[:, sl]
        k = k_ref[:, sl]
        v = v_ref[:, sl]
        S = s_ref[0, h]
        scores = _dot_nt(q, k) * dmask_ref[h]
        intra = _dot(scores.astype(BF16), v)
        cross = _dot(q, S.astype(BF16)) * qdec_ref[:, sl]
        o = intra + cross
        kd_t = (k.astype(F32) * kdec_ref[:, sl]).T.astype(BF16)
        s_ref[0, h] = gc_ref[h] * S + _dot(kd_t, v)
        mu = jnp.mean(o, axis=-1, keepdims=True)
        d = o - mu
        var = jnp.mean(d * d, axis=-1, keepdims=True)
        y = d * lax.rsqrt(var + EPS) * gn_ref[:, sl] * g_ref[:, sl].astype(F32)
        o_ref[:, sl] = y.astype(o_ref.dtype)


def _retention_tables(C):
    log_gamma = jnp.log1p(-jnp.exp2(-5.0 - jnp.arange(N_HEADS, dtype=F32)))
    idx = jnp.arange(C, dtype=F32)
    diff = idx[:, None] - idx[None, :]
    dmask = jnp.where(diff[None] >= 0,
                      jnp.exp(log_gamma[:, None, None] * jnp.maximum(diff, 0.0)[None]), 0.0)
    q_dec = jnp.exp(log_gamma[None, :] * (idx + 1.0)[:, None])
    k_dec = jnp.exp(log_gamma[None, :] * (C - 1.0 - idx)[:, None])
    g_c = jnp.exp(log_gamma * C)
    rep = lambda t: jnp.repeat(t, D_HEAD, axis=-1)
    return dmask, rep(q_dec), rep(k_dec), rep(g_c[:, None])[:, None, :]


def retention(proj, s0, gn, *, B, L, C):
    nc = L // C
    dmask, qdec, kdec, gc = _retention_tables(C)
    row = lambda col: pl.BlockSpec((C, MIX_W), lambda b, c: (b * nc + c, col))
    const = lambda shape: pl.BlockSpec(shape, lambda b, c: (0,) * len(shape))
    state_spec = pl.BlockSpec((1, N_HEADS, D_HEAD, D_HEAD), lambda b, c: (b, 0, 0, 0))
    return pl.pallas_call(
        _retention_kernel,
        grid=(B, nc),
        in_specs=[row(0), row(1), row(2), row(3), state_spec,
                  const((N_HEADS, C, C)), const((C, MIX_W)), const((C, MIX_W)),
                  const((N_HEADS, 1, D_HEAD)), const((1, MIX_W))],
        out_specs=[pl.BlockSpec((C, MIX_W), lambda b, c: (b * nc + c, 0)), state_spec],
        out_shape=[jax.ShapeDtypeStruct((B * L, MIX_W), BF16),
                   jax.ShapeDtypeStruct((B, N_HEADS, D_HEAD, D_HEAD), F32)],
        compiler_params=_cparams(("arbitrary", "arbitrary")),
        name="retention",
    )(proj, proj, proj, proj, s0, dmask, qdec, kdec, gc, gn.reshape(1, MIX_W))


def _attend_two_piece(q, k_a, k_b, v_a, v_b, bias_a, bias_b, a_valid):
    s_a = _dot_nt(q, k_a) + bias_a
    if a_valid is not None:
        s_a = jnp.where(a_valid, s_a, NEG_INF)
    s_b = _dot_nt(q, k_b) + bias_b
    m = jnp.maximum(jnp.max(s_a, axis=-1, keepdims=True), jnp.max(s_b, axis=-1, keepdims=True))
    p_a = jnp.exp(s_a - m)
    p_b = jnp.exp(s_b - m)
    denom = jnp.sum(p_a, axis=-1, keepdims=True) + jnp.sum(p_b, axis=-1, keepdims=True)
    o = _dot(p_a.astype(BF16), v_a) + _dot(p_b.astype(BF16), v_b)
    return o / denom


def _band_prompt_kernel(q_ref, kp_ref, kc_ref, vp_ref, vc_ref, bias_ref, o_ref, s_buf, p_buf):
    @pl.when((pl.program_id(0) == 0) & (pl.program_id(1) == 0))
    def _():
        p_buf[...] = jnp.zeros_like(p_buf)

    prev_off = jnp.where(pl.program_id(1) > 0, 0.0, NEG_INF).astype(F32)
    for h in range(N_HEADS):
        sl = slice(h * D_HEAD, (h + 1) * D_HEAD)
        q = q_ref[:, sl]
        s_buf[:, :BAND_TQ] = _dot_nt(q, kp_ref[:, sl]) + prev_off
        s_buf[:, BAND_TQ:] = _dot_nt(q, kc_ref[:, sl])
        denoms = []
        for s in range(BAND_TQ // BAND_SUB):
            r0 = s * BAND_SUB
            sc = s_buf[r0:r0 + BAND_SUB, r0:r0 + BAND_WIN] + bias_ref[h]
            p = jnp.exp(sc - jnp.max(sc, axis=-1, keepdims=True))
            denoms.append(jnp.sum(p, axis=-1, keepdims=True))
            p_buf[r0:r0 + BAND_SUB, r0:r0 + BAND_WIN] = p.astype(BF16)
        o = _dot(p_buf[:, :BAND_TQ], vp_ref[:, sl]) + _dot(p_buf[:, BAND_TQ:], vc_ref[:, sl])
        o_ref[:, sl] = (o / jnp.concatenate(denoms, axis=0)).astype(o_ref.dtype)


def _band_prompt_bias(bias_table):
    r = jnp.arange(BAND_SUB, dtype=jnp.int32)[:, None]
    j = jnp.arange(BAND_WIN, dtype=jnp.int32)[None, :]
    q_chunk = r // CHUNK + N_PREV_CHUNKS
    k_chunk = j // CHUNK
    allowed = (k_chunk <= q_chunk) & (k_chunk >= q_chunk - N_PREV_CHUNKS)
    n_diag = BAND_SUB + BAND_WIN - 1
    d = (BAND_PAST + BAND_SUB - 1) - jnp.arange(n_diag, dtype=jnp.int32)
    diag = bias_table.astype(F32)[:, jnp.clip(d, -REL_CLIP, REL_CLIP) + REL_CLIP]
    cut = jnp.tile(diag, (1, BAND_SUB + 1))[:, :BAND_SUB * (n_diag + 1)].reshape(N_HEADS, BAND_SUB, n_diag + 1)
    rows = cut[:, ::-1, :BAND_WIN]
    return jnp.where(allowed[None], rows, NEG_INF)


def band_prompt(proj, bias_table, *, B, L):
    nq = L // BAND_TQ
    cur = lambda col: pl.BlockSpec((BAND_TQ, MIX_W), lambda b, i: (b * nq + i, col))
    prev = lambda col: pl.BlockSpec((BAND_TQ, MIX_W), lambda b, i: (b * nq + jnp.maximum(i - 1, 0), col))
    return pl.pallas_call(
        _band_prompt_kernel,
        grid=(B, nq),
        in_specs=[cur(0), prev(1), cur(1), prev(2), cur(2),
                  pl.BlockSpec((N_HEADS, BAND_SUB, BAND_WIN), lambda b, i: (0, 0, 0))],
        out_specs=pl.BlockSpec((BAND_TQ, MIX_W), lambda b, i: (b * nq + i, 0)),
        out_shape=jax.ShapeDtypeStruct((B * L, MIX_W), BF16),
        scratch_shapes=[pltpu.VMEM((BAND_TQ, 2 * BAND_TQ), F32), pltpu.VMEM((BAND_TQ, 2 * BAND_TQ), BF16)],
        compiler_params=_cparams(("arbitrary", "arbitrary")),
        name="band_prompt",
    )(proj, proj, proj, proj, proj, _band_prompt_bias(bias_table))


def _band_sample_kernel(q_ref, kp_ref, kc_ref, vp_ref, vc_ref, bias_p_ref, bias_c_ref, o_ref):
    n_past = kp_ref.shape[1] // N_HEADS
    for h in range(N_HEADS):
        sl = slice(h * D_HEAD, (h + 1) * D_HEAD)
        kp = kp_ref.at[0][pl.ds(h, n_past, stride=N_HEADS), :].astype(BF16)
        vp = vp_ref.at[0][pl.ds(h, n_past, stride=N_HEADS), :].astype(BF16)
        o = _attend_two_piece(q_ref[:, sl], kp, kc_ref[:, sl], vp, vc_ref[:, sl], bias_p_ref[h], bias_c_ref[h], None)
        o_ref[:, sl] = o.astype(o_ref.dtype)


def band_sample(proj, past_k, past_v, bias_table, *, B, L):
    P = past_k.shape[1]
    qpos = PAST_LEN + jnp.arange(L, dtype=jnp.int32)
    kpos = jnp.concatenate([PAST_LEN - P + jnp.arange(P, dtype=jnp.int32), qpos])
    rel = jnp.clip(qpos[:, None] - kpos[None, :], -REL_CLIP, REL_CLIP) + REL_CLIP
    qc = (qpos // CHUNK)[:, None]
    kc = (kpos // CHUNK)[None, :]
    allowed = (kpos[None, :] >= 0) & (kc <= qc) & (kc >= qc - N_PREV_CHUNKS)
    bias = jnp.where(allowed[None], bias_table.astype(F32)[:, rel], NEG_INF)
    cur = lambda col: pl.BlockSpec((L, MIX_W), lambda b: (b, col))
    past = pl.BlockSpec((1, P * N_HEADS, D_HEAD), lambda b: (b, 0, 0))
    past_k = past_k.reshape(B, P * N_HEADS, D_HEAD)
    past_v = past_v.reshape(B, P * N_HEADS, D_HEAD)
    return pl.pallas_call(
        _band_sample_kernel,
        grid=(B,),
        in_specs=[cur(0), past, cur(1), past, cur(2),
                  pl.BlockSpec((N_HEADS, L, P), lambda b: (0, 0, 0)),
                  pl.BlockSpec((N_HEADS, L, L), lambda b: (0, 0, 0))],
        out_specs=pl.BlockSpec((L, MIX_W), lambda b: (b, 0)),
        out_shape=jax.ShapeDtypeStruct((B * L, MIX_W), BF16),
        compiler_params=_cparams(("arbitrary",)),
        name="band_sample",
    )(proj, past_k, proj, past_v, proj, bias[:, :, :P], bias[:, :, P:])


def _out_proj_kernel(o_ref, qm_ref, mk_ref, mv_ref, w_ref, x_ref, gf_ref, wr_ref, br_ref, tri_ref,
                     x1_ref, h2_ref, route_ref, cnt_ref):
    @pl.when(pl.program_id(0) == 0)
    def _():
        cnt_ref[...] = jnp.zeros_like(cnt_ref)

    parts = [o_ref[...]]
    for h in range(H_MEM):
        sl = slice(h * D_HEAD, (h + 1) * D_HEAD)
        mk = mk_ref.at[0, 0][pl.ds(h, N_MEM, stride=H_MEM), :].astype(BF16)
        mv = mv_ref.at[0, 0][pl.ds(h, N_MEM, stride=H_MEM), :].astype(BF16)
        s = _dot_nt(qm_ref[:, sl], mk)
        p = jnp.exp(s - jnp.max(s, axis=-1, keepdims=True))
        om = _dot(p.astype(BF16), mv) / jnp.sum(p, axis=-1, keepdims=True)
        parts.append(om.astype(BF16))
    x1 = x_ref[...] + _dot(jnp.concatenate(parts, axis=1), w_ref[...])
    x1_ref[...] = x1
    hn = x1 * lax.rsqrt(jnp.mean(x1 * x1, axis=-1, keepdims=True) + EPS) * gf_ref[...]
    _store_token_tiles(h2_ref, 0, hn)

    logits = _dot(hn.astype(BF16), wr_ref[...]) + br_ref[...]
    lane = lax.broadcasted_iota(jnp.int32, logits.shape, 1)
    lane_f = lane.astype(F32)
    far = float(LANES)
    is_group = (lane >= ROUTER_GROUP_LANE0) & (lane < ROUTER_GROUP_LANE0 + N_GROUPS)
    gl = jnp.where(is_group, logits, NEG_INF)
    g_max = jnp.max(gl, axis=-1, keepdims=True)
    g_sel = jnp.min(jnp.where(gl == g_max, lane_f, far), axis=-1, keepdims=True) - float(ROUTER_GROUP_LANE0)
    g_gate = 1.0 / jnp.sum(jnp.exp(gl - g_max), axis=-1, keepdims=True)
    in_group = jnp.right_shift(lane, 3).astype(F32) == g_sel
    el = jnp.where(in_group, logits, NEG_INF)
    m1 = jnp.max(el, axis=-1, keepdims=True)
    i1 = jnp.min(jnp.where(el == m1, lane_f, far), axis=-1, keepdims=True)
    el2 = jnp.where(lane_f == i1, NEG_INF, el)
    m2 = jnp.max(el2, axis=-1, keepdims=True)
    i2 = jnp.min(jnp.where(el2 == m2, lane_f, far), axis=-1, keepdims=True)
    e2 = jnp.exp(m2 - m1)
    p1 = 1.0 / (1.0 + e2)

    oh1 = lane_f == i1
    oh2 = lane_f == i2
    onehot = jnp.where(oh1 | oh2, 1.0, 0.0)
    before = cnt_ref[0:1, :] + _dot(tri_ref[...], onehot.astype(BF16))
    r1 = jnp.sum(jnp.where(oh1, before, 0.0), axis=-1, keepdims=True)
    r2 = jnp.sum(jnp.where(oh2, before, 0.0), axis=-1, keepdims=True)
    cnt_ref[0:1, :] = cnt_ref[0:1, :] + jnp.sum(onehot, axis=0, keepdims=True)

    route = jnp.where(lane == 0, i1, jnp.where(lane == 1, i2, jnp.where(
        lane == 2, g_gate * p1, jnp.where(lane == 3, g_gate * (e2 * p1), jnp.where(
            lane == 4, r1, jnp.where(lane == 5, r2, 0.0))))))
    route_ref[...] = route


def out_proj(mixed, proj, qm_block, mem_k, mem_v, layer, w_out, x, g_ffn, w_router, b_router, *, tm,
             rows_per_stream):
    T = x.shape[0]
    tiles_per_stream = rows_per_stream // tm
    row = lambda width, col: pl.BlockSpec((tm, width), lambda i: (i, col))
    const = lambda shape: pl.BlockSpec(shape, lambda i: (0,) * len(shape))
    mem = pl.BlockSpec((1, 1, N_MEM * H_MEM, D_HEAD), lambda i: (layer, i // tiles_per_stream, 0, 0))
    mem_k = mem_k.reshape(mem_k.shape[:2] + (N_MEM * H_MEM, D_HEAD))
    mem_v = mem_v.reshape(mem_v.shape[:2] + (N_MEM * H_MEM, D_HEAD))
    tri = (jnp.arange(tm)[None, :] < jnp.arange(tm)[:, None]).astype(BF16)
    return pl.pallas_call(
        _out_proj_kernel,
        grid=(T // tm,),
        in_specs=[row(MIX_W, 0), row(MEM_W, qm_block), mem, mem, const((MIX_W + MEM_W, D_MODEL)),
                  row(D_MODEL, 0), const((1, D_MODEL)), const((D_MODEL, LANES)), const((1, LANES)),
                  const((tm, tm))],
        out_specs=[row(D_MODEL, 0), pl.BlockSpec((tm * ROW_TILE, LANES), lambda i: (i, 0)), row(LANES, 0),
                   const((SUBLANES, LANES))],
        out_shape=[jax.ShapeDtypeStruct((T, D_MODEL), F32), jax.ShapeDtypeStruct((T * ROW_TILE, LANES), F32),
                   jax.ShapeDtypeStruct((T, LANES), F32), jax.ShapeDtypeStruct((SUBLANES, LANES), F32)],
        compiler_params=_cparams(("arbitrary",)),
        name="out_proj",
    )(mixed, proj, mem_k, mem_v, w_out, x, g_ffn.reshape(1, D_MODEL), w_router, b_router, tri)


def _dispatch_kernel(tm, pe_ref, pos_ref, h2_ref, xpad_hbm, zbuf, sem, zsem):
    i = pl.program_id(0)

    @pl.when(i == 0)
    def _():
        zbuf[...] = jnp.zeros_like(zbuf)

        def block_copy(first_row):
            start = pl.multiple_of(first_row * ROW_TILE, MOE_BM * ROW_TILE)
            return pltpu.make_async_copy(zbuf, xpad_hbm.at[pl.ds(start, MOE_BM * ROW_TILE)], zsem)

        def tail_copy(e):
            return block_copy(pe_ref[e] - MOE_BM)

        def nonempty(e):
            return pe_ref[e] > (pe_ref[e - 1] if e > 0 else 0)

        for e in range(N_EXPERTS):
            pl.when(nonempty(e))(lambda e=e: tail_copy(e).start())
        for e in range(N_EXPERTS):
            pl.when(nonempty(e))(lambda e=e: tail_copy(e).wait())

        def fill(b, carry):
            cp = block_copy(b * MOE_BM)
            cp.start()
            cp.wait()
            return carry

        lax.fori_loop(pe_ref[N_EXPERTS - 1] // MOE_BM, xpad_hbm.shape[0] // (MOE_BM * ROW_TILE), fill, 0)

    def body(r, carry):
        src = _token_tile(h2_ref, r)
        pltpu.make_async_copy(src, _token_tile(xpad_hbm, pos_ref[2 * r]), sem).start(priority=0)
        pltpu.make_async_copy(src, _token_tile(xpad_hbm, pos_ref[2 * r + 1]), sem).start(priority=1)
        return carry

    lax.fori_loop(0, tm, body, 0, unroll=8)
    for _ in range(2):
        pltpu.make_async_copy(h2_ref, xpad_hbm.at[pl.ds(0, tm * ROW_TILE)], sem).wait()


def moe_dispatch(pad_ends, pos, h2, n_rows):
    T = h2.shape[0] // ROW_TILE
    tm = min(T, DISPATCH_TM)
    grid_spec = pltpu.PrefetchScalarGridSpec(
        num_scalar_prefetch=1,
        grid=(T // tm,),
        in_specs=[pl.BlockSpec((2 * tm,), lambda i, pe: (i,), memory_space=pltpu.SMEM),
                  pl.BlockSpec((tm * ROW_TILE, LANES), lambda i, pe: (i, 0))],
        out_specs=pl.BlockSpec(memory_space=pl.ANY),
        scratch_shapes=[pltpu.VMEM((MOE_BM * ROW_TILE, LANES), F32), pltpu.SemaphoreType.DMA(()),
                        pltpu.SemaphoreType.DMA(())],
    )
    return pl.pallas_call(
        functools.partial(_dispatch_kernel, tm),
        grid_spec=grid_spec,
        out_shape=jax.ShapeDtypeStruct((n_rows * ROW_TILE, LANES), F32),
        compiler_params=_cparams(("arbitrary",)),
        name="moe_dispatch",
    )(pad_ends, pos, h2)


def _moe_kernel(be_ref, nu_ref, x_ref, wg_ref, wu_ref, wd_ref, y_ref, wg_s, wu_s, wd_s):
    i = pl.program_id(0)
    used = i < nu_ref[0]
    new_expert = jnp.logical_or(i == 0, be_ref[i] != be_ref[jnp.maximum(i - 1, 0)])

    @pl.when(jnp.logical_and(used, new_expert))
    def _():
        wg_s[...] = wg_ref[0, 0].astype(BF16)
        wu_s[...] = wu_ref[0, 0].astype(BF16)
        wd_s[...] = wd_ref[0, 0].astype(BF16)

    @pl.when(used)
    def _():
        x = _load_token_tiles(x_ref, 0, MOE_BM).astype(BF16)
        g = _dot(x, wg_s[...])
        u = _dot(x, wu_s[...])
        a = (g * _sigmoid(g) * u).astype(BF16)
        _store_token_tiles(y_ref, 0, _dot(a, wd_s[...]))

    @pl.when(jnp.logical_not(used))
    def _():
        y_ref[...] = jnp.zeros_like(y_ref)


def moe_experts(x_pad, block_e, n_used, layer, wg, wu, wd):
    n_rows = x_pad.shape[0] // ROW_TILE
    rows = pl.BlockSpec((MOE_BM * ROW_TILE, LANES), lambda i, be, nu: (i, 0))
    w_in = pl.BlockSpec((1, 1, D_MODEL, D_FF_EXPERT), lambda i, be, nu: (layer, be[i], 0, 0))
    w_dn = pl.BlockSpec((1, 1, D_FF_EXPERT, D_MODEL), lambda i, be, nu: (layer, be[i], 0, 0))
    grid_spec = pltpu.PrefetchScalarGridSpec(
        num_scalar_prefetch=2,
        grid=(n_rows // MOE_BM,),
        in_specs=[rows, w_in, w_in, w_dn],
        out_specs=rows,
        scratch_shapes=[pltpu.VMEM((D_MODEL, D_FF_EXPERT), BF16), pltpu.VMEM((D_MODEL, D_FF_EXPERT), BF16),
                        pltpu.VMEM((D_FF_EXPERT, D_MODEL), BF16)],
    )
    return pl.pallas_call(
        _moe_kernel,
        grid_spec=grid_spec,
        out_shape=jax.ShapeDtypeStruct((n_rows * ROW_TILE, LANES), F32),
        compiler_params=_cparams(("arbitrary",)),
        name="moe_experts",
    )(block_e, n_used, x_pad, wg, wu, wd)


def _combine_kernel(tm, pos_ref, route_ref, x1_ref, ypad_hbm, o_ref, ybuf, sem):
    def body(r, carry):
        pltpu.make_async_copy(_token_tile(ypad_hbm, pos_ref[2 * r]), _token_tile(ybuf, r), sem).start(priority=0)
        pltpu.make_async_copy(_token_tile(ypad_hbm, pos_ref[2 * r + 1]), _token_tile(ybuf, tm + r), sem).start(
            priority=1)
        return carry

    lax.fori_loop(0, tm, body, 0, unroll=8)
    pltpu.make_async_copy(ypad_hbm.at[pl.ds(0, 2 * tm * ROW_TILE)], ybuf, sem).wait()
    g1 = route_ref[:, 2:3]
    g2 = route_ref[:, 3:4]
    o_ref[...] = x1_ref[...] + (g1 * _load_token_tiles(ybuf, 0, tm) + g2 * _load_token_tiles(ybuf, tm, tm))


def moe_combine(pos, route, x1, y_pad):
    T = x1.shape[0]
    tm = min(T, COMBINE_TM)
    return pl.pallas_call(
        functools.partial(_combine_kernel, tm),
        grid=(T // tm,),
        in_specs=[pl.BlockSpec((2 * tm,), lambda i: (i,), memory_space=pltpu.SMEM),
                  pl.BlockSpec((tm, LANES), lambda i: (i, 0)),
                  pl.BlockSpec((tm, D_MODEL), lambda i: (i, 0)),
                  pl.BlockSpec(memory_space=pl.ANY)],
        out_specs=pl.BlockSpec((tm, D_MODEL), lambda i: (i, 0)),
        out_shape=jax.ShapeDtypeStruct((T, D_MODEL), F32),
        scratch_shapes=[pltpu.VMEM((2 * tm * ROW_TILE, LANES), F32), pltpu.SemaphoreType.DMA(())],
        compiler_params=_cparams(("arbitrary",)),
        name="moe_combine",
    )(pos, route, x1, y_pad)


def hier_moe(x1, h2, route, counts, layer, wg, wu, wd):
    T = x1.shape[0]
    n_assign = 2 * T
    n_blocks = (n_assign + N_EXPERTS * (MOE_BM - 1) + MOE_BM - 1) // MOE_BM
    experts = route[:, 0:2].astype(jnp.int32)
    ranks = route[:, 4:6].astype(jnp.int32)
    cnt = counts[0, :N_EXPERTS].astype(jnp.int32)
    padded = ((cnt + MOE_BM - 1) // MOE_BM) * MOE_BM
    pad_ends = jnp.cumsum(padded).astype(jnp.int32)
    pad_starts = pad_ends - padded
    is_e = experts[:, :, None] == jnp.arange(N_EXPERTS, dtype=jnp.int32)
    pos = (ranks + jnp.sum(jnp.where(is_e, pad_starts, 0), axis=-1)).reshape(-1)
    block_start = jnp.arange(n_blocks, dtype=jnp.int32) * MOE_BM
    block_e = jnp.minimum(jnp.sum(block_start[:, None] >= pad_ends[None, :], axis=1), N_EXPERTS - 1).astype(jnp.int32)
    n_used = (pad_ends[-1:] // MOE_BM).astype(jnp.int32)
    x_pad = moe_dispatch(pad_ends, pos, h2, n_blocks * MOE_BM)
    y_pad = moe_experts(x_pad, block_e, n_used, layer, wg, wu, wd)
    return moe_combine(pos, route, x1, y_pad)


def _rope_tables(pos):
    half = D_HEAD // 2
    inv = ROPE_BASE ** (-jnp.arange(half, dtype=F32) / half)
    ang = pos.astype(F32)[:, None] * inv[None, :]
    cos, sin = jnp.cos(ang), jnp.sin(ang)
    return jnp.concatenate([cos, cos], axis=-1), jnp.concatenate([-sin, sin], axis=-1)


def _router_weights(w_group, b_group, w_router, b_router):
    w = jnp.zeros((D_MODEL, LANES), F32)
    w = w.at[:, :N_EXPERTS].set(w_router).at[:, ROUTER_GROUP_LANE0:ROUTER_GROUP_LANE0 + N_GROUPS].set(w_group)
    b = jnp.zeros((1, LANES), F32)
    b = b.at[0, :N_EXPERTS].set(b_router).at[0, ROUTER_GROUP_LANE0:ROUTER_GROUP_LANE0 + N_GROUPS].set(b_group)
    return w.astype(BF16), b


def _tile(v, n):
    return jnp.tile(v.astype(F32), n)


def _run_trunk(x, B, L, mem_k, mem_v, ret_state0, band_past, P):
    prompt = band_past is None
    tm = 512 if prompt else L
    proj_tm = 512
    scale = D_HEAD ** -0.5
    ones = lambda n: jnp.ones((n * D_HEAD,), F32)
    if prompt:
        cos2, sin2 = _rope_tables(jnp.arange(L, dtype=jnp.int32))
        tab_map = lambda i: i % (L // proj_tm)
    else:
        cos2, sin2 = _rope_tables(PAST_LEN + jnp.arange(L, dtype=jnp.int32))
        cos2, sin2 = jnp.tile(cos2, (B, 1)), jnp.tile(sin2, (B, 1))
        tab_map = lambda i: i
    out = {}

    def ffn(layer, mixed, proj, qm_block, w_out, x):
        x1, h2, route, counts = out_proj(
            mixed, proj, qm_block, mem_k, mem_v, layer, w_out.astype(BF16), x, P["norm_ffn"][layer],
            *_router_weights(P["w_group"][layer], P["b_group"][layer], P["w_router"][layer], P["b_router"][layer]),
            tm=tm, rows_per_stream=L)
        return hier_moe(x1, h2, route, counts, layer, P["w_e_gate"], P["w_e_up"], P["w_e_down"])

    segs0 = (("rope", N_HEADS), ("rope", N_HEADS), ("plain", N_HEADS), ("silu", N_HEADS), ("norm", H_MEM))
    gain0 = jnp.concatenate([ones(N_HEADS), ones(N_HEADS) * scale, ones(2 * N_HEADS),
                             _tile(P["q_norm_mem"][0], H_MEM) * scale])
    proj = norm_proj(x, P["norm_mix"][0], P["w_in_ret"][0].astype(BF16), gain0, segs0, tm=proj_tm,
                     out_dtype=BF16, rope=(cos2, sin2, tab_map))
    mixed, S = retention(proj, ret_state0, P["gn_ret"][0], B=B, L=L, C=RET_CHUNK if prompt else L)
    out["ret_state"] = S
    x = ffn(0, mixed, proj, (4 * MIX_W) // MEM_W, P["w_out_ret"][0], x)

    segs1 = (("norm", N_HEADS), ("norm", N_HEADS), ("plain", N_HEADS), ("norm", H_MEM))
    gain1 = jnp.concatenate([_tile(P["q_norm_att"][0], N_HEADS) * scale, _tile(P["k_norm_att"][0], N_HEADS),
                             ones(N_HEADS), _tile(P["q_norm_mem"][1], H_MEM) * scale])
    w_in = P["w_in_att"][0].astype(BF16)
    if prompt:
        proj = norm_proj(x, P["norm_mix"][1], w_in, gain1, segs1, tm=proj_tm, out_dtype=BF16)
        keep = min(BAND_PAST, L)
        tiles = L // keep
        kv = norm_proj(x, P["norm_mix"][1], w_in[:, MIX_W:3 * MIX_W], gain1[MIX_W:3 * MIX_W], segs1[1:3],
                       tm=keep, out_dtype=F32, n_tiles=B, row_map=lambda i: i * tiles + tiles - 1)
        out["band_k"] = kv[:, :MIX_W].reshape(B, keep, N_HEADS, D_HEAD)
        out["band_v"] = kv[:, MIX_W:].reshape(B, keep, N_HEADS, D_HEAD)
        mixed = band_prompt(proj, P["rel_bias_att"][0], B=B, L=L)
    else:
        proj_f = norm_proj(x, P["norm_mix"][1], w_in, gain1, segs1, tm=proj_tm, out_dtype=F32)
        out["band_k"] = proj_f[:, MIX_W:2 * MIX_W].reshape(B, L, N_HEADS, D_HEAD)
        out["band_v"] = proj_f[:, 2 * MIX_W:3 * MIX_W].reshape(B, L, N_HEADS, D_HEAD)
        proj = proj_f.astype(BF16)
        mixed = band_sample(proj, band_past[0], band_past[1], P["rel_bias_att"][0], B=B, L=L)
    out["y"] = ffn(1, mixed, proj, (3 * MIX_W) // MEM_W, P["w_out_att"][0], x)
    return out


def kernel(x_prompt, x_sample, mem_prompt, state_ret, cache_band_k, cache_band_v, cache_mem_k, cache_mem_v, norm_mix, norm_ffn, norm_mem, w_in_ret, gn_ret, w_out_ret, w_in_att, q_norm_att, k_norm_att, rel_bias_att, w_out_att, w_mem_kv, q_norm_mem, k_norm_mem, w_group, b_group, w_router, b_router, w_e_gate, w_e_up, w_e_down):
    P = {"norm_mix": norm_mix, "norm_ffn": norm_ffn, "w_in_ret": w_in_ret, "gn_ret": gn_ret,
         "w_out_ret": w_out_ret, "w_in_att": w_in_att, "q_norm_att": q_norm_att,
         "k_norm_att": k_norm_att, "rel_bias_att": rel_bias_att, "w_out_att": w_out_att,
         "q_norm_mem": q_norm_mem, "w_group": w_group, "b_group": b_group, "w_router": w_router,
         "b_router": b_router, "w_e_gate": w_e_gate, "w_e_up": w_e_up, "w_e_down": w_e_down}
    B, L, D = x_prompt.shape
    Bd, Ld, _ = x_sample.shape
    depth = norm_mix.shape[0]

    mem_k_p, mem_v_p = [], []
    for i in range(depth):
        gain = jnp.concatenate([_tile(k_norm_mem[i], H_MEM), jnp.ones((MEM_W,), F32)])
        kv = norm_proj(mem_prompt.reshape(B * N_MEM, D), norm_mem[i], w_mem_kv[i].astype(BF16), gain,
                       (("norm", H_MEM), ("plain", H_MEM)), tm=N_MEM, out_dtype=F32)
        mem_k_p.append(kv[:, :MEM_W].reshape(B, N_MEM, H_MEM, D_HEAD))
        mem_v_p.append(kv[:, MEM_W:].reshape(B, N_MEM, H_MEM, D_HEAD))
    mem_k_p = jnp.stack(mem_k_p)
    mem_v_p = jnp.stack(mem_v_p)

    zeros_state = jnp.zeros((B, N_HEADS, D_HEAD, D_HEAD), F32)
    res_p = _run_trunk(x_prompt.reshape(B * L, D), B, L, mem_k_p, mem_v_p, zeros_state, None, P)
    res_s = _run_trunk(x_sample.reshape(Bd * Ld, D), Bd, Ld, cache_mem_k, cache_mem_v, state_ret[0],
                       (cache_band_k[0], cache_band_v[0]), P)

    return (res_p["y"].reshape(B, L, D), res_s["y"].reshape(Bd, Ld, D),
            res_p["ret_state"][None], res_s["ret_state"][None],
            res_p["band_k"][None], res_p["band_v"][None], res_s["band_k"][None], res_s["band_v"][None],
            mem_k_p, mem_v_p)
```

```python
import functools

import jax
import jax.numpy as jnp
from jax import lax
from jax.experimental import pallas as pl
from jax.experimental.pallas import tpu as pltpu

F32 = jnp.float32
BF16 = jnp.bfloat16

D_MODEL = 1024
D_HEAD = 128
CHUNK = 64
N_HEADS = 8
H_MEM = 4
N_MEM = 256
MEM_W = H_MEM * D_HEAD
MIX_W = N_HEADS * D_HEAD
N_PREV_CHUNKS = 8
BAND_PAST = N_PREV_CHUNKS * CHUNK
REL_CLIP = 128
N_GROUPS = 4
EXPERTS_PER_GROUP = 8
N_EXPERTS = N_GROUPS * EXPERTS_PER_GROUP
D_FF_EXPERT = 512
ROPE_BASE = 10000.0
EPS = 1e-6
NEG_INF = -1e30
PAST_LEN = 1024

LANES = 128
SUBLANES = 8

RET_CHUNK = 256
BAND_TQ = 512
BAND_SUB = 128
BAND_WIN = BAND_PAST + BAND_SUB
MOE_BM = 256
DISPATCH_TM = 1024
COMBINE_TM = 1024
ROUTE_ROWS = 128
ROUTER_GROUP_LANE0 = N_EXPERTS
VMEM_LIMIT = 56 * 1024 * 1024


def _cparams(sem):
    return pltpu.CompilerParams(dimension_semantics=sem, vmem_limit_bytes=VMEM_LIMIT)


def _dot(a, b):
    return jnp.dot(a, b, preferred_element_type=F32)


def _dot_nt(a, b):
    return lax.dot_general(a, b, (((1,), (1,)), ((), ())), preferred_element_type=F32)


def _sigmoid(x):
    return 1.0 / (1.0 + jnp.exp(-x))


ROW_TILE = D_MODEL // LANES
assert ROW_TILE == SUBLANES


def _load_token_tiles(ref, t0, n):
    return jnp.concatenate([ref[pl.ds(t0 * ROW_TILE + j, n, stride=ROW_TILE), :] for j in range(ROW_TILE)], axis=1)


def _store_token_tiles(ref, t0, val):
    n = val.shape[0]
    for j in range(ROW_TILE):
        ref[pl.ds(t0 * ROW_TILE + j, n, stride=ROW_TILE), :] = val[:, j * LANES:(j + 1) * LANES]


def _token_tile(ref, t):
    return ref.at[pl.ds(pl.multiple_of(t * ROW_TILE, ROW_TILE), ROW_TILE)]


def _norm_proj_kernel(segs, has_rope, x_ref, g_ref, w_ref, gain_ref, *rest):
    if has_rope:
        cos_ref, sin_ref, o_ref = rest
    else:
        (o_ref,) = rest
    x = x_ref[...]
    ms = jnp.mean(x * x, axis=-1, keepdims=True)
    h = (x * lax.rsqrt(ms + EPS) * g_ref[...]).astype(BF16)
    col = 0
    for kind, n_heads in segs:
        width = n_heads * D_HEAD
        acc = _dot(h, w_ref[:, col:col + width])
        for j in range(n_heads):
            a = acc[:, j * D_HEAD:(j + 1) * D_HEAD]
            c0 = col + j * D_HEAD
            gain = gain_ref[:, c0:c0 + D_HEAD]
            if kind == "rope":
                rot = pltpu.roll(a, D_HEAD // 2, 1)
                out = (a * cos_ref[...] + rot * sin_ref[...]) * gain
            elif kind == "norm":
                out = a * lax.rsqrt(jnp.mean(a * a, axis=-1, keepdims=True) + EPS) * gain
            elif kind == "silu":
                out = a * _sigmoid(a)
            else:
                out = a
            o_ref[:, c0:c0 + D_HEAD] = out.astype(o_ref.dtype)
        col += width


def norm_proj(x, g, w, gain, segs, *, tm, out_dtype, n_tiles=None, row_map=None, rope=None):
    T, D = x.shape
    N = w.shape[1]
    assert sum(n for _, n in segs) * D_HEAD == N
    if n_tiles is None:
        n_tiles = T // tm
    if row_map is None:
        row_map = lambda i: i
    in_specs = [
        pl.BlockSpec((tm, D), lambda i: (row_map(i), 0)),
        pl.BlockSpec((1, D), lambda i: (0, 0)),
        pl.BlockSpec((D, N), lambda i: (0, 0)),
        pl.BlockSpec((1, N), lambda i: (0, 0)),
    ]
    args = [x, g.reshape(1, D), w, gain.reshape(1, N)]
    if rope is not None:
        cos2, sin2, tab_map = rope
        in_specs += [pl.BlockSpec((tm, D_HEAD), lambda i: (tab_map(i), 0)),
                     pl.BlockSpec((tm, D_HEAD), lambda i: (tab_map(i), 0))]
        args += [cos2, sin2]
    return pl.pallas_call(
        functools.partial(_norm_proj_kernel, segs, rope is not None),
        grid=(n_tiles,),
        in_specs=in_specs,
        out_specs=pl.BlockSpec((tm, N), lambda i: (i, 0)),
        out_shape=jax.ShapeDtypeStruct((n_tiles * tm, N), out_dtype),
        compiler_params=_cparams(("arbitrary",)),
        name="norm_proj",
    )(*args)


def _retention_kernel(q_ref, k_ref, v_ref, g_ref, s0_ref, dmask_ref, qdec_ref, kdec_ref, gc_ref,
                      gn_ref, o_ref, s_ref):
    @pl.when(pl.program_id(1) == 0)
    def _():
        s_ref[...] = s0_ref[...]

    for h in range(N_HEADS):
        sl = slice(h * D_HEAD, (h + 1) * D_HEAD)
        q = q_ref[:, sl]
        k = k_ref[:, sl]
        v = v_ref[:, sl]
        S = s_ref[0, h]
        scores = _dot_nt(q, k) * dmask_ref[h]
        intra = _dot(scores.astype(BF16), v)
        cross = _dot(q, S.astype(BF16)) * qdec_ref[:, sl]
        o = intra + cross
        kd_t = (k.astype(F32) * kdec_ref[:, sl]).T.astype(BF16)
        s_ref[0, h] = gc_ref[h] * S + _dot(kd_t, v)
        mu = jnp.mean(o, axis=-1, keepdims=True)
        d = o - mu
        var = jnp.mean(d * d, axis=-1, keepdims=True)
        y = d * lax.rsqrt(var + EPS) * gn_ref[:, sl] * g_ref[:, sl].astype(F32)
        o_ref[:, sl] = y.astype(o_ref.dtype)


def _retention_tables(C):
    log_gamma = jnp.log1p(-jnp.exp2(-5.0 - jnp.arange(N_HEADS, dtype=F32)))
    idx = jnp.arange(C, dtype=F32)
    diff = idx[:, None] - idx[None, :]
    dmask = jnp.where(diff[None] >= 0,
                      jnp.exp(log_gamma[:, None, None] * jnp.maximum(diff, 0.0)[None]), 0.0)
    q_dec = jnp.exp(log_gamma[None, :] * (idx + 1.0)[:, None])
    k_dec = jnp.exp(log_gamma[None, :] * (C - 1.0 - idx)[:, None])
    g_c = jnp.exp(log_gamma * C)
    rep = lambda t: jnp.repeat(t, D_HEAD, axis=-1)
    return dmask, rep(q_dec), rep(k_dec), rep(g_c[:, None])[:, None, :]


def retention(proj, s0, gn, *, B, L, C):
    nc = L // C
    dmask, qdec, kdec, gc = _retention_tables(C)
    row = lambda col: pl.BlockSpec((C, MIX_W), lambda b, c: (b * nc + c, col))
    const = lambda shape: pl.BlockSpec(shape, lambda b, c: (0,) * len(shape))
    state_spec = pl.BlockSpec((1, N_HEADS, D_HEAD, D_HEAD), lambda b, c: (b, 0, 0, 0))
    return pl.pallas_call(
        _retention_kernel,
        grid=(B, nc),
        in_specs=[row(0), row(1), row(2), row(3), state_spec,
                  const((N_HEADS, C, C)), const((C, MIX_W)), const((C, MIX_W)),
                  const((N_HEADS, 1, D_HEAD)), const((1, MIX_W))],
        out_specs=[pl.BlockSpec((C, MIX_W), lambda b, c: (b * nc + c, 0)), state_spec],
        out_shape=[jax.ShapeDtypeStruct((B * L, MIX_W), BF16),
                   jax.ShapeDtypeStruct((B, N_HEADS, D_HEAD, D_HEAD), F32)],
        compiler_params=_cparams(("arbitrary", "arbitrary")),
        name="retention",
    )(proj, proj, proj, proj, s0, dmask, qdec, kdec, gc, gn.reshape(1, MIX_W))


def _attend_two_piece(q, k_a, k_b, v_a, v_b, bias_a, bias_b, a_valid):
    s_a = _dot_nt(q, k_a) + bias_a
    if a_valid is not None:
        s_a = jnp.where(a_valid, s_a, NEG_INF)
    s_b = _dot_nt(q, k_b) + bias_b
    m = jnp.maximum(jnp.max(s_a, axis=-1, keepdims=True), jnp.max(s_b, axis=-1, keepdims=True))
    p_a = jnp.exp(s_a - m)
    p_b = jnp.exp(s_b - m)
    denom = jnp.sum(p_a, axis=-1, keepdims=True) + jnp.sum(p_b, axis=-1, keepdims=True)
    o = _dot(p_a.astype(BF16), v_a) + _dot(p_b.astype(BF16), v_b)
    return o / denom


def _band_prompt_kernel(q_ref, kp_ref, kc_ref, vp_ref, vc_ref, bias_ref, o_ref, s_buf, p_buf):
    @pl.when((pl.program_id(0) == 0) & (pl.program_id(1) == 0))
    def _():
        p_buf[...] = jnp.zeros_like(p_buf)

    prev_off = jnp.where(pl.program_id(1) > 0, 0.0, NEG_INF).astype(F32)
    for h in range(N_HEADS):
        sl = slice(h * D_HEAD, (h + 1) * D_HEAD)
        q = q_ref[:, sl]
        s_buf[:, :BAND_TQ] = _dot_nt(q, kp_ref[:, sl]) + prev_off
        s_buf[:, BAND_TQ:] = _dot_nt(q, kc_ref[:, sl])
        denoms = []
        for s in range(BAND_TQ // BAND_SUB):
            r0 = s * BAND_SUB
            sc = s_buf[r0:r0 + BAND_SUB, r0:r0 + BAND_WIN] + bias_ref[h]
            p = jnp.exp(sc - jnp.max(sc, axis=-1, keepdims=True))
            denoms.append(jnp.sum(p, axis=-1, keepdims=True))
            p_buf[r0:r0 + BAND_SUB, r0:r0 + BAND_WIN] = p.astype(BF16)
        o = _dot(p_buf[:, :BAND_TQ], vp_ref[:, sl]) + _dot(p_buf[:, BAND_TQ:], vc_ref[:, sl])
        o_ref[:, sl] = (o / jnp.concatenate(denoms, axis=0)).astype(o_ref.dtype)


def _band_prompt_bias(bias_table):
    r = jnp.arange(BAND_SUB, dtype=jnp.int32)[:, None]
    j = jnp.arange(BAND_WIN, dtype=jnp.int32)[None, :]
    q_chunk = r // CHUNK + N_PREV_CHUNKS
    k_chunk = j // CHUNK
    allowed = (k_chunk <= q_chunk) & (k_chunk >= q_chunk - N_PREV_CHUNKS)
    n_diag = BAND_SUB + BAND_WIN - 1
    d = (BAND_PAST + BAND_SUB - 1) - jnp.arange(n_diag, dtype=jnp.int32)
    diag = bias_table.astype(F32)[:, jnp.clip(d, -REL_CLIP, REL_CLIP) + REL_CLIP]
    cut = jnp.tile(diag, (1, BAND_SUB + 1))[:, :BAND_SUB * (n_diag + 1)].reshape(N_HEADS, BAND_SUB, n_diag + 1)
    rows = cut[:, ::-1, :BAND_WIN]
    return jnp.where(allowed[None], rows, NEG_INF)


def band_prompt(proj, bias_table, *, B, L):
    nq = L // BAND_TQ
    cur = lambda col: pl.BlockSpec((BAND_TQ, MIX_W), lambda b, i: (b * nq + i, col))
    prev = lambda col: pl.BlockSpec((BAND_TQ, MIX_W), lambda b, i: (b * nq + jnp.maximum(i - 1, 0), col))
    return pl.pallas_call(
        _band_prompt_kernel,
        grid=(B, nq),
        in_specs=[cur(0), prev(1), cur(1), prev(2), cur(2),
                  pl.BlockSpec((N_HEADS, BAND_SUB, BAND_WIN), lambda b, i: (0, 0, 0))],
        out_specs=pl.BlockSpec((BAND_TQ, MIX_W), lambda b, i: (b * nq + i, 0)),
        out_shape=jax.ShapeDtypeStruct((B * L, MIX_W), BF16),
        scratch_shapes=[pltpu.VMEM((BAND_TQ, 2 * BAND_TQ), F32), pltpu.VMEM((BAND_TQ, 2 * BAND_TQ), BF16)],
        compiler_params=_cparams(("arbitrary", "arbitrary")),
        name="band_prompt",
    )(proj, proj, proj, proj, proj, _band_prompt_bias(bias_table))


def _band_sample_kernel(q_ref, kp_ref, kc_ref, vp_ref, vc_ref, bias_p_ref, bias_c_ref, o_ref):
    n_past = kp_ref.shape[1] // N_HEADS
    for h in range(N_HEADS):
        sl = slice(h * D_HEAD, (h + 1) * D_HEAD)
        kp = kp_ref.at[0][pl.ds(h, n_past, stride=N_HEADS), :].astype(BF16)
        vp = vp_ref.at[0][pl.ds(h, n_past, stride=N_HEADS), :].astype(BF16)
        o = _attend_two_piece(q_ref[:, sl], kp, kc_ref[:, sl], vp, vc_ref[:, sl], bias_p_ref[h], bias_c_ref[h], None)
        o_ref[:, sl] = o.astype(o_ref.dtype)


def band_sample(proj, past_k, past_v, bias_table, *, B, L):
    P = past_k.shape[1]
    qpos = PAST_LEN + jnp.arange(L, dtype=jnp.int32)
    kpos = jnp.concatenate([PAST_LEN - P + jnp.arange(P, dtype=jnp.int32), qpos])
    rel = jnp.clip(qpos[:, None] - kpos[None, :], -REL_CLIP, REL_CLIP) + REL_CLIP
    qc = (qpos // CHUNK)[:, None]
    kc = (kpos // CHUNK)[None, :]
    allowed = (kpos[None, :] >= 0) & (kc <= qc) & (kc >= qc - N_PREV_CHUNKS)
    bias = jnp.where(allowed[None], bias_table.astype(F32)[:, rel], NEG_INF)
    cur = lambda col: pl.BlockSpec((L, MIX_W), lambda b: (b, col))
    past = pl.BlockSpec((1, P * N_HEADS, D_HEAD), lambda b: (b, 0, 0))
    past_k = past_k.reshape(B, P * N_HEADS, D_HEAD)
    past_v = past_v.reshape(B, P * N_HEADS, D_HEAD)
    return pl.pallas_call(
        _band_sample_kernel,
        grid=(B,),
        in_specs=[cur(0), past, cur(1), past, cur(2),
                  pl.BlockSpec((N_HEADS, L, P), lambda b: (0, 0, 0)),
                  pl.BlockSpec((N_HEADS, L, L), lambda b: (0, 0, 0))],
        out_specs=pl.BlockSpec((L, MIX_W), lambda b: (b, 0)),
        out_shape=jax.ShapeDtypeStruct((B * L, MIX_W), BF16),
        compiler_params=_cparams(("arbitrary",)),
        name="band_sample",
    )(proj, past_k, proj, past_v, proj, bias[:, :, :P], bias[:, :, P:])


def _out_proj_kernel(o_ref, qm_ref, mk_ref, mv_ref, w_ref, x_ref, gf_ref, wr_ref, br_ref, tri_ref, cnt0_ref,
                     x1_ref, h2_ref, route_ref, cnt_ref):
    @pl.when(pl.program_id(0) == 0)
    def _():
        cnt_ref[...] = cnt0_ref[...]

    parts = [o_ref[...]]
    for h in range(H_MEM):
        sl = slice(h * D_HEAD, (h + 1) * D_HEAD)
        mk = mk_ref.at[0, 0][pl.ds(h, N_MEM, stride=H_MEM), :].astype(BF16)
        mv = mv_ref.at[0, 0][pl.ds(h, N_MEM, stride=H_MEM), :].astype(BF16)
        s = _dot_nt(qm_ref[:, sl], mk)
        p = jnp.exp(s - jnp.max(s, axis=-1, keepdims=True))
        om = _dot(p.astype(BF16), mv) / jnp.sum(p, axis=-1, keepdims=True)
        parts.append(om.astype(BF16))
    x1 = x_ref[...] + _dot(jnp.concatenate(parts, axis=1), w_ref[...])
    x1_ref[...] = x1
    hn = x1 * lax.rsqrt(jnp.mean(x1 * x1, axis=-1, keepdims=True) + EPS) * gf_ref[...]
    _store_token_tiles(h2_ref, 0, hn)

    logits = _dot(hn.astype(BF16), wr_ref[...]) + br_ref[...]
    tm = logits.shape[0]
    rows = min(tm, ROUTE_ROWS)
    tops = [_route_top2(logits[r0:r0 + rows]) for r0 in range(0, tm, rows)]

    onehot = jnp.concatenate([jnp.where(t[4] | t[5], 1.0, 0.0) for t in tops], axis=0)
    before = cnt_ref[0:1, :] + _dot(tri_ref[...], onehot.astype(BF16))
    cnt_ref[0:1, :] = cnt_ref[0:1, :] + jnp.sum(onehot, axis=0, keepdims=True)
    lane = lax.broadcasted_iota(jnp.int32, (rows, LANES), 1)
    for g, (i1, i2, w1, w2, oh1, oh2) in enumerate(tops):
        b = before[g * rows:(g + 1) * rows]
        r1 = jnp.sum(jnp.where(oh1, b, 0.0), axis=-1, keepdims=True)
        r2 = jnp.sum(jnp.where(oh2, b, 0.0), axis=-1, keepdims=True)
        route_ref[g * rows:(g + 1) * rows, :] = jnp.where(lane == 0, i1, jnp.where(lane == 1, i2, jnp.where(
            lane == 2, w1, jnp.where(lane == 3, w2, jnp.where(lane == 4, r1, jnp.where(lane == 5, r2, 0.0))))))


def _route_top2(logits):
    lane = lax.broadcasted_iota(jnp.int32, logits.shape, 1)
    lane_f = lane.astype(F32)
    far = float(LANES)
    is_group = (lane >= ROUTER_GROUP_LANE0) & (lane < ROUTER_GROUP_LANE0 + N_GROUPS)
    gl = jnp.where(is_group, logits, NEG_INF)
    g_max = jnp.max(gl, axis=-1, keepdims=True)
    g_sel = jnp.min(jnp.where(gl == g_max, lane_f, far), axis=-1, keepdims=True) - float(ROUTER_GROUP_LANE0)
    g_gate = 1.0 / jnp.sum(jnp.exp(gl - g_max), axis=-1, keepdims=True)
    in_group = jnp.right_shift(lane, 3).astype(F32) == g_sel
    el = jnp.where(in_group, logits, NEG_INF)
    m1 = jnp.max(el, axis=-1, keepdims=True)
    i1 = jnp.min(jnp.where(el == m1, lane_f, far), axis=-1, keepdims=True)
    el2 = jnp.where(lane_f == i1, NEG_INF, el)
    m2 = jnp.max(el2, axis=-1, keepdims=True)
    i2 = jnp.min(jnp.where(el2 == m2, lane_f, far), axis=-1, keepdims=True)
    e2 = jnp.exp(m2 - m1)
    p1 = 1.0 / (1.0 + e2)
    return i1, i2, g_gate * p1, g_gate * (e2 * p1), lane_f == i1, lane_f == i2


def out_proj(mixed, proj, qm_block, mem_k, mem_v, layer, w_out, x, g_ffn, w_router, b_router, counts0, *, tm,
             rows_per_stream):
    T = x.shape[0]
    tiles_per_stream = rows_per_stream // tm
    row = lambda width, col: pl.BlockSpec((tm, width), lambda i: (i, col))
    const = lambda shape: pl.BlockSpec(shape, lambda i: (0,) * len(shape))
    mem = pl.BlockSpec((1, 1, N_MEM * H_MEM, D_HEAD), lambda i: (layer, i // tiles_per_stream, 0, 0))
    mem_k = mem_k.reshape(mem_k.shape[:2] + (N_MEM * H_MEM, D_HEAD))
    mem_v = mem_v.reshape(mem_v.shape[:2] + (N_MEM * H_MEM, D_HEAD))
    tri = (jnp.arange(tm)[None, :] < jnp.arange(tm)[:, None]).astype(BF16)
    return pl.pallas_call(
        _out_proj_kernel,
        grid=(T // tm,),
        in_specs=[row(MIX_W, 0), row(MEM_W, qm_block), mem, mem, const((MIX_W + MEM_W, D_MODEL)),
                  row(D_MODEL, 0), const((1, D_MODEL)), const((D_MODEL, LANES)), const((1, LANES)),
                  const((tm, tm)), const((SUBLANES, LANES))],
        out_specs=[row(D_MODEL, 0), pl.BlockSpec((tm * ROW_TILE, LANES), lambda i: (i, 0)), row(LANES, 0),
                   const((SUBLANES, LANES))],
        out_shape=[jax.ShapeDtypeStruct((T, D_MODEL), F32), jax.ShapeDtypeStruct((T * ROW_TILE, LANES), F32),
                   jax.ShapeDtypeStruct((T, LANES), F32), jax.ShapeDtypeStruct((SUBLANES, LANES), F32)],
        compiler_params=_cparams(("arbitrary",)),
        name="out_proj",
    )(mixed, proj, mem_k, mem_v, w_out, x, g_ffn.reshape(1, D_MODEL), w_router, b_router, tri, counts0)


def _dispatch_kernel(tms, steps, pe_ref, pos_ref, *rest):
    h2_refs = rest[:len(tms)]
    xpad_hbm, zbuf, sem, zsem = rest[len(tms):]
    i = pl.program_id(0)

    @pl.when(i == 0)
    def _():
        zbuf[...] = jnp.zeros_like(zbuf)

        def block_copy(first_row):
            start = pl.multiple_of(first_row * ROW_TILE, MOE_BM * ROW_TILE)
            return pltpu.make_async_copy(zbuf, xpad_hbm.at[pl.ds(start, MOE_BM * ROW_TILE)], zsem)

        def tail_copy(e):
            return block_copy(pe_ref[e] - MOE_BM)

        def nonempty(e):
            return pe_ref[e] > (pe_ref[e - 1] if e > 0 else 0)

        for e in range(N_EXPERTS):
            pl.when(nonempty(e))(lambda e=e: tail_copy(e).start())
        for e in range(N_EXPERTS):
            pl.when(nonempty(e))(lambda e=e: tail_copy(e).wait())

        def fill(b, carry):
            cp = block_copy(b * MOE_BM)
            cp.start()
            cp.wait()
            return carry

        lax.fori_loop(pe_ref[N_EXPERTS - 1] // MOE_BM, xpad_hbm.shape[0] // (MOE_BM * ROW_TILE), fill, 0)

    def scatter_rows(h2_ref, tm):
        def body(r, carry):
            src = _token_tile(h2_ref, r)
            pltpu.make_async_copy(src, _token_tile(xpad_hbm, pos_ref[2 * r]), sem).start(priority=0)
            pltpu.make_async_copy(src, _token_tile(xpad_hbm, pos_ref[2 * r + 1]), sem).start(priority=1)
            return carry

        lax.fori_loop(0, tm, body, 0, unroll=8)
        for _ in range(2):
            pltpu.make_async_copy(h2_ref, xpad_hbm.at[pl.ds(0, tm * ROW_TILE)], sem).wait()

    first = 0
    for h2_ref, tm, n in zip(h2_refs, tms, steps):
        pl.when(jnp.logical_and(i >= first, i < first + n))(functools.partial(scatter_rows, h2_ref, tm))
        first += n


def moe_dispatch(pad_ends, pos_groups, h2_groups, n_rows):
    sizes = [h2.shape[0] // ROW_TILE for h2 in h2_groups]
    tms = [min(T, DISPATCH_TM) for T in sizes]
    steps = [T // tm for T, tm in zip(sizes, tms)]
    firsts = [sum(steps[:g]) for g in range(len(steps))]
    pos = jnp.concatenate([jnp.pad(p, (0, n * 2 * DISPATCH_TM - p.shape[0])) for p, n in zip(pos_groups, steps)])

    def rows(tm, first, n):
        return pl.BlockSpec((tm * ROW_TILE, LANES), lambda i, pe: (jnp.clip(i - first, 0, n - 1), 0))

    grid_spec = pltpu.PrefetchScalarGridSpec(
        num_scalar_prefetch=1,
        grid=(sum(steps),),
        in_specs=[pl.BlockSpec((2 * DISPATCH_TM,), lambda i, pe: (i,), memory_space=pltpu.SMEM)]
        + [rows(tm, first, n) for tm, first, n in zip(tms, firsts, steps)],
        out_specs=pl.BlockSpec(memory_space=pl.ANY),
        scratch_shapes=[pltpu.VMEM((MOE_BM * ROW_TILE, LANES), F32), pltpu.SemaphoreType.DMA(()),
                        pltpu.SemaphoreType.DMA(())],
    )
    return pl.pallas_call(
        functools.partial(_dispatch_kernel, tuple(tms), tuple(steps)),
        grid_spec=grid_spec,
        out_shape=jax.ShapeDtypeStruct((n_rows * ROW_TILE, LANES), F32),
        compiler_params=_cparams(("arbitrary",)),
        name="moe_dispatch",
    )(pad_ends, pos, *h2_groups)


def _moe_kernel(be_ref, nu_ref, x_ref, wg_ref, wu_ref, wd_ref, y_ref, wg_s, wu_s, wd_s):
    i = pl.program_id(0)
    used = i < nu_ref[0]
    new_expert = jnp.logical_or(i == 0, be_ref[i] != be_ref[jnp.maximum(i - 1, 0)])

    @pl.when(jnp.logical_and(used, new_expert))
    def _():
        wg_s[...] = wg_ref[0, 0].astype(BF16)
        wu_s[...] = wu_ref[0, 0].astype(BF16)
        wd_s[...] = wd_ref[0, 0].astype(BF16)

    @pl.when(used)
    def _():
        x = _load_token_tiles(x_ref, 0, MOE_BM).astype(BF16)
        g = _dot(x, wg_s[...])
        u = _dot(x, wu_s[...])
        a = (g * _sigmoid(g) * u).astype(BF16)
        _store_token_tiles(y_ref, 0, _dot(a, wd_s[...]))

    @pl.when(jnp.logical_not(used))
    def _():
        y_ref[...] = jnp.zeros_like(y_ref)


def moe_experts(x_pad, block_e, n_used, layer, wg, wu, wd):
    n_rows = x_pad.shape[0] // ROW_TILE
    rows = pl.BlockSpec((MOE_BM * ROW_TILE, LANES), lambda i, be, nu: (i, 0))
    w_in = pl.BlockSpec((1, 1, D_MODEL, D_FF_EXPERT), lambda i, be, nu: (layer, be[i], 0, 0))
    w_dn = pl.BlockSpec((1, 1, D_FF_EXPERT, D_MODEL), lambda i, be, nu: (layer, be[i], 0, 0))
    grid_spec = pltpu.PrefetchScalarGridSpec(
        num_scalar_prefetch=2,
        grid=(n_rows // MOE_BM,),
        in_specs=[rows, w_in, w_in, w_dn],
        out_specs=rows,
        scratch_shapes=[pltpu.VMEM((D_MODEL, D_FF_EXPERT), BF16), pltpu.VMEM((D_MODEL, D_FF_EXPERT), BF16),
                        pltpu.VMEM((D_FF_EXPERT, D_MODEL), BF16)],
    )
    return pl.pallas_call(
        _moe_kernel,
        grid_spec=grid_spec,
        out_shape=jax.ShapeDtypeStruct((n_rows * ROW_TILE, LANES), F32),
        compiler_params=_cparams(("arbitrary",)),
        name="moe_experts",
    )(block_e, n_used, x_pad, wg, wu, wd)


def _combine_kernel(tm, n_steps, pos_ref, pos_next_ref, route_ref, x1_ref, ypad_hbm, o_ref, ybuf, sems):
    i = pl.program_id(0)
    slot = lax.rem(i, 2)

    def gather(p_ref, s):
        buf = ybuf.at[s]

        def body(r, carry):
            pltpu.make_async_copy(_token_tile(ypad_hbm, p_ref[2 * r]), _token_tile(buf, r), sems.at[s]).start(
                priority=0)
            pltpu.make_async_copy(_token_tile(ypad_hbm, p_ref[2 * r + 1]), _token_tile(buf, tm + r),
                                  sems.at[s]).start(priority=1)
            return carry

        lax.fori_loop(0, tm, body, 0, unroll=8)

    @pl.when(i == 0)
    def _():
        gather(pos_ref, 0)

    @pl.when(i + 1 < n_steps)
    def _():
        gather(pos_next_ref, 1 - slot)

    buf = ybuf.at[slot]
    pltpu.make_async_copy(ypad_hbm.at[pl.ds(0, 2 * tm * ROW_TILE)], buf, sems.at[slot]).wait()
    g1 = route_ref[:, 2:3]
    g2 = route_ref[:, 3:4]
    o_ref[...] = x1_ref[...] + (g1 * _load_token_tiles(buf, 0, tm) + g2 * _load_token_tiles(buf, tm, tm))


def moe_combine(pos, route, x1, y_pad):
    T = x1.shape[0]
    tm = min(T, COMBINE_TM)
    n_steps = T // tm
    return pl.pallas_call(
        functools.partial(_combine_kernel, tm, n_steps),
        grid=(n_steps,),
        in_specs=[pl.BlockSpec((2 * tm,), lambda i: (i,), memory_space=pltpu.SMEM),
                  pl.BlockSpec((2 * tm,), lambda i: (jnp.minimum(i + 1, n_steps - 1),), memory_space=pltpu.SMEM),
                  pl.BlockSpec((tm, LANES), lambda i: (i, 0)),
                  pl.BlockSpec((tm, D_MODEL), lambda i: (i, 0)),
                  pl.BlockSpec(memory_space=pl.ANY)],
        out_specs=pl.BlockSpec((tm, D_MODEL), lambda i: (i, 0)),
        out_shape=jax.ShapeDtypeStruct((T, D_MODEL), F32),
        scratch_shapes=[pltpu.VMEM((2, 2 * tm * ROW_TILE, LANES), F32), pltpu.SemaphoreType.DMA((2,))],
        compiler_params=_cparams(("arbitrary",)),
        name="moe_combine",
    )(pos, pos, route, x1, y_pad)


def hier_moe(groups, counts, layer, wg, wu, wd):
    sizes = [x1.shape[0] for x1, _, _ in groups]
    n_assign = 2 * sum(sizes)
    n_blocks = (n_assign + N_EXPERTS * (MOE_BM - 1) + MOE_BM - 1) // MOE_BM
    head = jnp.concatenate([route[:, :6] for _, _, route in groups], axis=0)
    experts = head[:, 0:2].astype(jnp.int32)
    ranks = head[:, 4:6].astype(jnp.int32)
    cnt = counts[0, :N_EXPERTS].astype(jnp.int32)
    padded = ((cnt + MOE_BM - 1) // MOE_BM) * MOE_BM
    pad_ends = jnp.cumsum(padded).astype(jnp.int32)
    pad_starts = pad_ends - padded
    is_e = experts[:, :, None] == jnp.arange(N_EXPERTS, dtype=jnp.int32)
    pos = (ranks + jnp.sum(jnp.where(is_e, pad_starts, 0), axis=-1)).reshape(-1)
    block_start = jnp.arange(n_blocks, dtype=jnp.int32) * MOE_BM
    block_e = jnp.minimum(jnp.sum(block_start[:, None] >= pad_ends[None, :], axis=1), N_EXPERTS - 1).astype(jnp.int32)
    n_used = (pad_ends[-1:] // MOE_BM).astype(jnp.int32)
    starts = [2 * sum(sizes[:g]) for g in range(len(groups))]
    pos_g = [pos[s:s + 2 * n] for s, n in zip(starts, sizes)]
    x_pad = moe_dispatch(pad_ends, pos_g, [h2 for _, h2, _ in groups], n_blocks * MOE_BM)
    y_pad = moe_experts(x_pad, block_e, n_used, layer, wg, wu, wd)
    return [moe_combine(p, route, x1, y_pad) for (x1, _, route), p in zip(groups, pos_g)]


def _rope_tables(pos):
    half = D_HEAD // 2
    inv = ROPE_BASE ** (-jnp.arange(half, dtype=F32) / half)
    ang = pos.astype(F32)[:, None] * inv[None, :]
    cos, sin = jnp.cos(ang), jnp.sin(ang)
    return jnp.concatenate([cos, cos], axis=-1), jnp.concatenate([-sin, sin], axis=-1)


def _router_weights(w_group, b_group, w_router, b_router):
    w = jnp.zeros((D_MODEL, LANES), F32)
    w = w.at[:, :N_EXPERTS].set(w_router).at[:, ROUTER_GROUP_LANE0:ROUTER_GROUP_LANE0 + N_GROUPS].set(w_group)
    b = jnp.zeros((1, LANES), F32)
    b = b.at[0, :N_EXPERTS].set(b_router).at[0, ROUTER_GROUP_LANE0:ROUTER_GROUP_LANE0 + N_GROUPS].set(b_group)
    return w.astype(BF16), b


def _tile(v, n):
    return jnp.tile(v.astype(F32), n)


def _run_trunk(x, B, L, mem_k, mem_v, ret_state0, band_past, P):
    prompt = band_past is None
    tm = 512 if prompt else L
    proj_tm = 512
    scale = D_HEAD ** -0.5
    ones = lambda n: jnp.ones((n * D_HEAD,), F32)
    if prompt:
        cos2, sin2 = _rope_tables(jnp.arange(L, dtype=jnp.int32))
        tab_map = lambda i: i % (L // proj_tm)
    else:
        cos2, sin2 = _rope_tables(PAST_LEN + jnp.arange(L, dtype=jnp.int32))
        cos2, sin2 = jnp.tile(cos2, (B, 1)), jnp.tile(sin2, (B, 1))
        tab_map = lambda i: i
    out = {}

    def ffn(layer, mixed, proj, qm_block, w_out, x):
        def run(counts0):
            return out_proj(
                mixed, proj, qm_block, mem_k, mem_v, layer, w_out.astype(BF16), x, P["norm_ffn"][layer],
                *_router_weights(P["w_group"][layer], P["b_group"][layer], P["w_router"][layer],
                                 P["b_router"][layer]), counts0, tm=tm, rows_per_stream=L)
        return (yield run)

    segs0 = (("rope", N_HEADS), ("rope", N_HEADS), ("plain", N_HEADS), ("silu", N_HEADS), ("norm", H_MEM))
    gain0 = jnp.concatenate([ones(N_HEADS), ones(N_HEADS) * scale, ones(2 * N_HEADS),
                             _tile(P["q_norm_mem"][0], H_MEM) * scale])
    proj = norm_proj(x, P["norm_mix"][0], P["w_in_ret"][0].astype(BF16), gain0, segs0, tm=proj_tm,
                     out_dtype=BF16, rope=(cos2, sin2, tab_map))
    mixed, S = retention(proj, ret_state0, P["gn_ret"][0], B=B, L=L, C=RET_CHUNK if prompt else L)
    out["ret_state"] = S
    x = yield from ffn(0, mixed, proj, (4 * MIX_W) // MEM_W, P["w_out_ret"][0], x)

    segs1 = (("norm", N_HEADS), ("norm", N_HEADS), ("plain", N_HEADS), ("norm", H_MEM))
    gain1 = jnp.concatenate([_tile(P["q_norm_att"][0], N_HEADS) * scale, _tile(P["k_norm_att"][0], N_HEADS),
                             ones(N_HEADS), _tile(P["q_norm_mem"][1], H_MEM) * scale])
    w_in = P["w_in_att"][0].astype(BF16)
    if prompt:
        proj = norm_proj(x, P["norm_mix"][1], w_in, gain1, segs1, tm=proj_tm, out_dtype=BF16)
        keep = min(BAND_PAST, L)
        tiles = L // keep
        kv = norm_proj(x, P["norm_mix"][1], w_in[:, MIX_W:3 * MIX_W], gain1[MIX_W:3 * MIX_W], segs1[1:3],
                       tm=keep, out_dtype=F32, n_tiles=B, row_map=lambda i: i * tiles + tiles - 1)
        out["band_k"] = kv[:, :MIX_W].reshape(B, keep, N_HEADS, D_HEAD)
        out["band_v"] = kv[:, MIX_W:].reshape(B, keep, N_HEADS, D_HEAD)
        mixed = band_prompt(proj, P["rel_bias_att"][0], B=B, L=L)
    else:
        proj_f = norm_proj(x, P["norm_mix"][1], w_in, gain1, segs1, tm=proj_tm, out_dtype=F32)
        out["band_k"] = proj_f[:, MIX_W:2 * MIX_W].reshape(B, L, N_HEADS, D_HEAD)
        out["band_v"] = proj_f[:, 2 * MIX_W:3 * MIX_W].reshape(B, L, N_HEADS, D_HEAD)
        proj = proj_f.astype(BF16)
        mixed = band_sample(proj, band_past[0], band_past[1], P["rel_bias_att"][0], B=B, L=L)
    out["y"] = yield from ffn(1, mixed, proj, (3 * MIX_W) // MEM_W, P["w_out_att"][0], x)
    return out


def kernel(x_prompt, x_sample, mem_prompt, state_ret, cache_band_k, cache_band_v, cache_mem_k, cache_mem_v, norm_mix, norm_ffn, norm_mem, w_in_ret, gn_ret, w_out_ret, w_in_att, q_norm_att, k_norm_att, rel_bias_att, w_out_att, w_mem_kv, q_norm_mem, k_norm_mem, w_group, b_group, w_router, b_router, w_e_gate, w_e_up, w_e_down):
    P = {"norm_mix": norm_mix, "norm_ffn": norm_ffn, "w_in_ret": w_in_ret, "gn_ret": gn_ret,
         "w_out_ret": w_out_ret, "w_in_att": w_in_att, "q_norm_att": q_norm_att,
         "k_norm_att": k_norm_att, "rel_bias_att": rel_bias_att, "w_out_att": w_out_att,
         "q_norm_mem": q_norm_mem, "w_group": w_group, "b_group": b_group, "w_router": w_router,
         "b_router": b_router, "w_e_gate": w_e_gate, "w_e_up": w_e_up, "w_e_down": w_e_down}
    B, L, D = x_prompt.shape
    Bd, Ld, _ = x_sample.shape
    depth = norm_mix.shape[0]

    mem_k_p, mem_v_p = [], []
    for i in range(depth):
        gain = jnp.concatenate([_tile(k_norm_mem[i], H_MEM), jnp.ones((MEM_W,), F32)])
        kv = norm_proj(mem_prompt.reshape(B * N_MEM, D), norm_mem[i], w_mem_kv[i].astype(BF16), gain,
                       (("norm", H_MEM), ("plain", H_MEM)), tm=N_MEM, out_dtype=F32)
        mem_k_p.append(kv[:, :MEM_W].reshape(B, N_MEM, H_MEM, D_HEAD))
        mem_v_p.append(kv[:, MEM_W:].reshape(B, N_MEM, H_MEM, D_HEAD))
    mem_k_p = jnp.stack(mem_k_p)
    mem_v_p = jnp.stack(mem_v_p)

    zeros_state = jnp.zeros((B, N_HEADS, D_HEAD, D_HEAD), F32)
    trunks = [_run_trunk(x_prompt.reshape(B * L, D), B, L, mem_k_p, mem_v_p, zeros_state, None, P),
              _run_trunk(x_sample.reshape(Bd * Ld, D), Bd, Ld, cache_mem_k, cache_mem_v, state_ret[0],
                         (cache_band_k[0], cache_band_v[0]), P)]
    pending = [next(t) for t in trunks]
    results = []
    for layer in range(depth):
        counts = jnp.zeros((SUBLANES, LANES), F32)
        groups = []
        for run in pending:
            x1, h2, route, counts = run(counts)
            groups.append((x1, h2, route))
        outs = hier_moe(groups, counts, layer, w_e_gate, w_e_up, w_e_down)
        pending = []
        for t, x2 in zip(trunks, outs):
            try:
                pending.append(t.send(x2))
            except StopIteration as done:
                results.append(done.value)
    res_p, res_s = results

    return (res_p["y"].reshape(B, L, D), res_s["y"].reshape(Bd, Ld, D),
            res_p["ret_state"][None], res_s["ret_state"][None],
            res_p["band_k"][None], res_p["band_v"][None], res_s["band_k"][None], res_s["band_v"][None],
            mem_k_p, mem_v_p)
```

```python
import functools

import jax
import jax.numpy as jnp
from jax import lax
from jax.experimental import pallas as pl
from jax.experimental.pallas import tpu as pltpu

F32 = jnp.float32
BF16 = jnp.bfloat16

D_MODEL = 1024
D_HEAD = 128
CHUNK = 64
N_HEADS = 8
H_MEM = 4
N_MEM = 256
MEM_W = H_MEM * D_HEAD
MIX_W = N_HEADS * D_HEAD
N_PREV_CHUNKS = 8
BAND_PAST = N_PREV_CHUNKS * CHUNK
REL_CLIP = 128
N_GROUPS = 4
EXPERTS_PER_GROUP = 8
N_EXPERTS = N_GROUPS * EXPERTS_PER_GROUP
D_FF_EXPERT = 512
ROPE_BASE = 10000.0
EPS = 1e-6
NEG_INF = -1e30
PAST_LEN = 1024

LANES = 128
SUBLANES = 8

RET_CHUNK = 256
BAND_TQ = 512
BAND_SUB = 128
BAND_WIN = BAND_PAST + BAND_SUB
MOE_BM = 512
DISPATCH_TM = 2048
COMBINE_TM = 1024
ROUTE_ROWS = 128
ROUTER_GROUP_LANE0 = N_EXPERTS
VMEM_LIMIT = 56 * 1024 * 1024


def _cparams(sem):
    return pltpu.CompilerParams(dimension_semantics=sem, vmem_limit_bytes=VMEM_LIMIT)


def _dot(a, b):
    return jnp.dot(a, b, preferred_element_type=F32)


def _dot_nt(a, b):
    return lax.dot_general(a, b, (((1,), (1,)), ((), ())), preferred_element_type=F32)


def _sigmoid(x):
    return 1.0 / (1.0 + jnp.exp(-x))


ROW_TILE = D_MODEL // LANES
assert ROW_TILE == SUBLANES


def _load_token_tiles(ref, t0, n):
    return jnp.concatenate([ref[pl.ds(t0 * ROW_TILE + j, n, stride=ROW_TILE), :] for j in range(ROW_TILE)], axis=1)


def _store_token_tiles(ref, t0, val):
    n = val.shape[0]
    for j in range(ROW_TILE):
        ref[pl.ds(t0 * ROW_TILE + j, n, stride=ROW_TILE), :] = val[:, j * LANES:(j + 1) * LANES]


def _token_tile(ref, t):
    return ref.at[pl.ds(pl.multiple_of(t * ROW_TILE, ROW_TILE), ROW_TILE)]


def _norm_proj_kernel(segs, has_rope, x_ref, g_ref, w_ref, gain_ref, *rest):
    if has_rope:
        cos_ref, sin_ref, o_ref = rest
    else:
        (o_ref,) = rest
    x = x_ref[...]
    ms = jnp.mean(x * x, axis=-1, keepdims=True)
    h = (x * lax.rsqrt(ms + EPS) * g_ref[...]).astype(BF16)
    col = 0
    for kind, n_heads in segs:
        width = n_heads * D_HEAD
        acc = _dot(h, w_ref[:, col:col + width])
        for j in range(n_heads):
            a = acc[:, j * D_HEAD:(j + 1) * D_HEAD]
            c0 = col + j * D_HEAD
            gain = gain_ref[:, c0:c0 + D_HEAD]
            if kind == "rope":
                rot = pltpu.roll(a, D_HEAD // 2, 1)
                out = (a * cos_ref[...] + rot * sin_ref[...]) * gain
            elif kind == "norm":
                out = a * lax.rsqrt(jnp.mean(a * a, axis=-1, keepdims=True) + EPS) * gain
            elif kind == "silu":
                out = a * _sigmoid(a)
            else:
                out = a
            o_ref[:, c0:c0 + D_HEAD] = out.astype(o_ref.dtype)
        col += width


def norm_proj(x, g, w, gain, segs, *, tm, out_dtype, n_tiles=None, row_map=None, rope=None):
    T, D = x.shape
    N = w.shape[1]
    assert sum(n for _, n in segs) * D_HEAD == N
    if n_tiles is None:
        n_tiles = T // tm
    if row_map is None:
        row_map = lambda i: i
    in_specs = [
        pl.BlockSpec((tm, D), lambda i: (row_map(i), 0)),
        pl.BlockSpec((1, D), lambda i: (0, 0)),
        pl.BlockSpec((D, N), lambda i: (0, 0)),
        pl.BlockSpec((1, N), lambda i: (0, 0)),
    ]
    args = [x, g.reshape(1, D), w, gain.reshape(1, N)]
    if rope is not None:
        cos2, sin2, tab_map = rope
        in_specs += [pl.BlockSpec((tm, D_HEAD), lambda i: (tab_map(i), 0)),
                     pl.BlockSpec((tm, D_HEAD), lambda i: (tab_map(i), 0))]
        args += [cos2, sin2]
    return pl.pallas_call(
        functools.partial(_norm_proj_kernel, segs, rope is not None),
        grid=(n_tiles,),
        in_specs=in_specs,
        out_specs=pl.BlockSpec((tm, N), lambda i: (i, 0)),
        out_shape=jax.ShapeDtypeStruct((n_tiles * tm, N), out_dtype),
        compiler_params=_cparams(("arbitrary",)),
        name="norm_proj",
    )(*args)


def _retention_kernel(q_ref, k_ref, v_ref, g_ref, s0_ref, dmask_ref, qdec_ref, kdec_ref, gc_ref,
                      gn_ref, o_ref, s_ref):
    @pl.when(pl.program_id(1) == 0)
    def _():
        s_ref[...] = s0_ref[...]

    for h in range(N_HEADS):
        sl = slice(h * D_HEAD, (h + 1) * D_HEAD)
        q = q_ref[:, sl]
        k = k_ref[:, sl]
        v = v_ref[:, sl]
        S = s_ref[0, h]
        scores = _dot_nt(q, k) * dmask_ref[h]
        intra = _dot(scores.astype(BF16), v)
        cross = _dot(q, S.astype(BF16)) * qdec_ref[:, sl]
        o = intra + cross
        kd = (k.astype(F32) * kdec_ref[:, sl]).astype(BF16)
        s_ref[0, h] = gc_ref[h] * S + lax.dot_general(kd, v, (((0,), (0,)), ((), ())), preferred_element_type=F32)
        mu = jnp.mean(o, axis=-1, keepdims=True)
        d = o - mu
        var = jnp.mean(d * d, axis=-1, keepdims=True)
        y = d * lax.rsqrt(var + EPS) * gn_ref[:, sl] * g_ref[:, sl].astype(F32)
        o_ref[:, sl] = y.astype(o_ref.dtype)


def _retention_tables(C):
    log_gamma = jnp.log1p(-jnp.exp2(-5.0 - jnp.arange(N_HEADS, dtype=F32)))
    idx = jnp.arange(C, dtype=F32)
    diff = idx[:, None] - idx[None, :]
    dmask = jnp.where(diff[None] >= 0,
                      jnp.exp(log_gamma[:, None, None] * jnp.maximum(diff, 0.0)[None]), 0.0)
    q_dec = jnp.exp(log_gamma[None, :] * (idx + 1.0)[:, None])
    k_dec = jnp.exp(log_gamma[None, :] * (C - 1.0 - idx)[:, None])
    g_c = jnp.exp(log_gamma * C)
    rep = lambda t: jnp.repeat(t, D_HEAD, axis=-1)
    return dmask, rep(q_dec), rep(k_dec), rep(g_c[:, None])[:, None, :]


def retention(proj, s0, gn, *, B, L, C):
    nc = L // C
    dmask, qdec, kdec, gc = _retention_tables(C)
    row = lambda col: pl.BlockSpec((C, MIX_W), lambda b, c: (b * nc + c, col))
    const = lambda shape: pl.BlockSpec(shape, lambda b, c: (0,) * len(shape))
    state_spec = pl.BlockSpec((1, N_HEADS, D_HEAD, D_HEAD), lambda b, c: (b, 0, 0, 0))
    return pl.pallas_call(
        _retention_kernel,
        grid=(B, nc),
        in_specs=[row(0), row(1), row(2), row(3), state_spec,
                  const((N_HEADS, C, C)), const((C, MIX_W)), const((C, MIX_W)),
                  const((N_HEADS, 1, D_HEAD)), const((1, MIX_W))],
        out_specs=[pl.BlockSpec((C, MIX_W), lambda b, c: (b * nc + c, 0)), state_spec],
        out_shape=[jax.ShapeDtypeStruct((B * L, MIX_W), BF16),
                   jax.ShapeDtypeStruct((B, N_HEADS, D_HEAD, D_HEAD), F32)],
        compiler_params=_cparams(("arbitrary", "arbitrary")),
        name="retention",
    )(proj, proj, proj, proj, s0, dmask, qdec, kdec, gc, gn.reshape(1, MIX_W))


def _attend_two_piece(q, k_a, k_b, v_a, v_b, bias_a, bias_b, a_valid):
    s_a = _dot_nt(q, k_a) + bias_a
    if a_valid is not None:
        s_a = jnp.where(a_valid, s_a, NEG_INF)
    s_b = _dot_nt(q, k_b) + bias_b
    m = jnp.maximum(jnp.max(s_a, axis=-1, keepdims=True), jnp.max(s_b, axis=-1, keepdims=True))
    p_a = jnp.exp(s_a - m)
    p_b = jnp.exp(s_b - m)
    denom = jnp.sum(p_a, axis=-1, keepdims=True) + jnp.sum(p_b, axis=-1, keepdims=True)
    o = _dot(p_a.astype(BF16), v_a) + _dot(p_b.astype(BF16), v_b)
    return o / denom


def _band_prompt_kernel(q_ref, kp_ref, kc_ref, vp_ref, vc_ref, bias_ref, o_ref, s_buf, p_buf):
    @pl.when((pl.program_id(0) == 0) & (pl.program_id(1) == 0))
    def _():
        p_buf[...] = jnp.zeros_like(p_buf)

    prev_off = jnp.where(pl.program_id(1) > 0, 0.0, NEG_INF).astype(F32)
    for h in range(N_HEADS):
        sl = slice(h * D_HEAD, (h + 1) * D_HEAD)
        q = q_ref[:, sl]
        s_buf[:, :BAND_TQ] = _dot_nt(q, kp_ref[:, sl]) + prev_off
        s_buf[:, BAND_TQ:] = _dot_nt(q, kc_ref[:, sl])
        denoms = []
        for s in range(BAND_TQ // BAND_SUB):
            r0 = s * BAND_SUB
            sc = s_buf[r0:r0 + BAND_SUB, r0:r0 + BAND_WIN] + bias_ref[h]
            p = jnp.exp(sc - jnp.max(sc, axis=-1, keepdims=True))
            denoms.append(jnp.sum(p, axis=-1, keepdims=True))
            p_buf[r0:r0 + BAND_SUB, r0:r0 + BAND_WIN] = p.astype(BF16)
        o = _dot(p_buf[:, :BAND_TQ], vp_ref[:, sl]) + _dot(p_buf[:, BAND_TQ:], vc_ref[:, sl])
        o_ref[:, sl] = (o / jnp.concatenate(denoms, axis=0)).astype(o_ref.dtype)


def _band_prompt_bias(bias_table):
    r = jnp.arange(BAND_SUB, dtype=jnp.int32)[:, None]
    j = jnp.arange(BAND_WIN, dtype=jnp.int32)[None, :]
    q_chunk = r // CHUNK + N_PREV_CHUNKS
    k_chunk = j // CHUNK
    allowed = (k_chunk <= q_chunk) & (k_chunk >= q_chunk - N_PREV_CHUNKS)
    n_diag = BAND_SUB + BAND_WIN - 1
    d = (BAND_PAST + BAND_SUB - 1) - jnp.arange(n_diag, dtype=jnp.int32)
    diag = bias_table.astype(F32)[:, jnp.clip(d, -REL_CLIP, REL_CLIP) + REL_CLIP]
    cut = jnp.tile(diag, (1, BAND_SUB + 1))[:, :BAND_SUB * (n_diag + 1)].reshape(N_HEADS, BAND_SUB, n_diag + 1)
    rows = cut[:, ::-1, :BAND_WIN]
    return jnp.where(allowed[None], rows, NEG_INF)


def band_prompt(proj, bias_table, *, B, L):
    nq = L // BAND_TQ
    cur = lambda col: pl.BlockSpec((BAND_TQ, MIX_W), lambda b, i: (b * nq + i, col))
    prev = lambda col: pl.BlockSpec((BAND_TQ, MIX_W), lambda b, i: (b * nq + jnp.maximum(i - 1, 0), col))
    return pl.pallas_call(
        _band_prompt_kernel,
        grid=(B, nq),
        in_specs=[cur(0), prev(1), cur(1), prev(2), cur(2),
                  pl.BlockSpec((N_HEADS, BAND_SUB, BAND_WIN), lambda b, i: (0, 0, 0))],
        out_specs=pl.BlockSpec((BAND_TQ, MIX_W), lambda b, i: (b * nq + i, 0)),
        out_shape=jax.ShapeDtypeStruct((B * L, MIX_W), BF16),
        scratch_shapes=[pltpu.VMEM((BAND_TQ, 2 * BAND_TQ), F32), pltpu.VMEM((BAND_TQ, 2 * BAND_TQ), BF16)],
        compiler_params=_cparams(("arbitrary", "arbitrary")),
        name="band_prompt",
    )(proj, proj, proj, proj, proj, _band_prompt_bias(bias_table))


def _band_sample_kernel(q_ref, kp_ref, kc_ref, vp_ref, vc_ref, bias_p_ref, bias_c_ref, o_ref):
    n_past = kp_ref.shape[1] // N_HEADS
    for h in range(N_HEADS):
        sl = slice(h * D_HEAD, (h + 1) * D_HEAD)
        kp = kp_ref.at[0][pl.ds(h, n_past, stride=N_HEADS), :].astype(BF16)
        vp = vp_ref.at[0][pl.ds(h, n_past, stride=N_HEADS), :].astype(BF16)
        o = _attend_two_piece(q_ref[:, sl], kp, kc_ref[:, sl], vp, vc_ref[:, sl], bias_p_ref[h], bias_c_ref[h], None)
        o_ref[:, sl] = o.astype(o_ref.dtype)


def band_sample(proj, past_k, past_v, bias_table, *, B, L):
    P = past_k.shape[1]
    qpos = PAST_LEN + jnp.arange(L, dtype=jnp.int32)
    kpos = jnp.concatenate([PAST_LEN - P + jnp.arange(P, dtype=jnp.int32), qpos])
    rel = jnp.clip(qpos[:, None] - kpos[None, :], -REL_CLIP, REL_CLIP) + REL_CLIP
    qc = (qpos // CHUNK)[:, None]
    kc = (kpos // CHUNK)[None, :]
    allowed = (kpos[None, :] >= 0) & (kc <= qc) & (kc >= qc - N_PREV_CHUNKS)
    bias = jnp.where(allowed[None], bias_table.astype(F32)[:, rel], NEG_INF)
    cur = lambda col: pl.BlockSpec((L, MIX_W), lambda b: (b, col))
    past = pl.BlockSpec((1, P * N_HEADS, D_HEAD), lambda b: (b, 0, 0))
    past_k = past_k.reshape(B, P * N_HEADS, D_HEAD)
    past_v = past_v.reshape(B, P * N_HEADS, D_HEAD)
    return pl.pallas_call(
        _band_sample_kernel,
        grid=(B,),
        in_specs=[cur(0), past, cur(1), past, cur(2),
                  pl.BlockSpec((N_HEADS, L, P), lambda b: (0, 0, 0)),
                  pl.BlockSpec((N_HEADS, L, L), lambda b: (0, 0, 0))],
        out_specs=pl.BlockSpec((L, MIX_W), lambda b: (b, 0)),
        out_shape=jax.ShapeDtypeStruct((B * L, MIX_W), BF16),
        compiler_params=_cparams(("arbitrary",)),
        name="band_sample",
    )(proj, past_k, proj, past_v, proj, bias[:, :, :P], bias[:, :, P:])


def _out_proj_kernel(o_ref, qm_ref, mk_ref, mv_ref, w_ref, x_ref, gf_ref, wr_ref, br_ref, tri_ref, cnt0_ref,
                     x1_ref, h2_ref, route_ref, cnt_ref):
    @pl.when(pl.program_id(0) == 0)
    def _():
        cnt_ref[...] = cnt0_ref[...]

    parts = [o_ref[...]]
    for h in range(H_MEM):
        sl = slice(h * D_HEAD, (h + 1) * D_HEAD)
        mk = mk_ref.at[0, 0][pl.ds(h, N_MEM, stride=H_MEM), :].astype(BF16)
        mv = mv_ref.at[0, 0][pl.ds(h, N_MEM, stride=H_MEM), :].astype(BF16)
        s = _dot_nt(qm_ref[:, sl], mk)
        p = jnp.exp(s - jnp.max(s, axis=-1, keepdims=True))
        om = _dot(p.astype(BF16), mv) / jnp.sum(p, axis=-1, keepdims=True)
        parts.append(om.astype(BF16))
    x1 = x_ref[...] + _dot(jnp.concatenate(parts, axis=1), w_ref[...])
    x1_ref[...] = x1
    hn = x1 * lax.rsqrt(jnp.mean(x1 * x1, axis=-1, keepdims=True) + EPS) * gf_ref[...]
    _store_token_tiles(h2_ref, 0, hn)

    logits = _dot(hn.astype(BF16), wr_ref[...]) + br_ref[...]
    tm = logits.shape[0]
    rows = min(tm, ROUTE_ROWS)
    tops = [_route_top2(logits[r0:r0 + rows]) for r0 in range(0, tm, rows)]

    onehot = jnp.concatenate([jnp.where(t[4] | t[5], 1.0, 0.0) for t in tops], axis=0)
    before = cnt_ref[0:1, :] + _dot(tri_ref[...], onehot.astype(BF16))
    cnt_ref[0:1, :] = cnt_ref[0:1, :] + jnp.sum(onehot, axis=0, keepdims=True)
    lane = lax.broadcasted_iota(jnp.int32, (rows, LANES), 1)
    for g, (i1, i2, w1, w2, oh1, oh2) in enumerate(tops):
        b = before[g * rows:(g + 1) * rows]
        r1 = jnp.sum(jnp.where(oh1, b, 0.0), axis=-1, keepdims=True)
        r2 = jnp.sum(jnp.where(oh2, b, 0.0), axis=-1, keepdims=True)
        route_ref[g * rows:(g + 1) * rows, :] = jnp.where(lane == 0, i1, jnp.where(lane == 1, i2, jnp.where(
            lane == 2, w1, jnp.where(lane == 3, w2, jnp.where(lane == 4, r1, jnp.where(lane == 5, r2, 0.0))))))


def _route_top2(logits):
    lane = lax.broadcasted_iota(jnp.int32, logits.shape, 1)
    lane_f = lane.astype(F32)
    far = float(LANES)
    is_group = (lane >= ROUTER_GROUP_LANE0) & (lane < ROUTER_GROUP_LANE0 + N_GROUPS)
    gl = jnp.where(is_group, logits, NEG_INF)
    g_max = jnp.max(gl, axis=-1, keepdims=True)
    g_sel = jnp.min(jnp.where(gl == g_max, lane_f, far), axis=-1, keepdims=True) - float(ROUTER_GROUP_LANE0)
    g_gate = 1.0 / jnp.sum(jnp.exp(gl - g_max), axis=-1, keepdims=True)
    in_group = jnp.right_shift(lane, 3).astype(F32) == g_sel
    el = jnp.where(in_group, logits, NEG_INF)
    m1 = jnp.max(el, axis=-1, keepdims=True)
    i1 = jnp.min(jnp.where(el == m1, lane_f, far), axis=-1, keepdims=True)
    el2 = jnp.where(lane_f == i1, NEG_INF, el)
    m2 = jnp.max(el2, axis=-1, keepdims=True)
    i2 = jnp.min(jnp.where(el2 == m2, lane_f, far), axis=-1, keepdims=True)
    e2 = jnp.exp(m2 - m1)
    p1 = 1.0 / (1.0 + e2)
    return i1, i2, g_gate * p1, g_gate * (e2 * p1), lane_f == i1, lane_f == i2


def out_proj(mixed, proj, qm_block, mem_k, mem_v, layer, w_out, x, g_ffn, w_router, b_router, counts0, *, tm,
             rows_per_stream):
    T = x.shape[0]
    tiles_per_stream = rows_per_stream // tm
    row = lambda width, col: pl.BlockSpec((tm, width), lambda i: (i, col))
    const = lambda shape: pl.BlockSpec(shape, lambda i: (0,) * len(shape))
    mem = pl.BlockSpec((1, 1, N_MEM * H_MEM, D_HEAD), lambda i: (layer, i // tiles_per_stream, 0, 0))
    mem_k = mem_k.reshape(mem_k.shape[:2] + (N_MEM * H_MEM, D_HEAD))
    mem_v = mem_v.reshape(mem_v.shape[:2] + (N_MEM * H_MEM, D_HEAD))
    tri = (jnp.arange(tm)[None, :] < jnp.arange(tm)[:, None]).astype(BF16)
    return pl.pallas_call(
        _out_proj_kernel,
        grid=(T // tm,),
        in_specs=[row(MIX_W, 0), row(MEM_W, qm_block), mem, mem, const((MIX_W + MEM_W, D_MODEL)),
                  row(D_MODEL, 0), const((1, D_MODEL)), const((D_MODEL, LANES)), const((1, LANES)),
                  const((tm, tm)), const((SUBLANES, LANES))],
        out_specs=[row(D_MODEL, 0), pl.BlockSpec((tm * ROW_TILE, LANES), lambda i: (i, 0)), row(LANES, 0),
                   const((SUBLANES, LANES))],
        out_shape=[jax.ShapeDtypeStruct((T, D_MODEL), F32), jax.ShapeDtypeStruct((T * ROW_TILE, LANES), F32),
                   jax.ShapeDtypeStruct((T, LANES), F32), jax.ShapeDtypeStruct((SUBLANES, LANES), F32)],
        compiler_params=_cparams(("arbitrary",)),
        name="out_proj",
    )(mixed, proj, mem_k, mem_v, w_out, x, g_ffn.reshape(1, D_MODEL), w_router, b_router, tri, counts0)


def _dispatch_kernel(tms, steps, pe_ref, pos_ref, *rest):
    h2_refs = rest[:len(tms)]
    xpad_hbm, zbuf, sem, zsem = rest[len(tms):]
    i = pl.program_id(0)

    @pl.when(i == 0)
    def _():
        zbuf[...] = jnp.zeros_like(zbuf)

        def block_copy(first_row):
            start = pl.multiple_of(first_row * ROW_TILE, MOE_BM * ROW_TILE)
            return pltpu.make_async_copy(zbuf, xpad_hbm.at[pl.ds(start, MOE_BM * ROW_TILE)], zsem)

        def tail_copy(e):
            return block_copy(pe_ref[e] - MOE_BM)

        def nonempty(e):
            return pe_ref[e] > (pe_ref[e - 1] if e > 0 else 0)

        for e in range(N_EXPERTS):
            pl.when(nonempty(e))(lambda e=e: tail_copy(e).start())
        for e in range(N_EXPERTS):
            pl.when(nonempty(e))(lambda e=e: tail_copy(e).wait())

        def fill(b, carry):
            cp = block_copy(b * MOE_BM)
            cp.start()
            cp.wait()
            return carry

        lax.fori_loop(pe_ref[N_EXPERTS - 1] // MOE_BM, xpad_hbm.shape[0] // (MOE_BM * ROW_TILE), fill, 0)

    def scatter_rows(h2_ref, tm):
        def body(r, carry):
            src = _token_tile(h2_ref, r)
            pltpu.make_async_copy(src, _token_tile(xpad_hbm, pos_ref[2 * r]), sem).start(priority=0)
            pltpu.make_async_copy(src, _token_tile(xpad_hbm, pos_ref[2 * r + 1]), sem).start(priority=1)
            return carry

        lax.fori_loop(0, tm, body, 0, unroll=8)
        for _ in range(2):
            pltpu.make_async_copy(h2_ref, xpad_hbm.at[pl.ds(0, tm * ROW_TILE)], sem).wait()

    first = 0
    for h2_ref, tm, n in zip(h2_refs, tms, steps):
        pl.when(jnp.logical_and(i >= first, i < first + n))(functools.partial(scatter_rows, h2_ref, tm))
        first += n


def moe_dispatch(pad_ends, pos_groups, h2_groups, n_rows):
    sizes = [h2.shape[0] // ROW_TILE for h2 in h2_groups]
    tms = [min(T, DISPATCH_TM) for T in sizes]
    steps = [T // tm for T, tm in zip(sizes, tms)]
    firsts = [sum(steps[:g]) for g in range(len(steps))]
    pos = jnp.concatenate([jnp.pad(p, (0, n * 2 * DISPATCH_TM - p.shape[0])) for p, n in zip(pos_groups, steps)])

    def rows(tm, first, n):
        return pl.BlockSpec((tm * ROW_TILE, LANES), lambda i, pe: (jnp.clip(i - first, 0, n - 1), 0))

    grid_spec = pltpu.PrefetchScalarGridSpec(
        num_scalar_prefetch=1,
        grid=(sum(steps),),
        in_specs=[pl.BlockSpec((2 * DISPATCH_TM,), lambda i, pe: (i,), memory_space=pltpu.SMEM)]
        + [rows(tm, first, n) for tm, first, n in zip(tms, firsts, steps)],
        out_specs=pl.BlockSpec(memory_space=pl.ANY),
        scratch_shapes=[pltpu.VMEM((MOE_BM * ROW_TILE, LANES), F32), pltpu.SemaphoreType.DMA(()),
                        pltpu.SemaphoreType.DMA(())],
    )
    return pl.pallas_call(
        functools.partial(_dispatch_kernel, tuple(tms), tuple(steps)),
        grid_spec=grid_spec,
        out_shape=jax.ShapeDtypeStruct((n_rows * ROW_TILE, LANES), F32),
        compiler_params=_cparams(("arbitrary",)),
        name="moe_dispatch",
    )(pad_ends, pos, *h2_groups)


def _moe_kernel(be_ref, nu_ref, x_ref, wg_ref, wu_ref, wd_ref, y_ref, wg_s, wu_s, wd_s):
    i = pl.program_id(0)
    used = i < nu_ref[0]
    new_expert = jnp.logical_or(i == 0, be_ref[i] != be_ref[jnp.maximum(i - 1, 0)])

    @pl.when(jnp.logical_and(used, new_expert))
    def _():
        wg_s[...] = wg_ref[0, 0].astype(BF16)
        wu_s[...] = wu_ref[0, 0].astype(BF16)
        wd_s[...] = wd_ref[0, 0].astype(BF16)

    @pl.when(used)
    def _():
        x = _load_token_tiles(x_ref, 0, MOE_BM).astype(BF16)
        g = _dot(x, wg_s[...])
        u = _dot(x, wu_s[...])
        a = (g * _sigmoid(g) * u).astype(BF16)
        _store_token_tiles(y_ref, 0, _dot(a, wd_s[...]))

    @pl.when(jnp.logical_not(used))
    def _():
        y_ref[...] = jnp.zeros_like(y_ref)


def moe_experts(x_pad, block_e, n_used, layer, wg, wu, wd):
    n_rows = x_pad.shape[0] // ROW_TILE
    rows = pl.BlockSpec((MOE_BM * ROW_TILE, LANES), lambda i, be, nu: (i, 0))
    w_in = pl.BlockSpec((1, 1, D_MODEL, D_FF_EXPERT), lambda i, be, nu: (layer, be[i], 0, 0))
    w_dn = pl.BlockSpec((1, 1, D_FF_EXPERT, D_MODEL), lambda i, be, nu: (layer, be[i], 0, 0))
    grid_spec = pltpu.PrefetchScalarGridSpec(
        num_scalar_prefetch=2,
        grid=(n_rows // MOE_BM,),
        in_specs=[rows, w_in, w_in, w_dn],
        out_specs=rows,
        scratch_shapes=[pltpu.VMEM((D_MODEL, D_FF_EXPERT), BF16), pltpu.VMEM((D_MODEL, D_FF_EXPERT), BF16),
                        pltpu.VMEM((D_FF_EXPERT, D_MODEL), BF16)],
    )
    return pl.pallas_call(
        _moe_kernel,
        grid_spec=grid_spec,
        out_shape=jax.ShapeDtypeStruct((n_rows * ROW_TILE, LANES), F32),
        compiler_params=_cparams(("arbitrary",)),
        name="moe_experts",
    )(block_e, n_used, x_pad, wg, wu, wd)


def _combine_kernel(tm, n_steps, pos_ref, pos_next_ref, route_ref, x1_ref, ypad_hbm, o_ref, ybuf, sems):
    i = pl.program_id(0)
    slot = lax.rem(i, 2)

    def gather(p_ref, s):
        buf = ybuf.at[s]

        def body(r, carry):
            pltpu.make_async_copy(_token_tile(ypad_hbm, p_ref[2 * r]), _token_tile(buf, r), sems.at[s]).start(
                priority=0)
            pltpu.make_async_copy(_token_tile(ypad_hbm, p_ref[2 * r + 1]), _token_tile(buf, tm + r),
                                  sems.at[s]).start(priority=1)
            return carry

        lax.fori_loop(0, tm, body, 0, unroll=8)

    @pl.when(i == 0)
    def _():
        gather(pos_ref, 0)

    @pl.when(i + 1 < n_steps)
    def _():
        gather(pos_next_ref, 1 - slot)

    buf = ybuf.at[slot]
    pltpu.make_async_copy(ypad_hbm.at[pl.ds(0, 2 * tm * ROW_TILE)], buf, sems.at[slot]).wait()
    g1 = route_ref[:, 2:3]
    g2 = route_ref[:, 3:4]
    o_ref[...] = x1_ref[...] + (g1 * _load_token_tiles(buf, 0, tm) + g2 * _load_token_tiles(buf, tm, tm))


def moe_combine(pos, route, x1, y_pad):
    T = x1.shape[0]
    tm = min(T, COMBINE_TM)
    n_steps = T // tm
    return pl.pallas_call(
        functools.partial(_combine_kernel, tm, n_steps),
        grid=(n_steps,),
        in_specs=[pl.BlockSpec((2 * tm,), lambda i: (i,), memory_space=pltpu.SMEM),
                  pl.BlockSpec((2 * tm,), lambda i: (jnp.minimum(i + 1, n_steps - 1),), memory_space=pltpu.SMEM),
                  pl.BlockSpec((tm, LANES), lambda i: (i, 0)),
                  pl.BlockSpec((tm, D_MODEL), lambda i: (i, 0)),
                  pl.BlockSpec(memory_space=pl.ANY)],
        out_specs=pl.BlockSpec((tm, D_MODEL), lambda i: (i, 0)),
        out_shape=jax.ShapeDtypeStruct((T, D_MODEL), F32),
        scratch_shapes=[pltpu.VMEM((2, 2 * tm * ROW_TILE, LANES), F32), pltpu.SemaphoreType.DMA((2,))],
        compiler_params=_cparams(("arbitrary",)),
        name="moe_combine",
    )(pos, pos, route, x1, y_pad)


def hier_moe(groups, counts, layer, wg, wu, wd):
    sizes = [x1.shape[0] for x1, _, _ in groups]
    n_assign = 2 * sum(sizes)
    n_blocks = (n_assign + N_EXPERTS * (MOE_BM - 1) + MOE_BM - 1) // MOE_BM
    head = jnp.concatenate([route[:, :6] for _, _, route in groups], axis=0)
    experts = head[:, 0:2].astype(jnp.int32)
    ranks = head[:, 4:6].astype(jnp.int32)
    cnt = counts[0, :N_EXPERTS].astype(jnp.int32)
    padded = ((cnt + MOE_BM - 1) // MOE_BM) * MOE_BM
    pad_ends = jnp.cumsum(padded).astype(jnp.int32)
    pad_starts = pad_ends - padded
    is_e = experts[:, :, None] == jnp.arange(N_EXPERTS, dtype=jnp.int32)
    pos = (ranks + jnp.sum(jnp.where(is_e, pad_starts, 0), axis=-1)).reshape(-1)
    block_start = jnp.arange(n_blocks, dtype=jnp.int32) * MOE_BM
    block_e = jnp.minimum(jnp.sum(block_start[:, None] >= pad_ends[None, :], axis=1), N_EXPERTS - 1).astype(jnp.int32)
    n_used = (pad_ends[-1:] // MOE_BM).astype(jnp.int32)
    starts = [2 * sum(sizes[:g]) for g in range(len(groups))]
    pos_g = [pos[s:s + 2 * n] for s, n in zip(starts, sizes)]
    x_pad = moe_dispatch(pad_ends, pos_g, [h2 for _, h2, _ in groups], n_blocks * MOE_BM)
    y_pad = moe_experts(x_pad, block_e, n_used, layer, wg, wu, wd)
    return [moe_combine(p, route, x1, y_pad) for (x1, _, route), p in zip(groups, pos_g)]


def _rope_tables(pos):
    half = D_HEAD // 2
    inv = ROPE_BASE ** (-jnp.arange(half, dtype=F32) / half)
    ang = pos.astype(F32)[:, None] * inv[None, :]
    cos, sin = jnp.cos(ang), jnp.sin(ang)
    return jnp.concatenate([cos, cos], axis=-1), jnp.concatenate([-sin, sin], axis=-1)


def _router_weights(w_group, b_group, w_router, b_router):
    w = jnp.zeros((D_MODEL, LANES), F32)
    w = w.at[:, :N_EXPERTS].set(w_router).at[:, ROUTER_GROUP_LANE0:ROUTER_GROUP_LANE0 + N_GROUPS].set(w_group)
    b = jnp.zeros((1, LANES), F32)
    b = b.at[0, :N_EXPERTS].set(b_router).at[0, ROUTER_GROUP_LANE0:ROUTER_GROUP_LANE0 + N_GROUPS].set(b_group)
    return w.astype(BF16), b


def _tile(v, n):
    return jnp.tile(v.astype(F32), n)


def _run_trunk(x, B, L, mem_k, mem_v, ret_state0, band_past, P):
    prompt = band_past is None
    tm = 512 if prompt else L
    proj_tm = 512
    scale = D_HEAD ** -0.5
    ones = lambda n: jnp.ones((n * D_HEAD,), F32)
    if prompt:
        cos2, sin2 = _rope_tables(jnp.arange(L, dtype=jnp.int32))
        tab_map = lambda i: i % (L // proj_tm)
    else:
        cos2, sin2 = _rope_tables(PAST_LEN + jnp.arange(L, dtype=jnp.int32))
        cos2, sin2 = jnp.tile(cos2, (B, 1)), jnp.tile(sin2, (B, 1))
        tab_map = lambda i: i
    out = {}

    def ffn(layer, mixed, proj, qm_block, w_out, x):
        def run(counts0):
            return out_proj(
                mixed, proj, qm_block, mem_k, mem_v, layer, w_out.astype(BF16), x, P["norm_ffn"][layer],
                *_router_weights(P["w_group"][layer], P["b_group"][layer], P["w_router"][layer],
                                 P["b_router"][layer]), counts0, tm=tm, rows_per_stream=L)
        return (yield run)

    segs0 = (("rope", N_HEADS), ("rope", N_HEADS), ("plain", N_HEADS), ("silu", N_HEADS), ("norm", H_MEM))
    gain0 = jnp.concatenate([ones(N_HEADS), ones(N_HEADS) * scale, ones(2 * N_HEADS),
                             _tile(P["q_norm_mem"][0], H_MEM) * scale])
    proj = norm_proj(x, P["norm_mix"][0], P["w_in_ret"][0].astype(BF16), gain0, segs0, tm=proj_tm,
                     out_dtype=BF16, rope=(cos2, sin2, tab_map))
    mixed, S = retention(proj, ret_state0, P["gn_ret"][0], B=B, L=L, C=RET_CHUNK if prompt else L)
    out["ret_state"] = S
    x = yield from ffn(0, mixed, proj, (4 * MIX_W) // MEM_W, P["w_out_ret"][0], x)

    segs1 = (("norm", N_HEADS), ("norm", N_HEADS), ("plain", N_HEADS), ("norm", H_MEM))
    gain1 = jnp.concatenate([_tile(P["q_norm_att"][0], N_HEADS) * scale, _tile(P["k_norm_att"][0], N_HEADS),
                             ones(N_HEADS), _tile(P["q_norm_mem"][1], H_MEM) * scale])
    w_in = P["w_in_att"][0].astype(BF16)
    if prompt:
        proj = norm_proj(x, P["norm_mix"][1], w_in, gain1, segs1, tm=proj_tm, out_dtype=BF16)
        keep = min(BAND_PAST, L)
        tiles = L // keep
        kv = norm_proj(x, P["norm_mix"][1], w_in[:, MIX_W:3 * MIX_W], gain1[MIX_W:3 * MIX_W], segs1[1:3],
                       tm=keep, out_dtype=F32, n_tiles=B, row_map=lambda i: i * tiles + tiles - 1)
        out["band_k"] = kv[:, :MIX_W].reshape(B, keep, N_HEADS, D_HEAD)
        out["band_v"] = kv[:, MIX_W:].reshape(B, keep, N_HEADS, D_HEAD)
        mixed = band_prompt(proj, P["rel_bias_att"][0], B=B, L=L)
    else:
        proj_f = norm_proj(x, P["norm_mix"][1], w_in, gain1, segs1, tm=proj_tm, out_dtype=F32)
        out["band_k"] = proj_f[:, MIX_W:2 * MIX_W].reshape(B, L, N_HEADS, D_HEAD)
        out["band_v"] = proj_f[:, 2 * MIX_W:3 * MIX_W].reshape(B, L, N_HEADS, D_HEAD)
        proj = proj_f.astype(BF16)
        mixed = band_sample(proj, band_past[0], band_past[1], P["rel_bias_att"][0], B=B, L=L)
    out["y"] = yield from ffn(1, mixed, proj, (3 * MIX_W) // MEM_W, P["w_out_att"][0], x)
    return out


def kernel(x_prompt, x_sample, mem_prompt, state_ret, cache_band_k, cache_band_v, cache_mem_k, cache_mem_v, norm_mix, norm_ffn, norm_mem, w_in_ret, gn_ret, w_out_ret, w_in_att, q_norm_att, k_norm_att, rel_bias_att, w_out_att, w_mem_kv, q_norm_mem, k_norm_mem, w_group, b_group, w_router, b_router, w_e_gate, w_e_up, w_e_down):
    P = {"norm_mix": norm_mix, "norm_ffn": norm_ffn, "w_in_ret": w_in_ret, "gn_ret": gn_ret,
         "w_out_ret": w_out_ret, "w_in_att": w_in_att, "q_norm_att": q_norm_att,
         "k_norm_att": k_norm_att, "rel_bias_att": rel_bias_att, "w_out_att": w_out_att,
         "q_norm_mem": q_norm_mem, "w_group": w_group, "b_group": b_group, "w_router": w_router,
         "b_router": b_router, "w_e_gate": w_e_gate, "w_e_up": w_e_up, "w_e_down": w_e_down}
    B, L, D = x_prompt.shape
    Bd, Ld, _ = x_sample.shape
    depth = norm_mix.shape[0]

    mem_k_p, mem_v_p = [], []
    for i in range(depth):
        gain = jnp.concatenate([_tile(k_norm_mem[i], H_MEM), jnp.ones((MEM_W,), F32)])
        kv = norm_proj(mem_prompt.reshape(B * N_MEM, D), norm_mem[i], w_mem_kv[i].astype(BF16), gain,
                       (("norm", H_MEM), ("plain", H_MEM)), tm=N_MEM, out_dtype=F32)
        mem_k_p.append(kv[:, :MEM_W].reshape(B, N_MEM, H_MEM, D_HEAD))
        mem_v_p.append(kv[:, MEM_W:].reshape(B, N_MEM, H_MEM, D_HEAD))
    mem_k_p = jnp.stack(mem_k_p)
    mem_v_p = jnp.stack(mem_v_p)

    zeros_state = jnp.zeros((B, N_HEADS, D_HEAD, D_HEAD), F32)
    trunks = [_run_trunk(x_prompt.reshape(B * L, D), B, L, mem_k_p, mem_v_p, zeros_state, None, P),
              _run_trunk(x_sample.reshape(Bd * Ld, D), Bd, Ld, cache_mem_k, cache_mem_v, state_ret[0],
                         (cache_band_k[0], cache_band_v[0]), P)]
    pending = [next(t) for t in trunks]
    results = []
    for layer in range(depth):
        counts = jnp.zeros((SUBLANES, LANES), F32)
        groups = []
        for run in pending:
            x1, h2, route, counts = run(counts)
            groups.append((x1, h2, route))
        outs = hier_moe(groups, counts, layer, w_e_gate, w_e_up, w_e_down)
        pending = []
        for t, x2 in zip(trunks, outs):
            try:
                pending.append(t.send(x2))
            except StopIteration as done:
                results.append(done.value)
    res_p, res_s = results

    return (res_p["y"].reshape(B, L, D), res_s["y"].reshape(Bd, Ld, D),
            res_p["ret_state"][None], res_s["ret_state"][None],
            res_p["band_k"][None], res_p["band_v"][None], res_s["band_k"][None], res_s["band_v"][None],
            mem_k_p, mem_v_p)
```

```python
import functools

import jax
import jax.numpy as jnp
from jax import lax
from jax.experimental import pallas as pl
from jax.experimental.pallas import tpu as pltpu

F32 = jnp.float32
BF16 = jnp.bfloat16

D_MODEL = 1024
D_HEAD = 128
CHUNK = 64
N_HEADS = 8
H_MEM = 4
N_MEM = 256
MEM_W = H_MEM * D_HEAD
MIX_W = N_HEADS * D_HEAD
N_PREV_CHUNKS = 8
BAND_PAST = N_PREV_CHUNKS * CHUNK
REL_CLIP = 128
N_GROUPS = 4
EXPERTS_PER_GROUP = 8
N_EXPERTS = N_GROUPS * EXPERTS_PER_GROUP
D_FF_EXPERT = 512
ROPE_BASE = 10000.0
EPS = 1e-6
NEG_INF = -1e30
PAST_LEN = 1024

LANES = 128
SUBLANES = 8

RET_CHUNK = 256
BAND_TQ = 512
BAND_SUB = 128
BAND_WIN = BAND_PAST + BAND_SUB
MOE_BM = 512
DISPATCH_TM = 2048
COMBINE_TM = 1024
OUT_PROJ_TM = 1024
OUT_PROJ_SAMPLE_TM = 128
ROUTE_ROWS = 128
ROUTER_GROUP_LANE0 = N_EXPERTS
VMEM_LIMIT = 56 * 1024 * 1024


def _cparams(sem):
    return pltpu.CompilerParams(dimension_semantics=sem, vmem_limit_bytes=VMEM_LIMIT)


def _dot(a, b):
    return jnp.dot(a, b, preferred_element_type=F32)


def _dot_nt(a, b):
    return lax.dot_general(a, b, (((1,), (1,)), ((), ())), preferred_element_type=F32)


def _sigmoid(x):
    return 1.0 / (1.0 + jnp.exp(-x))


ROW_TILE = D_MODEL // LANES
assert ROW_TILE == SUBLANES


def _load_token_tiles(ref, t0, n):
    return jnp.concatenate([ref[pl.ds(t0 * ROW_TILE + j, n, stride=ROW_TILE), :] for j in range(ROW_TILE)], axis=1)


def _store_token_tiles(ref, t0, val):
    n = val.shape[0]
    for j in range(ROW_TILE):
        ref[pl.ds(t0 * ROW_TILE + j, n, stride=ROW_TILE), :] = val[:, j * LANES:(j + 1) * LANES]


def _token_tile(ref, t):
    return ref.at[pl.ds(pl.multiple_of(t * ROW_TILE, ROW_TILE), ROW_TILE)]


def _norm_proj_kernel(segs, has_rope, x_ref, g_ref, w_ref, gain_ref, *rest):
    if has_rope:
        cos_ref, sin_ref, o_ref = rest
    else:
        (o_ref,) = rest
    x = x_ref[...]
    ms = jnp.mean(x * x, axis=-1, keepdims=True)
    h = (x * lax.rsqrt(ms + EPS) * g_ref[...]).astype(BF16)
    col = 0
    for kind, n_heads in segs:
        width = n_heads * D_HEAD
        acc = _dot(h, w_ref[:, col:col + width])
        for j in range(n_heads):
            a = acc[:, j * D_HEAD:(j + 1) * D_HEAD]
            c0 = col + j * D_HEAD
            gain = gain_ref[:, c0:c0 + D_HEAD]
            if kind == "rope":
                rot = pltpu.roll(a, D_HEAD // 2, 1)
                out = (a * cos_ref[...] + rot * sin_ref[...]) * gain
            elif kind == "norm":
                out = a * lax.rsqrt(jnp.mean(a * a, axis=-1, keepdims=True) + EPS) * gain
            elif kind == "silu":
                out = a * _sigmoid(a)
            else:
                out = a
            o_ref[:, c0:c0 + D_HEAD] = out.astype(o_ref.dtype)
        col += width


def norm_proj(x, g, w, gain, segs, *, tm, out_dtype, n_tiles=None, row_map=None, rope=None):
    T, D = x.shape
    N = w.shape[1]
    assert sum(n for _, n in segs) * D_HEAD == N
    if n_tiles is None:
        n_tiles = T // tm
    if row_map is None:
        row_map = lambda i: i
    in_specs = [
        pl.BlockSpec((tm, D), lambda i: (row_map(i), 0)),
        pl.BlockSpec((1, D), lambda i: (0, 0)),
        pl.BlockSpec((D, N), lambda i: (0, 0)),
        pl.BlockSpec((1, N), lambda i: (0, 0)),
    ]
    args = [x, g.reshape(1, D), w, gain.reshape(1, N)]
    if rope is not None:
        cos2, sin2, tab_map = rope
        in_specs += [pl.BlockSpec((tm, D_HEAD), lambda i: (tab_map(i), 0)),
                     pl.BlockSpec((tm, D_HEAD), lambda i: (tab_map(i), 0))]
        args += [cos2, sin2]
    return pl.pallas_call(
        functools.partial(_norm_proj_kernel, segs, rope is not None),
        grid=(n_tiles,),
        in_specs=in_specs,
        out_specs=pl.BlockSpec((tm, N), lambda i: (i, 0)),
        out_shape=jax.ShapeDtypeStruct((n_tiles * tm, N), out_dtype),
        compiler_params=_cparams(("arbitrary",)),
        name="norm_proj",
    )(*args)


def _retention_kernel(q_ref, k_ref, v_ref, g_ref, s0_ref, dmask_ref, qdec_ref, kdec_ref, gc_ref,
                      gn_ref, o_ref, s_ref):
    @pl.when(pl.program_id(1) == 0)
    def _():
        s_ref[...] = s0_ref[...]

    for h in range(N_HEADS):
        sl = slice(h * D_HEAD, (h + 1) * D_HEAD)
        q = q_ref[:, sl]
        k = k_ref[:, sl]
        v = v_ref[:, sl]
        S = s_ref[0, h]
        scores = _dot_nt(q, k) * dmask_ref[h]
        intra = _dot(scores.astype(BF16), v)
        cross = _dot(q, S.astype(BF16)) * qdec_ref[:, sl]
        o = intra + cross
        kd = (k.astype(F32) * kdec_ref[:, sl]).astype(BF16)
        s_ref[0, h] = gc_ref[h] * S + lax.dot_general(kd, v, (((0,), (0,)), ((), ())), preferred_element_type=F32)
        mu = jnp.mean(o, axis=-1, keepdims=True)
        d = o - mu
        var = jnp.mean(d * d, axis=-1, keepdims=True)
        y = d * lax.rsqrt(var + EPS) * gn_ref[:, sl] * g_ref[:, sl].astype(F32)
        o_ref[:, sl] = y.astype(o_ref.dtype)


def _retention_tables(C):
    log_gamma = jnp.log1p(-jnp.exp2(-5.0 - jnp.arange(N_HEADS, dtype=F32)))
    idx = jnp.arange(C, dtype=F32)
    diff = idx[:, None] - idx[None, :]
    dmask = jnp.where(diff[None] >= 0,
                      jnp.exp(log_gamma[:, None, None] * jnp.maximum(diff, 0.0)[None]), 0.0)
    q_dec = jnp.exp(log_gamma[None, :] * (idx + 1.0)[:, None])
    k_dec = jnp.exp(log_gamma[None, :] * (C - 1.0 - idx)[:, None])
    g_c = jnp.exp(log_gamma * C)
    rep = lambda t: jnp.repeat(t, D_HEAD, axis=-1)
    return dmask, rep(q_dec), rep(k_dec), rep(g_c[:, None])[:, None, :]


def retention(proj, s0, gn, *, B, L, C):
    nc = L // C
    dmask, qdec, kdec, gc = _retention_tables(C)
    row = lambda col: pl.BlockSpec((C, MIX_W), lambda b, c: (b * nc + c, col))
    const = lambda shape: pl.BlockSpec(shape, lambda b, c: (0,) * len(shape))
    state_spec = pl.BlockSpec((1, N_HEADS, D_HEAD, D_HEAD), lambda b, c: (b, 0, 0, 0))
    return pl.pallas_call(
        _retention_kernel,
        grid=(B, nc),
        in_specs=[row(0), row(1), row(2), row(3), state_spec,
                  const((N_HEADS, C, C)), const((C, MIX_W)), const((C, MIX_W)),
                  const((N_HEADS, 1, D_HEAD)), const((1, MIX_W))],
        out_specs=[pl.BlockSpec((C, MIX_W), lambda b, c: (b * nc + c, 0)), state_spec],
        out_shape=[jax.ShapeDtypeStruct((B * L, MIX_W), BF16),
                   jax.ShapeDtypeStruct((B, N_HEADS, D_HEAD, D_HEAD), F32)],
        compiler_params=_cparams(("arbitrary", "arbitrary")),
        name="retention",
    )(proj, proj, proj, proj, s0, dmask, qdec, kdec, gc, gn.reshape(1, MIX_W))


def _attend_two_piece(q, k_a, k_b, v_a, v_b, bias_a, bias_b, a_valid):
    s_a = _dot_nt(q, k_a) + bias_a
    if a_valid is not None:
        s_a = jnp.where(a_valid, s_a, NEG_INF)
    s_b = _dot_nt(q, k_b) + bias_b
    m = jnp.maximum(jnp.max(s_a, axis=-1, keepdims=True), jnp.max(s_b, axis=-1, keepdims=True))
    p_a = jnp.exp(s_a - m)
    p_b = jnp.exp(s_b - m)
    denom = jnp.sum(p_a, axis=-1, keepdims=True) + jnp.sum(p_b, axis=-1, keepdims=True)
    o = _dot(p_a.astype(BF16), v_a) + _dot(p_b.astype(BF16), v_b)
    return o / denom


def _band_prompt_kernel(q_ref, kp_ref, kc_ref, vp_ref, vc_ref, bias_ref, o_ref, s_buf, p_buf):
    @pl.when((pl.program_id(0) == 0) & (pl.program_id(1) == 0))
    def _():
        p_buf[...] = jnp.zeros_like(p_buf)

    prev_off = jnp.where(pl.program_id(1) > 0, 0.0, NEG_INF).astype(F32)
    for h in range(N_HEADS):
        sl = slice(h * D_HEAD, (h + 1) * D_HEAD)
        q = q_ref[:, sl]
        s_buf[:, :BAND_TQ] = _dot_nt(q, kp_ref[:, sl]) + prev_off
        s_buf[:, BAND_TQ:] = _dot_nt(q, kc_ref[:, sl])
        denoms = []
        for s in range(BAND_TQ // BAND_SUB):
            r0 = s * BAND_SUB
            sc = s_buf[r0:r0 + BAND_SUB, r0:r0 + BAND_WIN] + bias_ref[h]
            p = jnp.exp(sc - jnp.max(sc, axis=-1, keepdims=True))
            denoms.append(jnp.sum(p, axis=-1, keepdims=True))
            p_buf[r0:r0 + BAND_SUB, r0:r0 + BAND_WIN] = p.astype(BF16)
        o = _dot(p_buf[:, :BAND_TQ], vp_ref[:, sl]) + _dot(p_buf[:, BAND_TQ:], vc_ref[:, sl])
        o_ref[:, sl] = (o / jnp.concatenate(denoms, axis=0)).astype(o_ref.dtype)


def _band_prompt_bias(bias_table):
    r = jnp.arange(BAND_SUB, dtype=jnp.int32)[:, None]
    j = jnp.arange(BAND_WIN, dtype=jnp.int32)[None, :]
    q_chunk = r // CHUNK + N_PREV_CHUNKS
    k_chunk = j // CHUNK
    allowed = (k_chunk <= q_chunk) & (k_chunk >= q_chunk - N_PREV_CHUNKS)
    n_diag = BAND_SUB + BAND_WIN - 1
    d = (BAND_PAST + BAND_SUB - 1) - jnp.arange(n_diag, dtype=jnp.int32)
    diag = bias_table.astype(F32)[:, jnp.clip(d, -REL_CLIP, REL_CLIP) + REL_CLIP]
    cut = jnp.tile(diag, (1, BAND_SUB + 1))[:, :BAND_SUB * (n_diag + 1)].reshape(N_HEADS, BAND_SUB, n_diag + 1)
    rows = cut[:, ::-1, :BAND_WIN]
    return jnp.where(allowed[None], rows, NEG_INF)


def band_prompt(proj, bias_table, *, B, L):
    nq = L // BAND_TQ
    cur = lambda col: pl.BlockSpec((BAND_TQ, MIX_W), lambda b, i: (b * nq + i, col))
    prev = lambda col: pl.BlockSpec((BAND_TQ, MIX_W), lambda b, i: (b * nq + jnp.maximum(i - 1, 0), col))
    return pl.pallas_call(
        _band_prompt_kernel,
        grid=(B, nq),
        in_specs=[cur(0), prev(1), cur(1), prev(2), cur(2),
                  pl.BlockSpec((N_HEADS, BAND_SUB, BAND_WIN), lambda b, i: (0, 0, 0))],
        out_specs=pl.BlockSpec((BAND_TQ, MIX_W), lambda b, i: (b * nq + i, 0)),
        out_shape=jax.ShapeDtypeStruct((B * L, MIX_W), BF16),
        scratch_shapes=[pltpu.VMEM((BAND_TQ, 2 * BAND_TQ), F32), pltpu.VMEM((BAND_TQ, 2 * BAND_TQ), BF16)],
        compiler_params=_cparams(("arbitrary", "arbitrary")),
        name="band_prompt",
    )(proj, proj, proj, proj, proj, _band_prompt_bias(bias_table))


def _band_sample_kernel(q_ref, kp_ref, kc_ref, vp_ref, vc_ref, bias_p_ref, bias_c_ref, o_ref):
    n_past = kp_ref.shape[1] // N_HEADS
    for h in range(N_HEADS):
        sl = slice(h * D_HEAD, (h + 1) * D_HEAD)
        kp = kp_ref.at[0][pl.ds(h, n_past, stride=N_HEADS), :].astype(BF16)
        vp = vp_ref.at[0][pl.ds(h, n_past, stride=N_HEADS), :].astype(BF16)
        o = _attend_two_piece(q_ref[:, sl], kp, kc_ref[:, sl], vp, vc_ref[:, sl], bias_p_ref[h], bias_c_ref[h], None)
        o_ref[:, sl] = o.astype(o_ref.dtype)


def band_sample(proj, past_k, past_v, bias_table, *, B, L):
    P = past_k.shape[1]
    qpos = PAST_LEN + jnp.arange(L, dtype=jnp.int32)
    kpos = jnp.concatenate([PAST_LEN - P + jnp.arange(P, dtype=jnp.int32), qpos])
    rel = jnp.clip(qpos[:, None] - kpos[None, :], -REL_CLIP, REL_CLIP) + REL_CLIP
    qc = (qpos // CHUNK)[:, None]
    kc = (kpos // CHUNK)[None, :]
    allowed = (kpos[None, :] >= 0) & (kc <= qc) & (kc >= qc - N_PREV_CHUNKS)
    bias = jnp.where(allowed[None], bias_table.astype(F32)[:, rel], NEG_INF)
    cur = lambda col: pl.BlockSpec((L, MIX_W), lambda b: (b, col))
    past = pl.BlockSpec((1, P * N_HEADS, D_HEAD), lambda b: (b, 0, 0))
    past_k = past_k.reshape(B, P * N_HEADS, D_HEAD)
    past_v = past_v.reshape(B, P * N_HEADS, D_HEAD)
    return pl.pallas_call(
        _band_sample_kernel,
        grid=(B,),
        in_specs=[cur(0), past, cur(1), past, cur(2),
                  pl.BlockSpec((N_HEADS, L, P), lambda b: (0, 0, 0)),
                  pl.BlockSpec((N_HEADS, L, L), lambda b: (0, 0, 0))],
        out_specs=pl.BlockSpec((L, MIX_W), lambda b: (b, 0)),
        out_shape=jax.ShapeDtypeStruct((B * L, MIX_W), BF16),
        compiler_params=_cparams(("arbitrary",)),
        name="band_sample",
    )(proj, past_k, proj, past_v, proj, bias[:, :, :P], bias[:, :, P:])


def _out_proj_kernel(n_streams, n_groups, o_ref, qm_ref, mk_ref, mv_ref, w_ref, x_ref, gf_ref, wr_ref, br_ref,
                     tri_ref, cnt0_ref, x1_ref, h2_ref, route_ref, cnt_ref):
    @pl.when(pl.program_id(0) == 0)
    def _():
        cnt_ref[...] = cnt0_ref[...]

    tm = x_ref.shape[0]
    sr = tm // n_streams
    gr = tm // n_groups
    om_heads = []
    for h in range(H_MEM):
        sl = slice(h * D_HEAD, (h + 1) * D_HEAD)
        chunks = []
        for c in range(n_streams):
            mk = mk_ref.at[0, c][pl.ds(h, N_MEM, stride=H_MEM), :].astype(BF16)
            mv = mv_ref.at[0, c][pl.ds(h, N_MEM, stride=H_MEM), :].astype(BF16)
            s = _dot_nt(qm_ref[c * sr:(c + 1) * sr, sl], mk)
            p = jnp.exp(s - jnp.max(s, axis=-1, keepdims=True))
            om = _dot(p.astype(BF16), mv) / jnp.sum(p, axis=-1, keepdims=True)
            chunks.append(om.astype(BF16))
        om_heads.append(chunks[0] if n_streams == 1 else jnp.concatenate(chunks, axis=0))

    rows = min(gr, ROUTE_ROWS)
    lane = lax.broadcasted_iota(jnp.int32, (rows, LANES), 1)
    for g in range(n_groups):
        g0 = g * gr
        feat = jnp.concatenate([o_ref[g0:g0 + gr, :]] + [om[g0:g0 + gr] for om in om_heads], axis=1)
        x1 = x_ref[g0:g0 + gr, :] + _dot(feat, w_ref[...])
        x1_ref[g0:g0 + gr, :] = x1
        hn = x1 * lax.rsqrt(jnp.mean(x1 * x1, axis=-1, keepdims=True) + EPS) * gf_ref[...]
        _store_token_tiles(h2_ref, g0, hn)

        logits = _dot(hn.astype(BF16), wr_ref[...]) + br_ref[...]
        tops = [_route_top2(logits[r0:r0 + rows]) for r0 in range(0, gr, rows)]

        onehot = jnp.concatenate([jnp.where(t[4] | t[5], 1.0, 0.0) for t in tops], axis=0)
        before = cnt_ref[0:1, :] + _dot(tri_ref[...], onehot.astype(BF16))
        cnt_ref[0:1, :] = cnt_ref[0:1, :] + jnp.sum(onehot, axis=0, keepdims=True)
        for k, (i1, i2, w1, w2, oh1, oh2) in enumerate(tops):
            b = before[k * rows:(k + 1) * rows]
            r1 = jnp.sum(jnp.where(oh1, b, 0.0), axis=-1, keepdims=True)
            r2 = jnp.sum(jnp.where(oh2, b, 0.0), axis=-1, keepdims=True)
            route_ref[g0 + k * rows:g0 + (k + 1) * rows, :] = jnp.where(lane == 0, i1, jnp.where(
                lane == 1, i2, jnp.where(lane == 2, w1, jnp.where(lane == 3, w2, jnp.where(
                    lane == 4, r1, jnp.where(lane == 5, r2, 0.0))))))


def _route_top2(logits):
    lane = lax.broadcasted_iota(jnp.int32, logits.shape, 1)
    lane_f = lane.astype(F32)
    far = float(LANES)
    is_group = (lane >= ROUTER_GROUP_LANE0) & (lane < ROUTER_GROUP_LANE0 + N_GROUPS)
    gl = jnp.where(is_group, logits, NEG_INF)
    g_max = jnp.max(gl, axis=-1, keepdims=True)
    g_sel = jnp.min(jnp.where(gl == g_max, lane_f, far), axis=-1, keepdims=True) - float(ROUTER_GROUP_LANE0)
    g_gate = 1.0 / jnp.sum(jnp.exp(gl - g_max), axis=-1, keepdims=True)
    in_group = jnp.right_shift(lane, 3).astype(F32) == g_sel
    el = jnp.where(in_group, logits, NEG_INF)
    m1 = jnp.max(el, axis=-1, keepdims=True)
    i1 = jnp.min(jnp.where(el == m1, lane_f, far), axis=-1, keepdims=True)
    el2 = jnp.where(lane_f == i1, NEG_INF, el)
    m2 = jnp.max(el2, axis=-1, keepdims=True)
    i2 = jnp.min(jnp.where(el2 == m2, lane_f, far), axis=-1, keepdims=True)
    e2 = jnp.exp(m2 - m1)
    p1 = 1.0 / (1.0 + e2)
    return i1, i2, g_gate * p1, g_gate * (e2 * p1), lane_f == i1, lane_f == i2


def out_proj(mixed, proj, qm_block, mem_k, mem_v, layer, w_out, x, g_ffn, w_router, b_router, counts0, *, tm,
             n_groups, rows_per_stream):
    T = x.shape[0]
    n_streams = max(1, tm // rows_per_stream)
    tiles_per_stream = max(1, rows_per_stream // tm)
    gr = tm // n_groups
    row = lambda width, col: pl.BlockSpec((tm, width), lambda i: (i, col))
    const = lambda shape: pl.BlockSpec(shape, lambda i: (0,) * len(shape))
    mem = pl.BlockSpec((1, n_streams, N_MEM * H_MEM, D_HEAD), lambda i: (layer, i // tiles_per_stream, 0, 0))
    mem_k = mem_k.reshape(mem_k.shape[:2] + (N_MEM * H_MEM, D_HEAD))
    mem_v = mem_v.reshape(mem_v.shape[:2] + (N_MEM * H_MEM, D_HEAD))
    tri = (jnp.arange(gr)[None, :] < jnp.arange(gr)[:, None]).astype(BF16)
    return pl.pallas_call(
        functools.partial(_out_proj_kernel, n_streams, n_groups),
        grid=(T // tm,),
        in_specs=[row(MIX_W, 0), row(MEM_W, qm_block), mem, mem, const((MIX_W + MEM_W, D_MODEL)),
                  row(D_MODEL, 0), const((1, D_MODEL)), const((D_MODEL, LANES)), const((1, LANES)),
                  const((gr, gr)), const((SUBLANES, LANES))],
        out_specs=[row(D_MODEL, 0), pl.BlockSpec((tm * ROW_TILE, LANES), lambda i: (i, 0)), row(LANES, 0),
                   const((SUBLANES, LANES))],
        out_shape=[jax.ShapeDtypeStruct((T, D_MODEL), F32), jax.ShapeDtypeStruct((T * ROW_TILE, LANES), F32),
                   jax.ShapeDtypeStruct((T, LANES), F32), jax.ShapeDtypeStruct((SUBLANES, LANES), F32)],
        compiler_params=_cparams(("arbitrary",)),
        name="out_proj",
    )(mixed, proj, mem_k, mem_v, w_out, x, g_ffn.reshape(1, D_MODEL), w_router, b_router, tri, counts0)


def _dispatch_kernel(tms, steps, pe_ref, pos_ref, *rest):
    h2_refs = rest[:len(tms)]
    xpad_hbm, zbuf, sem, zsem = rest[len(tms):]
    i = pl.program_id(0)

    @pl.when(i == 0)
    def _():
        zbuf[...] = jnp.zeros_like(zbuf)

        def block_copy(first_row):
            start = pl.multiple_of(first_row * ROW_TILE, MOE_BM * ROW_TILE)
            return pltpu.make_async_copy(zbuf, xpad_hbm.at[pl.ds(start, MOE_BM * ROW_TILE)], zsem)

        def tail_copy(e):
            return block_copy(pe_ref[e] - MOE_BM)

        def nonempty(e):
            return pe_ref[e] > (pe_ref[e - 1] if e > 0 else 0)

        for e in range(N_EXPERTS):
            pl.when(nonempty(e))(lambda e=e: tail_copy(e).start())
        for e in range(N_EXPERTS):
            pl.when(nonempty(e))(lambda e=e: tail_copy(e).wait())

        def fill(b, carry):
            cp = block_copy(b * MOE_BM)
            cp.start()
            cp.wait()
            return carry

        lax.fori_loop(pe_ref[N_EXPERTS - 1] // MOE_BM, xpad_hbm.shape[0] // (MOE_BM * ROW_TILE), fill, 0)

    def scatter_rows(h2_ref, tm):
        def body(r, carry):
            src = _token_tile(h2_ref, r)
            pltpu.make_async_copy(src, _token_tile(xpad_hbm, pos_ref[2 * r]), sem).start(priority=0)
            pltpu.make_async_copy(src, _token_tile(xpad_hbm, pos_ref[2 * r + 1]), sem).start(priority=1)
            return carry

        lax.fori_loop(0, tm, body, 0, unroll=8)
        for _ in range(2):
            pltpu.make_async_copy(h2_ref, xpad_hbm.at[pl.ds(0, tm * ROW_TILE)], sem).wait()

    first = 0
    for h2_ref, tm, n in zip(h2_refs, tms, steps):
        pl.when(jnp.logical_and(i >= first, i < first + n))(functools.partial(scatter_rows, h2_ref, tm))
        first += n


def moe_dispatch(pad_ends, pos_groups, h2_groups, n_rows):
    sizes = [h2.shape[0] // ROW_TILE for h2 in h2_groups]
    tms = [min(T, DISPATCH_TM) for T in sizes]
    steps = [T // tm for T, tm in zip(sizes, tms)]
    firsts = [sum(steps[:g]) for g in range(len(steps))]
    pos = jnp.concatenate([jnp.pad(p, (0, n * 2 * DISPATCH_TM - p.shape[0])) for p, n in zip(pos_groups, steps)])

    def rows(tm, first, n):
        return pl.BlockSpec((tm * ROW_TILE, LANES), lambda i, pe: (jnp.clip(i - first, 0, n - 1), 0))

    grid_spec = pltpu.PrefetchScalarGridSpec(
        num_scalar_prefetch=1,
        grid=(sum(steps),),
        in_specs=[pl.BlockSpec((2 * DISPATCH_TM,), lambda i, pe: (i,), memory_space=pltpu.SMEM)]
        + [rows(tm, first, n) for tm, first, n in zip(tms, firsts, steps)],
        out_specs=pl.BlockSpec(memory_space=pl.ANY),
        scratch_shapes=[pltpu.VMEM((MOE_BM * ROW_TILE, LANES), F32), pltpu.SemaphoreType.DMA(()),
                        pltpu.SemaphoreType.DMA(())],
    )
    return pl.pallas_call(
        functools.partial(_dispatch_kernel, tuple(tms), tuple(steps)),
        grid_spec=grid_spec,
        out_shape=jax.ShapeDtypeStruct((n_rows * ROW_TILE, LANES), F32),
        compiler_params=_cparams(("arbitrary",)),
        name="moe_dispatch",
    )(pad_ends, pos, *h2_groups)


def _moe_kernel(be_ref, nu_ref, x_ref, wg_ref, wu_ref, wd_ref, y_ref, wg_s, wu_s, wd_s):
    i = pl.program_id(0)
    used = i < nu_ref[0]
    new_expert = jnp.logical_or(i == 0, be_ref[i] != be_ref[jnp.maximum(i - 1, 0)])

    @pl.when(jnp.logical_and(used, new_expert))
    def _():
        wg_s[...] = wg_ref[0, 0].astype(BF16)
        wu_s[...] = wu_ref[0, 0].astype(BF16)
        wd_s[...] = wd_ref[0, 0].astype(BF16)

    @pl.when(used)
    def _():
        x = _load_token_tiles(x_ref, 0, MOE_BM).astype(BF16)
        g = _dot(x, wg_s[...])
        u = _dot(x, wu_s[...])
        a = (g * _sigmoid(g) * u).astype(BF16)
        _store_token_tiles(y_ref, 0, _dot(a, wd_s[...]))

    @pl.when(jnp.logical_not(used))
    def _():
        y_ref[...] = jnp.zeros_like(y_ref)


def moe_experts(x_pad, block_e, n_used, layer, wg, wu, wd):
    n_rows = x_pad.shape[0] // ROW_TILE
    rows = pl.BlockSpec((MOE_BM * ROW_TILE, LANES), lambda i, be, nu: (i, 0))
    w_in = pl.BlockSpec((1, 1, D_MODEL, D_FF_EXPERT), lambda i, be, nu: (layer, be[i], 0, 0))
    w_dn = pl.BlockSpec((1, 1, D_FF_EXPERT, D_MODEL), lambda i, be, nu: (layer, be[i], 0, 0))
    grid_spec = pltpu.PrefetchScalarGridSpec(
        num_scalar_prefetch=2,
        grid=(n_rows // MOE_BM,),
        in_specs=[rows, w_in, w_in, w_dn],
        out_specs=rows,
        scratch_shapes=[pltpu.VMEM((D_MODEL, D_FF_EXPERT), BF16), pltpu.VMEM((D_MODEL, D_FF_EXPERT), BF16),
                        pltpu.VMEM((D_FF_EXPERT, D_MODEL), BF16)],
    )
    return pl.pallas_call(
        _moe_kernel,
        grid_spec=grid_spec,
        out_shape=jax.ShapeDtypeStruct((n_rows * ROW_TILE, LANES), F32),
        compiler_params=_cparams(("arbitrary",)),
        name="moe_experts",
    )(block_e, n_used, x_pad, wg, wu, wd)


def _combine_kernel(tm, n_steps, pos_ref, pos_next_ref, route_ref, x1_ref, ypad_hbm, o_ref, ybuf, sems):
    i = pl.program_id(0)
    slot = lax.rem(i, 2)

    def gather(p_ref, s):
        buf = ybuf.at[s]

        def body(r, carry):
            pltpu.make_async_copy(_token_tile(ypad_hbm, p_ref[2 * r]), _token_tile(buf, r), sems.at[s]).start(
                priority=0)
            pltpu.make_async_copy(_token_tile(ypad_hbm, p_ref[2 * r + 1]), _token_tile(buf, tm + r),
                                  sems.at[s]).start(priority=1)
            return carry

        lax.fori_loop(0, tm, body, 0, unroll=8)

    @pl.when(i == 0)
    def _():
        gather(pos_ref, 0)

    @pl.when(i + 1 < n_steps)
    def _():
        gather(pos_next_ref, 1 - slot)

    buf = ybuf.at[slot]
    pltpu.make_async_copy(ypad_hbm.at[pl.ds(0, 2 * tm * ROW_TILE)], buf, sems.at[slot]).wait()
    g1 = route_ref[:, 2:3]
    g2 = route_ref[:, 3:4]
    o_ref[...] = x1_ref[...] + (g1 * _load_token_tiles(buf, 0, tm) + g2 * _load_token_tiles(buf, tm, tm))


def moe_combine(pos, route, x1, y_pad):
    T = x1.shape[0]
    tm = min(T, COMBINE_TM)
    n_steps = T // tm
    return pl.pallas_call(
        functools.partial(_combine_kernel, tm, n_steps),
        grid=(n_steps,),
        in_specs=[pl.BlockSpec((2 * tm,), lambda i: (i,), memory_space=pltpu.SMEM),
                  pl.BlockSpec((2 * tm,), lambda i: (jnp.minimum(i + 1, n_steps - 1),), memory_space=pltpu.SMEM),
                  pl.BlockSpec((tm, LANES), lambda i: (i, 0)),
                  pl.BlockSpec((tm, D_MODEL), lambda i: (i, 0)),
                  pl.BlockSpec(memory_space=pl.ANY)],
        out_specs=pl.BlockSpec((tm, D_MODEL), lambda i: (i, 0)),
        out_shape=jax.ShapeDtypeStruct((T, D_MODEL), F32),
        scratch_shapes=[pltpu.VMEM((2, 2 * tm * ROW_TILE, LANES), F32), pltpu.SemaphoreType.DMA((2,))],
        compiler_params=_cparams(("arbitrary",)),
        name="moe_combine",
    )(pos, pos, route, x1, y_pad)


def hier_moe(groups, counts, layer, wg, wu, wd):
    sizes = [x1.shape[0] for x1, _, _ in groups]
    n_assign = 2 * sum(sizes)
    n_blocks = (n_assign + N_EXPERTS * (MOE_BM - 1) + MOE_BM - 1) // MOE_BM
    head = jnp.concatenate([route[:, :6] for _, _, route in groups], axis=0)
    experts = head[:, 0:2].astype(jnp.int32)
    ranks = head[:, 4:6].astype(jnp.int32)
    cnt = counts[0, :N_EXPERTS].astype(jnp.int32)
    padded = ((cnt + MOE_BM - 1) // MOE_BM) * MOE_BM
    pad_ends = jnp.cumsum(padded).astype(jnp.int32)
    pad_starts = pad_ends - padded
    is_e = experts[:, :, None] == jnp.arange(N_EXPERTS, dtype=jnp.int32)
    pos = (ranks + jnp.sum(jnp.where(is_e, pad_starts, 0), axis=-1)).reshape(-1)
    block_start = jnp.arange(n_blocks, dtype=jnp.int32) * MOE_BM
    block_e = jnp.minimum(jnp.sum(block_start[:, None] >= pad_ends[None, :], axis=1), N_EXPERTS - 1).astype(jnp.int32)
    n_used = (pad_ends[-1:] // MOE_BM).astype(jnp.int32)
    starts = [2 * sum(sizes[:g]) for g in range(len(groups))]
    pos_g = [pos[s:s + 2 * n] for s, n in zip(starts, sizes)]
    x_pad = moe_dispatch(pad_ends, pos_g, [h2 for _, h2, _ in groups], n_blocks * MOE_BM)
    y_pad = moe_experts(x_pad, block_e, n_used, layer, wg, wu, wd)
    return [moe_combine(p, route, x1, y_pad) for (x1, _, route), p in zip(groups, pos_g)]


def _rope_tables(pos):
    half = D_HEAD // 2
    inv = ROPE_BASE ** (-jnp.arange(half, dtype=F32) / half)
    ang = pos.astype(F32)[:, None] * inv[None, :]
    cos, sin = jnp.cos(ang), jnp.sin(ang)
    return jnp.concatenate([cos, cos], axis=-1), jnp.concatenate([-sin, sin], axis=-1)


def _router_weights(w_group, b_group, w_router, b_router):
    w = jnp.zeros((D_MODEL, LANES), F32)
    w = w.at[:, :N_EXPERTS].set(w_router).at[:, ROUTER_GROUP_LANE0:ROUTER_GROUP_LANE0 + N_GROUPS].set(w_group)
    b = jnp.zeros((1, LANES), F32)
    b = b.at[0, :N_EXPERTS].set(b_router).at[0, ROUTER_GROUP_LANE0:ROUTER_GROUP_LANE0 + N_GROUPS].set(b_group)
    return w.astype(BF16), b


def _tile(v, n):
    return jnp.tile(v.astype(F32), n)


def _run_trunk(x, B, L, mem_k, mem_v, ret_state0, band_past, P):
    prompt = band_past is None
    out_tm, out_groups = (OUT_PROJ_TM, OUT_PROJ_TM // 512) if prompt else (OUT_PROJ_SAMPLE_TM, 1)
    proj_tm = 512
    scale = D_HEAD ** -0.5
    ones = lambda n: jnp.ones((n * D_HEAD,), F32)
    if prompt:
        cos2, sin2 = _rope_tables(jnp.arange(L, dtype=jnp.int32))
        tab_map = lambda i: i % (L // proj_tm)
    else:
        cos2, sin2 = _rope_tables(PAST_LEN + jnp.arange(L, dtype=jnp.int32))
        cos2, sin2 = jnp.tile(cos2, (B, 1)), jnp.tile(sin2, (B, 1))
        tab_map = lambda i: i
    out = {}

    def ffn(layer, mixed, proj, qm_block, w_out, x):
        def run(counts0):
            return out_proj(
                mixed, proj, qm_block, mem_k, mem_v, layer, w_out.astype(BF16), x, P["norm_ffn"][layer],
                *_router_weights(P["w_group"][layer], P["b_group"][layer], P["w_router"][layer],
                                 P["b_router"][layer]), counts0, tm=out_tm, n_groups=out_groups, rows_per_stream=L)
        return (yield run)

    segs0 = (("rope", N_HEADS), ("rope", N_HEADS), ("plain", N_HEADS), ("silu", N_HEADS), ("norm", H_MEM))
    gain0 = jnp.concatenate([ones(N_HEADS), ones(N_HEADS) * scale, ones(2 * N_HEADS),
                             _tile(P["q_norm_mem"][0], H_MEM) * scale])
    proj = norm_proj(x, P["norm_mix"][0], P["w_in_ret"][0].astype(BF16), gain0, segs0, tm=proj_tm,
                     out_dtype=BF16, rope=(cos2, sin2, tab_map))
    mixed, S = retention(proj, ret_state0, P["gn_ret"][0], B=B, L=L, C=RET_CHUNK if prompt else L)
    out["ret_state"] = S
    x = yield from ffn(0, mixed, proj, (4 * MIX_W) // MEM_W, P["w_out_ret"][0], x)

    segs1 = (("norm", N_HEADS), ("norm", N_HEADS), ("plain", N_HEADS), ("norm", H_MEM))
    gain1 = jnp.concatenate([_tile(P["q_norm_att"][0], N_HEADS) * scale, _tile(P["k_norm_att"][0], N_HEADS),
                             ones(N_HEADS), _tile(P["q_norm_mem"][1], H_MEM) * scale])
    w_in = P["w_in_att"][0].astype(BF16)
    if prompt:
        proj = norm_proj(x, P["norm_mix"][1], w_in, gain1, segs1, tm=proj_tm, out_dtype=BF16)
        keep = min(BAND_PAST, L)
        tiles = L // keep
        kv = norm_proj(x, P["norm_mix"][1], w_in[:, MIX_W:3 * MIX_W], gain1[MIX_W:3 * MIX_W], segs1[1:3],
                       tm=keep, out_dtype=F32, n_tiles=B, row_map=lambda i: i * tiles + tiles - 1)
        out["band_k"] = kv[:, :MIX_W].reshape(B, keep, N_HEADS, D_HEAD)
        out["band_v"] = kv[:, MIX_W:].reshape(B, keep, N_HEADS, D_HEAD)
        mixed = band_prompt(proj, P["rel_bias_att"][0], B=B, L=L)
    else:
        proj_f = norm_proj(x, P["norm_mix"][1], w_in, gain1, segs1, tm=proj_tm, out_dtype=F32)
        out["band_k"] = proj_f[:, MIX_W:2 * MIX_W].reshape(B, L, N_HEADS, D_HEAD)
        out["band_v"] = proj_f[:, 2 * MIX_W:3 * MIX_W].reshape(B, L, N_HEADS, D_HEAD)
        proj = proj_f.astype(BF16)
        mixed = band_sample(proj, band_past[0], band_past[1], P["rel_bias_att"][0], B=B, L=L)
    out["y"] = yield from ffn(1, mixed, proj, (3 * MIX_W) // MEM_W, P["w_out_att"][0], x)
    return out


def kernel(x_prompt, x_sample, mem_prompt, state_ret, cache_band_k, cache_band_v, cache_mem_k, cache_mem_v, norm_mix, norm_ffn, norm_mem, w_in_ret, gn_ret, w_out_ret, w_in_att, q_norm_att, k_norm_att, rel_bias_att, w_out_att, w_mem_kv, q_norm_mem, k_norm_mem, w_group, b_group, w_router, b_router, w_e_gate, w_e_up, w_e_down):
    P = {"norm_mix": norm_mix, "norm_ffn": norm_ffn, "w_in_ret": w_in_ret, "gn_ret": gn_ret,
         "w_out_ret": w_out_ret, "w_in_att": w_in_att, "q_norm_att": q_norm_att,
         "k_norm_att": k_norm_att, "rel_bias_att": rel_bias_att, "w_out_att": w_out_att,
         "q_norm_mem": q_norm_mem, "w_group": w_group, "b_group": b_group, "w_router": w_router,
         "b_router": b_router, "w_e_gate": w_e_gate, "w_e_up": w_e_up, "w_e_down": w_e_down}
    B, L, D = x_prompt.shape
    Bd, Ld, _ = x_sample.shape
    depth = norm_mix.shape[0]

    mem_k_p, mem_v_p = [], []
    for i in range(depth):
        gain = jnp.concatenate([_tile(k_norm_mem[i], H_MEM), jnp.ones((MEM_W,), F32)])
        kv = norm_proj(mem_prompt.reshape(B * N_MEM, D), norm_mem[i], w_mem_kv[i].astype(BF16), gain,
                       (("norm", H_MEM), ("plain", H_MEM)), tm=N_MEM, out_dtype=F32)
        mem_k_p.append(kv[:, :MEM_W].reshape(B, N_MEM, H_MEM, D_HEAD))
        mem_v_p.append(kv[:, MEM_W:].reshape(B, N_MEM, H_MEM, D_HEAD))
    mem_k_p = jnp.stack(mem_k_p)
    mem_v_p = jnp.stack(mem_v_p)

    zeros_state = jnp.zeros((B, N_HEADS, D_HEAD, D_HEAD), F32)
    trunks = [_run_trunk(x_prompt.reshape(B * L, D), B, L, mem_k_p, mem_v_p, zeros_state, None, P),
              _run_trunk(x_sample.reshape(Bd * Ld, D), Bd, Ld, cache_mem_k, cache_mem_v, state_ret[0],
                         (cache_band_k[0], cache_band_v[0]), P)]
    pending = [next(t) for t in trunks]
    results = []
    for layer in range(depth):
        counts = jnp.zeros((SUBLANES, LANES), F32)
        groups = []
        for run in pending:
            x1, h2, route, counts = run(counts)
            groups.append((x1, h2, route))
        outs = hier_moe(groups, counts, layer, w_e_gate, w_e_up, w_e_down)
        pending = []
        for t, x2 in zip(trunks, outs):
            try:
                pending.append(t.send(x2))
            except StopIteration as done:
                results.append(done.value)
    res_p, res_s = results

    return (res_p["y"].reshape(B, L, D), res_s["y"].reshape(Bd, Ld, D),
            res_p["ret_state"][None], res_s["ret_state"][None],
            res_p["band_k"][None], res_p["band_v"][None], res_s["band_k"][None], res_s["band_v"][None],
            mem_k_p, mem_v_p)
```

```python
import functools

import jax
import jax.numpy as jnp
from jax import lax
from jax.experimental import pallas as pl
from jax.experimental.pallas import tpu as pltpu

F32 = jnp.float32
BF16 = jnp.bfloat16

D_MODEL = 1024
D_HEAD = 128
CHUNK = 64
N_HEADS = 8
H_MEM = 4
N_MEM = 256
MEM_W = H_MEM * D_HEAD
MIX_W = N_HEADS * D_HEAD
N_PREV_CHUNKS = 8
BAND_PAST = N_PREV_CHUNKS * CHUNK
REL_CLIP = 128
N_GROUPS = 4
EXPERTS_PER_GROUP = 8
N_EXPERTS = N_GROUPS * EXPERTS_PER_GROUP
D_FF_EXPERT = 512
ROPE_BASE = 10000.0
EPS = 1e-6
NEG_INF = -1e30
PAST_LEN = 1024

LANES = 128
SUBLANES = 8

RET_CHUNK = 256
BAND_TQ = 512
BAND_SUB = 128
BAND_WIN = BAND_PAST + BAND_SUB
MOE_BM = 512
DISPATCH_TM = 2048
COMBINE_TM = 1024
OUT_PROJ_TM = 1024
OUT_PROJ_SAMPLE_TM = 128
ROUTE_ROWS = 128
ROUTER_GROUP_LANE0 = N_EXPERTS
VMEM_LIMIT = 56 * 1024 * 1024


def _cparams(sem):
    return pltpu.CompilerParams(dimension_semantics=sem, vmem_limit_bytes=VMEM_LIMIT)


def _dot(a, b):
    return jnp.dot(a, b, preferred_element_type=F32)


def _dot_nt(a, b):
    return lax.dot_general(a, b, (((1,), (1,)), ((), ())), preferred_element_type=F32)


def _sigmoid(x):
    return 1.0 / (1.0 + jnp.exp(-x))


ROW_TILE = D_MODEL // LANES
assert ROW_TILE == SUBLANES


def _load_token_tiles(ref, t0, n):
    return jnp.concatenate([ref[pl.ds(t0 * ROW_TILE + j, n, stride=ROW_TILE), :] for j in range(ROW_TILE)], axis=1)


def _store_token_tiles(ref, t0, val):
    n = val.shape[0]
    for j in range(ROW_TILE):
        ref[pl.ds(t0 * ROW_TILE + j, n, stride=ROW_TILE), :] = val[:, j * LANES:(j + 1) * LANES]


def _token_tile(ref, t):
    return ref.at[pl.ds(pl.multiple_of(t * ROW_TILE, ROW_TILE), ROW_TILE)]


def _norm_proj_kernel(segs, has_rope, x_ref, g_ref, w_ref, gain_ref, *rest):
    if has_rope:
        cos_ref, sin_ref, o_ref = rest
    else:
        (o_ref,) = rest
    x = x_ref[...]
    ms = jnp.mean(x * x, axis=-1, keepdims=True)
    h = (x * lax.rsqrt(ms + EPS) * g_ref[...]).astype(BF16)
    col = 0
    for kind, n_heads in segs:
        width = n_heads * D_HEAD
        acc = _dot(h, w_ref[:, col:col + width])
        for j in range(n_heads):
            a = acc[:, j * D_HEAD:(j + 1) * D_HEAD]
            c0 = col + j * D_HEAD
            gain = gain_ref[:, c0:c0 + D_HEAD]
            if kind == "rope":
                rot = pltpu.roll(a, D_HEAD // 2, 1)
                out = (a * cos_ref[...] + rot * sin_ref[...]) * gain
            elif kind == "norm":
                out = a * lax.rsqrt(jnp.mean(a * a, axis=-1, keepdims=True) + EPS) * gain
            elif kind == "silu":
                out = a * _sigmoid(a)
            else:
                out = a
            o_ref[:, c0:c0 + D_HEAD] = out.astype(o_ref.dtype)
        col += width


def norm_proj(x, g, w, gain, segs, *, tm, out_dtype, n_tiles=None, row_map=None, rope=None):
    T, D = x.shape
    N = w.shape[1]
    assert sum(n for _, n in segs) * D_HEAD == N
    if n_tiles is None:
        n_tiles = T // tm
    if row_map is None:
        row_map = lambda i: i
    in_specs = [
        pl.BlockSpec((tm, D), lambda i: (row_map(i), 0)),
        pl.BlockSpec((1, D), lambda i: (0, 0)),
        pl.BlockSpec((D, N), lambda i: (0, 0)),
        pl.BlockSpec((1, N), lambda i: (0, 0)),
    ]
    args = [x, g.reshape(1, D), w, gain.reshape(1, N)]
    if rope is not None:
        cos2, sin2, tab_map = rope
        in_specs += [pl.BlockSpec((tm, D_HEAD), lambda i: (tab_map(i), 0)),
                     pl.BlockSpec((tm, D_HEAD), lambda i: (tab_map(i), 0))]
        args += [cos2, sin2]
    return pl.pallas_call(
        functools.partial(_norm_proj_kernel, segs, rope is not None),
        grid=(n_tiles,),
        in_specs=in_specs,
        out_specs=pl.BlockSpec((tm, N), lambda i: (i, 0)),
        out_shape=jax.ShapeDtypeStruct((n_tiles * tm, N), out_dtype),
        compiler_params=_cparams(("arbitrary",)),
        name="norm_proj",
    )(*args)


def _retention_kernel(q_ref, k_ref, v_ref, g_ref, s0_ref, dmask_ref, qdec_ref, kdec_ref, gc_ref,
                      gn_ref, o_ref, s_ref):
    @pl.when(pl.program_id(1) == 0)
    def _():
        s_ref[...] = s0_ref[...]

    for h in range(N_HEADS):
        sl = slice(h * D_HEAD, (h + 1) * D_HEAD)
        q = q_ref[:, sl]
        k = k_ref[:, sl]
        v = v_ref[:, sl]
        S = s_ref[0, h]
        scores = _dot_nt(q, k) * dmask_ref[h]
        intra = _dot(scores.astype(BF16), v)
        cross = _dot(q, S.astype(BF16)) * qdec_ref[:, sl]
        o = intra + cross
        kd = (k.astype(F32) * kdec_ref[:, sl]).astype(BF16)
        s_ref[0, h] = gc_ref[h] * S + lax.dot_general(kd, v, (((0,), (0,)), ((), ())), preferred_element_type=F32)
        mu = jnp.mean(o, axis=-1, keepdims=True)
        d = o - mu
        var = jnp.mean(d * d, axis=-1, keepdims=True)
        y = d * lax.rsqrt(var + EPS) * gn_ref[:, sl] * g_ref[:, sl].astype(F32)
        o_ref[:, sl] = y.astype(o_ref.dtype)


def _retention_tables(C):
    log_gamma = jnp.log1p(-jnp.exp2(-5.0 - jnp.arange(N_HEADS, dtype=F32)))
    idx = jnp.arange(C, dtype=F32)
    diff = idx[:, None] - idx[None, :]
    dmask = jnp.where(diff[None] >= 0,
                      jnp.exp(log_gamma[:, None, None] * jnp.maximum(diff, 0.0)[None]), 0.0)
    q_dec = jnp.exp(log_gamma[None, :] * (idx + 1.0)[:, None])
    k_dec = jnp.exp(log_gamma[None, :] * (C - 1.0 - idx)[:, None])
    g_c = jnp.exp(log_gamma * C)
    rep = lambda t: jnp.repeat(t, D_HEAD, axis=-1)
    return dmask, rep(q_dec), rep(k_dec), rep(g_c[:, None])[:, None, :]


def retention(proj, s0, gn, *, B, L, C):
    nc = L // C
    dmask, qdec, kdec, gc = _retention_tables(C)
    row = lambda col: pl.BlockSpec((C, MIX_W), lambda b, c: (b * nc + c, col))
    const = lambda shape: pl.BlockSpec(shape, lambda b, c: (0,) * len(shape))
    state_spec = pl.BlockSpec((1, N_HEADS, D_HEAD, D_HEAD), lambda b, c: (b, 0, 0, 0))
    return pl.pallas_call(
        _retention_kernel,
        grid=(B, nc),
        in_specs=[row(0), row(1), row(2), row(3), state_spec,
                  const((N_HEADS, C, C)), const((C, MIX_W)), const((C, MIX_W)),
                  const((N_HEADS, 1, D_HEAD)), const((1, MIX_W))],
        out_specs=[pl.BlockSpec((C, MIX_W), lambda b, c: (b * nc + c, 0)), state_spec],
        out_shape=[jax.ShapeDtypeStruct((B * L, MIX_W), BF16),
                   jax.ShapeDtypeStruct((B, N_HEADS, D_HEAD, D_HEAD), F32)],
        compiler_params=_cparams(("arbitrary", "arbitrary")),
        name="retention",
    )(proj, proj, proj, proj, s0, dmask, qdec, kdec, gc, gn.reshape(1, MIX_W))


def _attend_two_piece(q, k_a, k_b, v_a, v_b, bias_a, bias_b, a_valid):
    s_a = _dot_nt(q, k_a) + bias_a
    if a_valid is not None:
        s_a = jnp.where(a_valid, s_a, NEG_INF)
    s_b = _dot_nt(q, k_b) + bias_b
    m = jnp.maximum(jnp.max(s_a, axis=-1, keepdims=True), jnp.max(s_b, axis=-1, keepdims=True))
    p_a = jnp.exp(s_a - m)
    p_b = jnp.exp(s_b - m)
    denom = jnp.sum(p_a, axis=-1, keepdims=True) + jnp.sum(p_b, axis=-1, keepdims=True)
    o = _dot(p_a.astype(BF16), v_a) + _dot(p_b.astype(BF16), v_b)
    return o / denom


def _band_prompt_kernel(q_ref, kp_ref, kc_ref, vp_ref, vc_ref, bias_ref, o_ref, s_buf, p_buf):
    @pl.when((pl.program_id(0) == 0) & (pl.program_id(1) == 0))
    def _():
        p_buf[...] = jnp.zeros_like(p_buf)

    prev_off = jnp.where(pl.program_id(1) > 0, 0.0, NEG_INF).astype(F32)
    for h in range(N_HEADS):
        sl = slice(h * D_HEAD, (h + 1) * D_HEAD)
        q = q_ref[:, sl]
        s_buf[:, :BAND_TQ] = _dot_nt(q, kp_ref[:, sl]) + prev_off
        s_buf[:, BAND_TQ:] = _dot_nt(q, kc_ref[:, sl])
        denoms = []
        for s in range(BAND_TQ // BAND_SUB):
            r0 = s * BAND_SUB
            sc = s_buf[r0:r0 + BAND_SUB, r0:r0 + BAND_WIN] + bias_ref[h]
            p = jnp.exp(sc - jnp.max(sc, axis=-1, keepdims=True))
            denoms.append(jnp.sum(p, axis=-1, keepdims=True))
            p_buf[r0:r0 + BAND_SUB, r0:r0 + BAND_WIN] = p.astype(BF16)
        o = _dot(p_buf[:, :BAND_TQ], vp_ref[:, sl]) + _dot(p_buf[:, BAND_TQ:], vc_ref[:, sl])
        o_ref[:, sl] = (o / jnp.concatenate(denoms, axis=0)).astype(o_ref.dtype)


def _rel_bias(bias_table, offset, n_rows, width):
    n = n_rows + width - 1
    d = (offset + n_rows - 1) - jnp.arange(n, dtype=jnp.int32)
    diag = bias_table.astype(F32)[:, jnp.clip(d, -REL_CLIP, REL_CLIP) + REL_CLIP]
    cut = jnp.tile(diag, (1, n_rows + 1))[:, :n_rows * (n + 1)].reshape(diag.shape[0], n_rows, n + 1)
    return cut[:, ::-1, :width]


def _band_prompt_bias(bias_table):
    r = jnp.arange(BAND_SUB, dtype=jnp.int32)[:, None]
    j = jnp.arange(BAND_WIN, dtype=jnp.int32)[None, :]
    q_chunk = r // CHUNK + N_PREV_CHUNKS
    k_chunk = j // CHUNK
    allowed = (k_chunk <= q_chunk) & (k_chunk >= q_chunk - N_PREV_CHUNKS)
    return jnp.where(allowed[None], _rel_bias(bias_table, BAND_PAST, BAND_SUB, BAND_WIN), NEG_INF)


def band_prompt(proj, bias_table, *, B, L):
    nq = L // BAND_TQ
    cur = lambda col: pl.BlockSpec((BAND_TQ, MIX_W), lambda b, i: (b * nq + i, col))
    prev = lambda col: pl.BlockSpec((BAND_TQ, MIX_W), lambda b, i: (b * nq + jnp.maximum(i - 1, 0), col))
    return pl.pallas_call(
        _band_prompt_kernel,
        grid=(B, nq),
        in_specs=[cur(0), prev(1), cur(1), prev(2), cur(2),
                  pl.BlockSpec((N_HEADS, BAND_SUB, BAND_WIN), lambda b, i: (0, 0, 0))],
        out_specs=pl.BlockSpec((BAND_TQ, MIX_W), lambda b, i: (b * nq + i, 0)),
        out_shape=jax.ShapeDtypeStruct((B * L, MIX_W), BF16),
        scratch_shapes=[pltpu.VMEM((BAND_TQ, 2 * BAND_TQ), F32), pltpu.VMEM((BAND_TQ, 2 * BAND_TQ), BF16)],
        compiler_params=_cparams(("arbitrary", "arbitrary")),
        name="band_prompt",
    )(proj, proj, proj, proj, proj, _band_prompt_bias(bias_table))


def _band_sample_kernel(q_ref, kp_ref, kc_ref, vp_ref, vc_ref, bias_p_ref, bias_c_ref, o_ref):
    n_past = kp_ref.shape[1] // N_HEADS
    for h in range(N_HEADS):
        sl = slice(h * D_HEAD, (h + 1) * D_HEAD)
        kp = kp_ref.at[0][pl.ds(h, n_past, stride=N_HEADS), :].astype(BF16)
        vp = vp_ref.at[0][pl.ds(h, n_past, stride=N_HEADS), :].astype(BF16)
        o = _attend_two_piece(q_ref[:, sl], kp, kc_ref[:, sl], vp, vc_ref[:, sl], bias_p_ref[h], bias_c_ref[h], None)
        o_ref[:, sl] = o.astype(o_ref.dtype)


def band_sample(proj, past_k, past_v, bias_table, *, B, L):
    P = past_k.shape[1]
    qpos = PAST_LEN + jnp.arange(L, dtype=jnp.int32)
    kpos = jnp.concatenate([PAST_LEN - P + jnp.arange(P, dtype=jnp.int32), qpos])
    qc = (qpos // CHUNK)[:, None]
    kc = (kpos // CHUNK)[None, :]
    allowed = (kpos[None, :] >= 0) & (kc <= qc) & (kc >= qc - N_PREV_CHUNKS)
    rel_bias = jnp.concatenate([_rel_bias(bias_table, P, L, P), _rel_bias(bias_table, 0, L, L)], axis=-1)
    bias = jnp.where(allowed[None], rel_bias, NEG_INF)
    cur = lambda col: pl.BlockSpec((L, MIX_W), lambda b: (b, col))
    past = pl.BlockSpec((1, P * N_HEADS, D_HEAD), lambda b: (b, 0, 0))
    past_k = past_k.reshape(B, P * N_HEADS, D_HEAD)
    past_v = past_v.reshape(B, P * N_HEADS, D_HEAD)
    return pl.pallas_call(
        _band_sample_kernel,
        grid=(B,),
        in_specs=[cur(0), past, cur(1), past, cur(2),
                  pl.BlockSpec((N_HEADS, L, P), lambda b: (0, 0, 0)),
                  pl.BlockSpec((N_HEADS, L, L), lambda b: (0, 0, 0))],
        out_specs=pl.BlockSpec((L, MIX_W), lambda b: (b, 0)),
        out_shape=jax.ShapeDtypeStruct((B * L, MIX_W), BF16),
        compiler_params=_cparams(("arbitrary",)),
        name="band_sample",
    )(proj, past_k, proj, past_v, proj, bias[:, :, :P], bias[:, :, P:])


def _out_proj_kernel(n_streams, n_groups, o_ref, qm_ref, mk_ref, mv_ref, w_ref, x_ref, gf_ref, wr_ref, br_ref,
                     tri_ref, cnt0_ref, x1_ref, h2_ref, route_ref, route_t_ref, cnt_ref):
    @pl.when(pl.program_id(0) == 0)
    def _():
        cnt_ref[...] = cnt0_ref[...]

    tm = x_ref.shape[0]
    sr = tm // n_streams
    gr = tm // n_groups
    om_heads = []
    for h in range(H_MEM):
        sl = slice(h * D_HEAD, (h + 1) * D_HEAD)
        chunks = []
        for c in range(n_streams):
            mk = mk_ref.at[0, c][pl.ds(h, N_MEM, stride=H_MEM), :].astype(BF16)
            mv = mv_ref.at[0, c][pl.ds(h, N_MEM, stride=H_MEM), :].astype(BF16)
            s = _dot_nt(qm_ref[c * sr:(c + 1) * sr, sl], mk)
            p = jnp.exp(s - jnp.max(s, axis=-1, keepdims=True))
            om = _dot(p.astype(BF16), mv) / jnp.sum(p, axis=-1, keepdims=True)
            chunks.append(om.astype(BF16))
        om_heads.append(chunks[0] if n_streams == 1 else jnp.concatenate(chunks, axis=0))

    rows = min(gr, ROUTE_ROWS)
    lane = lax.broadcasted_iota(jnp.int32, (rows, LANES), 1)
    for g in range(n_groups):
        g0 = g * gr
        feat = jnp.concatenate([o_ref[g0:g0 + gr, :]] + [om[g0:g0 + gr] for om in om_heads], axis=1)
        x1 = x_ref[g0:g0 + gr, :] + _dot(feat, w_ref[...])
        x1_ref[g0:g0 + gr, :] = x1
        hn = x1 * lax.rsqrt(jnp.mean(x1 * x1, axis=-1, keepdims=True) + EPS) * gf_ref[...]
        _store_token_tiles(h2_ref, g0, hn)

        logits = _dot(hn.astype(BF16), wr_ref[...]) + br_ref[...]
        tops = [_route_top2(logits[r0:r0 + rows]) for r0 in range(0, gr, rows)]

        onehot = jnp.concatenate([jnp.where(t[4] | t[5], 1.0, 0.0) for t in tops], axis=0)
        before = cnt_ref[0:1, :] + _dot(tri_ref[...], onehot.astype(BF16))
        cnt_ref[0:1, :] = cnt_ref[0:1, :] + jnp.sum(onehot, axis=0, keepdims=True)
        for k, (i1, i2, w1, w2, oh1, oh2) in enumerate(tops):
            b = before[k * rows:(k + 1) * rows]
            r1 = jnp.sum(jnp.where(oh1, b, 0.0), axis=-1, keepdims=True)
            r2 = jnp.sum(jnp.where(oh2, b, 0.0), axis=-1, keepdims=True)
            route = jnp.where(lane == 0, i1, jnp.where(lane == 1, i2, jnp.where(lane == 2, w1, jnp.where(
                lane == 3, w2, jnp.where(lane == 4, r1, jnp.where(lane == 5, r2, 0.0))))))
            c0 = g0 + k * rows
            route_ref[c0:c0 + rows, :] = route
            route_t_ref[:, c0:c0 + rows] = route.T[:SUBLANES, :]


def _route_top2(logits):
    lane = lax.broadcasted_iota(jnp.int32, logits.shape, 1)
    lane_f = lane.astype(F32)
    far = float(LANES)
    is_group = (lane >= ROUTER_GROUP_LANE0) & (lane < ROUTER_GROUP_LANE0 + N_GROUPS)
    gl = jnp.where(is_group, logits, NEG_INF)
    g_max = jnp.max(gl, axis=-1, keepdims=True)
    g_sel = jnp.min(jnp.where(gl == g_max, lane_f, far), axis=-1, keepdims=True) - float(ROUTER_GROUP_LANE0)
    g_gate = 1.0 / jnp.sum(jnp.exp(gl - g_max), axis=-1, keepdims=True)
    in_group = jnp.right_shift(lane, 3).astype(F32) == g_sel
    el = jnp.where(in_group, logits, NEG_INF)
    m1 = jnp.max(el, axis=-1, keepdims=True)
    i1 = jnp.min(jnp.where(el == m1, lane_f, far), axis=-1, keepdims=True)
    el2 = jnp.where(lane_f == i1, NEG_INF, el)
    m2 = jnp.max(el2, axis=-1, keepdims=True)
    i2 = jnp.min(jnp.where(el2 == m2, lane_f, far), axis=-1, keepdims=True)
    e2 = jnp.exp(m2 - m1)
    p1 = 1.0 / (1.0 + e2)
    return i1, i2, g_gate * p1, g_gate * (e2 * p1), lane_f == i1, lane_f == i2


def out_proj(mixed, proj, qm_block, mem_k, mem_v, layer, w_out, x, g_ffn, w_router, b_router, counts0, *, tm,
             n_groups, rows_per_stream):
    T = x.shape[0]
    n_streams = max(1, tm // rows_per_stream)
    tiles_per_stream = max(1, rows_per_stream // tm)
    gr = tm // n_groups
    assert min(gr, ROUTE_ROWS) % LANES == 0
    row = lambda width, col: pl.BlockSpec((tm, width), lambda i: (i, col))
    const = lambda shape: pl.BlockSpec(shape, lambda i: (0,) * len(shape))
    mem = pl.BlockSpec((1, n_streams, N_MEM * H_MEM, D_HEAD), lambda i: (layer, i // tiles_per_stream, 0, 0))
    mem_k = mem_k.reshape(mem_k.shape[:2] + (N_MEM * H_MEM, D_HEAD))
    mem_v = mem_v.reshape(mem_v.shape[:2] + (N_MEM * H_MEM, D_HEAD))
    tri = (jnp.arange(gr)[None, :] < jnp.arange(gr)[:, None]).astype(BF16)
    return pl.pallas_call(
        functools.partial(_out_proj_kernel, n_streams, n_groups),
        grid=(T // tm,),
        in_specs=[row(MIX_W, 0), row(MEM_W, qm_block), mem, mem, const((MIX_W + MEM_W, D_MODEL)),
                  row(D_MODEL, 0), const((1, D_MODEL)), const((D_MODEL, LANES)), const((1, LANES)),
                  const((gr, gr)), const((SUBLANES, LANES))],
        out_specs=[row(D_MODEL, 0), pl.BlockSpec((tm * ROW_TILE, LANES), lambda i: (i, 0)), row(LANES, 0),
                   pl.BlockSpec((SUBLANES, tm), lambda i: (0, i)), const((SUBLANES, LANES))],
        out_shape=[jax.ShapeDtypeStruct((T, D_MODEL), F32), jax.ShapeDtypeStruct((T * ROW_TILE, LANES), F32),
                   jax.ShapeDtypeStruct((T, LANES), F32), jax.ShapeDtypeStruct((SUBLANES, T), F32),
                   jax.ShapeDtypeStruct((SUBLANES, LANES), F32)],
        compiler_params=_cparams(("arbitrary",)),
        name="out_proj",
    )(mixed, proj, mem_k, mem_v, w_out, x, g_ffn.reshape(1, D_MODEL), w_router, b_router, tri, counts0)


def _dispatch_kernel(tms, steps, pe_ref, pos0_ref, pos1_ref, *rest):
    h2_refs = rest[:len(tms)]
    xpad_hbm, zbuf, sem, zsem = rest[len(tms):]
    i = pl.program_id(0)

    @pl.when(i == 0)
    def _():
        zbuf[...] = jnp.zeros_like(zbuf)

        def block_copy(first_row):
            start = pl.multiple_of(first_row * ROW_TILE, MOE_BM * ROW_TILE)
            return pltpu.make_async_copy(zbuf, xpad_hbm.at[pl.ds(start, MOE_BM * ROW_TILE)], zsem)

        def tail_copy(e):
            return block_copy(pe_ref[e] - MOE_BM)

        def nonempty(e):
            return pe_ref[e] > (pe_ref[e - 1] if e > 0 else 0)

        for e in range(N_EXPERTS):
            pl.when(nonempty(e))(lambda e=e: tail_copy(e).start())
        for e in range(N_EXPERTS):
            pl.when(nonempty(e))(lambda e=e: tail_copy(e).wait())

        def fill(b, carry):
            cp = block_copy(b * MOE_BM)
            cp.start()
            cp.wait()
            return carry

        lax.fori_loop(pe_ref[N_EXPERTS - 1] // MOE_BM, xpad_hbm.shape[0] // (MOE_BM * ROW_TILE), fill, 0)

    def scatter_rows(h2_ref, tm):
        def body(r, carry):
            src = _token_tile(h2_ref, r)
            pltpu.make_async_copy(src, _token_tile(xpad_hbm, pos0_ref[r]), sem).start(priority=0)
            pltpu.make_async_copy(src, _token_tile(xpad_hbm, pos1_ref[r]), sem).start(priority=1)
            return carry

        lax.fori_loop(0, tm, body, 0, unroll=8)
        for _ in range(2):
            pltpu.make_async_copy(h2_ref, xpad_hbm.at[pl.ds(0, tm * ROW_TILE)], sem).wait()

    first = 0
    for h2_ref, tm, n in zip(h2_refs, tms, steps):
        pl.when(jnp.logical_and(i >= first, i < first + n))(functools.partial(scatter_rows, h2_ref, tm))
        first += n


def moe_dispatch(pad_ends, pos_groups, h2_groups, n_rows):
    sizes = [h2.shape[0] // ROW_TILE for h2 in h2_groups]
    tms = [min(T, DISPATCH_TM) for T in sizes]
    steps = [T // tm for T, tm in zip(sizes, tms)]
    firsts = [sum(steps[:g]) for g in range(len(steps))]
    pos0, pos1 = (jnp.concatenate([jnp.pad(p[k], (0, n * DISPATCH_TM - p[k].shape[0]))
                                   for p, n in zip(pos_groups, steps)]) for k in range(2))

    def rows(tm, first, n):
        return pl.BlockSpec((tm * ROW_TILE, LANES), lambda i, pe: (jnp.clip(i - first, 0, n - 1), 0))

    grid_spec = pltpu.PrefetchScalarGridSpec(
        num_scalar_prefetch=1,
        grid=(sum(steps),),
        in_specs=[pl.BlockSpec((DISPATCH_TM,), lambda i, pe: (i,), memory_space=pltpu.SMEM)] * 2
        + [rows(tm, first, n) for tm, first, n in zip(tms, firsts, steps)],
        out_specs=pl.BlockSpec(memory_space=pl.ANY),
        scratch_shapes=[pltpu.VMEM((MOE_BM * ROW_TILE, LANES), F32), pltpu.SemaphoreType.DMA(()),
                        pltpu.SemaphoreType.DMA(())],
    )
    return pl.pallas_call(
        functools.partial(_dispatch_kernel, tuple(tms), tuple(steps)),
        grid_spec=grid_spec,
        out_shape=jax.ShapeDtypeStruct((n_rows * ROW_TILE, LANES), F32),
        compiler_params=_cparams(("arbitrary",)),
        name="moe_dispatch",
    )(pad_ends, pos0, pos1, *h2_groups)


def _moe_kernel(be_ref, nu_ref, x_ref, wg_ref, wu_ref, wd_ref, y_ref, wg_s, wu_s, wd_s):
    i = pl.program_id(0)
    used = i < nu_ref[0]
    new_expert = jnp.logical_or(i == 0, be_ref[i] != be_ref[jnp.maximum(i - 1, 0)])

    @pl.when(jnp.logical_and(used, new_expert))
    def _():
        wg_s[...] = wg_ref[0, 0].astype(BF16)
        wu_s[...] = wu_ref[0, 0].astype(BF16)
        wd_s[...] = wd_ref[0, 0].astype(BF16)

    @pl.when(used)
    def _():
        x = _load_token_tiles(x_ref, 0, MOE_BM).astype(BF16)
        g = _dot(x, wg_s[...])
        u = _dot(x, wu_s[...])
        a = (g * _sigmoid(g) * u).astype(BF16)
        _store_token_tiles(y_ref, 0, _dot(a, wd_s[...]))

    @pl.when(jnp.logical_not(used))
    def _():
        y_ref[...] = jnp.zeros_like(y_ref)


def moe_experts(x_pad, block_e, n_used, layer, wg, wu, wd):
    n_rows = x_pad.shape[0] // ROW_TILE
    rows = pl.BlockSpec((MOE_BM * ROW_TILE, LANES), lambda i, be, nu: (i, 0))
    w_in = pl.BlockSpec((1, 1, D_MODEL, D_FF_EXPERT), lambda i, be, nu: (layer, be[i], 0, 0))
    w_dn = pl.BlockSpec((1, 1, D_FF_EXPERT, D_MODEL), lambda i, be, nu: (layer, be[i], 0, 0))
    grid_spec = pltpu.PrefetchScalarGridSpec(
        num_scalar_prefetch=2,
        grid=(n_rows // MOE_BM,),
        in_specs=[rows, w_in, w_in, w_dn],
        out_specs=rows,
        scratch_shapes=[pltpu.VMEM((D_MODEL, D_FF_EXPERT), BF16), pltpu.VMEM((D_MODEL, D_FF_EXPERT), BF16),
                        pltpu.VMEM((D_FF_EXPERT, D_MODEL), BF16)],
    )
    return pl.pallas_call(
        _moe_kernel,
        grid_spec=grid_spec,
        out_shape=jax.ShapeDtypeStruct((n_rows * ROW_TILE, LANES), F32),
        compiler_params=_cparams(("arbitrary",)),
        name="moe_experts",
    )(block_e, n_used, x_pad, wg, wu, wd)


def _combine_kernel(tm, n_steps, pos0_ref, pos1_ref, pos0_next_ref, pos1_next_ref, route_ref, x1_ref, ypad_hbm,
                    o_ref, ybuf, sems):
    i = pl.program_id(0)
    slot = lax.rem(i, 2)

    def gather(p0_ref, p1_ref, s):
        buf = ybuf.at[s]

        def body(r, carry):
            pltpu.make_async_copy(_token_tile(ypad_hbm, p0_ref[r]), _token_tile(buf, r), sems.at[s]).start(
                priority=0)
            pltpu.make_async_copy(_token_tile(ypad_hbm, p1_ref[r]), _token_tile(buf, tm + r), sems.at[s]).start(
                priority=1)
            return carry

        lax.fori_loop(0, tm, body, 0, unroll=8)

    @pl.when(i == 0)
    def _():
        gather(pos0_ref, pos1_ref, 0)

    @pl.when(i + 1 < n_steps)
    def _():
        gather(pos0_next_ref, pos1_next_ref, 1 - slot)

    buf = ybuf.at[slot]
    pltpu.make_async_copy(ypad_hbm.at[pl.ds(0, 2 * tm * ROW_TILE)], buf, sems.at[slot]).wait()
    g1 = route_ref[:, 2:3]
    g2 = route_ref[:, 3:4]
    o_ref[...] = x1_ref[...] + (g1 * _load_token_tiles(buf, 0, tm) + g2 * _load_token_tiles(buf, tm, tm))


def moe_combine(pos, route, x1, y_pad):
    pos0, pos1 = pos
    T = x1.shape[0]
    tm = min(T, COMBINE_TM)
    n_steps = T // tm
    return pl.pallas_call(
        functools.partial(_combine_kernel, tm, n_steps),
        grid=(n_steps,),
        in_specs=[pl.BlockSpec((tm,), lambda i: (i,), memory_space=pltpu.SMEM)] * 2
        + [pl.BlockSpec((tm,), lambda i: (jnp.minimum(i + 1, n_steps - 1),), memory_space=pltpu.SMEM)] * 2
        + [pl.BlockSpec((tm, LANES), lambda i: (i, 0)), pl.BlockSpec((tm, D_MODEL), lambda i: (i, 0)),
           pl.BlockSpec(memory_space=pl.ANY)],
        out_specs=pl.BlockSpec((tm, D_MODEL), lambda i: (i, 0)),
        out_shape=jax.ShapeDtypeStruct((T, D_MODEL), F32),
        scratch_shapes=[pltpu.VMEM((2, 2 * tm * ROW_TILE, LANES), F32), pltpu.SemaphoreType.DMA((2,))],
        compiler_params=_cparams(("arbitrary",)),
        name="moe_combine",
    )(pos0, pos1, pos0, pos1, route, x1, y_pad)


def hier_moe(groups, counts, layer, wg, wu, wd):
    sizes = [g[0].shape[0] for g in groups]
    n_assign = 2 * sum(sizes)
    n_blocks = (n_assign + N_EXPERTS * (MOE_BM - 1) + MOE_BM - 1) // MOE_BM
    fields = jnp.concatenate([g[3] for g in groups], axis=1)
    experts = fields[0:2].astype(jnp.int32)
    ranks = fields[4:6].astype(jnp.int32)
    cnt = counts[0, :N_EXPERTS].astype(jnp.int32)
    padded = ((cnt + MOE_BM - 1) // MOE_BM) * MOE_BM
    pad_ends = jnp.cumsum(padded).astype(jnp.int32)
    pad_starts = pad_ends - padded
    is_e = experts[:, :, None] == jnp.arange(N_EXPERTS, dtype=jnp.int32)
    pos = ranks + jnp.sum(jnp.where(is_e, pad_starts, 0), axis=-1)
    block_start = jnp.arange(n_blocks, dtype=jnp.int32) * MOE_BM
    block_e = jnp.minimum(jnp.sum(block_start[:, None] >= pad_ends[None, :], axis=1), N_EXPERTS - 1).astype(jnp.int32)
    n_used = (pad_ends[-1:] // MOE_BM).astype(jnp.int32)
    starts = [sum(sizes[:g]) for g in range(len(groups))]
    pos_g = [(pos[0, s:s + n], pos[1, s:s + n]) for s, n in zip(starts, sizes)]
    x_pad = moe_dispatch(pad_ends, pos_g, [g[1] for g in groups], n_blocks * MOE_BM)
    y_pad = moe_experts(x_pad, block_e, n_used, layer, wg, wu, wd)
    return [moe_combine(p, g[2], g[0], y_pad) for g, p in zip(groups, pos_g)]


def _rope_tables(pos):
    half = D_HEAD // 2
    inv = ROPE_BASE ** (-jnp.arange(half, dtype=F32) / half)
    ang = pos.astype(F32)[:, None] * inv[None, :]
    cos, sin = jnp.cos(ang), jnp.sin(ang)
    return jnp.concatenate([cos, cos], axis=-1), jnp.concatenate([-sin, sin], axis=-1)


def _router_weights(w_group, b_group, w_router, b_router):
    w = jnp.zeros((D_MODEL, LANES), F32)
    w = w.at[:, :N_EXPERTS].set(w_router).at[:, ROUTER_GROUP_LANE0:ROUTER_GROUP_LANE0 + N_GROUPS].set(w_group)
    b = jnp.zeros((1, LANES), F32)
    b = b.at[0, :N_EXPERTS].set(b_router).at[0, ROUTER_GROUP_LANE0:ROUTER_GROUP_LANE0 + N_GROUPS].set(b_group)
    return w.astype(BF16), b


def _tile(v, n):
    return jnp.tile(v.astype(F32), n)


def _run_trunk(x, B, L, mem_k, mem_v, ret_state0, band_past, P):
    prompt = band_past is None
    out_tm, out_groups = (OUT_PROJ_TM, OUT_PROJ_TM // 512) if prompt else (OUT_PROJ_SAMPLE_TM, 1)
    proj_tm = 512
    scale = D_HEAD ** -0.5
    ones = lambda n: jnp.ones((n * D_HEAD,), F32)
    if prompt:
        cos2, sin2 = _rope_tables(jnp.arange(L, dtype=jnp.int32))
        tab_map = lambda i: i % (L // proj_tm)
    else:
        cos2, sin2 = _rope_tables(PAST_LEN + jnp.arange(L, dtype=jnp.int32))
        cos2, sin2 = jnp.tile(cos2, (B, 1)), jnp.tile(sin2, (B, 1))
        tab_map = lambda i: i
    out = {}

    def ffn(layer, mixed, proj, qm_block, w_out, x):
        def run(counts0):
            return out_proj(
                mixed, proj, qm_block, mem_k, mem_v, layer, w_out.astype(BF16), x, P["norm_ffn"][layer],
                *_router_weights(P["w_group"][layer], P["b_group"][layer], P["w_router"][layer],
                                 P["b_router"][layer]), counts0, tm=out_tm, n_groups=out_groups, rows_per_stream=L)
        return (yield run)

    segs0 = (("rope", N_HEADS), ("rope", N_HEADS), ("plain", N_HEADS), ("silu", N_HEADS), ("norm", H_MEM))
    gain0 = jnp.concatenate([ones(N_HEADS), ones(N_HEADS) * scale, ones(2 * N_HEADS),
                             _tile(P["q_norm_mem"][0], H_MEM) * scale])
    proj = norm_proj(x, P["norm_mix"][0], P["w_in_ret"][0].astype(BF16), gain0, segs0, tm=proj_tm,
                     out_dtype=BF16, rope=(cos2, sin2, tab_map))
    mixed, S = retention(proj, ret_state0, P["gn_ret"][0], B=B, L=L, C=RET_CHUNK if prompt else L)
    out["ret_state"] = S
    x = yield from ffn(0, mixed, proj, (4 * MIX_W) // MEM_W, P["w_out_ret"][0], x)

    segs1 = (("norm", N_HEADS), ("norm", N_HEADS), ("plain", N_HEADS), ("norm", H_MEM))
    gain1 = jnp.concatenate([_tile(P["q_norm_att"][0], N_HEADS) * scale, _tile(P["k_norm_att"][0], N_HEADS),
                             ones(N_HEADS), _tile(P["q_norm_mem"][1], H_MEM) * scale])
    w_in = P["w_in_att"][0].astype(BF16)
    if prompt:
        proj = norm_proj(x, P["norm_mix"][1], w_in, gain1, segs1, tm=proj_tm, out_dtype=BF16)
        keep = min(BAND_PAST, L)
        tiles = L // keep
        kv = norm_proj(x, P["norm_mix"][1], w_in[:, MIX_W:3 * MIX_W], gain1[MIX_W:3 * MIX_W], segs1[1:3],
                       tm=keep, out_dtype=F32, n_tiles=B, row_map=lambda i: i * tiles + tiles - 1)
        out["band_k"] = kv[:, :MIX_W].reshape(B, keep, N_HEADS, D_HEAD)
        out["band_v"] = kv[:, MIX_W:].reshape(B, keep, N_HEADS, D_HEAD)
        mixed = band_prompt(proj, P["rel_bias_att"][0], B=B, L=L)
    else:
        proj_f = norm_proj(x, P["norm_mix"][1], w_in, gain1, segs1, tm=proj_tm, out_dtype=F32)
        out["band_k"] = proj_f[:, MIX_W:2 * MIX_W].reshape(B, L, N_HEADS, D_HEAD)
        out["band_v"] = proj_f[:, 2 * MIX_W:3 * MIX_W].reshape(B, L, N_HEADS, D_HEAD)
        proj = proj_f.astype(BF16)
        mixed = band_sample(proj, band_past[0], band_past[1], P["rel_bias_att"][0], B=B, L=L)
    out["y"] = yield from ffn(1, mixed, proj, (3 * MIX_W) // MEM_W, P["w_out_att"][0], x)
    return out


def kernel(x_prompt, x_sample, mem_prompt, state_ret, cache_band_k, cache_band_v, cache_mem_k, cache_mem_v, norm_mix, norm_ffn, norm_mem, w_in_ret, gn_ret, w_out_ret, w_in_att, q_norm_att, k_norm_att, rel_bias_att, w_out_att, w_mem_kv, q_norm_mem, k_norm_mem, w_group, b_group, w_router, b_router, w_e_gate, w_e_up, w_e_down):
    P = {"norm_mix": norm_mix, "norm_ffn": norm_ffn, "w_in_ret": w_in_ret, "gn_ret": gn_ret,
         "w_out_ret": w_out_ret, "w_in_att": w_in_att, "q_norm_att": q_norm_att,
         "k_norm_att": k_norm_att, "rel_bias_att": rel_bias_att, "w_out_att": w_out_att,
         "q_norm_mem": q_norm_mem, "w_group": w_group, "b_group": b_group, "w_router": w_router,
         "b_router": b_router, "w_e_gate": w_e_gate, "w_e_up": w_e_up, "w_e_down": w_e_down}
    B, L, D = x_prompt.shape
    Bd, Ld, _ = x_sample.shape
    depth = norm_mix.shape[0]

    mem_k_p, mem_v_p = [], []
    for i in range(depth):
        gain = jnp.concatenate([_tile(k_norm_mem[i], H_MEM), jnp.ones((MEM_W,), F32)])
        kv = norm_proj(mem_prompt.reshape(B * N_MEM, D), norm_mem[i], w_mem_kv[i].astype(BF16), gain,
                       (("norm", H_MEM), ("plain", H_MEM)), tm=N_MEM, out_dtype=F32)
        mem_k_p.append(kv[:, :MEM_W].reshape(B, N_MEM, H_MEM, D_HEAD))
        mem_v_p.append(kv[:, MEM_W:].reshape(B, N_MEM, H_MEM, D_HEAD))
    mem_k_p = jnp.stack(mem_k_p)
    mem_v_p = jnp.stack(mem_v_p)

    zeros_state = jnp.zeros((B, N_HEADS, D_HEAD, D_HEAD), F32)
    trunks = [_run_trunk(x_prompt.reshape(B * L, D), B, L, mem_k_p, mem_v_p, zeros_state, None, P),
              _run_trunk(x_sample.reshape(Bd * Ld, D), Bd, Ld, cache_mem_k, cache_mem_v, state_ret[0],
                         (cache_band_k[0], cache_band_v[0]), P)]
    pending = [next(t) for t in trunks]
    results = []
    for layer in range(depth):
        counts = jnp.zeros((SUBLANES, LANES), F32)
        groups = []
        for run in pending:
            x1, h2, route, route_t, counts = run(counts)
            groups.append((x1, h2, route, route_t))
        outs = hier_moe(groups, counts, layer, w_e_gate, w_e_up, w_e_down)
        pending = []
        for t, x2 in zip(trunks, outs):
            try:
                pending.append(t.send(x2))
            except StopIteration as done:
                results.append(done.value)
    res_p, res_s = results

    return (res_p["y"].reshape(B, L, D), res_s["y"].reshape(Bd, Ld, D),
            res_p["ret_state"][None], res_s["ret_state"][None],
            res_p["band_k"][None], res_p["band_v"][None], res_s["band_k"][None], res_s["band_v"][None],
            mem_k_p, mem_v_p)
```

```python
import functools

import jax
import jax.numpy as jnp
from jax import lax
from jax.experimental import pallas as pl
from jax.experimental.pallas import tpu as pltpu

F32 = jnp.float32
BF16 = jnp.bfloat16

D_MODEL = 1024
D_HEAD = 128
CHUNK = 64
N_HEADS = 8
H_MEM = 4
N_MEM = 256
MEM_W = H_MEM * D_HEAD
MIX_W = N_HEADS * D_HEAD
N_PREV_CHUNKS = 8
BAND_PAST = N_PREV_CHUNKS * CHUNK
REL_CLIP = 128
N_GROUPS = 4
EXPERTS_PER_GROUP = 8
N_EXPERTS = N_GROUPS * EXPERTS_PER_GROUP
D_FF_EXPERT = 512
ROPE_BASE = 10000.0
EPS = 1e-6
NEG_INF = -1e30
PAST_LEN = 1024

LANES = 128
SUBLANES = 8

RET_CHUNK = 256
BAND_TQ = 512
BAND_SUB = 128
BAND_WIN = BAND_PAST + BAND_SUB
MOE_BM = 512
DISPATCH_TM = 2048
COMBINE_TM = 1024
PROJ_TM = 1024
OUT_PROJ_TM = 1024
OUT_PROJ_SAMPLE_TM = 128
ROUTE_ROWS = 128
ROUTER_GROUP_LANE0 = N_EXPERTS
VMEM_LIMIT = 56 * 1024 * 1024


def _cparams(sem):
    return pltpu.CompilerParams(dimension_semantics=sem, vmem_limit_bytes=VMEM_LIMIT)


def _dot(a, b):
    return jnp.dot(a, b, preferred_element_type=F32)


def _dot_nt(a, b):
    return lax.dot_general(a, b, (((1,), (1,)), ((), ())), preferred_element_type=F32)


def _sigmoid(x):
    return 1.0 / (1.0 + jnp.exp(-x))


ROW_TILE = D_MODEL // LANES
assert ROW_TILE == SUBLANES


def _load_token_tiles(ref, t0, n):
    return jnp.concatenate([ref[pl.ds(t0 * ROW_TILE + j, n, stride=ROW_TILE), :] for j in range(ROW_TILE)], axis=1)


def _store_token_tiles(ref, t0, val):
    n = val.shape[0]
    for j in range(ROW_TILE):
        ref[pl.ds(t0 * ROW_TILE + j, n, stride=ROW_TILE), :] = val[:, j * LANES:(j + 1) * LANES]


def _token_tile(ref, t):
    return ref.at[pl.ds(pl.multiple_of(t * ROW_TILE, ROW_TILE), ROW_TILE)]


def _norm_proj_kernel(segs, has_rope, x_ref, g_ref, w_ref, gain_ref, *rest):
    if has_rope:
        cos_ref, sin_ref, o_ref = rest
    else:
        (o_ref,) = rest
    x = x_ref[...]
    ms = jnp.mean(x * x, axis=-1, keepdims=True)
    h = (x * lax.rsqrt(ms + EPS) * g_ref[...]).astype(BF16)
    col = 0
    for kind, n_heads in segs:
        width = n_heads * D_HEAD
        acc = _dot(h, w_ref[:, col:col + width])
        for j in range(n_heads):
            a = acc[:, j * D_HEAD:(j + 1) * D_HEAD]
            c0 = col + j * D_HEAD
            gain = gain_ref[:, c0:c0 + D_HEAD]
            if kind == "rope":
                rot = pltpu.roll(a, D_HEAD // 2, 1)
                out = (a * cos_ref[...] + rot * sin_ref[...]) * gain
            elif kind == "norm":
                out = a * lax.rsqrt(jnp.mean(a * a, axis=-1, keepdims=True) + EPS) * gain
            elif kind == "silu":
                out = a * _sigmoid(a)
            else:
                out = a
            o_ref[:, c0:c0 + D_HEAD] = out.astype(o_ref.dtype)
        col += width


def norm_proj(x, g, w, gain, segs, *, tm, out_dtype, n_tiles=None, row_map=None, rope=None):
    T, D = x.shape
    N = w.shape[1]
    assert sum(n for _, n in segs) * D_HEAD == N
    if n_tiles is None:
        n_tiles = T // tm
    if row_map is None:
        row_map = lambda i: i
    in_specs = [
        pl.BlockSpec((tm, D), lambda i: (row_map(i), 0)),
        pl.BlockSpec((1, D), lambda i: (0, 0)),
        pl.BlockSpec((D, N), lambda i: (0, 0)),
        pl.BlockSpec((1, N), lambda i: (0, 0)),
    ]
    args = [x, g.reshape(1, D), w, gain.reshape(1, N)]
    if rope is not None:
        cos2, sin2, tab_map = rope
        in_specs += [pl.BlockSpec((tm, D_HEAD), lambda i: (tab_map(i), 0)),
                     pl.BlockSpec((tm, D_HEAD), lambda i: (tab_map(i), 0))]
        args += [cos2, sin2]
    return pl.pallas_call(
        functools.partial(_norm_proj_kernel, segs, rope is not None),
        grid=(n_tiles,),
        in_specs=in_specs,
        out_specs=pl.BlockSpec((tm, N), lambda i: (i, 0)),
        out_shape=jax.ShapeDtypeStruct((n_tiles * tm, N), out_dtype),
        compiler_params=_cparams(("arbitrary",)),
        name="norm_proj",
    )(*args)


def _retention_kernel(q_ref, k_ref, v_ref, g_ref, s0_ref, dmask_ref, qdec_ref, kdec_ref, gc_ref,
                      gn_ref, o_ref, s_ref):
    @pl.when(pl.program_id(1) == 0)
    def _():
        s_ref[...] = s0_ref[...]

    for h in range(N_HEADS):
        sl = slice(h * D_HEAD, (h + 1) * D_HEAD)
        q = q_ref[:, sl]
        k = k_ref[:, sl]
        v = v_ref[:, sl]
        S = s_ref[0, h]
        scores = _dot_nt(q, k) * dmask_ref[h]
        intra = _dot(scores.astype(BF16), v)
        cross = _dot(q, S.astype(BF16)) * qdec_ref[:, sl]
        o = intra + cross
        kd = (k.astype(F32) * kdec_ref[:, sl]).astype(BF16)
        s_ref[0, h] = gc_ref[h] * S + lax.dot_general(kd, v, (((0,), (0,)), ((), ())), preferred_element_type=F32)
        mu = jnp.mean(o, axis=-1, keepdims=True)
        d = o - mu
        var = jnp.mean(d * d, axis=-1, keepdims=True)
        y = d * lax.rsqrt(var + EPS) * gn_ref[:, sl] * g_ref[:, sl].astype(F32)
        o_ref[:, sl] = y.astype(o_ref.dtype)


def _retention_tables(C):
    log_gamma = jnp.log1p(-jnp.exp2(-5.0 - jnp.arange(N_HEADS, dtype=F32)))
    idx = jnp.arange(C, dtype=F32)
    diff = idx[:, None] - idx[None, :]
    dmask = jnp.where(diff[None] >= 0,
                      jnp.exp(log_gamma[:, None, None] * jnp.maximum(diff, 0.0)[None]), 0.0)
    q_dec = jnp.exp(log_gamma[None, :] * (idx + 1.0)[:, None])
    k_dec = jnp.exp(log_gamma[None, :] * (C - 1.0 - idx)[:, None])
    g_c = jnp.exp(log_gamma * C)
    rep = lambda t: jnp.repeat(t, D_HEAD, axis=-1)
    return dmask, rep(q_dec), rep(k_dec), rep(g_c[:, None])[:, None, :]


def retention(proj, s0, gn, *, B, L, C):
    nc = L // C
    dmask, qdec, kdec, gc = _retention_tables(C)
    row = lambda col: pl.BlockSpec((C, MIX_W), lambda b, c: (b * nc + c, col))
    const = lambda shape: pl.BlockSpec(shape, lambda b, c: (0,) * len(shape))
    state_spec = pl.BlockSpec((1, N_HEADS, D_HEAD, D_HEAD), lambda b, c: (b, 0, 0, 0))
    return pl.pallas_call(
        _retention_kernel,
        grid=(B, nc),
        in_specs=[row(0), row(1), row(2), row(3), state_spec,
                  const((N_HEADS, C, C)), const((C, MIX_W)), const((C, MIX_W)),
                  const((N_HEADS, 1, D_HEAD)), const((1, MIX_W))],
        out_specs=[pl.BlockSpec((C, MIX_W), lambda b, c: (b * nc + c, 0)), state_spec],
        out_shape=[jax.ShapeDtypeStruct((B * L, MIX_W), BF16),
                   jax.ShapeDtypeStruct((B, N_HEADS, D_HEAD, D_HEAD), F32)],
        compiler_params=_cparams(("arbitrary", "arbitrary")),
        name="retention",
    )(proj, proj, proj, proj, s0, dmask, qdec, kdec, gc, gn.reshape(1, MIX_W))


def _attend_two_piece(q, k_a, k_b, v_a, v_b, bias_a, bias_b, a_valid):
    s_a = _dot_nt(q, k_a) + bias_a
    if a_valid is not None:
        s_a = jnp.where(a_valid, s_a, NEG_INF)
    s_b = _dot_nt(q, k_b) + bias_b
    m = jnp.maximum(jnp.max(s_a, axis=-1, keepdims=True), jnp.max(s_b, axis=-1, keepdims=True))
    p_a = jnp.exp(s_a - m)
    p_b = jnp.exp(s_b - m)
    denom = jnp.sum(p_a, axis=-1, keepdims=True) + jnp.sum(p_b, axis=-1, keepdims=True)
    o = _dot(p_a.astype(BF16), v_a) + _dot(p_b.astype(BF16), v_b)
    return o / denom


def _band_prompt_kernel(q_ref, kp_ref, kc_ref, vp_ref, vc_ref, bias_ref, o_ref, s_buf, p_buf):
    @pl.when((pl.program_id(0) == 0) & (pl.program_id(1) == 0))
    def _():
        p_buf[...] = jnp.zeros_like(p_buf)

    prev_off = jnp.where(pl.program_id(1) > 0, 0.0, NEG_INF).astype(F32)
    for h in range(N_HEADS):
        sl = slice(h * D_HEAD, (h + 1) * D_HEAD)
        q = q_ref[:, sl]
        s_buf[:, :BAND_TQ] = _dot_nt(q, kp_ref[:, sl]) + prev_off
        s_buf[:, BAND_TQ:] = _dot_nt(q, kc_ref[:, sl])
        denoms = []
        for s in range(BAND_TQ // BAND_SUB):
            r0 = s * BAND_SUB
            sc = s_buf[r0:r0 + BAND_SUB, r0:r0 + BAND_WIN] + bias_ref[h]
            p = jnp.exp(sc - jnp.max(sc, axis=-1, keepdims=True))
            denoms.append(jnp.sum(p, axis=-1, keepdims=True))
            p_buf[r0:r0 + BAND_SUB, r0:r0 + BAND_WIN] = p.astype(BF16)
        o = _dot(p_buf[:, :BAND_TQ], vp_ref[:, sl]) + _dot(p_buf[:, BAND_TQ:], vc_ref[:, sl])
        o_ref[:, sl] = (o / jnp.concatenate(denoms, axis=0)).astype(o_ref.dtype)


def _rel_bias(bias_table, offset, n_rows, width):
    n = n_rows + width - 1
    v = jnp.arange(n, dtype=jnp.int32)
    v = jnp.where(v < width, v, v - n)
    diag = bias_table.astype(F32)[:, jnp.clip(offset - v, -REL_CLIP, REL_CLIP) + REL_CLIP]
    cut = jnp.tile(diag, (1, n_rows))[:, :n_rows * (n - 1)].reshape(diag.shape[0], n_rows, n - 1)
    return cut[:, :, :width]


def _band_prompt_bias(bias_table):
    r = jnp.arange(BAND_SUB, dtype=jnp.int32)[:, None]
    j = jnp.arange(BAND_WIN, dtype=jnp.int32)[None, :]
    q_chunk = r // CHUNK + N_PREV_CHUNKS
    k_chunk = j // CHUNK
    allowed = (k_chunk <= q_chunk) & (k_chunk >= q_chunk - N_PREV_CHUNKS)
    return jnp.where(allowed[None], _rel_bias(bias_table, BAND_PAST, BAND_SUB, BAND_WIN), NEG_INF)


def band_prompt(proj, bias_table, *, B, L):
    nq = L // BAND_TQ
    cur = lambda col: pl.BlockSpec((BAND_TQ, MIX_W), lambda b, i: (b * nq + i, col))
    prev = lambda col: pl.BlockSpec((BAND_TQ, MIX_W), lambda b, i: (b * nq + jnp.maximum(i - 1, 0), col))
    return pl.pallas_call(
        _band_prompt_kernel,
        grid=(B, nq),
        in_specs=[cur(0), prev(1), cur(1), prev(2), cur(2),
                  pl.BlockSpec((N_HEADS, BAND_SUB, BAND_WIN), lambda b, i: (0, 0, 0))],
        out_specs=pl.BlockSpec((BAND_TQ, MIX_W), lambda b, i: (b * nq + i, 0)),
        out_shape=jax.ShapeDtypeStruct((B * L, MIX_W), BF16),
        scratch_shapes=[pltpu.VMEM((BAND_TQ, 2 * BAND_TQ), F32), pltpu.VMEM((BAND_TQ, 2 * BAND_TQ), BF16)],
        compiler_params=_cparams(("arbitrary", "arbitrary")),
        name="band_prompt",
    )(proj, proj, proj, proj, proj, _band_prompt_bias(bias_table))


def _band_sample_kernel(q_ref, kp_ref, kc_ref, vp_ref, vc_ref, bias_p_ref, bias_c_ref, o_ref):
    n_past = kp_ref.shape[1] // N_HEADS
    for h in range(N_HEADS):
        sl = slice(h * D_HEAD, (h + 1) * D_HEAD)
        kp = kp_ref.at[0][pl.ds(h, n_past, stride=N_HEADS), :].astype(BF16)
        vp = vp_ref.at[0][pl.ds(h, n_past, stride=N_HEADS), :].astype(BF16)
        o = _attend_two_piece(q_ref[:, sl], kp, kc_ref[:, sl], vp, vc_ref[:, sl], bias_p_ref[h], bias_c_ref[h], None)
        o_ref[:, sl] = o.astype(o_ref.dtype)


def band_sample(proj, past_k, past_v, bias_table, *, B, L):
    P = past_k.shape[1]
    qpos = PAST_LEN + jnp.arange(L, dtype=jnp.int32)
    kpos = jnp.concatenate([PAST_LEN - P + jnp.arange(P, dtype=jnp.int32), qpos])
    qc = (qpos // CHUNK)[:, None]
    kc = (kpos // CHUNK)[None, :]
    allowed = (kpos[None, :] >= 0) & (kc <= qc) & (kc >= qc - N_PREV_CHUNKS)
    rel_bias = jnp.concatenate([_rel_bias(bias_table, P, L, P), _rel_bias(bias_table, 0, L, L)], axis=-1)
    bias = jnp.where(allowed[None], rel_bias, NEG_INF)
    cur = lambda col: pl.BlockSpec((L, MIX_W), lambda b: (b, col))
    past = pl.BlockSpec((1, P * N_HEADS, D_HEAD), lambda b: (b, 0, 0))
    past_k = past_k.reshape(B, P * N_HEADS, D_HEAD)
    past_v = past_v.reshape(B, P * N_HEADS, D_HEAD)
    return pl.pallas_call(
        _band_sample_kernel,
        grid=(B,),
        in_specs=[cur(0), past, cur(1), past, cur(2),
                  pl.BlockSpec((N_HEADS, L, P), lambda b: (0, 0, 0)),
                  pl.BlockSpec((N_HEADS, L, L), lambda b: (0, 0, 0))],
        out_specs=pl.BlockSpec((L, MIX_W), lambda b: (b, 0)),
        out_shape=jax.ShapeDtypeStruct((B * L, MIX_W), BF16),
        compiler_params=_cparams(("arbitrary",)),
        name="band_sample",
    )(proj, past_k, proj, past_v, proj, bias[:, :, :P], bias[:, :, P:])


def _out_proj_kernel(n_streams, n_groups, o_ref, qm_ref, mk_ref, mv_ref, w_ref, x_ref, gf_ref, wr_ref, br_ref,
                     tri_ref, cnt0_ref, x1_ref, h2_ref, route_ref, route_t_ref, cnt_ref):
    @pl.when(pl.program_id(0) == 0)
    def _():
        cnt_ref[...] = cnt0_ref[...]

    tm = x_ref.shape[0]
    sr = tm // n_streams
    gr = tm // n_groups
    om_heads = []
    for h in range(H_MEM):
        sl = slice(h * D_HEAD, (h + 1) * D_HEAD)
        chunks = []
        for c in range(n_streams):
            mk = mk_ref.at[0, c][pl.ds(h, N_MEM, stride=H_MEM), :].astype(BF16)
            mv = mv_ref.at[0, c][pl.ds(h, N_MEM, stride=H_MEM), :].astype(BF16)
            s = _dot_nt(qm_ref[c * sr:(c + 1) * sr, sl], mk)
            p = jnp.exp(s - jnp.max(s, axis=-1, keepdims=True))
            om = _dot(p.astype(BF16), mv) / jnp.sum(p, axis=-1, keepdims=True)
            chunks.append(om.astype(BF16))
        om_heads.append(chunks[0] if n_streams == 1 else jnp.concatenate(chunks, axis=0))

    rows = min(gr, ROUTE_ROWS)
    lane = lax.broadcasted_iota(jnp.int32, (rows, LANES), 1)
    for g in range(n_groups):
        g0 = g * gr
        feat = jnp.concatenate([o_ref[g0:g0 + gr, :]] + [om[g0:g0 + gr] for om in om_heads], axis=1)
        x1 = x_ref[g0:g0 + gr, :] + _dot(feat, w_ref[...])
        x1_ref[g0:g0 + gr, :] = x1
        hn = x1 * lax.rsqrt(jnp.mean(x1 * x1, axis=-1, keepdims=True) + EPS) * gf_ref[...]
        _store_token_tiles(h2_ref, g0, hn)

        logits = _dot(hn.astype(BF16), wr_ref[...]) + br_ref[...]
        tops = [_route_top2(logits[r0:r0 + rows]) for r0 in range(0, gr, rows)]

        onehot = jnp.concatenate([jnp.where(t[4] | t[5], 1.0, 0.0) for t in tops], axis=0)
        before = cnt_ref[0:1, :] + _dot(tri_ref[...], onehot.astype(BF16))
        cnt_ref[0:1, :] = cnt_ref[0:1, :] + jnp.sum(onehot, axis=0, keepdims=True)
        for k, (i1, i2, w1, w2, oh1, oh2) in enumerate(tops):
            b = before[k * rows:(k + 1) * rows]
            r1 = jnp.sum(jnp.where(oh1, b, 0.0), axis=-1, keepdims=True)
            r2 = jnp.sum(jnp.where(oh2, b, 0.0), axis=-1, keepdims=True)
            route = jnp.where(lane == 0, i1, jnp.where(lane == 1, i2, jnp.where(lane == 2, w1, jnp.where(
                lane == 3, w2, jnp.where(lane == 4, r1, jnp.where(lane == 5, r2, 0.0))))))
            c0 = g0 + k * rows
            route_ref[c0:c0 + rows, :] = route
            route_t_ref[:, c0:c0 + rows] = route.T[:SUBLANES, :]


def _route_top2(logits):
    lane = lax.broadcasted_iota(jnp.int32, logits.shape, 1)
    lane_f = lane.astype(F32)
    far = float(LANES)
    is_group = (lane >= ROUTER_GROUP_LANE0) & (lane < ROUTER_GROUP_LANE0 + N_GROUPS)
    gl = jnp.where(is_group, logits, NEG_INF)
    g_max = jnp.max(gl, axis=-1, keepdims=True)
    g_sel = jnp.min(jnp.where(gl == g_max, lane_f, far), axis=-1, keepdims=True) - float(ROUTER_GROUP_LANE0)
    g_gate = 1.0 / jnp.sum(jnp.exp(gl - g_max), axis=-1, keepdims=True)
    in_group = jnp.right_shift(lane, 3).astype(F32) == g_sel
    el = jnp.where(in_group, logits, NEG_INF)
    m1 = jnp.max(el, axis=-1, keepdims=True)
    i1 = jnp.min(jnp.where(el == m1, lane_f, far), axis=-1, keepdims=True)
    el2 = jnp.where(lane_f == i1, NEG_INF, el)
    m2 = jnp.max(el2, axis=-1, keepdims=True)
    i2 = jnp.min(jnp.where(el2 == m2, lane_f, far), axis=-1, keepdims=True)
    e2 = jnp.exp(m2 - m1)
    p1 = 1.0 / (1.0 + e2)
    return i1, i2, g_gate * p1, g_gate * (e2 * p1), lane_f == i1, lane_f == i2


def out_proj(mixed, proj, qm_block, mem_k, mem_v, layer, w_out, x, g_ffn, w_router, b_router, counts0, *, tm,
             n_groups, rows_per_stream):
    T = x.shape[0]
    n_streams = max(1, tm // rows_per_stream)
    tiles_per_stream = max(1, rows_per_stream // tm)
    gr = tm // n_groups
    assert min(gr, ROUTE_ROWS) % LANES == 0
    row = lambda width, col: pl.BlockSpec((tm, width), lambda i: (i, col))
    const = lambda shape: pl.BlockSpec(shape, lambda i: (0,) * len(shape))
    mem = pl.BlockSpec((1, n_streams, N_MEM * H_MEM, D_HEAD), lambda i: (layer, i // tiles_per_stream, 0, 0))
    mem_k = mem_k.reshape(mem_k.shape[:2] + (N_MEM * H_MEM, D_HEAD))
    mem_v = mem_v.reshape(mem_v.shape[:2] + (N_MEM * H_MEM, D_HEAD))
    tri = (jnp.arange(gr)[None, :] < jnp.arange(gr)[:, None]).astype(BF16)
    return pl.pallas_call(
        functools.partial(_out_proj_kernel, n_streams, n_groups),
        grid=(T // tm,),
        in_specs=[row(MIX_W, 0), row(MEM_W, qm_block), mem, mem, const((MIX_W + MEM_W, D_MODEL)),
                  row(D_MODEL, 0), const((1, D_MODEL)), const((D_MODEL, LANES)), const((1, LANES)),
                  const((gr, gr)), const((SUBLANES, LANES))],
        out_specs=[row(D_MODEL, 0), pl.BlockSpec((tm * ROW_TILE, LANES), lambda i: (i, 0)), row(LANES, 0),
                   pl.BlockSpec((SUBLANES, tm), lambda i: (0, i)), const((SUBLANES, LANES))],
        out_shape=[jax.ShapeDtypeStruct((T, D_MODEL), F32), jax.ShapeDtypeStruct((T * ROW_TILE, LANES), F32),
                   jax.ShapeDtypeStruct((T, LANES), F32), jax.ShapeDtypeStruct((SUBLANES, T), F32),
                   jax.ShapeDtypeStruct((SUBLANES, LANES), F32)],
        compiler_params=_cparams(("arbitrary",)),
        name="out_proj",
    )(mixed, proj, mem_k, mem_v, w_out, x, g_ffn.reshape(1, D_MODEL), w_router, b_router, tri, counts0)


def _dispatch_kernel(tms, steps, pe_ref, pos0_ref, pos1_ref, *rest):
    h2_refs = rest[:len(tms)]
    xpad_hbm, zbuf, sem, zsem = rest[len(tms):]
    i = pl.program_id(0)

    @pl.when(i == 0)
    def _():
        zbuf[...] = jnp.zeros_like(zbuf)

        def block_copy(first_row):
            start = pl.multiple_of(first_row * ROW_TILE, MOE_BM * ROW_TILE)
            return pltpu.make_async_copy(zbuf, xpad_hbm.at[pl.ds(start, MOE_BM * ROW_TILE)], zsem)

        def tail_copy(e):
            return block_copy(pe_ref[e] - MOE_BM)

        def nonempty(e):
            return pe_ref[e] > (pe_ref[e - 1] if e > 0 else 0)

        for e in range(N_EXPERTS):
            pl.when(nonempty(e))(lambda e=e: tail_copy(e).start())
        for e in range(N_EXPERTS):
            pl.when(nonempty(e))(lambda e=e: tail_copy(e).wait())

        def fill(b, carry):
            cp = block_copy(b * MOE_BM)
            cp.start()
            cp.wait()
            return carry

        lax.fori_loop(pe_ref[N_EXPERTS - 1] // MOE_BM, xpad_hbm.shape[0] // (MOE_BM * ROW_TILE), fill, 0)

    def scatter_rows(h2_ref, tm):
        def body(r, carry):
            src = _token_tile(h2_ref, r)
            pltpu.make_async_copy(src, _token_tile(xpad_hbm, pos0_ref[r]), sem).start(priority=0)
            pltpu.make_async_copy(src, _token_tile(xpad_hbm, pos1_ref[r]), sem).start(priority=1)
            return carry

        lax.fori_loop(0, tm, body, 0, unroll=8)
        for _ in range(2):
            pltpu.make_async_copy(h2_ref, xpad_hbm.at[pl.ds(0, tm * ROW_TILE)], sem).wait()

    first = 0
    for h2_ref, tm, n in zip(h2_refs, tms, steps):
        pl.when(jnp.logical_and(i >= first, i < first + n))(functools.partial(scatter_rows, h2_ref, tm))
        first += n


def moe_dispatch(pad_ends, pos_groups, h2_groups, n_rows):
    sizes = [h2.shape[0] // ROW_TILE for h2 in h2_groups]
    tms = [min(T, DISPATCH_TM) for T in sizes]
    steps = [T // tm for T, tm in zip(sizes, tms)]
    firsts = [sum(steps[:g]) for g in range(len(steps))]
    pos0, pos1 = (jnp.concatenate([jnp.pad(p[k], (0, n * DISPATCH_TM - p[k].shape[0]))
                                   for p, n in zip(pos_groups, steps)]) for k in range(2))

    def rows(tm, first, n):
        return pl.BlockSpec((tm * ROW_TILE, LANES), lambda i, pe: (jnp.clip(i - first, 0, n - 1), 0))

    grid_spec = pltpu.PrefetchScalarGridSpec(
        num_scalar_prefetch=1,
        grid=(sum(steps),),
        in_specs=[pl.BlockSpec((DISPATCH_TM,), lambda i, pe: (i,), memory_space=pltpu.SMEM)] * 2
        + [rows(tm, first, n) for tm, first, n in zip(tms, firsts, steps)],
        out_specs=pl.BlockSpec(memory_space=pl.ANY),
        scratch_shapes=[pltpu.VMEM((MOE_BM * ROW_TILE, LANES), F32), pltpu.SemaphoreType.DMA(()),
                        pltpu.SemaphoreType.DMA(())],
    )
    return pl.pallas_call(
        functools.partial(_dispatch_kernel, tuple(tms), tuple(steps)),
        grid_spec=grid_spec,
        out_shape=jax.ShapeDtypeStruct((n_rows * ROW_TILE, LANES), F32),
        compiler_params=_cparams(("arbitrary",)),
        name="moe_dispatch",
    )(pad_ends, pos0, pos1, *h2_groups)


def _moe_kernel(be_ref, nu_ref, x_ref, wg_ref, wu_ref, wd_ref, y_ref, wg_s, wu_s, wd_s):
    i = pl.program_id(0)
    used = i < nu_ref[0]
    new_expert = jnp.logical_or(i == 0, be_ref[i] != be_ref[jnp.maximum(i - 1, 0)])

    @pl.when(jnp.logical_and(used, new_expert))
    def _():
        wg_s[...] = wg_ref[0, 0].astype(BF16)
        wu_s[...] = wu_ref[0, 0].astype(BF16)
        wd_s[...] = wd_ref[0, 0].astype(BF16)

    @pl.when(used)
    def _():
        x = _load_token_tiles(x_ref, 0, MOE_BM).astype(BF16)
        g = _dot(x, wg_s[...])
        u = _dot(x, wu_s[...])
        a = (g * _sigmoid(g) * u).astype(BF16)
        _store_token_tiles(y_ref, 0, _dot(a, wd_s[...]))

    @pl.when(jnp.logical_not(used))
    def _():
        y_ref[...] = jnp.zeros_like(y_ref)


def moe_experts(x_pad, block_e, n_used, layer, wg, wu, wd):
    n_rows = x_pad.shape[0] // ROW_TILE
    rows = pl.BlockSpec((MOE_BM * ROW_TILE, LANES), lambda i, be, nu: (i, 0))
    w_in = pl.BlockSpec((1, 1, D_MODEL, D_FF_EXPERT), lambda i, be, nu: (layer, be[i], 0, 0))
    w_dn = pl.BlockSpec((1, 1, D_FF_EXPERT, D_MODEL), lambda i, be, nu: (layer, be[i], 0, 0))
    grid_spec = pltpu.PrefetchScalarGridSpec(
        num_scalar_prefetch=2,
        grid=(n_rows // MOE_BM,),
        in_specs=[pl.BlockSpec((MOE_BM * ROW_TILE, LANES), lambda i, be, nu: (jnp.minimum(i, nu[0] - 1), 0)),
                  w_in, w_in, w_dn],
        out_specs=rows,
        scratch_shapes=[pltpu.VMEM((D_MODEL, D_FF_EXPERT), BF16), pltpu.VMEM((D_MODEL, D_FF_EXPERT), BF16),
                        pltpu.VMEM((D_FF_EXPERT, D_MODEL), BF16)],
    )
    return pl.pallas_call(
        _moe_kernel,
        grid_spec=grid_spec,
        out_shape=jax.ShapeDtypeStruct((n_rows * ROW_TILE, LANES), F32),
        compiler_params=_cparams(("arbitrary",)),
        name="moe_experts",
    )(block_e, n_used, x_pad, wg, wu, wd)


def _combine_kernel(tm, n_steps, pos0_ref, pos1_ref, pos0_next_ref, pos1_next_ref, route_ref, x1_ref, ypad_hbm,
                    o_ref, ybuf, sems):
    i = pl.program_id(0)
    slot = lax.rem(i, 2)

    def gather(p0_ref, p1_ref, s):
        buf = ybuf.at[s]

        def body(r, carry):
            pltpu.make_async_copy(_token_tile(ypad_hbm, p0_ref[r]), _token_tile(buf, r), sems.at[s]).start(
                priority=0)
            pltpu.make_async_copy(_token_tile(ypad_hbm, p1_ref[r]), _token_tile(buf, tm + r), sems.at[s]).start(
                priority=1)
            return carry

        lax.fori_loop(0, tm, body, 0, unroll=8)

    @pl.when(i == 0)
    def _():
        gather(pos0_ref, pos1_ref, 0)

    @pl.when(i + 1 < n_steps)
    def _():
        gather(pos0_next_ref, pos1_next_ref, 1 - slot)

    buf = ybuf.at[slot]
    pltpu.make_async_copy(ypad_hbm.at[pl.ds(0, 2 * tm * ROW_TILE)], buf, sems.at[slot]).wait()
    g1 = route_ref[:, 2:3]
    g2 = route_ref[:, 3:4]
    o_ref[...] = x1_ref[...] + (g1 * _load_token_tiles(buf, 0, tm) + g2 * _load_token_tiles(buf, tm, tm))


def moe_combine(pos, route, x1, y_pad):
    pos0, pos1 = pos
    T = x1.shape[0]
    tm = min(T, COMBINE_TM)
    n_steps = T // tm
    return pl.pallas_call(
        functools.partial(_combine_kernel, tm, n_steps),
        grid=(n_steps,),
        in_specs=[pl.BlockSpec((tm,), lambda i: (i,), memory_space=pltpu.SMEM)] * 2
        + [pl.BlockSpec((tm,), lambda i: (jnp.minimum(i + 1, n_steps - 1),), memory_space=pltpu.SMEM)] * 2
        + [pl.BlockSpec((tm, LANES), lambda i: (i, 0)), pl.BlockSpec((tm, D_MODEL), lambda i: (i, 0)),
           pl.BlockSpec(memory_space=pl.ANY)],
        out_specs=pl.BlockSpec((tm, D_MODEL), lambda i: (i, 0)),
        out_shape=jax.ShapeDtypeStruct((T, D_MODEL), F32),
        scratch_shapes=[pltpu.VMEM((2, 2 * tm * ROW_TILE, LANES), F32), pltpu.SemaphoreType.DMA((2,))],
        compiler_params=_cparams(("arbitrary",)),
        name="moe_combine",
    )(pos0, pos1, pos0, pos1, route, x1, y_pad)


def hier_moe(groups, counts, layer, wg, wu, wd):
    sizes = [g[0].shape[0] for g in groups]
    n_assign = 2 * sum(sizes)
    n_blocks = (n_assign + N_EXPERTS * (MOE_BM - 1) + MOE_BM - 1) // MOE_BM
    fields = jnp.concatenate([g[3] for g in groups], axis=1)
    experts = fields[0:2].astype(jnp.int32)
    ranks = fields[4:6].astype(jnp.int32)
    cnt = counts[0, :N_EXPERTS].astype(jnp.int32)
    padded = ((cnt + MOE_BM - 1) // MOE_BM) * MOE_BM
    pad_ends = jnp.cumsum(padded).astype(jnp.int32)
    pad_starts = pad_ends - padded
    is_e = experts[:, :, None] == jnp.arange(N_EXPERTS, dtype=jnp.int32)
    pos = ranks + jnp.sum(jnp.where(is_e, pad_starts, 0), axis=-1)
    block_start = jnp.arange(n_blocks, dtype=jnp.int32) * MOE_BM
    block_e = jnp.minimum(jnp.sum(block_start[:, None] >= pad_ends[None, :], axis=1), N_EXPERTS - 1).astype(jnp.int32)
    n_used = (pad_ends[-1:] // MOE_BM).astype(jnp.int32)
    starts = [sum(sizes[:g]) for g in range(len(groups))]
    pos_g = [(pos[0, s:s + n], pos[1, s:s + n]) for s, n in zip(starts, sizes)]
    x_pad = moe_dispatch(pad_ends, pos_g, [g[1] for g in groups], n_blocks * MOE_BM)
    y_pad = moe_experts(x_pad, block_e, n_used, layer, wg, wu, wd)
    return [moe_combine(p, g[2], g[0], y_pad) for g, p in zip(groups, pos_g)]


def _rope_tables(pos):
    half = D_HEAD // 2
    inv = ROPE_BASE ** (-jnp.arange(half, dtype=F32) / half)
    ang = pos.astype(F32)[:, None] * inv[None, :]
    cos, sin = jnp.cos(ang), jnp.sin(ang)
    return jnp.concatenate([cos, cos], axis=-1), jnp.concatenate([-sin, sin], axis=-1)


def _router_weights(w_group, b_group, w_router, b_router):
    w = jnp.zeros((D_MODEL, LANES), F32)
    w = w.at[:, :N_EXPERTS].set(w_router).at[:, ROUTER_GROUP_LANE0:ROUTER_GROUP_LANE0 + N_GROUPS].set(w_group)
    b = jnp.zeros((1, LANES), F32)
    b = b.at[0, :N_EXPERTS].set(b_router).at[0, ROUTER_GROUP_LANE0:ROUTER_GROUP_LANE0 + N_GROUPS].set(b_group)
    return w.astype(BF16), b


def _tile(v, n):
    return jnp.tile(v.astype(F32), n)


def _run_trunk(x, B, L, mem_k, mem_v, ret_state0, band_past, P):
    prompt = band_past is None
    out_tm, out_groups = (OUT_PROJ_TM, OUT_PROJ_TM // 512) if prompt else (OUT_PROJ_SAMPLE_TM, 1)
    proj_tm = min(B * L, PROJ_TM)
    scale = D_HEAD ** -0.5
    ones = lambda n: jnp.ones((n * D_HEAD,), F32)
    if prompt:
        cos2, sin2 = _rope_tables(jnp.arange(L, dtype=jnp.int32))
        tab_map = lambda i: i % (L // proj_tm)
    else:
        cos2, sin2 = _rope_tables(PAST_LEN + jnp.arange(L, dtype=jnp.int32))
        cos2, sin2 = jnp.tile(cos2, (B, 1)), jnp.tile(sin2, (B, 1))
        tab_map = lambda i: i
    out = {}

    def ffn(layer, mixed, proj, qm_block, w_out, x):
        def run(counts0):
            return out_proj(
                mixed, proj, qm_block, mem_k, mem_v, layer, w_out.astype(BF16), x, P["norm_ffn"][layer],
                *_router_weights(P["w_group"][layer], P["b_group"][layer], P["w_router"][layer],
                                 P["b_router"][layer]), counts0, tm=out_tm, n_groups=out_groups, rows_per_stream=L)
        return (yield run)

    segs0 = (("rope", N_HEADS), ("rope", N_HEADS), ("plain", N_HEADS), ("silu", N_HEADS), ("norm", H_MEM))
    gain0 = jnp.concatenate([ones(N_HEADS), ones(N_HEADS) * scale, ones(2 * N_HEADS),
                             _tile(P["q_norm_mem"][0], H_MEM) * scale])
    proj = norm_proj(x, P["norm_mix"][0], P["w_in_ret"][0].astype(BF16), gain0, segs0, tm=proj_tm,
                     out_dtype=BF16, rope=(cos2, sin2, tab_map))
    mixed, S = retention(proj, ret_state0, P["gn_ret"][0], B=B, L=L, C=RET_CHUNK if prompt else L)
    out["ret_state"] = S
    x = yield from ffn(0, mixed, proj, (4 * MIX_W) // MEM_W, P["w_out_ret"][0], x)

    segs1 = (("norm", N_HEADS), ("norm", N_HEADS), ("plain", N_HEADS), ("norm", H_MEM))
    gain1 = jnp.concatenate([_tile(P["q_norm_att"][0], N_HEADS) * scale, _tile(P["k_norm_att"][0], N_HEADS),
                             ones(N_HEADS), _tile(P["q_norm_mem"][1], H_MEM) * scale])
    w_in = P["w_in_att"][0].astype(BF16)
    if prompt:
        proj = norm_proj(x, P["norm_mix"][1], w_in, gain1, segs1, tm=proj_tm, out_dtype=BF16)
        keep = min(BAND_PAST, L)
        tiles = L // keep
        kv = norm_proj(x, P["norm_mix"][1], w_in[:, MIX_W:3 * MIX_W], gain1[MIX_W:3 * MIX_W], segs1[1:3],
                       tm=keep, out_dtype=F32, n_tiles=B, row_map=lambda i: i * tiles + tiles - 1)
        out["band_k"] = kv[:, :MIX_W].reshape(B, keep, N_HEADS, D_HEAD)
        out["band_v"] = kv[:, MIX_W:].reshape(B, keep, N_HEADS, D_HEAD)
        mixed = band_prompt(proj, P["rel_bias_att"][0], B=B, L=L)
    else:
        proj_f = norm_proj(x, P["norm_mix"][1], w_in, gain1, segs1, tm=proj_tm, out_dtype=F32)
        out["band_k"] = proj_f[:, MIX_W:2 * MIX_W].reshape(B, L, N_HEADS, D_HEAD)
        out["band_v"] = proj_f[:, 2 * MIX_W:3 * MIX_W].reshape(B, L, N_HEADS, D_HEAD)
        proj = proj_f.astype(BF16)
        mixed = band_sample(proj, band_past[0], band_past[1], P["rel_bias_att"][0], B=B, L=L)
    out["y"] = yield from ffn(1, mixed, proj, (3 * MIX_W) // MEM_W, P["w_out_att"][0], x)
    return out


def kernel(x_prompt, x_sample, mem_prompt, state_ret, cache_band_k, cache_band_v, cache_mem_k, cache_mem_v, norm_mix, norm_ffn, norm_mem, w_in_ret, gn_ret, w_out_ret, w_in_att, q_norm_att, k_norm_att, rel_bias_att, w_out_att, w_mem_kv, q_norm_mem, k_norm_mem, w_group, b_group, w_router, b_router, w_e_gate, w_e_up, w_e_down):
    P = {"norm_mix": norm_mix, "norm_ffn": norm_ffn, "w_in_ret": w_in_ret, "gn_ret": gn_ret,
         "w_out_ret": w_out_ret, "w_in_att": w_in_att, "q_norm_att": q_norm_att,
         "k_norm_att": k_norm_att, "rel_bias_att": rel_bias_att, "w_out_att": w_out_att,
         "q_norm_mem": q_norm_mem, "w_group": w_group, "b_group": b_group, "w_router": w_router,
         "b_router": b_router, "w_e_gate": w_e_gate, "w_e_up": w_e_up, "w_e_down": w_e_down}
    B, L, D = x_prompt.shape
    Bd, Ld, _ = x_sample.shape
    depth = norm_mix.shape[0]

    mem_k_p, mem_v_p = [], []
    for i in range(depth):
        gain = jnp.concatenate([_tile(k_norm_mem[i], H_MEM), jnp.ones((MEM_W,), F32)])
        kv = norm_proj(mem_prompt.reshape(B * N_MEM, D), norm_mem[i], w_mem_kv[i].astype(BF16), gain,
                       (("norm", H_MEM), ("plain", H_MEM)), tm=N_MEM, out_dtype=F32)
        mem_k_p.append(kv[:, :MEM_W].reshape(B, N_MEM, H_MEM, D_HEAD))
        mem_v_p.append(kv[:, MEM_W:].reshape(B, N_MEM, H_MEM, D_HEAD))
    mem_k_p = jnp.stack(mem_k_p)
    mem_v_p = jnp.stack(mem_v_p)

    zeros_state = jnp.zeros((B, N_HEADS, D_HEAD, D_HEAD), F32)
    trunks = [_run_trunk(x_prompt.reshape(B * L, D), B, L, mem_k_p, mem_v_p, zeros_state, None, P),
              _run_trunk(x_sample.reshape(Bd * Ld, D), Bd, Ld, cache_mem_k, cache_mem_v, state_ret[0],
                         (cache_band_k[0], cache_band_v[0]), P)]
    pending = [next(t) for t in trunks]
    results = []
    for layer in range(depth):
        counts = jnp.zeros((SUBLANES, LANES), F32)
        groups = []
        for run in pending:
            x1, h2, route, route_t, counts = run(counts)
            groups.append((x1, h2, route, route_t))
        outs = hier_moe(groups, counts, layer, w_e_gate, w_e_up, w_e_down)
        pending = []
        for t, x2 in zip(trunks, outs):
            try:
                pending.append(t.send(x2))
            except StopIteration as done:
                results.append(done.value)
    res_p, res_s = results

    return (res_p["y"].reshape(B, L, D), res_s["y"].reshape(Bd, Ld, D),
            res_p["ret_state"][None], res_s["ret_state"][None],
            res_p["band_k"][None], res_p["band_v"][None], res_s["band_k"][None], res_s["band_v"][None],
            mem_k_p, mem_v_p)
```

```python
import functools

import jax
import jax.numpy as jnp
from jax import lax
from jax.experimental import pallas as pl
from jax.experimental.pallas import tpu as pltpu

F32 = jnp.float32
BF16 = jnp.bfloat16

D_MODEL = 1024
D_HEAD = 128
CHUNK = 64
N_HEADS = 8
H_MEM = 4
N_MEM = 256
MEM_W = H_MEM * D_HEAD
MIX_W = N_HEADS * D_HEAD
N_PREV_CHUNKS = 8
BAND_PAST = N_PREV_CHUNKS * CHUNK
REL_CLIP = 128
N_GROUPS = 4
EXPERTS_PER_GROUP = 8
N_EXPERTS = N_GROUPS * EXPERTS_PER_GROUP
D_FF_EXPERT = 512
ROPE_BASE = 10000.0
EPS = 1e-6
NEG_INF = -1e30
PAST_LEN = 1024

LANES = 128
SUBLANES = 8

RET_CHUNK = 256
BAND_TQ = 512
BAND_SUB = 128
BAND_WIN = BAND_PAST + BAND_SUB
MOE_BM = 512
DISPATCH_TM = 1024
COMBINE_TM = 1024
PROJ_TM = 1024
OUT_PROJ_TM = 1024
OUT_PROJ_GROUP = 512
OUT_PROJ_SAMPLE_TM = 128
ROUTE_ROWS = 128
ROUTER_GROUP_LANE0 = N_EXPERTS
V7X_VMEM_BYTES = 64 * 1024 * 1024
VMEM_LIMIT = V7X_VMEM_BYTES - 8 * 1024 * 1024


def _cparams(sem):
    return pltpu.CompilerParams(dimension_semantics=sem, vmem_limit_bytes=VMEM_LIMIT)


def _dot(a, b):
    return jnp.dot(a, b, preferred_element_type=F32)


def _dot_nt(a, b):
    return lax.dot_general(a, b, (((1,), (1,)), ((), ())), preferred_element_type=F32)


def _sigmoid(x):
    return 1.0 / (1.0 + jnp.exp(-x))


ROW_TILE = D_MODEL // LANES
assert ROW_TILE == SUBLANES


def _load_token_tiles(ref, t0, n):
    return jnp.concatenate([ref[pl.ds(t0 * ROW_TILE + j, n, stride=ROW_TILE), :] for j in range(ROW_TILE)], axis=1)


def _store_token_tiles(ref, t0, val):
    n = val.shape[0]
    for j in range(ROW_TILE):
        ref[pl.ds(t0 * ROW_TILE + j, n, stride=ROW_TILE), :] = val[:, j * LANES:(j + 1) * LANES]


def _token_tile(ref, t):
    return ref.at[pl.ds(pl.multiple_of(t * ROW_TILE, ROW_TILE), ROW_TILE)]


def _norm_proj_kernel(segs, has_rope, x_ref, g_ref, w_ref, gain_ref, *rest):
    if has_rope:
        cos_ref, sin_ref, o_ref = rest
    else:
        (o_ref,) = rest
    x = x_ref[...]
    ms = jnp.mean(x * x, axis=-1, keepdims=True)
    h = (x * lax.rsqrt(ms + EPS) * g_ref[...]).astype(BF16)
    col = 0
    for kind, n_heads in segs:
        width = n_heads * D_HEAD
        acc = _dot(h, w_ref[:, col:col + width])
        for j in range(n_heads):
            a = acc[:, j * D_HEAD:(j + 1) * D_HEAD]
            c0 = col + j * D_HEAD
            gain = gain_ref[:, c0:c0 + D_HEAD]
            if kind == "rope":
                rot = pltpu.roll(a, D_HEAD // 2, 1)
                out = (a * cos_ref[...] + rot * sin_ref[...]) * gain
            elif kind == "norm":
                out = a * lax.rsqrt(jnp.mean(a * a, axis=-1, keepdims=True) + EPS) * gain
            elif kind == "silu":
                out = a * _sigmoid(a)
            else:
                out = a
            o_ref[:, c0:c0 + D_HEAD] = out.astype(o_ref.dtype)
        col += width


def norm_proj(x, g, w, gain, segs, *, tm, out_dtype, n_tiles=None, row_map=None, rope=None):
    T, D = x.shape
    N = w.shape[1]
    assert sum(n for _, n in segs) * D_HEAD == N
    if n_tiles is None:
        n_tiles = T // tm
    if row_map is None:
        row_map = lambda i: i
    in_specs = [
        pl.BlockSpec((tm, D), lambda i: (row_map(i), 0)),
        pl.BlockSpec((1, D), lambda i: (0, 0)),
        pl.BlockSpec((D, N), lambda i: (0, 0)),
        pl.BlockSpec((1, N), lambda i: (0, 0)),
    ]
    args = [x, g.reshape(1, D), w, gain.reshape(1, N)]
    if rope is not None:
        cos2, sin2, tab_map = rope
        in_specs += [pl.BlockSpec((tm, D_HEAD), lambda i: (tab_map(i), 0)),
                     pl.BlockSpec((tm, D_HEAD), lambda i: (tab_map(i), 0))]
        args += [cos2, sin2]
    return pl.pallas_call(
        functools.partial(_norm_proj_kernel, segs, rope is not None),
        grid=(n_tiles,),
        in_specs=in_specs,
        out_specs=pl.BlockSpec((tm, N), lambda i: (i, 0)),
        out_shape=jax.ShapeDtypeStruct((n_tiles * tm, N), out_dtype),
        compiler_params=_cparams(("arbitrary",)),
        name="norm_proj",
    )(*args)


def _retention_kernel(q_ref, k_ref, v_ref, g_ref, s0_ref, dmask_ref, qdec_ref, kdec_ref, gc_ref,
                      gn_ref, o_ref, s_ref):
    @pl.when(pl.program_id(1) == 0)
    def _():
        s_ref[...] = s0_ref[...]

    for h in range(N_HEADS):
        sl = slice(h * D_HEAD, (h + 1) * D_HEAD)
        q = q_ref[:, sl]
        k = k_ref[:, sl]
        v = v_ref[:, sl]
        S = s_ref[0, h]
        scores = _dot_nt(q, k) * dmask_ref[h]
        intra = _dot(scores.astype(BF16), v)
        cross = _dot(q, S.astype(BF16)) * qdec_ref[:, sl]
        o = intra + cross
        kd = (k.astype(F32) * kdec_ref[:, sl]).astype(BF16)
        s_ref[0, h] = gc_ref[h] * S + lax.dot_general(kd, v, (((0,), (0,)), ((), ())), preferred_element_type=F32)
        mu = jnp.mean(o, axis=-1, keepdims=True)
        d = o - mu
        var = jnp.mean(d * d, axis=-1, keepdims=True)
        y = d * lax.rsqrt(var + EPS) * gn_ref[:, sl] * g_ref[:, sl].astype(F32)
        o_ref[:, sl] = y.astype(o_ref.dtype)


def _retention_tables(C):
    log_gamma = jnp.log1p(-jnp.exp2(-5.0 - jnp.arange(N_HEADS, dtype=F32)))
    idx = jnp.arange(C, dtype=F32)
    diff = idx[:, None] - idx[None, :]
    dmask = jnp.where(diff[None] >= 0,
                      jnp.exp(log_gamma[:, None, None] * jnp.maximum(diff, 0.0)[None]), 0.0)
    q_dec = jnp.exp(log_gamma[None, :] * (idx + 1.0)[:, None])
    k_dec = jnp.exp(log_gamma[None, :] * (C - 1.0 - idx)[:, None])
    g_c = jnp.exp(log_gamma * C)
    rep = lambda t: jnp.repeat(t, D_HEAD, axis=-1)
    return dmask, rep(q_dec), rep(k_dec), rep(g_c[:, None])[:, None, :]


def retention(proj, s0, gn, *, B, L, C):
    nc = L // C
    dmask, qdec, kdec, gc = _retention_tables(C)
    row = lambda col: pl.BlockSpec((C, MIX_W), lambda b, c: (b * nc + c, col))
    const = lambda shape: pl.BlockSpec(shape, lambda b, c: (0,) * len(shape))
    state_spec = pl.BlockSpec((1, N_HEADS, D_HEAD, D_HEAD), lambda b, c: (b, 0, 0, 0))
    return pl.pallas_call(
        _retention_kernel,
        grid=(B, nc),
        in_specs=[row(0), row(1), row(2), row(3), state_spec,
                  const((N_HEADS, C, C)), const((C, MIX_W)), const((C, MIX_W)),
                  const((N_HEADS, 1, D_HEAD)), const((1, MIX_W))],
        out_specs=[pl.BlockSpec((C, MIX_W), lambda b, c: (b * nc + c, 0)), state_spec],
        out_shape=[jax.ShapeDtypeStruct((B * L, MIX_W), BF16),
                   jax.ShapeDtypeStruct((B, N_HEADS, D_HEAD, D_HEAD), F32)],
        compiler_params=_cparams(("arbitrary", "arbitrary")),
        name="retention",
    )(proj, proj, proj, proj, s0, dmask, qdec, kdec, gc, gn.reshape(1, MIX_W))


def _attend_two_piece(q, k_a, k_b, v_a, v_b, bias_a, bias_b):
    s_a = _dot_nt(q, k_a) + bias_a
    s_b = _dot_nt(q, k_b) + bias_b
    m = jnp.maximum(jnp.max(s_a, axis=-1, keepdims=True), jnp.max(s_b, axis=-1, keepdims=True))
    p_a = jnp.exp(s_a - m)
    p_b = jnp.exp(s_b - m)
    denom = jnp.sum(p_a, axis=-1, keepdims=True) + jnp.sum(p_b, axis=-1, keepdims=True)
    o = _dot(p_a.astype(BF16), v_a) + _dot(p_b.astype(BF16), v_b)
    return o / denom


def _band_prompt_kernel(q_ref, kp_ref, kc_ref, vp_ref, vc_ref, bias_ref, o_ref, s_buf, p_buf):
    @pl.when((pl.program_id(0) == 0) & (pl.program_id(1) == 0))
    def _():
        p_buf[...] = jnp.zeros_like(p_buf)

    prev_off = jnp.where(pl.program_id(1) > 0, 0.0, NEG_INF).astype(F32)
    for h in range(N_HEADS):
        sl = slice(h * D_HEAD, (h + 1) * D_HEAD)
        q = q_ref[:, sl]
        s_buf[:, :BAND_TQ] = _dot_nt(q, kp_ref[:, sl]) + prev_off
        s_buf[:, BAND_TQ:] = _dot_nt(q, kc_ref[:, sl])
        denoms = []
        for s in range(BAND_TQ // BAND_SUB):
            r0 = s * BAND_SUB
            sc = s_buf[r0:r0 + BAND_SUB, r0:r0 + BAND_WIN] + bias_ref[h]
            p = jnp.exp(sc - jnp.max(sc, axis=-1, keepdims=True))
            denoms.append(jnp.sum(p, axis=-1, keepdims=True))
            p_buf[r0:r0 + BAND_SUB, r0:r0 + BAND_WIN] = p.astype(BF16)
        o = _dot(p_buf[:, :BAND_TQ], vp_ref[:, sl]) + _dot(p_buf[:, BAND_TQ:], vc_ref[:, sl])
        o_ref[:, sl] = (o / jnp.concatenate(denoms, axis=0)).astype(o_ref.dtype)


def _rel_bias(bias_table, offset, n_rows, width):
    n = n_rows + width - 1
    v = jnp.arange(n, dtype=jnp.int32)
    v = jnp.where(v < width, v, v - n)
    diag = bias_table.astype(F32)[:, jnp.clip(offset - v, -REL_CLIP, REL_CLIP) + REL_CLIP]
    cut = jnp.tile(diag, (1, n_rows))[:, :n_rows * (n - 1)].reshape(diag.shape[0], n_rows, n - 1)
    return cut[:, :, :width]


def _band_prompt_bias(bias_table):
    r = jnp.arange(BAND_SUB, dtype=jnp.int32)[:, None]
    j = jnp.arange(BAND_WIN, dtype=jnp.int32)[None, :]
    q_chunk = r // CHUNK + N_PREV_CHUNKS
    k_chunk = j // CHUNK
    allowed = (k_chunk <= q_chunk) & (k_chunk >= q_chunk - N_PREV_CHUNKS)
    return jnp.where(allowed[None], _rel_bias(bias_table, BAND_PAST, BAND_SUB, BAND_WIN), NEG_INF)


def band_prompt(proj, bias_table, *, B, L):
    nq = L // BAND_TQ
    cur = lambda col: pl.BlockSpec((BAND_TQ, MIX_W), lambda b, i: (b * nq + i, col))
    prev = lambda col: pl.BlockSpec((BAND_TQ, MIX_W), lambda b, i: (b * nq + jnp.maximum(i - 1, 0), col))
    return pl.pallas_call(
        _band_prompt_kernel,
        grid=(B, nq),
        in_specs=[cur(0), prev(1), cur(1), prev(2), cur(2),
                  pl.BlockSpec((N_HEADS, BAND_SUB, BAND_WIN), lambda b, i: (0, 0, 0))],
        out_specs=pl.BlockSpec((BAND_TQ, MIX_W), lambda b, i: (b * nq + i, 0)),
        out_shape=jax.ShapeDtypeStruct((B * L, MIX_W), BF16),
        scratch_shapes=[pltpu.VMEM((BAND_TQ, 2 * BAND_TQ), F32), pltpu.VMEM((BAND_TQ, 2 * BAND_TQ), BF16)],
        compiler_params=_cparams(("arbitrary", "arbitrary")),
        name="band_prompt",
    )(proj, proj, proj, proj, proj, _band_prompt_bias(bias_table))


def _band_sample_kernel(q_ref, kp_ref, kc_ref, vp_ref, vc_ref, bias_p_ref, bias_c_ref, o_ref):
    n_past = kp_ref.shape[1] // N_HEADS
    for h in range(N_HEADS):
        sl = slice(h * D_HEAD, (h + 1) * D_HEAD)
        kp = kp_ref.at[0][pl.ds(h, n_past, stride=N_HEADS), :].astype(BF16)
        vp = vp_ref.at[0][pl.ds(h, n_past, stride=N_HEADS), :].astype(BF16)
        o = _attend_two_piece(q_ref[:, sl], kp, kc_ref[:, sl], vp, vc_ref[:, sl], bias_p_ref[h], bias_c_ref[h])
        o_ref[:, sl] = o.astype(o_ref.dtype)


def band_sample(proj, past_k, past_v, bias_table, *, B, L):
    P = past_k.shape[1]
    qpos = PAST_LEN + jnp.arange(L, dtype=jnp.int32)
    kpos = jnp.concatenate([PAST_LEN - P + jnp.arange(P, dtype=jnp.int32), qpos])
    qc = (qpos // CHUNK)[:, None]
    kc = (kpos // CHUNK)[None, :]
    allowed = (kpos[None, :] >= 0) & (kc <= qc) & (kc >= qc - N_PREV_CHUNKS)
    rel_bias = jnp.concatenate([_rel_bias(bias_table, P, L, P), _rel_bias(bias_table, 0, L, L)], axis=-1)
    bias = jnp.where(allowed[None], rel_bias, NEG_INF)
    cur = lambda col: pl.BlockSpec((L, MIX_W), lambda b: (b, col))
    past = pl.BlockSpec((1, P * N_HEADS, D_HEAD), lambda b: (b, 0, 0))
    past_k = past_k.reshape(B, P * N_HEADS, D_HEAD)
    past_v = past_v.reshape(B, P * N_HEADS, D_HEAD)
    return pl.pallas_call(
        _band_sample_kernel,
        grid=(B,),
        in_specs=[cur(0), past, cur(1), past, cur(2),
                  pl.BlockSpec((N_HEADS, L, P), lambda b: (0, 0, 0)),
                  pl.BlockSpec((N_HEADS, L, L), lambda b: (0, 0, 0))],
        out_specs=pl.BlockSpec((L, MIX_W), lambda b: (b, 0)),
        out_shape=jax.ShapeDtypeStruct((B * L, MIX_W), BF16),
        compiler_params=_cparams(("arbitrary",)),
        name="band_sample",
    )(proj, past_k, proj, past_v, proj, bias[:, :, :P], bias[:, :, P:])


def _out_proj_kernel(n_streams, n_groups, o_ref, qm_ref, mk_ref, mv_ref, w_ref, x_ref, gf_ref, wr_ref, br_ref,
                     tri_ref, cnt0_ref, x1_ref, h2_ref, route_ref, route_t_ref, cnt_ref):
    @pl.when(pl.program_id(0) == 0)
    def _():
        cnt_ref[...] = cnt0_ref[...]

    tm = x_ref.shape[0]
    sr = tm // n_streams
    gr = tm // n_groups
    om_heads = []
    for h in range(H_MEM):
        sl = slice(h * D_HEAD, (h + 1) * D_HEAD)
        chunks = []
        for c in range(n_streams):
            mk = mk_ref.at[0, c][pl.ds(h, N_MEM, stride=H_MEM), :].astype(BF16)
            mv = mv_ref.at[0, c][pl.ds(h, N_MEM, stride=H_MEM), :].astype(BF16)
            s = _dot_nt(qm_ref[c * sr:(c + 1) * sr, sl], mk)
            p = jnp.exp(s - jnp.max(s, axis=-1, keepdims=True))
            om = _dot(p.astype(BF16), mv) / jnp.sum(p, axis=-1, keepdims=True)
            chunks.append(om.astype(BF16))
        om_heads.append(chunks[0] if n_streams == 1 else jnp.concatenate(chunks, axis=0))

    rows = min(gr, ROUTE_ROWS)
    lane = lax.broadcasted_iota(jnp.int32, (rows, LANES), 1)
    for g in range(n_groups):
        g0 = g * gr
        feat = jnp.concatenate([o_ref[g0:g0 + gr, :]] + [om[g0:g0 + gr] for om in om_heads], axis=1)
        x1 = x_ref[g0:g0 + gr, :] + _dot(feat, w_ref[...])
        x1_ref[g0:g0 + gr, :] = x1
        hn = x1 * lax.rsqrt(jnp.mean(x1 * x1, axis=-1, keepdims=True) + EPS) * gf_ref[...]
        _store_token_tiles(h2_ref, g0, hn)

        logits = _dot(hn.astype(BF16), wr_ref[...]) + br_ref[...]
        tops = [_route_top2(logits[r0:r0 + rows]) for r0 in range(0, gr, rows)]

        onehot = jnp.concatenate([jnp.where(t[4] | t[5], 1.0, 0.0) for t in tops], axis=0)
        before = cnt_ref[0:1, :] + _dot(tri_ref[...], onehot.astype(BF16))
        cnt_ref[0:1, :] = cnt_ref[0:1, :] + jnp.sum(onehot, axis=0, keepdims=True)
        for k, (i1, i2, w1, w2, oh1, oh2) in enumerate(tops):
            b = before[k * rows:(k + 1) * rows]
            r1 = jnp.sum(jnp.where(oh1, b, 0.0), axis=-1, keepdims=True)
            r2 = jnp.sum(jnp.where(oh2, b, 0.0), axis=-1, keepdims=True)
            route = jnp.where(lane == 0, i1, jnp.where(lane == 1, i2, jnp.where(lane == 2, w1, jnp.where(
                lane == 3, w2, jnp.where(lane == 4, r1, jnp.where(lane == 5, r2, 0.0))))))
            c0 = g0 + k * rows
            route_ref[c0:c0 + rows, :] = route
            route_t_ref[:, c0:c0 + rows] = route.T[:SUBLANES, :]


def _route_top2(logits):
    lane = lax.broadcasted_iota(jnp.int32, logits.shape, 1)
    lane_f = lane.astype(F32)
    far = float(LANES)
    is_group = (lane >= ROUTER_GROUP_LANE0) & (lane < ROUTER_GROUP_LANE0 + N_GROUPS)
    gl = jnp.where(is_group, logits, NEG_INF)
    g_max = jnp.max(gl, axis=-1, keepdims=True)
    g_sel = jnp.min(jnp.where(gl == g_max, lane_f, far), axis=-1, keepdims=True) - float(ROUTER_GROUP_LANE0)
    g_gate = 1.0 / jnp.sum(jnp.exp(gl - g_max), axis=-1, keepdims=True)
    in_group = jnp.right_shift(lane, EXPERTS_PER_GROUP.bit_length() - 1).astype(F32) == g_sel
    el = jnp.where(in_group, logits, NEG_INF)
    m1 = jnp.max(el, axis=-1, keepdims=True)
    i1 = jnp.min(jnp.where(el == m1, lane_f, far), axis=-1, keepdims=True)
    el2 = jnp.where(lane_f == i1, NEG_INF, el)
    m2 = jnp.max(el2, axis=-1, keepdims=True)
    i2 = jnp.min(jnp.where(el2 == m2, lane_f, far), axis=-1, keepdims=True)
    e2 = jnp.exp(m2 - m1)
    p1 = 1.0 / (1.0 + e2)
    return i1, i2, g_gate * p1, g_gate * (e2 * p1), lane_f == i1, lane_f == i2


def out_proj(mixed, proj, qm_block, mem_k, mem_v, layer, w_out, x, g_ffn, w_router, b_router, counts0, *, tm,
             n_groups, rows_per_stream):
    T = x.shape[0]
    n_streams = max(1, tm // rows_per_stream)
    tiles_per_stream = max(1, rows_per_stream // tm)
    gr = tm // n_groups
    assert min(gr, ROUTE_ROWS) % LANES == 0
    row = lambda width, col: pl.BlockSpec((tm, width), lambda i: (i, col))
    const = lambda shape: pl.BlockSpec(shape, lambda i: (0,) * len(shape))
    mem = pl.BlockSpec((1, n_streams, N_MEM * H_MEM, D_HEAD), lambda i: (layer, i // tiles_per_stream, 0, 0))
    mem_k = mem_k.reshape(mem_k.shape[:2] + (N_MEM * H_MEM, D_HEAD))
    mem_v = mem_v.reshape(mem_v.shape[:2] + (N_MEM * H_MEM, D_HEAD))
    tri = (jnp.arange(gr)[None, :] < jnp.arange(gr)[:, None]).astype(BF16)
    return pl.pallas_call(
        functools.partial(_out_proj_kernel, n_streams, n_groups),
        grid=(T // tm,),
        in_specs=[row(MIX_W, 0), row(MEM_W, qm_block), mem, mem, const((MIX_W + MEM_W, D_MODEL)),
                  row(D_MODEL, 0), const((1, D_MODEL)), const((D_MODEL, LANES)), const((1, LANES)),
                  const((gr, gr)), const((SUBLANES, LANES))],
        out_specs=[row(D_MODEL, 0), pl.BlockSpec((tm * ROW_TILE, LANES), lambda i: (i, 0)), row(LANES, 0),
                   pl.BlockSpec((SUBLANES, tm), lambda i: (0, i)), const((SUBLANES, LANES))],
        out_shape=[jax.ShapeDtypeStruct((T, D_MODEL), F32), jax.ShapeDtypeStruct((T * ROW_TILE, LANES), F32),
                   jax.ShapeDtypeStruct((T, LANES), F32), jax.ShapeDtypeStruct((SUBLANES, T), F32),
                   jax.ShapeDtypeStruct((SUBLANES, LANES), F32)],
        compiler_params=_cparams(("arbitrary",)),
        name="out_proj",
    )(mixed, proj, mem_k, mem_v, w_out, x, g_ffn.reshape(1, D_MODEL), w_router, b_router, tri, counts0)


def _dispatch_kernel(tms, steps, pe_ref, pos0_ref, pos1_ref, *rest):
    h2_refs = rest[:len(tms)]
    xpad_hbm, zbuf, sem, zsem = rest[len(tms):]
    i = pl.program_id(0)

    @pl.when(i == 0)
    def _():
        zbuf[...] = jnp.zeros_like(zbuf)

        def block_copy(first_row):
            start = pl.multiple_of(first_row * ROW_TILE, MOE_BM * ROW_TILE)
            return pltpu.make_async_copy(zbuf, xpad_hbm.at[pl.ds(start, MOE_BM * ROW_TILE)], zsem)

        def tail_copy(e):
            return block_copy(pe_ref[e] - MOE_BM)

        def nonempty(e):
            return pe_ref[e] > (pe_ref[e - 1] if e > 0 else 0)

        for e in range(N_EXPERTS):
            pl.when(nonempty(e))(lambda e=e: tail_copy(e).start())
        for e in range(N_EXPERTS):
            pl.when(nonempty(e))(lambda e=e: tail_copy(e).wait())

        def fill(b, carry):
            cp = block_copy(b * MOE_BM)
            cp.start()
            cp.wait()
            return carry

        lax.fori_loop(pe_ref[N_EXPERTS - 1] // MOE_BM, xpad_hbm.shape[0] // (MOE_BM * ROW_TILE), fill, 0)

    def scatter_rows(h2_ref, tm):
        def body(r, carry):
            src = _token_tile(h2_ref, r)
            pltpu.make_async_copy(src, _token_tile(xpad_hbm, pos0_ref[r]), sem).start(priority=0)
            pltpu.make_async_copy(src, _token_tile(xpad_hbm, pos1_ref[r]), sem).start(priority=1)
            return carry

        lax.fori_loop(0, tm, body, 0, unroll=8)
        for _ in range(2):
            pltpu.make_async_copy(h2_ref, xpad_hbm.at[pl.ds(0, tm * ROW_TILE)], sem).wait()

    first = 0
    for h2_ref, tm, n in zip(h2_refs, tms, steps):
        pl.when(jnp.logical_and(i >= first, i < first + n))(functools.partial(scatter_rows, h2_ref, tm))
        first += n


def moe_dispatch(pad_ends, pos_groups, h2_groups, n_rows):
    sizes = [h2.shape[0] // ROW_TILE for h2 in h2_groups]
    tms = [min(T, DISPATCH_TM) for T in sizes]
    steps = [T // tm for T, tm in zip(sizes, tms)]
    firsts = [sum(steps[:g]) for g in range(len(steps))]
    pos0, pos1 = (jnp.concatenate([jnp.pad(p[k], (0, n * DISPATCH_TM - p[k].shape[0]))
                                   for p, n in zip(pos_groups, steps)]) for k in range(2))

    def rows(tm, first, n):
        return pl.BlockSpec((tm * ROW_TILE, LANES), lambda i, pe: (jnp.clip(i - first, 0, n - 1), 0))

    grid_spec = pltpu.PrefetchScalarGridSpec(
        num_scalar_prefetch=1,
        grid=(sum(steps),),
        in_specs=[pl.BlockSpec((DISPATCH_TM,), lambda i, pe: (i,), memory_space=pltpu.SMEM)] * 2
        + [rows(tm, first, n) for tm, first, n in zip(tms, firsts, steps)],
        out_specs=pl.BlockSpec(memory_space=pl.ANY),
        scratch_shapes=[pltpu.VMEM((MOE_BM * ROW_TILE, LANES), F32), pltpu.SemaphoreType.DMA(()),
                        pltpu.SemaphoreType.DMA(())],
    )
    return pl.pallas_call(
        functools.partial(_dispatch_kernel, tuple(tms), tuple(steps)),
        grid_spec=grid_spec,
        out_shape=jax.ShapeDtypeStruct((n_rows * ROW_TILE, LANES), F32),
        compiler_params=_cparams(("arbitrary",)),
        name="moe_dispatch",
    )(pad_ends, pos0, pos1, *h2_groups)


def _moe_kernel(be_ref, nu_ref, x_ref, wg_ref, wu_ref, wd_ref, y_ref, wg_s, wu_s, wd_s):
    i = pl.program_id(0)
    used = i < nu_ref[0]
    new_expert = jnp.logical_or(i == 0, be_ref[i] != be_ref[jnp.maximum(i - 1, 0)])

    @pl.when(jnp.logical_and(used, new_expert))
    def _():
        wg_s[...] = wg_ref[0, 0].astype(BF16)
        wu_s[...] = wu_ref[0, 0].astype(BF16)
        wd_s[...] = wd_ref[0, 0].astype(BF16)

    @pl.when(used)
    def _():
        x = _load_token_tiles(x_ref, 0, MOE_BM).astype(BF16)
        g = _dot(x, wg_s[...])
        u = _dot(x, wu_s[...])
        a = (g * _sigmoid(g) * u).astype(BF16)
        _store_token_tiles(y_ref, 0, _dot(a, wd_s[...]))

    @pl.when(jnp.logical_not(used))
    def _():
        y_ref[...] = jnp.zeros_like(y_ref)


def moe_experts(x_pad, block_e, n_used, layer, wg, wu, wd):
    n_rows = x_pad.shape[0] // ROW_TILE
    rows = pl.BlockSpec((MOE_BM * ROW_TILE, LANES), lambda i, be, nu: (i, 0))
    w_in = pl.BlockSpec((1, 1, D_MODEL, D_FF_EXPERT), lambda i, be, nu: (layer, be[i], 0, 0))
    w_dn = pl.BlockSpec((1, 1, D_FF_EXPERT, D_MODEL), lambda i, be, nu: (layer, be[i], 0, 0))
    grid_spec = pltpu.PrefetchScalarGridSpec(
        num_scalar_prefetch=2,
        grid=(n_rows // MOE_BM,),
        in_specs=[pl.BlockSpec((MOE_BM * ROW_TILE, LANES), lambda i, be, nu: (jnp.minimum(i, nu[0] - 1), 0)),
                  w_in, w_in, w_dn],
        out_specs=rows,
        scratch_shapes=[pltpu.VMEM((D_MODEL, D_FF_EXPERT), BF16), pltpu.VMEM((D_MODEL, D_FF_EXPERT), BF16),
                        pltpu.VMEM((D_FF_EXPERT, D_MODEL), BF16)],
    )
    return pl.pallas_call(
        _moe_kernel,
        grid_spec=grid_spec,
        out_shape=jax.ShapeDtypeStruct((n_rows * ROW_TILE, LANES), F32),
        compiler_params=_cparams(("arbitrary",)),
        name="moe_experts",
    )(block_e, n_used, x_pad, wg, wu, wd)


def _combine_kernel(tm, n_steps, pos0_ref, pos1_ref, pos0_next_ref, pos1_next_ref, route_ref, x1_ref, ypad_hbm,
                    o_ref, ybuf, sems):
    i = pl.program_id(0)
    slot = lax.rem(i, 2)

    def gather(p0_ref, p1_ref, s):
        buf = ybuf.at[s]

        def body(r, carry):
            pltpu.make_async_copy(_token_tile(ypad_hbm, p0_ref[r]), _token_tile(buf, r), sems.at[s]).start(
                priority=0)
            pltpu.make_async_copy(_token_tile(ypad_hbm, p1_ref[r]), _token_tile(buf, tm + r), sems.at[s]).start(
                priority=1)
            return carry

        lax.fori_loop(0, tm, body, 0, unroll=8)

    @pl.when(i == 0)
    def _():
        gather(pos0_ref, pos1_ref, 0)

    @pl.when(i + 1 < n_steps)
    def _():
        gather(pos0_next_ref, pos1_next_ref, 1 - slot)

    buf = ybuf.at[slot]
    pltpu.make_async_copy(ypad_hbm.at[pl.ds(0, 2 * tm * ROW_TILE)], buf, sems.at[slot]).wait()
    g1 = route_ref[:, 2:3]
    g2 = route_ref[:, 3:4]
    o_ref[...] = x1_ref[...] + (g1 * _load_token_tiles(buf, 0, tm) + g2 * _load_token_tiles(buf, tm, tm))


def moe_combine(pos, route, x1, y_pad):
    pos0, pos1 = pos
    T = x1.shape[0]
    tm = min(T, COMBINE_TM)
    n_steps = T // tm
    return pl.pallas_call(
        functools.partial(_combine_kernel, tm, n_steps),
        grid=(n_steps,),
        in_specs=[pl.BlockSpec((tm,), lambda i: (i,), memory_space=pltpu.SMEM)] * 2
        + [pl.BlockSpec((tm,), lambda i: (jnp.minimum(i + 1, n_steps - 1),), memory_space=pltpu.SMEM)] * 2
        + [pl.BlockSpec((tm, LANES), lambda i: (i, 0)), pl.BlockSpec((tm, D_MODEL), lambda i: (i, 0)),
           pl.BlockSpec(memory_space=pl.ANY)],
        out_specs=pl.BlockSpec((tm, D_MODEL), lambda i: (i, 0)),
        out_shape=jax.ShapeDtypeStruct((T, D_MODEL), F32),
        scratch_shapes=[pltpu.VMEM((2, 2 * tm * ROW_TILE, LANES), F32), pltpu.SemaphoreType.DMA((2,))],
        compiler_params=_cparams(("arbitrary",)),
        name="moe_combine",
    )(pos0, pos1, pos0, pos1, route, x1, y_pad)


def hier_moe(groups, counts, layer, wg, wu, wd):
    sizes = [g[0].shape[0] for g in groups]
    n_assign = 2 * sum(sizes)
    n_blocks = (n_assign + N_EXPERTS * (MOE_BM - 1) + MOE_BM - 1) // MOE_BM
    fields = jnp.concatenate([g[3] for g in groups], axis=1)
    experts = fields[0:2].astype(jnp.int32)
    ranks = fields[4:6].astype(jnp.int32)
    cnt = counts[0, :N_EXPERTS].astype(jnp.int32)
    padded = ((cnt + MOE_BM - 1) // MOE_BM) * MOE_BM
    pad_ends = jnp.cumsum(padded).astype(jnp.int32)
    pad_starts = pad_ends - padded
    is_e = experts[:, :, None] == jnp.arange(N_EXPERTS, dtype=jnp.int32)
    pos = ranks + jnp.sum(jnp.where(is_e, pad_starts, 0), axis=-1)
    block_start = jnp.arange(n_blocks, dtype=jnp.int32) * MOE_BM
    block_e = jnp.minimum(jnp.sum(block_start[:, None] >= pad_ends[None, :], axis=1), N_EXPERTS - 1).astype(jnp.int32)
    n_used = (pad_ends[-1:] // MOE_BM).astype(jnp.int32)
    starts = [sum(sizes[:g]) for g in range(len(groups))]
    pos_g = [(pos[0, s:s + n], pos[1, s:s + n]) for s, n in zip(starts, sizes)]
    x_pad = moe_dispatch(pad_ends, pos_g, [g[1] for g in groups], n_blocks * MOE_BM)
    y_pad = moe_experts(x_pad, block_e, n_used, layer, wg, wu, wd)
    return [moe_combine(p, g[2], g[0], y_pad) for g, p in zip(groups, pos_g)]


def _rope_tables(pos):
    half = D_HEAD // 2
    inv = ROPE_BASE ** (-jnp.arange(half, dtype=F32) / half)
    ang = pos.astype(F32)[:, None] * inv[None, :]
    cos, sin = jnp.cos(ang), jnp.sin(ang)
    return jnp.concatenate([cos, cos], axis=-1), jnp.concatenate([-sin, sin], axis=-1)


def _router_weights(w_group, b_group, w_router, b_router):
    w = jnp.zeros((D_MODEL, LANES), F32)
    w = w.at[:, :N_EXPERTS].set(w_router).at[:, ROUTER_GROUP_LANE0:ROUTER_GROUP_LANE0 + N_GROUPS].set(w_group)
    b = jnp.zeros((1, LANES), F32)
    b = b.at[0, :N_EXPERTS].set(b_router).at[0, ROUTER_GROUP_LANE0:ROUTER_GROUP_LANE0 + N_GROUPS].set(b_group)
    return w.astype(BF16), b


def _tile(v, n):
    return jnp.tile(v.astype(F32), n)


def _run_trunk(x, B, L, mem_k, mem_v, ret_state0, band_past, P):
    prompt = band_past is None
    out_tm, out_groups = (OUT_PROJ_TM, OUT_PROJ_TM // OUT_PROJ_GROUP) if prompt else (OUT_PROJ_SAMPLE_TM, 1)
    proj_tm = min(B * L, PROJ_TM)
    scale = D_HEAD ** -0.5
    ones = lambda n: jnp.ones((n * D_HEAD,), F32)
    if prompt:
        cos2, sin2 = _rope_tables(jnp.arange(L, dtype=jnp.int32))
        tab_map = lambda i: i % (L // proj_tm)
    else:
        cos2, sin2 = _rope_tables(PAST_LEN + jnp.arange(L, dtype=jnp.int32))
        cos2, sin2 = jnp.tile(cos2, (B, 1)), jnp.tile(sin2, (B, 1))
        tab_map = lambda i: i
    out = {}

    def ffn(layer, mixed, proj, qm_block, w_out, x):
        def run(counts0):
            return out_proj(
                mixed, proj, qm_block, mem_k, mem_v, layer, w_out.astype(BF16), x, P["norm_ffn"][layer],
                *_router_weights(P["w_group"][layer], P["b_group"][layer], P["w_router"][layer],
                                 P["b_router"][layer]), counts0, tm=out_tm, n_groups=out_groups, rows_per_stream=L)
        return (yield run)

    segs0 = (("rope", N_HEADS), ("rope", N_HEADS), ("plain", N_HEADS), ("silu", N_HEADS), ("norm", H_MEM))
    gain0 = jnp.concatenate([ones(N_HEADS), ones(N_HEADS) * scale, ones(2 * N_HEADS),
                             _tile(P["q_norm_mem"][0], H_MEM) * scale])
    proj = norm_proj(x, P["norm_mix"][0], P["w_in_ret"][0].astype(BF16), gain0, segs0, tm=proj_tm,
                     out_dtype=BF16, rope=(cos2, sin2, tab_map))
    mixed, S = retention(proj, ret_state0, P["gn_ret"][0], B=B, L=L, C=RET_CHUNK if prompt else L)
    out["ret_state"] = S
    x = yield from ffn(0, mixed, proj, (4 * MIX_W) // MEM_W, P["w_out_ret"][0], x)

    segs1 = (("norm", N_HEADS), ("norm", N_HEADS), ("plain", N_HEADS), ("norm", H_MEM))
    gain1 = jnp.concatenate([_tile(P["q_norm_att"][0], N_HEADS) * scale, _tile(P["k_norm_att"][0], N_HEADS),
                             ones(N_HEADS), _tile(P["q_norm_mem"][1], H_MEM) * scale])
    w_in = P["w_in_att"][0].astype(BF16)
    if prompt:
        proj = norm_proj(x, P["norm_mix"][1], w_in, gain1, segs1, tm=proj_tm, out_dtype=BF16)
        keep = min(BAND_PAST, L)
        tiles = L // keep
        kv = norm_proj(x, P["norm_mix"][1], w_in[:, MIX_W:3 * MIX_W], gain1[MIX_W:3 * MIX_W], segs1[1:3],
                       tm=keep, out_dtype=F32, n_tiles=B, row_map=lambda i: i * tiles + tiles - 1)
        out["band_k"] = kv[:, :MIX_W].reshape(B, keep, N_HEADS, D_HEAD)
        out["band_v"] = kv[:, MIX_W:].reshape(B, keep, N_HEADS, D_HEAD)
        mixed = band_prompt(proj, P["rel_bias_att"][0], B=B, L=L)
    else:
        proj_f = norm_proj(x, P["norm_mix"][1], w_in, gain1, segs1, tm=proj_tm, out_dtype=F32)
        out["band_k"] = proj_f[:, MIX_W:2 * MIX_W].reshape(B, L, N_HEADS, D_HEAD)
        out["band_v"] = proj_f[:, 2 * MIX_W:3 * MIX_W].reshape(B, L, N_HEADS, D_HEAD)
        proj = proj_f.astype(BF16)
        mixed = band_sample(proj, band_past[0], band_past[1], P["rel_bias_att"][0], B=B, L=L)
    out["y"] = yield from ffn(1, mixed, proj, (3 * MIX_W) // MEM_W, P["w_out_att"][0], x)
    return out


def kernel(x_prompt, x_sample, mem_prompt, state_ret, cache_band_k, cache_band_v, cache_mem_k, cache_mem_v, norm_mix, norm_ffn, norm_mem, w_in_ret, gn_ret, w_out_ret, w_in_att, q_norm_att, k_norm_att, rel_bias_att, w_out_att, w_mem_kv, q_norm_mem, k_norm_mem, w_group, b_group, w_router, b_router, w_e_gate, w_e_up, w_e_down):
    P = {"norm_mix": norm_mix, "norm_ffn": norm_ffn, "w_in_ret": w_in_ret, "gn_ret": gn_ret,
         "w_out_ret": w_out_ret, "w_in_att": w_in_att, "q_norm_att": q_norm_att,
         "k_norm_att": k_norm_att, "rel_bias_att": rel_bias_att, "w_out_att": w_out_att,
         "q_norm_mem": q_norm_mem, "w_group": w_group, "b_group": b_group, "w_router": w_router,
         "b_router": b_router, "w_e_gate": w_e_gate, "w_e_up": w_e_up, "w_e_down": w_e_down}
    B, L, D = x_prompt.shape
    Bd, Ld, _ = x_sample.shape
    depth = norm_mix.shape[0]

    mem_k_p, mem_v_p = [], []
    for i in range(depth):
        gain = jnp.concatenate([_tile(k_norm_mem[i], H_MEM), jnp.ones((MEM_W,), F32)])
        kv = norm_proj(mem_prompt.reshape(B * N_MEM, D), norm_mem[i], w_mem_kv[i].astype(BF16), gain,
                       (("norm", H_MEM), ("plain", H_MEM)), tm=N_MEM, out_dtype=F32)
        mem_k_p.append(kv[:, :MEM_W].reshape(B, N_MEM, H_MEM, D_HEAD))
        mem_v_p.append(kv[:, MEM_W:].reshape(B, N_MEM, H_MEM, D_HEAD))
    mem_k_p = jnp.stack(mem_k_p)
    mem_v_p = jnp.stack(mem_v_p)

    zeros_state = jnp.zeros((B, N_HEADS, D_HEAD, D_HEAD), F32)
    trunks = [_run_trunk(x_prompt.reshape(B * L, D), B, L, mem_k_p, mem_v_p, zeros_state, None, P),
              _run_trunk(x_sample.reshape(Bd * Ld, D), Bd, Ld, cache_mem_k, cache_mem_v, state_ret[0],
                         (cache_band_k[0], cache_band_v[0]), P)]
    pending = [next(t) for t in trunks]
    results = []
    for layer in range(depth):
        counts = jnp.zeros((SUBLANES, LANES), F32)
        groups = []
        for run in pending:
            x1, h2, route, route_t, counts = run(counts)
            groups.append((x1, h2, route, route_t))
        outs = hier_moe(groups, counts, layer, w_e_gate, w_e_up, w_e_down)
        pending = []
        for t, x2 in zip(trunks, outs):
            try:
                pending.append(t.send(x2))
            except StopIteration as done:
                results.append(done.value)
    res_p, res_s = results

    return (res_p["y"].reshape(B, L, D), res_s["y"].reshape(Bd, Ld, D),
            res_p["ret_state"][None], res_s["ret_state"][None],
            res_p["band_k"][None], res_p["band_v"][None], res_s["band_k"][None], res_s["band_v"][None],
            mem_k_p, mem_v_p)
```

```python
import functools

import jax
import jax.numpy as jnp
from jax import lax
from jax.experimental import pallas as pl
from jax.experimental.pallas import tpu as pltpu

F32 = jnp.float32
BF16 = jnp.bfloat16

D_MODEL = 1024
D_HEAD = 128
CHUNK = 64
N_HEADS = 8
H_MEM = 4
N_MEM = 256
MEM_W = H_MEM * D_HEAD
MIX_W = N_HEADS * D_HEAD
N_PREV_CHUNKS = 8
BAND_PAST = N_PREV_CHUNKS * CHUNK
REL_CLIP = 128
N_GROUPS = 4
EXPERTS_PER_GROUP = 8
N_EXPERTS = N_GROUPS * EXPERTS_PER_GROUP
D_FF_EXPERT = 512
ROPE_BASE = 10000.0
EPS = 1e-6
NEG_INF = -1e30
PAST_LEN = 1024

LANES = 128
SUBLANES = 8

RET_CHUNK = 256
BAND_TQ = 512
BAND_SUB = 128
BAND_WIN = BAND_PAST + BAND_SUB
MOE_BM = 512
DISPATCH_TM = 4096
COMBINE_TM = 1024
PROJ_TM = 1024
OUT_PROJ_TM = 1024
OUT_PROJ_GROUP = 512
OUT_PROJ_SAMPLE_TM = 128
ROUTE_ROWS = 128
ROUTER_GROUP_LANE0 = N_EXPERTS
V7X_VMEM_BYTES = 64 * 1024 * 1024
VMEM_LIMIT = V7X_VMEM_BYTES - 8 * 1024 * 1024


def _cparams(sem):
    return pltpu.CompilerParams(dimension_semantics=sem, vmem_limit_bytes=VMEM_LIMIT)


def _dot(a, b):
    return jnp.dot(a, b, preferred_element_type=F32)


def _dot_nt(a, b):
    return lax.dot_general(a, b, (((1,), (1,)), ((), ())), preferred_element_type=F32)


def _sigmoid(x):
    return 1.0 / (1.0 + jnp.exp(-x))


ROW_TILE = D_MODEL // LANES
assert ROW_TILE == SUBLANES


def _load_token_tiles(ref, t0, n):
    return jnp.concatenate([ref[pl.ds(t0 * ROW_TILE + j, n, stride=ROW_TILE), :] for j in range(ROW_TILE)], axis=1)


def _store_token_tiles(ref, t0, val):
    n = val.shape[0]
    for j in range(ROW_TILE):
        ref[pl.ds(t0 * ROW_TILE + j, n, stride=ROW_TILE), :] = val[:, j * LANES:(j + 1) * LANES]


def _token_tile(ref, t):
    return ref.at[pl.ds(pl.multiple_of(t * ROW_TILE, ROW_TILE), ROW_TILE)]


def _norm_proj_kernel(segs, has_rope, x_ref, g_ref, w_ref, gain_ref, *rest):
    if has_rope:
        cos_ref, sin_ref, o_ref = rest
    else:
        (o_ref,) = rest
    x = x_ref[...]
    ms = jnp.mean(x * x, axis=-1, keepdims=True)
    h = (x * lax.rsqrt(ms + EPS) * g_ref[...]).astype(BF16)
    col = 0
    for kind, n_heads in segs:
        width = n_heads * D_HEAD
        acc = _dot(h, w_ref[:, col:col + width])
        for j in range(n_heads):
            a = acc[:, j * D_HEAD:(j + 1) * D_HEAD]
            c0 = col + j * D_HEAD
            gain = gain_ref[:, c0:c0 + D_HEAD]
            if kind == "rope":
                rot = pltpu.roll(a, D_HEAD // 2, 1)
                out = (a * cos_ref[...] + rot * sin_ref[...]) * gain
            elif kind == "norm":
                out = a * lax.rsqrt(jnp.mean(a * a, axis=-1, keepdims=True) + EPS) * gain
            elif kind == "silu":
                out = a * _sigmoid(a)
            else:
                out = a
            o_ref[:, c0:c0 + D_HEAD] = out.astype(o_ref.dtype)
        col += width


def norm_proj(x, g, w, gain, segs, *, tm, out_dtype, n_tiles=None, row_map=None, rope=None):
    T, D = x.shape
    N = w.shape[1]
    assert sum(n for _, n in segs) * D_HEAD == N
    if n_tiles is None:
        n_tiles = T // tm
    if row_map is None:
        row_map = lambda i: i
    in_specs = [
        pl.BlockSpec((tm, D), lambda i: (row_map(i), 0)),
        pl.BlockSpec((1, D), lambda i: (0, 0)),
        pl.BlockSpec((D, N), lambda i: (0, 0)),
        pl.BlockSpec((1, N), lambda i: (0, 0)),
    ]
    args = [x, g.reshape(1, D), w, gain.reshape(1, N)]
    if rope is not None:
        cos2, sin2, tab_map = rope
        in_specs += [pl.BlockSpec((tm, D_HEAD), lambda i: (tab_map(i), 0)),
                     pl.BlockSpec((tm, D_HEAD), lambda i: (tab_map(i), 0))]
        args += [cos2, sin2]
    return pl.pallas_call(
        functools.partial(_norm_proj_kernel, segs, rope is not None),
        grid=(n_tiles,),
        in_specs=in_specs,
        out_specs=pl.BlockSpec((tm, N), lambda i: (i, 0)),
        out_shape=jax.ShapeDtypeStruct((n_tiles * tm, N), out_dtype),
        compiler_params=_cparams(("arbitrary",)),
        name="norm_proj",
    )(*args)


def _retention_kernel(q_ref, k_ref, v_ref, g_ref, s0_ref, dmask_ref, qdec_ref, kdec_ref, gc_ref,
                      gn_ref, o_ref, s_ref):
    @pl.when(pl.program_id(1) == 0)
    def _():
        s_ref[...] = s0_ref[...]

    for h in range(N_HEADS):
        sl = slice(h * D_HEAD, (h + 1) * D_HEAD)
        q = q_ref[:, sl]
        k = k_ref[:, sl]
        v = v_ref[:, sl]
        S = s_ref[0, h]
        scores = _dot_nt(q, k) * dmask_ref[h]
        intra = _dot(scores.astype(BF16), v)
        cross = _dot(q, S.astype(BF16)) * qdec_ref[:, sl]
        o = intra + cross
        kd = (k.astype(F32) * kdec_ref[:, sl]).astype(BF16)
        s_ref[0, h] = gc_ref[h] * S + lax.dot_general(kd, v, (((0,), (0,)), ((), ())), preferred_element_type=F32)
        mu = jnp.mean(o, axis=-1, keepdims=True)
        d = o - mu
        var = jnp.mean(d * d, axis=-1, keepdims=True)
        y = d * lax.rsqrt(var + EPS) * gn_ref[:, sl] * g_ref[:, sl].astype(F32)
        o_ref[:, sl] = y.astype(o_ref.dtype)


def _retention_tables(C):
    log_gamma = jnp.log1p(-jnp.exp2(-5.0 - jnp.arange(N_HEADS, dtype=F32)))
    idx = jnp.arange(C, dtype=F32)
    diff = idx[:, None] - idx[None, :]
    dmask = jnp.where(diff[None] >= 0,
                      jnp.exp(log_gamma[:, None, None] * jnp.maximum(diff, 0.0)[None]), 0.0)
    q_dec = jnp.exp(log_gamma[None, :] * (idx + 1.0)[:, None])
    k_dec = jnp.exp(log_gamma[None, :] * (C - 1.0 - idx)[:, None])
    g_c = jnp.exp(log_gamma * C)
    rep = lambda t: jnp.repeat(t, D_HEAD, axis=-1)
    return dmask, rep(q_dec), rep(k_dec), rep(g_c[:, None])[:, None, :]


def retention(proj, s0, gn, *, B, L, C):
    nc = L // C
    dmask, qdec, kdec, gc = _retention_tables(C)
    row = lambda col: pl.BlockSpec((C, MIX_W), lambda b, c: (b * nc + c, col))
    const = lambda shape: pl.BlockSpec(shape, lambda b, c: (0,) * len(shape))
    state_spec = pl.BlockSpec((1, N_HEADS, D_HEAD, D_HEAD), lambda b, c: (b, 0, 0, 0))
    return pl.pallas_call(
        _retention_kernel,
        grid=(B, nc),
        in_specs=[row(0), row(1), row(2), row(3), state_spec,
                  const((N_HEADS, C, C)), const((C, MIX_W)), const((C, MIX_W)),
                  const((N_HEADS, 1, D_HEAD)), const((1, MIX_W))],
        out_specs=[pl.BlockSpec((C, MIX_W), lambda b, c: (b * nc + c, 0)), state_spec],
        out_shape=[jax.ShapeDtypeStruct((B * L, MIX_W), BF16),
                   jax.ShapeDtypeStruct((B, N_HEADS, D_HEAD, D_HEAD), F32)],
        compiler_params=_cparams(("arbitrary", "arbitrary")),
        name="retention",
    )(proj, proj, proj, proj, s0, dmask, qdec, kdec, gc, gn.reshape(1, MIX_W))


def _attend_two_piece(q, k_a, k_b, v_a, v_b, bias_a, bias_b):
    s_a = _dot_nt(q, k_a) + bias_a
    s_b = _dot_nt(q, k_b) + bias_b
    m = jnp.maximum(jnp.max(s_a, axis=-1, keepdims=True), jnp.max(s_b, axis=-1, keepdims=True))
    p_a = jnp.exp(s_a - m)
    p_b = jnp.exp(s_b - m)
    denom = jnp.sum(p_a, axis=-1, keepdims=True) + jnp.sum(p_b, axis=-1, keepdims=True)
    o = _dot(p_a.astype(BF16), v_a) + _dot(p_b.astype(BF16), v_b)
    return o / denom


def _band_prompt_kernel(q_ref, kp_ref, kc_ref, vp_ref, vc_ref, bias_ref, o_ref, s_buf, p_buf):
    @pl.when((pl.program_id(0) == 0) & (pl.program_id(1) == 0))
    def _():
        p_buf[...] = jnp.zeros_like(p_buf)

    prev_off = jnp.where(pl.program_id(1) > 0, 0.0, NEG_INF).astype(F32)
    for h in range(N_HEADS):
        sl = slice(h * D_HEAD, (h + 1) * D_HEAD)
        q = q_ref[:, sl]
        s_buf[:, :BAND_TQ] = _dot_nt(q, kp_ref[:, sl]) + prev_off
        s_buf[:, BAND_TQ:] = _dot_nt(q, kc_ref[:, sl])
        denoms = []
        for s in range(BAND_TQ // BAND_SUB):
            r0 = s * BAND_SUB
            sc = s_buf[r0:r0 + BAND_SUB, r0:r0 + BAND_WIN] + bias_ref[h]
            p = jnp.exp(sc - jnp.max(sc, axis=-1, keepdims=True))
            denoms.append(jnp.sum(p, axis=-1, keepdims=True))
            p_buf[r0:r0 + BAND_SUB, r0:r0 + BAND_WIN] = p.astype(BF16)
        o = _dot(p_buf[:, :BAND_TQ], vp_ref[:, sl]) + _dot(p_buf[:, BAND_TQ:], vc_ref[:, sl])
        o_ref[:, sl] = (o / jnp.concatenate(denoms, axis=0)).astype(o_ref.dtype)


def _rel_bias(bias_table, offset, n_rows, width):
    n = n_rows + width - 1
    v = jnp.arange(n, dtype=jnp.int32)
    v = jnp.where(v < width, v, v - n)
    diag = bias_table.astype(F32)[:, jnp.clip(offset - v, -REL_CLIP, REL_CLIP) + REL_CLIP]
    cut = jnp.tile(diag, (1, n_rows))[:, :n_rows * (n - 1)].reshape(diag.shape[0], n_rows, n - 1)
    return cut[:, :, :width]


def _band_prompt_bias(bias_table):
    r = jnp.arange(BAND_SUB, dtype=jnp.int32)[:, None]
    j = jnp.arange(BAND_WIN, dtype=jnp.int32)[None, :]
    q_chunk = r // CHUNK + N_PREV_CHUNKS
    k_chunk = j // CHUNK
    allowed = (k_chunk <= q_chunk) & (k_chunk >= q_chunk - N_PREV_CHUNKS)
    return jnp.where(allowed[None], _rel_bias(bias_table, BAND_PAST, BAND_SUB, BAND_WIN), NEG_INF)


def band_prompt(proj, bias_table, *, B, L):
    nq = L // BAND_TQ
    cur = lambda col: pl.BlockSpec((BAND_TQ, MIX_W), lambda b, i: (b * nq + i, col))
    prev = lambda col: pl.BlockSpec((BAND_TQ, MIX_W), lambda b, i: (b * nq + jnp.maximum(i - 1, 0), col))
    return pl.pallas_call(
        _band_prompt_kernel,
        grid=(B, nq),
        in_specs=[cur(0), prev(1), cur(1), prev(2), cur(2),
                  pl.BlockSpec((N_HEADS, BAND_SUB, BAND_WIN), lambda b, i: (0, 0, 0))],
        out_specs=pl.BlockSpec((BAND_TQ, MIX_W), lambda b, i: (b * nq + i, 0)),
        out_shape=jax.ShapeDtypeStruct((B * L, MIX_W), BF16),
        scratch_shapes=[pltpu.VMEM((BAND_TQ, 2 * BAND_TQ), F32), pltpu.VMEM((BAND_TQ, 2 * BAND_TQ), BF16)],
        compiler_params=_cparams(("arbitrary", "arbitrary")),
        name="band_prompt",
    )(proj, proj, proj, proj, proj, _band_prompt_bias(bias_table))


def _band_sample_kernel(q_ref, kp_ref, kc_ref, vp_ref, vc_ref, bias_p_ref, bias_c_ref, o_ref):
    n_past = kp_ref.shape[1] // N_HEADS
    for h in range(N_HEADS):
        sl = slice(h * D_HEAD, (h + 1) * D_HEAD)
        kp = kp_ref.at[0][pl.ds(h, n_past, stride=N_HEADS), :].astype(BF16)
        vp = vp_ref.at[0][pl.ds(h, n_past, stride=N_HEADS), :].astype(BF16)
        o = _attend_two_piece(q_ref[:, sl], kp, kc_ref[:, sl], vp, vc_ref[:, sl], bias_p_ref[h], bias_c_ref[h])
        o_ref[:, sl] = o.astype(o_ref.dtype)


def band_sample(proj, past_k, past_v, bias_table, *, B, L):
    P = past_k.shape[1]
    qpos = PAST_LEN + jnp.arange(L, dtype=jnp.int32)
    kpos = jnp.concatenate([PAST_LEN - P + jnp.arange(P, dtype=jnp.int32), qpos])
    qc = (qpos // CHUNK)[:, None]
    kc = (kpos // CHUNK)[None, :]
    allowed = (kpos[None, :] >= 0) & (kc <= qc) & (kc >= qc - N_PREV_CHUNKS)
    rel_bias = jnp.concatenate([_rel_bias(bias_table, P, L, P), _rel_bias(bias_table, 0, L, L)], axis=-1)
    bias = jnp.where(allowed[None], rel_bias, NEG_INF)
    cur = lambda col: pl.BlockSpec((L, MIX_W), lambda b: (b, col))
    past = pl.BlockSpec((1, P * N_HEADS, D_HEAD), lambda b: (b, 0, 0))
    past_k = past_k.reshape(B, P * N_HEADS, D_HEAD)
    past_v = past_v.reshape(B, P * N_HEADS, D_HEAD)
    return pl.pallas_call(
        _band_sample_kernel,
        grid=(B,),
        in_specs=[cur(0), past, cur(1), past, cur(2),
                  pl.BlockSpec((N_HEADS, L, P), lambda b: (0, 0, 0)),
                  pl.BlockSpec((N_HEADS, L, L), lambda b: (0, 0, 0))],
        out_specs=pl.BlockSpec((L, MIX_W), lambda b: (b, 0)),
        out_shape=jax.ShapeDtypeStruct((B * L, MIX_W), BF16),
        compiler_params=_cparams(("arbitrary",)),
        name="band_sample",
    )(proj, past_k, proj, past_v, proj, bias[:, :, :P], bias[:, :, P:])


def _out_proj_kernel(n_streams, n_groups, o_ref, qm_ref, mk_ref, mv_ref, w_ref, x_ref, gf_ref, wr_ref, br_ref,
                     tri_ref, cnt0_ref, x1_ref, h2_ref, route_ref, route_t_ref, cnt_ref):
    @pl.when(pl.program_id(0) == 0)
    def _():
        cnt_ref[...] = cnt0_ref[...]

    tm = x_ref.shape[0]
    sr = tm // n_streams
    gr = tm // n_groups
    om_heads = []
    for h in range(H_MEM):
        sl = slice(h * D_HEAD, (h + 1) * D_HEAD)
        chunks = []
        for c in range(n_streams):
            mk = mk_ref.at[0, c][pl.ds(h, N_MEM, stride=H_MEM), :].astype(BF16)
            mv = mv_ref.at[0, c][pl.ds(h, N_MEM, stride=H_MEM), :].astype(BF16)
            s = _dot_nt(qm_ref[c * sr:(c + 1) * sr, sl], mk)
            p = jnp.exp(s - jnp.max(s, axis=-1, keepdims=True))
            om = _dot(p.astype(BF16), mv) / jnp.sum(p, axis=-1, keepdims=True)
            chunks.append(om.astype(BF16))
        om_heads.append(chunks[0] if n_streams == 1 else jnp.concatenate(chunks, axis=0))

    rows = min(gr, ROUTE_ROWS)
    lane = lax.broadcasted_iota(jnp.int32, (rows, LANES), 1)
    for g in range(n_groups):
        g0 = g * gr
        feat = jnp.concatenate([o_ref[g0:g0 + gr, :]] + [om[g0:g0 + gr] for om in om_heads], axis=1)
        x1 = x_ref[g0:g0 + gr, :] + _dot(feat, w_ref[...])
        x1_ref[g0:g0 + gr, :] = x1
        hn = x1 * lax.rsqrt(jnp.mean(x1 * x1, axis=-1, keepdims=True) + EPS) * gf_ref[...]
        _store_token_tiles(h2_ref, g0, hn)

        logits = _dot(hn.astype(BF16), wr_ref[...]) + br_ref[...]
        tops = [_route_top2(logits[r0:r0 + rows]) for r0 in range(0, gr, rows)]

        onehot = jnp.concatenate([jnp.where(t[4] | t[5], 1.0, 0.0) for t in tops], axis=0)
        before = cnt_ref[0:1, :] + _dot(tri_ref[...], onehot.astype(BF16))
        cnt_ref[0:1, :] = cnt_ref[0:1, :] + jnp.sum(onehot, axis=0, keepdims=True)
        for k, (i1, i2, w1, w2, oh1, oh2) in enumerate(tops):
            b = before[k * rows:(k + 1) * rows]
            r1 = jnp.sum(jnp.where(oh1, b, 0.0), axis=-1, keepdims=True)
            r2 = jnp.sum(jnp.where(oh2, b, 0.0), axis=-1, keepdims=True)
            route = jnp.where(lane == 0, i1, jnp.where(lane == 1, i2, jnp.where(lane == 2, w1, jnp.where(
                lane == 3, w2, jnp.where(lane == 4, r1, jnp.where(lane == 5, r2, 0.0))))))
            c0 = g0 + k * rows
            route_ref[c0:c0 + rows, :] = route
            route_t_ref[:, c0:c0 + rows] = route.T[:SUBLANES, :]


def _route_top2(logits):
    lane = lax.broadcasted_iota(jnp.int32, logits.shape, 1)
    lane_f = lane.astype(F32)
    far = float(LANES)
    is_group = (lane >= ROUTER_GROUP_LANE0) & (lane < ROUTER_GROUP_LANE0 + N_GROUPS)
    gl = jnp.where(is_group, logits, NEG_INF)
    g_max = jnp.max(gl, axis=-1, keepdims=True)
    g_sel = jnp.min(jnp.where(gl == g_max, lane_f, far), axis=-1, keepdims=True) - float(ROUTER_GROUP_LANE0)
    g_gate = 1.0 / jnp.sum(jnp.exp(gl - g_max), axis=-1, keepdims=True)
    in_group = jnp.right_shift(lane, EXPERTS_PER_GROUP.bit_length() - 1).astype(F32) == g_sel
    el = jnp.where(in_group, logits, NEG_INF)
    m1 = jnp.max(el, axis=-1, keepdims=True)
    i1 = jnp.min(jnp.where(el == m1, lane_f, far), axis=-1, keepdims=True)
    el2 = jnp.where(lane_f == i1, NEG_INF, el)
    m2 = jnp.max(el2, axis=-1, keepdims=True)
    i2 = jnp.min(jnp.where(el2 == m2, lane_f, far), axis=-1, keepdims=True)
    e2 = jnp.exp(m2 - m1)
    p1 = 1.0 / (1.0 + e2)
    return i1, i2, g_gate * p1, g_gate * (e2 * p1), lane_f == i1, lane_f == i2


def out_proj(mixed, proj, qm_block, mem_k, mem_v, layer, w_out, x, g_ffn, w_router, b_router, counts0, *, tm,
             n_groups, rows_per_stream):
    T = x.shape[0]
    n_streams = max(1, tm // rows_per_stream)
    tiles_per_stream = max(1, rows_per_stream // tm)
    gr = tm // n_groups
    assert min(gr, ROUTE_ROWS) % LANES == 0
    row = lambda width, col: pl.BlockSpec((tm, width), lambda i: (i, col))
    const = lambda shape: pl.BlockSpec(shape, lambda i: (0,) * len(shape))
    mem = pl.BlockSpec((1, n_streams, N_MEM * H_MEM, D_HEAD), lambda i: (layer, i // tiles_per_stream, 0, 0))
    mem_k = mem_k.reshape(mem_k.shape[:2] + (N_MEM * H_MEM, D_HEAD))
    mem_v = mem_v.reshape(mem_v.shape[:2] + (N_MEM * H_MEM, D_HEAD))
    tri = (jnp.arange(gr)[None, :] < jnp.arange(gr)[:, None]).astype(BF16)
    return pl.pallas_call(
        functools.partial(_out_proj_kernel, n_streams, n_groups),
        grid=(T // tm,),
        in_specs=[row(MIX_W, 0), row(MEM_W, qm_block), mem, mem, const((MIX_W + MEM_W, D_MODEL)),
                  row(D_MODEL, 0), const((1, D_MODEL)), const((D_MODEL, LANES)), const((1, LANES)),
                  const((gr, gr)), const((SUBLANES, LANES))],
        out_specs=[row(D_MODEL, 0), pl.BlockSpec((tm * ROW_TILE, LANES), lambda i: (i, 0)), row(LANES, 0),
                   pl.BlockSpec((SUBLANES, tm), lambda i: (0, i)), const((SUBLANES, LANES))],
        out_shape=[jax.ShapeDtypeStruct((T, D_MODEL), F32), jax.ShapeDtypeStruct((T * ROW_TILE, LANES), F32),
                   jax.ShapeDtypeStruct((T, LANES), F32), jax.ShapeDtypeStruct((SUBLANES, T), F32),
                   jax.ShapeDtypeStruct((SUBLANES, LANES), F32)],
        compiler_params=_cparams(("arbitrary",)),
        name="out_proj",
    )(mixed, proj, mem_k, mem_v, w_out, x, g_ffn.reshape(1, D_MODEL), w_router, b_router, tri, counts0)


def _dispatch_kernel(tms, steps, pe_ref, pos0_ref, pos1_ref, *rest):
    h2_refs = rest[:len(tms)]
    xpad_hbm, zbuf, sem, zsem = rest[len(tms):]
    i = pl.program_id(0)

    @pl.when(i == 0)
    def _():
        zbuf[...] = jnp.zeros_like(zbuf)

        def block_copy(first_row):
            start = pl.multiple_of(first_row * ROW_TILE, MOE_BM * ROW_TILE)
            return pltpu.make_async_copy(zbuf, xpad_hbm.at[pl.ds(start, MOE_BM * ROW_TILE)], zsem)

        def tail_copy(e):
            return block_copy(pe_ref[e] - MOE_BM)

        def nonempty(e):
            return pe_ref[e] > (pe_ref[e - 1] if e > 0 else 0)

        for e in range(N_EXPERTS):
            pl.when(nonempty(e))(lambda e=e: tail_copy(e).start(priority=e % 2))
        for e in range(N_EXPERTS):
            pl.when(nonempty(e))(lambda e=e: tail_copy(e).wait())

        def fill(b, carry):
            cp = block_copy(b * MOE_BM)
            cp.start()
            cp.wait()
            return carry

        lax.fori_loop(pe_ref[N_EXPERTS - 1] // MOE_BM, xpad_hbm.shape[0] // (MOE_BM * ROW_TILE), fill, 0)

    def scatter_rows(h2_ref, tm):
        def body(r, carry):
            src = _token_tile(h2_ref, r)
            pltpu.make_async_copy(src, _token_tile(xpad_hbm, pos0_ref[r]), sem).start(priority=0)
            pltpu.make_async_copy(src, _token_tile(xpad_hbm, pos1_ref[r]), sem).start(priority=1)
            return carry

        lax.fori_loop(0, tm, body, 0, unroll=8)
        for _ in range(2):
            pltpu.make_async_copy(h2_ref, xpad_hbm.at[pl.ds(0, tm * ROW_TILE)], sem).wait()

    first = 0
    for h2_ref, tm, n in zip(h2_refs, tms, steps):
        pl.when(jnp.logical_and(i >= first, i < first + n))(functools.partial(scatter_rows, h2_ref, tm))
        first += n


def moe_dispatch(pad_ends, pos_groups, h2_groups, n_rows):
    sizes = [h2.shape[0] // ROW_TILE for h2 in h2_groups]
    tms = [min(T, DISPATCH_TM) for T in sizes]
    steps = [T // tm for T, tm in zip(sizes, tms)]
    firsts = [sum(steps[:g]) for g in range(len(steps))]
    pos0, pos1 = (jnp.concatenate([jnp.pad(p[k], (0, n * DISPATCH_TM - p[k].shape[0]))
                                   for p, n in zip(pos_groups, steps)]) for k in range(2))

    def rows(tm, first, n):
        return pl.BlockSpec((tm * ROW_TILE, LANES), lambda i, pe: (jnp.clip(i - first, 0, n - 1), 0))

    grid_spec = pltpu.PrefetchScalarGridSpec(
        num_scalar_prefetch=1,
        grid=(sum(steps),),
        in_specs=[pl.BlockSpec((DISPATCH_TM,), lambda i, pe: (i,), memory_space=pltpu.SMEM)] * 2
        + [rows(tm, first, n) for tm, first, n in zip(tms, firsts, steps)],
        out_specs=pl.BlockSpec(memory_space=pl.ANY),
        scratch_shapes=[pltpu.VMEM((MOE_BM * ROW_TILE, LANES), F32), pltpu.SemaphoreType.DMA(()),
                        pltpu.SemaphoreType.DMA(())],
    )
    return pl.pallas_call(
        functools.partial(_dispatch_kernel, tuple(tms), tuple(steps)),
        grid_spec=grid_spec,
        out_shape=jax.ShapeDtypeStruct((n_rows * ROW_TILE, LANES), F32),
        compiler_params=_cparams(("arbitrary",)),
        name="moe_dispatch",
    )(pad_ends, pos0, pos1, *h2_groups)


def _moe_kernel(be_ref, nu_ref, x_ref, wg_ref, wu_ref, wd_ref, y_ref, wg_s, wu_s, wd_s):
    i = pl.program_id(0)
    used = i < nu_ref[0]
    new_expert = jnp.logical_or(i == 0, be_ref[i] != be_ref[jnp.maximum(i - 1, 0)])

    @pl.when(jnp.logical_and(used, new_expert))
    def _():
        wg_s[...] = wg_ref[0, 0].astype(BF16)
        wu_s[...] = wu_ref[0, 0].astype(BF16)
        wd_s[...] = wd_ref[0, 0].astype(BF16)

    @pl.when(used)
    def _():
        x = _load_token_tiles(x_ref, 0, MOE_BM).astype(BF16)
        g = _dot(x, wg_s[...])
        u = _dot(x, wu_s[...])
        a = (g * _sigmoid(g) * u).astype(BF16)
        _store_token_tiles(y_ref, 0, _dot(a, wd_s[...]))

    @pl.when(jnp.logical_not(used))
    def _():
        y_ref[...] = jnp.zeros_like(y_ref)


def moe_experts(x_pad, block_e, n_used, layer, wg, wu, wd):
    n_rows = x_pad.shape[0] // ROW_TILE
    rows = pl.BlockSpec((MOE_BM * ROW_TILE, LANES), lambda i, be, nu: (i, 0))
    w_in = pl.BlockSpec((1, 1, D_MODEL, D_FF_EXPERT), lambda i, be, nu: (layer, be[i], 0, 0))
    w_dn = pl.BlockSpec((1, 1, D_FF_EXPERT, D_MODEL), lambda i, be, nu: (layer, be[i], 0, 0))
    grid_spec = pltpu.PrefetchScalarGridSpec(
        num_scalar_prefetch=2,
        grid=(n_rows // MOE_BM,),
        in_specs=[pl.BlockSpec((MOE_BM * ROW_TILE, LANES), lambda i, be, nu: (jnp.minimum(i, nu[0] - 1), 0)),
                  w_in, w_in, w_dn],
        out_specs=rows,
        scratch_shapes=[pltpu.VMEM((D_MODEL, D_FF_EXPERT), BF16), pltpu.VMEM((D_MODEL, D_FF_EXPERT), BF16),
                        pltpu.VMEM((D_FF_EXPERT, D_MODEL), BF16)],
    )
    return pl.pallas_call(
        _moe_kernel,
        grid_spec=grid_spec,
        out_shape=jax.ShapeDtypeStruct((n_rows * ROW_TILE, LANES), F32),
        compiler_params=_cparams(("arbitrary",)),
        name="moe_experts",
    )(block_e, n_used, x_pad, wg, wu, wd)


def _combine_kernel(tm, n_steps, pos0_ref, pos1_ref, pos0_next_ref, pos1_next_ref, route_ref, x1_ref, ypad_hbm,
                    o_ref, ybuf, sems):
    i = pl.program_id(0)
    slot = lax.rem(i, 2)

    def gather(p0_ref, p1_ref, s):
        buf = ybuf.at[s]

        def body(r, carry):
            pltpu.make_async_copy(_token_tile(ypad_hbm, p0_ref[r]), _token_tile(buf, r), sems.at[s]).start(
                priority=0)
            pltpu.make_async_copy(_token_tile(ypad_hbm, p1_ref[r]), _token_tile(buf, tm + r), sems.at[s]).start(
                priority=1)
            return carry

        lax.fori_loop(0, tm, body, 0, unroll=8)

    @pl.when(i == 0)
    def _():
        gather(pos0_ref, pos1_ref, 0)

    @pl.when(i + 1 < n_steps)
    def _():
        gather(pos0_next_ref, pos1_next_ref, 1 - slot)

    buf = ybuf.at[slot]
    pltpu.make_async_copy(ypad_hbm.at[pl.ds(0, 2 * tm * ROW_TILE)], buf, sems.at[slot]).wait()
    g1 = route_ref[:, 2:3]
    g2 = route_ref[:, 3:4]
    o_ref[...] = x1_ref[...] + (g1 * _load_token_tiles(buf, 0, tm) + g2 * _load_token_tiles(buf, tm, tm))


def moe_combine(pos, route, x1, y_pad):
    pos0, pos1 = pos
    T = x1.shape[0]
    tm = min(T, COMBINE_TM)
    n_steps = T // tm
    return pl.pallas_call(
        functools.partial(_combine_kernel, tm, n_steps),
        grid=(n_steps,),
        in_specs=[pl.BlockSpec((tm,), lambda i: (i,), memory_space=pltpu.SMEM)] * 2
        + [pl.BlockSpec((tm,), lambda i: (jnp.minimum(i + 1, n_steps - 1),), memory_space=pltpu.SMEM)] * 2
        + [pl.BlockSpec((tm, LANES), lambda i: (i, 0)), pl.BlockSpec((tm, D_MODEL), lambda i: (i, 0)),
           pl.BlockSpec(memory_space=pl.ANY)],
        out_specs=pl.BlockSpec((tm, D_MODEL), lambda i: (i, 0)),
        out_shape=jax.ShapeDtypeStruct((T, D_MODEL), F32),
        scratch_shapes=[pltpu.VMEM((2, 2 * tm * ROW_TILE, LANES), F32), pltpu.SemaphoreType.DMA((2,))],
        compiler_params=_cparams(("arbitrary",)),
        name="moe_combine",
    )(pos0, pos1, pos0, pos1, route, x1, y_pad)


def hier_moe(groups, counts, layer, wg, wu, wd):
    sizes = [g[0].shape[0] for g in groups]
    n_assign = 2 * sum(sizes)
    n_blocks = (n_assign + N_EXPERTS * (MOE_BM - 1) + MOE_BM - 1) // MOE_BM
    fields = jnp.concatenate([g[3] for g in groups], axis=1)
    experts = fields[0:2].astype(jnp.int32)
    ranks = fields[4:6].astype(jnp.int32)
    cnt = counts[0, :N_EXPERTS].astype(jnp.int32)
    padded = ((cnt + MOE_BM - 1) // MOE_BM) * MOE_BM
    pad_ends = jnp.cumsum(padded).astype(jnp.int32)
    pad_starts = pad_ends - padded
    is_e = experts[:, :, None] == jnp.arange(N_EXPERTS, dtype=jnp.int32)
    pos = ranks + jnp.sum(jnp.where(is_e, pad_starts, 0), axis=-1)
    block_start = jnp.arange(n_blocks, dtype=jnp.int32) * MOE_BM
    block_e = jnp.minimum(jnp.sum(block_start[:, None] >= pad_ends[None, :], axis=1), N_EXPERTS - 1).astype(jnp.int32)
    n_used = (pad_ends[-1:] // MOE_BM).astype(jnp.int32)
    starts = [sum(sizes[:g]) for g in range(len(groups))]
    pos_g = [(pos[0, s:s + n], pos[1, s:s + n]) for s, n in zip(starts, sizes)]
    x_pad = moe_dispatch(pad_ends, pos_g, [g[1] for g in groups], n_blocks * MOE_BM)
    y_pad = moe_experts(x_pad, block_e, n_used, layer, wg, wu, wd)
    return [moe_combine(p, g[2], g[0], y_pad) for g, p in zip(groups, pos_g)]


def _rope_tables(pos):
    half = D_HEAD // 2
    inv = ROPE_BASE ** (-jnp.arange(half, dtype=F32) / half)
    ang = pos.astype(F32)[:, None] * inv[None, :]
    cos, sin = jnp.cos(ang), jnp.sin(ang)
    return jnp.concatenate([cos, cos], axis=-1), jnp.concatenate([-sin, sin], axis=-1)


def _router_weights(w_group, b_group, w_router, b_router):
    w = jnp.zeros((D_MODEL, LANES), F32)
    w = w.at[:, :N_EXPERTS].set(w_router).at[:, ROUTER_GROUP_LANE0:ROUTER_GROUP_LANE0 + N_GROUPS].set(w_group)
    b = jnp.zeros((1, LANES), F32)
    b = b.at[0, :N_EXPERTS].set(b_router).at[0, ROUTER_GROUP_LANE0:ROUTER_GROUP_LANE0 + N_GROUPS].set(b_group)
    return w.astype(BF16), b


def _tile(v, n):
    return jnp.tile(v.astype(F32), n)


def _run_trunk(x, B, L, mem_k, mem_v, ret_state0, band_past, P):
    prompt = band_past is None
    out_tm, out_groups = (OUT_PROJ_TM, OUT_PROJ_TM // OUT_PROJ_GROUP) if prompt else (OUT_PROJ_SAMPLE_TM, 1)
    proj_tm = min(B * L, PROJ_TM)
    scale = D_HEAD ** -0.5
    ones = lambda n: jnp.ones((n * D_HEAD,), F32)
    if prompt:
        cos2, sin2 = _rope_tables(jnp.arange(L, dtype=jnp.int32))
        tab_map = lambda i: i % (L // proj_tm)
    else:
        cos2, sin2 = _rope_tables(PAST_LEN + jnp.arange(L, dtype=jnp.int32))
        cos2, sin2 = jnp.tile(cos2, (B, 1)), jnp.tile(sin2, (B, 1))
        tab_map = lambda i: i
    out = {}

    def ffn(layer, mixed, proj, qm_block, w_out, x):
        def run(counts0):
            return out_proj(
                mixed, proj, qm_block, mem_k, mem_v, layer, w_out.astype(BF16), x, P["norm_ffn"][layer],
                *_router_weights(P["w_group"][layer], P["b_group"][layer], P["w_router"][layer],
                                 P["b_router"][layer]), counts0, tm=out_tm, n_groups=out_groups, rows_per_stream=L)
        return (yield run)

    segs0 = (("rope", N_HEADS), ("rope", N_HEADS), ("plain", N_HEADS), ("silu", N_HEADS), ("norm", H_MEM))
    gain0 = jnp.concatenate([ones(N_HEADS), ones(N_HEADS) * scale, ones(2 * N_HEADS),
                             _tile(P["q_norm_mem"][0], H_MEM) * scale])
    proj = norm_proj(x, P["norm_mix"][0], P["w_in_ret"][0].astype(BF16), gain0, segs0, tm=proj_tm,
                     out_dtype=BF16, rope=(cos2, sin2, tab_map))
    mixed, S = retention(proj, ret_state0, P["gn_ret"][0], B=B, L=L, C=RET_CHUNK if prompt else L)
    out["ret_state"] = S
    x = yield from ffn(0, mixed, proj, (4 * MIX_W) // MEM_W, P["w_out_ret"][0], x)

    segs1 = (("norm", N_HEADS), ("norm", N_HEADS), ("plain", N_HEADS), ("norm", H_MEM))
    gain1 = jnp.concatenate([_tile(P["q_norm_att"][0], N_HEADS) * scale, _tile(P["k_norm_att"][0], N_HEADS),
                             ones(N_HEADS), _tile(P["q_norm_mem"][1], H_MEM) * scale])
    w_in = P["w_in_att"][0].astype(BF16)
    if prompt:
        proj = norm_proj(x, P["norm_mix"][1], w_in, gain1, segs1, tm=proj_tm, out_dtype=BF16)
        keep = min(BAND_PAST, L)
        tiles = L // keep
        kv = norm_proj(x, P["norm_mix"][1], w_in[:, MIX_W:3 * MIX_W], gain1[MIX_W:3 * MIX_W], segs1[1:3],
                       tm=keep, out_dtype=F32, n_tiles=B, row_map=lambda i: i * tiles + tiles - 1)
        out["band_k"] = kv[:, :MIX_W].reshape(B, keep, N_HEADS, D_HEAD)
        out["band_v"] = kv[:, MIX_W:].reshape(B, keep, N_HEADS, D_HEAD)
        mixed = band_prompt(proj, P["rel_bias_att"][0], B=B, L=L)
    else:
        proj_f = norm_proj(x, P["norm_mix"][1], w_in, gain1, segs1, tm=proj_tm, out_dtype=F32)
        out["band_k"] = proj_f[:, MIX_W:2 * MIX_W].reshape(B, L, N_HEADS, D_HEAD)
        out["band_v"] = proj_f[:, 2 * MIX_W:3 * MIX_W].reshape(B, L, N_HEADS, D_HEAD)
        proj = proj_f.astype(BF16)
        mixed = band_sample(proj, band_past[0], band_past[1], P["rel_bias_att"][0], B=B, L=L)
    out["y"] = yield from ffn(1, mixed, proj, (3 * MIX_W) // MEM_W, P["w_out_att"][0], x)
    return out


def kernel(x_prompt, x_sample, mem_prompt, state_ret, cache_band_k, cache_band_v, cache_mem_k, cache_mem_v, norm_mix, norm_ffn, norm_mem, w_in_ret, gn_ret, w_out_ret, w_in_att, q_norm_att, k_norm_att, rel_bias_att, w_out_att, w_mem_kv, q_norm_mem, k_norm_mem, w_group, b_group, w_router, b_router, w_e_gate, w_e_up, w_e_down):
    P = {"norm_mix": norm_mix, "norm_ffn": norm_ffn, "w_in_ret": w_in_ret, "gn_ret": gn_ret,
         "w_out_ret": w_out_ret, "w_in_att": w_in_att, "q_norm_att": q_norm_att,
         "k_norm_att": k_norm_att, "rel_bias_att": rel_bias_att, "w_out_att": w_out_att,
         "q_norm_mem": q_norm_mem, "w_group": w_group, "b_group": b_group, "w_router": w_router,
         "b_router": b_router, "w_e_gate": w_e_gate, "w_e_up": w_e_up, "w_e_down": w_e_down}
    B, L, D = x_prompt.shape
    Bd, Ld, _ = x_sample.shape
    depth = norm_mix.shape[0]

    mem_k_p, mem_v_p = [], []
    for i in range(depth):
        gain = jnp.concatenate([_tile(k_norm_mem[i], H_MEM), jnp.ones((MEM_W,), F32)])
        kv = norm_proj(mem_prompt.reshape(B * N_MEM, D), norm_mem[i], w_mem_kv[i].astype(BF16), gain,
                       (("norm", H_MEM), ("plain", H_MEM)), tm=N_MEM, out_dtype=F32)
        mem_k_p.append(kv[:, :MEM_W].reshape(B, N_MEM, H_MEM, D_HEAD))
        mem_v_p.append(kv[:, MEM_W:].reshape(B, N_MEM, H_MEM, D_HEAD))
    mem_k_p = jnp.stack(mem_k_p)
    mem_v_p = jnp.stack(mem_v_p)

    zeros_state = jnp.zeros((B, N_HEADS, D_HEAD, D_HEAD), F32)
    trunks = [_run_trunk(x_prompt.reshape(B * L, D), B, L, mem_k_p, mem_v_p, zeros_state, None, P),
              _run_trunk(x_sample.reshape(Bd * Ld, D), Bd, Ld, cache_mem_k, cache_mem_v, state_ret[0],
                         (cache_band_k[0], cache_band_v[0]), P)]
    pending = [next(t) for t in trunks]
    results = []
    for layer in range(depth):
        counts = jnp.zeros((SUBLANES, LANES), F32)
        groups = []
        for run in pending:
            x1, h2, route, route_t, counts = run(counts)
            groups.append((x1, h2, route, route_t))
        outs = hier_moe(groups, counts, layer, w_e_gate, w_e_up, w_e_down)
        pending = []
        for t, x2 in zip(trunks, outs):
            try:
                pending.append(t.send(x2))
            except StopIteration as done:
                results.append(done.value)
    res_p, res_s = results

    return (res_p["y"].reshape(B, L, D), res_s["y"].reshape(Bd, Ld, D),
            res_p["ret_state"][None], res_s["ret_state"][None],
            res_p["band_k"][None], res_p["band_v"][None], res_s["band_k"][None], res_s["band_v"][None],
            mem_k_p, mem_v_p)
```

```python
import functools

import jax
import jax.numpy as jnp
from jax import lax
from jax.experimental import pallas as pl
from jax.experimental.pallas import tpu as pltpu

F32 = jnp.float32
BF16 = jnp.bfloat16

D_MODEL = 1024
D_HEAD = 128
CHUNK = 64
N_HEADS = 8
H_MEM = 4
N_MEM = 256
MEM_W = H_MEM * D_HEAD
MIX_W = N_HEADS * D_HEAD
N_PREV_CHUNKS = 8
BAND_PAST = N_PREV_CHUNKS * CHUNK
REL_CLIP = 128
N_GROUPS = 4
EXPERTS_PER_GROUP = 8
N_EXPERTS = N_GROUPS * EXPERTS_PER_GROUP
D_FF_EXPERT = 512
ROPE_BASE = 10000.0
EPS = 1e-6
NEG_INF = -1e30
PAST_LEN = 1024

LANES = 128
SUBLANES = 8

ROPE_BLOCK = 128
RET_CHUNK = 256
BAND_TQ = 512
BAND_SUB = 128
BAND_WIN = BAND_PAST + BAND_SUB
MOE_BM = 512
DISPATCH_TM = 4096
COMBINE_TM = 1024
PROJ_TM = 1024
OUT_PROJ_TM = 1024
OUT_PROJ_GROUP = 512
OUT_PROJ_SAMPLE_TM = 128
ROUTE_ROWS = 128
ROUTER_GROUP_LANE0 = N_EXPERTS
V7X_VMEM_BYTES = 64 * 1024 * 1024
VMEM_LIMIT = V7X_VMEM_BYTES - 8 * 1024 * 1024


def _cparams(sem):
    return pltpu.CompilerParams(dimension_semantics=sem, vmem_limit_bytes=VMEM_LIMIT)


def _dot(a, b):
    return jnp.dot(a, b, preferred_element_type=F32)


def _dot_nt(a, b):
    return lax.dot_general(a, b, (((1,), (1,)), ((), ())), preferred_element_type=F32)


def _sigmoid(x):
    return 1.0 / (1.0 + jnp.exp(-x))


ROW_TILE = D_MODEL // LANES
assert ROW_TILE == SUBLANES


def _load_token_tiles(ref, t0, n):
    return jnp.concatenate([ref[pl.ds(t0 * ROW_TILE + j, n, stride=ROW_TILE), :] for j in range(ROW_TILE)], axis=1)


def _store_token_tiles(ref, t0, val):
    n = val.shape[0]
    for j in range(ROW_TILE):
        ref[pl.ds(t0 * ROW_TILE + j, n, stride=ROW_TILE), :] = val[:, j * LANES:(j + 1) * LANES]


def _token_tile(ref, t):
    return ref.at[pl.ds(pl.multiple_of(t * ROW_TILE, ROW_TILE), ROW_TILE)]


def _norm_proj_kernel(segs, has_rope, x_ref, g_ref, w_ref, gain_ref, *rest):
    if has_rope:
        cos_ref, sin_ref, o_ref = rest
    else:
        (o_ref,) = rest
    x = x_ref[...]
    ms = jnp.mean(x * x, axis=-1, keepdims=True)
    h = (x * lax.rsqrt(ms + EPS) * g_ref[...]).astype(BF16)
    col = 0
    for kind, n_heads in segs:
        width = n_heads * D_HEAD
        acc = _dot(h, w_ref[:, col:col + width])
        for j in range(n_heads):
            a = acc[:, j * D_HEAD:(j + 1) * D_HEAD]
            c0 = col + j * D_HEAD
            gain = gain_ref[:, c0:c0 + D_HEAD]
            if kind == "rope":
                rot = pltpu.roll(a, D_HEAD // 2, 1)
                out = (a * cos_ref[...] + rot * sin_ref[...]) * gain
            elif kind == "norm":
                out = a * lax.rsqrt(jnp.mean(a * a, axis=-1, keepdims=True) + EPS) * gain
            elif kind == "silu":
                out = a * _sigmoid(a)
            else:
                out = a
            o_ref[:, c0:c0 + D_HEAD] = out.astype(o_ref.dtype)
        col += width


def norm_proj(x, g, w, gain, segs, *, tm, out_dtype, n_tiles=None, row_map=None, rope=None):
    T, D = x.shape
    N = w.shape[1]
    assert sum(n for _, n in segs) * D_HEAD == N
    if n_tiles is None:
        n_tiles = T // tm
    if row_map is None:
        row_map = lambda i: i
    in_specs = [
        pl.BlockSpec((tm, D), lambda i: (row_map(i), 0)),
        pl.BlockSpec((1, D), lambda i: (0, 0)),
        pl.BlockSpec((D, N), lambda i: (0, 0)),
        pl.BlockSpec((1, N), lambda i: (0, 0)),
    ]
    args = [x, g.reshape(1, D), w, gain.reshape(1, N)]
    if rope is not None:
        cos2, sin2, tab_map = rope
        in_specs += [pl.BlockSpec((tm, D_HEAD), lambda i: (tab_map(i), 0)),
                     pl.BlockSpec((tm, D_HEAD), lambda i: (tab_map(i), 0))]
        args += [cos2, sin2]
    return pl.pallas_call(
        functools.partial(_norm_proj_kernel, segs, rope is not None),
        grid=(n_tiles,),
        in_specs=in_specs,
        out_specs=pl.BlockSpec((tm, N), lambda i: (i, 0)),
        out_shape=jax.ShapeDtypeStruct((n_tiles * tm, N), out_dtype),
        compiler_params=_cparams(("arbitrary",)),
        name="norm_proj",
    )(*args)


def _retention_kernel(q_ref, k_ref, v_ref, g_ref, s0_ref, dmask_ref, qdec_ref, kdec_ref, gc_ref,
                      gn_ref, o_ref, s_ref):
    @pl.when(pl.program_id(1) == 0)
    def _():
        s_ref[...] = s0_ref[...]

    for h in range(N_HEADS):
        sl = slice(h * D_HEAD, (h + 1) * D_HEAD)
        q = q_ref[:, sl]
        k = k_ref[:, sl]
        v = v_ref[:, sl]
        S = s_ref[0, h]
        scores = _dot_nt(q, k) * dmask_ref[h]
        intra = _dot(scores.astype(BF16), v)
        cross = _dot(q, S.astype(BF16)) * qdec_ref[:, sl]
        o = intra + cross
        kd = (k.astype(F32) * kdec_ref[:, sl]).astype(BF16)
        s_ref[0, h] = gc_ref[h] * S + lax.dot_general(kd, v, (((0,), (0,)), ((), ())), preferred_element_type=F32)
        mu = jnp.mean(o, axis=-1, keepdims=True)
        d = o - mu
        var = jnp.mean(d * d, axis=-1, keepdims=True)
        y = d * lax.rsqrt(var + EPS) * gn_ref[:, sl] * g_ref[:, sl].astype(F32)
        o_ref[:, sl] = y.astype(o_ref.dtype)


def _retention_tables(C):
    log_gamma = jnp.log1p(-jnp.exp2(-5.0 - jnp.arange(N_HEADS, dtype=F32)))
    idx = jnp.arange(C, dtype=F32)
    diff = idx[:, None] - idx[None, :]
    dmask = jnp.where(diff[None] >= 0,
                      jnp.exp(log_gamma[:, None, None] * jnp.maximum(diff, 0.0)[None]), 0.0)
    q_dec = jnp.exp(log_gamma[None, :] * (idx + 1.0)[:, None])
    k_dec = jnp.exp(log_gamma[None, :] * (C - 1.0 - idx)[:, None])
    g_c = jnp.exp(log_gamma * C)
    rep = lambda t: jnp.repeat(t, D_HEAD, axis=-1)
    return dmask, rep(q_dec), rep(k_dec), rep(g_c[:, None])[:, None, :]


def retention(proj, s0, gn, *, B, L, C):
    nc = L // C
    dmask, qdec, kdec, gc = _retention_tables(C)
    row = lambda col: pl.BlockSpec((C, MIX_W), lambda b, c: (b * nc + c, col))
    const = lambda shape: pl.BlockSpec(shape, lambda b, c: (0,) * len(shape))
    state_spec = pl.BlockSpec((1, N_HEADS, D_HEAD, D_HEAD), lambda b, c: (b, 0, 0, 0))
    return pl.pallas_call(
        _retention_kernel,
        grid=(B, nc),
        in_specs=[row(0), row(1), row(2), row(3), state_spec,
                  const((N_HEADS, C, C)), const((C, MIX_W)), const((C, MIX_W)),
                  const((N_HEADS, 1, D_HEAD)), const((1, MIX_W))],
        out_specs=[pl.BlockSpec((C, MIX_W), lambda b, c: (b * nc + c, 0)), state_spec],
        out_shape=[jax.ShapeDtypeStruct((B * L, MIX_W), BF16),
                   jax.ShapeDtypeStruct((B, N_HEADS, D_HEAD, D_HEAD), F32)],
        compiler_params=_cparams(("arbitrary", "arbitrary")),
        name="retention",
    )(proj, proj, proj, proj, s0, dmask, qdec, kdec, gc, gn.reshape(1, MIX_W))


def _attend_two_piece(q, k_a, k_b, v_a, v_b, bias_a, bias_b):
    s_a = _dot_nt(q, k_a) + bias_a
    s_b = _dot_nt(q, k_b) + bias_b
    m = jnp.maximum(jnp.max(s_a, axis=-1, keepdims=True), jnp.max(s_b, axis=-1, keepdims=True))
    p_a = jnp.exp(s_a - m)
    p_b = jnp.exp(s_b - m)
    denom = jnp.sum(p_a, axis=-1, keepdims=True) + jnp.sum(p_b, axis=-1, keepdims=True)
    o = _dot(p_a.astype(BF16), v_a) + _dot(p_b.astype(BF16), v_b)
    return o / denom


def _band_prompt_kernel(q_ref, kp_ref, kc_ref, vp_ref, vc_ref, bias_ref, o_ref, s_buf, p_buf):
    @pl.when((pl.program_id(0) == 0) & (pl.program_id(1) == 0))
    def _():
        p_buf[...] = jnp.zeros_like(p_buf)

    prev_off = jnp.where(pl.program_id(1) > 0, 0.0, NEG_INF).astype(F32)
    for h in range(N_HEADS):
        sl = slice(h * D_HEAD, (h + 1) * D_HEAD)
        q = q_ref[:, sl]
        s_buf[:, :BAND_TQ] = _dot_nt(q, kp_ref[:, sl]) + prev_off
        s_buf[:, BAND_TQ:] = _dot_nt(q, kc_ref[:, sl])
        denoms = []
        for s in range(BAND_TQ // BAND_SUB):
            r0 = s * BAND_SUB
            sc = s_buf[r0:r0 + BAND_SUB, r0:r0 + BAND_WIN] + bias_ref[h]
            p = jnp.exp(sc - jnp.max(sc, axis=-1, keepdims=True))
            denoms.append(jnp.sum(p, axis=-1, keepdims=True))
            p_buf[r0:r0 + BAND_SUB, r0:r0 + BAND_WIN] = p.astype(BF16)
        o = _dot(p_buf[:, :BAND_TQ], vp_ref[:, sl]) + _dot(p_buf[:, BAND_TQ:], vc_ref[:, sl])
        o_ref[:, sl] = (o / jnp.concatenate(denoms, axis=0)).astype(o_ref.dtype)


def _rel_bias(bias_table, offset, n_rows, width):
    n = n_rows + width - 1
    v = jnp.arange(n, dtype=jnp.int32)
    v = jnp.where(v < width, v, v - n)
    diag = bias_table.astype(F32)[:, jnp.clip(offset - v, -REL_CLIP, REL_CLIP) + REL_CLIP]
    cut = jnp.tile(diag, (1, n_rows))[:, :n_rows * (n - 1)].reshape(diag.shape[0], n_rows, n - 1)
    return cut[:, :, :width]


def _band_prompt_bias(bias_table):
    r = jnp.arange(BAND_SUB, dtype=jnp.int32)[:, None]
    j = jnp.arange(BAND_WIN, dtype=jnp.int32)[None, :]
    q_chunk = r // CHUNK + N_PREV_CHUNKS
    k_chunk = j // CHUNK
    allowed = (k_chunk <= q_chunk) & (k_chunk >= q_chunk - N_PREV_CHUNKS)
    return jnp.where(allowed[None], _rel_bias(bias_table, BAND_PAST, BAND_SUB, BAND_WIN), NEG_INF)


def band_prompt(proj, bias_table, *, B, L):
    nq = L // BAND_TQ
    cur = lambda col: pl.BlockSpec((BAND_TQ, MIX_W), lambda b, i: (b * nq + i, col))
    prev = lambda col: pl.BlockSpec((BAND_TQ, MIX_W), lambda b, i: (b * nq + jnp.maximum(i - 1, 0), col))
    return pl.pallas_call(
        _band_prompt_kernel,
        grid=(B, nq),
        in_specs=[cur(0), prev(1), cur(1), prev(2), cur(2),
                  pl.BlockSpec((N_HEADS, BAND_SUB, BAND_WIN), lambda b, i: (0, 0, 0))],
        out_specs=pl.BlockSpec((BAND_TQ, MIX_W), lambda b, i: (b * nq + i, 0)),
        out_shape=jax.ShapeDtypeStruct((B * L, MIX_W), BF16),
        scratch_shapes=[pltpu.VMEM((BAND_TQ, 2 * BAND_TQ), F32), pltpu.VMEM((BAND_TQ, 2 * BAND_TQ), BF16)],
        compiler_params=_cparams(("arbitrary", "arbitrary")),
        name="band_prompt",
    )(proj, proj, proj, proj, proj, _band_prompt_bias(bias_table))


def _band_sample_kernel(q_ref, kp_ref, kc_ref, vp_ref, vc_ref, bias_p_ref, bias_c_ref, o_ref):
    n_past = kp_ref.shape[1] // N_HEADS
    for h in range(N_HEADS):
        sl = slice(h * D_HEAD, (h + 1) * D_HEAD)
        kp = kp_ref.at[0][pl.ds(h, n_past, stride=N_HEADS), :].astype(BF16)
        vp = vp_ref.at[0][pl.ds(h, n_past, stride=N_HEADS), :].astype(BF16)
        o = _attend_two_piece(q_ref[:, sl], kp, kc_ref[:, sl], vp, vc_ref[:, sl], bias_p_ref[h], bias_c_ref[h])
        o_ref[:, sl] = o.astype(o_ref.dtype)


def band_sample(proj, past_k, past_v, bias_table, *, B, L):
    P = past_k.shape[1]
    qpos = PAST_LEN + jnp.arange(L, dtype=jnp.int32)
    kpos = jnp.concatenate([PAST_LEN - P + jnp.arange(P, dtype=jnp.int32), qpos])
    qc = (qpos // CHUNK)[:, None]
    kc = (kpos // CHUNK)[None, :]
    allowed = (kpos[None, :] >= 0) & (kc <= qc) & (kc >= qc - N_PREV_CHUNKS)
    rel_bias = jnp.concatenate([_rel_bias(bias_table, P, L, P), _rel_bias(bias_table, 0, L, L)], axis=-1)
    bias = jnp.where(allowed[None], rel_bias, NEG_INF)
    cur = lambda col: pl.BlockSpec((L, MIX_W), lambda b: (b, col))
    past = pl.BlockSpec((1, P * N_HEADS, D_HEAD), lambda b: (b, 0, 0))
    past_k = past_k.reshape(B, P * N_HEADS, D_HEAD)
    past_v = past_v.reshape(B, P * N_HEADS, D_HEAD)
    return pl.pallas_call(
        _band_sample_kernel,
        grid=(B,),
        in_specs=[cur(0), past, cur(1), past, cur(2),
                  pl.BlockSpec((N_HEADS, L, P), lambda b: (0, 0, 0)),
                  pl.BlockSpec((N_HEADS, L, L), lambda b: (0, 0, 0))],
        out_specs=pl.BlockSpec((L, MIX_W), lambda b: (b, 0)),
        out_shape=jax.ShapeDtypeStruct((B * L, MIX_W), BF16),
        compiler_params=_cparams(("arbitrary",)),
        name="band_sample",
    )(proj, past_k, proj, past_v, proj, bias[:, :, :P], bias[:, :, P:])


def _out_proj_kernel(n_streams, n_groups, o_ref, qm_ref, mk_ref, mv_ref, w_ref, x_ref, gf_ref, wr_ref, br_ref,
                     tri_ref, cnt0_ref, x1_ref, h2_ref, route_ref, route_t_ref, cnt_ref):
    @pl.when(pl.program_id(0) == 0)
    def _():
        cnt_ref[...] = cnt0_ref[...]

    tm = x_ref.shape[0]
    sr = tm // n_streams
    gr = tm // n_groups
    om_heads = []
    for h in range(H_MEM):
        sl = slice(h * D_HEAD, (h + 1) * D_HEAD)
        chunks = []
        for c in range(n_streams):
            mk = mk_ref.at[0, c][pl.ds(h, N_MEM, stride=H_MEM), :].astype(BF16)
            mv = mv_ref.at[0, c][pl.ds(h, N_MEM, stride=H_MEM), :].astype(BF16)
            s = _dot_nt(qm_ref[c * sr:(c + 1) * sr, sl], mk)
            p = jnp.exp(s - jnp.max(s, axis=-1, keepdims=True))
            om = _dot(p.astype(BF16), mv) / jnp.sum(p, axis=-1, keepdims=True)
            chunks.append(om.astype(BF16))
        om_heads.append(chunks[0] if n_streams == 1 else jnp.concatenate(chunks, axis=0))

    rows = min(gr, ROUTE_ROWS)
    lane = lax.broadcasted_iota(jnp.int32, (rows, LANES), 1)
    for g in range(n_groups):
        g0 = g * gr
        feat = jnp.concatenate([o_ref[g0:g0 + gr, :]] + [om[g0:g0 + gr] for om in om_heads], axis=1)
        x1 = x_ref[g0:g0 + gr, :] + _dot(feat, w_ref[...])
        x1_ref[g0:g0 + gr, :] = x1
        hn = x1 * lax.rsqrt(jnp.mean(x1 * x1, axis=-1, keepdims=True) + EPS) * gf_ref[...]
        _store_token_tiles(h2_ref, g0, hn)

        logits = _dot(hn.astype(BF16), wr_ref[...]) + br_ref[...]
        tops = [_route_top2(logits[r0:r0 + rows]) for r0 in range(0, gr, rows)]

        onehot = jnp.concatenate([jnp.where(t[4] | t[5], 1.0, 0.0) for t in tops], axis=0)
        before = cnt_ref[0:1, :] + _dot(tri_ref[...], onehot.astype(BF16))
        cnt_ref[0:1, :] = cnt_ref[0:1, :] + jnp.sum(onehot, axis=0, keepdims=True)
        for k, (i1, i2, w1, w2, oh1, oh2) in enumerate(tops):
            b = before[k * rows:(k + 1) * rows]
            r1 = jnp.sum(jnp.where(oh1, b, 0.0), axis=-1, keepdims=True)
            r2 = jnp.sum(jnp.where(oh2, b, 0.0), axis=-1, keepdims=True)
            route = jnp.where(lane == 0, i1, jnp.where(lane == 1, i2, jnp.where(lane == 2, w1, jnp.where(
                lane == 3, w2, jnp.where(lane == 4, r1, jnp.where(lane == 5, r2, 0.0))))))
            c0 = g0 + k * rows
            route_ref[c0:c0 + rows, :] = route
            route_t_ref[:, c0:c0 + rows] = route.T[:SUBLANES, :]


def _route_top2(logits):
    lane = lax.broadcasted_iota(jnp.int32, logits.shape, 1)
    lane_f = lane.astype(F32)
    far = float(LANES)
    is_group = (lane >= ROUTER_GROUP_LANE0) & (lane < ROUTER_GROUP_LANE0 + N_GROUPS)
    gl = jnp.where(is_group, logits, NEG_INF)
    g_max = jnp.max(gl, axis=-1, keepdims=True)
    g_sel = jnp.min(jnp.where(gl == g_max, lane_f, far), axis=-1, keepdims=True) - float(ROUTER_GROUP_LANE0)
    g_gate = 1.0 / jnp.sum(jnp.exp(gl - g_max), axis=-1, keepdims=True)
    in_group = jnp.right_shift(lane, EXPERTS_PER_GROUP.bit_length() - 1).astype(F32) == g_sel
    el = jnp.where(in_group, logits, NEG_INF)
    m1 = jnp.max(el, axis=-1, keepdims=True)
    i1 = jnp.min(jnp.where(el == m1, lane_f, far), axis=-1, keepdims=True)
    el2 = jnp.where(lane_f == i1, NEG_INF, el)
    m2 = jnp.max(el2, axis=-1, keepdims=True)
    i2 = jnp.min(jnp.where(el2 == m2, lane_f, far), axis=-1, keepdims=True)
    e2 = jnp.exp(m2 - m1)
    p1 = 1.0 / (1.0 + e2)
    return i1, i2, g_gate * p1, g_gate * (e2 * p1), lane_f == i1, lane_f == i2


def out_proj(mixed, proj, qm_block, mem_k, mem_v, layer, w_out, x, g_ffn, w_router, b_router, counts0, *, tm,
             n_groups, rows_per_stream):
    T = x.shape[0]
    n_streams = max(1, tm // rows_per_stream)
    tiles_per_stream = max(1, rows_per_stream // tm)
    gr = tm // n_groups
    assert min(gr, ROUTE_ROWS) % LANES == 0
    row = lambda width, col: pl.BlockSpec((tm, width), lambda i: (i, col))
    const = lambda shape: pl.BlockSpec(shape, lambda i: (0,) * len(shape))
    mem = pl.BlockSpec((1, n_streams, N_MEM * H_MEM, D_HEAD), lambda i: (layer, i // tiles_per_stream, 0, 0))
    mem_k = mem_k.reshape(mem_k.shape[:2] + (N_MEM * H_MEM, D_HEAD))
    mem_v = mem_v.reshape(mem_v.shape[:2] + (N_MEM * H_MEM, D_HEAD))
    tri = (jnp.arange(gr)[None, :] < jnp.arange(gr)[:, None]).astype(BF16)
    return pl.pallas_call(
        functools.partial(_out_proj_kernel, n_streams, n_groups),
        grid=(T // tm,),
        in_specs=[row(MIX_W, 0), row(MEM_W, qm_block), mem, mem, const((MIX_W + MEM_W, D_MODEL)),
                  row(D_MODEL, 0), const((1, D_MODEL)), const((D_MODEL, LANES)), const((1, LANES)),
                  const((gr, gr)), const((SUBLANES, LANES))],
        out_specs=[row(D_MODEL, 0), pl.BlockSpec((tm * ROW_TILE, LANES), lambda i: (i, 0)), row(LANES, 0),
                   pl.BlockSpec((SUBLANES, tm), lambda i: (0, i)), const((SUBLANES, LANES))],
        out_shape=[jax.ShapeDtypeStruct((T, D_MODEL), F32), jax.ShapeDtypeStruct((T * ROW_TILE, LANES), F32),
                   jax.ShapeDtypeStruct((T, LANES), F32), jax.ShapeDtypeStruct((SUBLANES, T), F32),
                   jax.ShapeDtypeStruct((SUBLANES, LANES), F32)],
        compiler_params=_cparams(("arbitrary",)),
        name="out_proj",
    )(mixed, proj, mem_k, mem_v, w_out, x, g_ffn.reshape(1, D_MODEL), w_router, b_router, tri, counts0)


def _dispatch_kernel(tms, steps, pe_ref, pos0_ref, pos1_ref, *rest):
    h2_refs = rest[:len(tms)]
    xpad_hbm, zbuf, sem, zsem = rest[len(tms):]
    i = pl.program_id(0)

    @pl.when(i == 0)
    def _():
        zbuf[...] = jnp.zeros_like(zbuf)

        def block_copy(first_row):
            start = pl.multiple_of(first_row * ROW_TILE, MOE_BM * ROW_TILE)
            return pltpu.make_async_copy(zbuf, xpad_hbm.at[pl.ds(start, MOE_BM * ROW_TILE)], zsem)

        def tail_copy(e):
            return block_copy(pe_ref[e] - MOE_BM)

        def nonempty(e):
            return pe_ref[e] > (pe_ref[e - 1] if e > 0 else 0)

        for e in range(N_EXPERTS):
            pl.when(nonempty(e))(lambda e=e: tail_copy(e).start(priority=e % 2))
        for e in range(N_EXPERTS):
            pl.when(nonempty(e))(lambda e=e: tail_copy(e).wait())

        def fill(b, carry):
            cp = block_copy(b * MOE_BM)
            cp.start()
            cp.wait()
            return carry

        lax.fori_loop(pe_ref[N_EXPERTS - 1] // MOE_BM, xpad_hbm.shape[0] // (MOE_BM * ROW_TILE), fill, 0)

    def scatter_rows(h2_ref, tm):
        def body(r, carry):
            src = _token_tile(h2_ref, r)
            pltpu.make_async_copy(src, _token_tile(xpad_hbm, pos0_ref[r]), sem).start(priority=0)
            pltpu.make_async_copy(src, _token_tile(xpad_hbm, pos1_ref[r]), sem).start(priority=1)
            return carry

        lax.fori_loop(0, tm, body, 0, unroll=8)
        for _ in range(2):
            pltpu.make_async_copy(h2_ref, xpad_hbm.at[pl.ds(0, tm * ROW_TILE)], sem).wait()

    first = 0
    for h2_ref, tm, n in zip(h2_refs, tms, steps):
        pl.when(jnp.logical_and(i >= first, i < first + n))(functools.partial(scatter_rows, h2_ref, tm))
        first += n


def moe_dispatch(pad_ends, pos_groups, h2_groups, n_rows):
    sizes = [h2.shape[0] // ROW_TILE for h2 in h2_groups]
    tms = [min(T, DISPATCH_TM) for T in sizes]
    steps = [T // tm for T, tm in zip(sizes, tms)]
    firsts = [sum(steps[:g]) for g in range(len(steps))]
    pos0, pos1 = (jnp.concatenate([jnp.pad(p[k], (0, n * DISPATCH_TM - p[k].shape[0]))
                                   for p, n in zip(pos_groups, steps)]) for k in range(2))

    def rows(tm, first, n):
        return pl.BlockSpec((tm * ROW_TILE, LANES), lambda i, pe: (jnp.clip(i - first, 0, n - 1), 0))

    grid_spec = pltpu.PrefetchScalarGridSpec(
        num_scalar_prefetch=1,
        grid=(sum(steps),),
        in_specs=[pl.BlockSpec((DISPATCH_TM,), lambda i, pe: (i,), memory_space=pltpu.SMEM)] * 2
        + [rows(tm, first, n) for tm, first, n in zip(tms, firsts, steps)],
        out_specs=pl.BlockSpec(memory_space=pl.ANY),
        scratch_shapes=[pltpu.VMEM((MOE_BM * ROW_TILE, LANES), F32), pltpu.SemaphoreType.DMA(()),
                        pltpu.SemaphoreType.DMA(())],
    )
    return pl.pallas_call(
        functools.partial(_dispatch_kernel, tuple(tms), tuple(steps)),
        grid_spec=grid_spec,
        out_shape=jax.ShapeDtypeStruct((n_rows * ROW_TILE, LANES), F32),
        compiler_params=_cparams(("arbitrary",)),
        name="moe_dispatch",
    )(pad_ends, pos0, pos1, *h2_groups)


def _moe_kernel(be_ref, nu_ref, x_ref, wg_ref, wu_ref, wd_ref, y_ref, wg_s, wu_s, wd_s):
    i = pl.program_id(0)
    used = i < nu_ref[0]
    new_expert = jnp.logical_or(i == 0, be_ref[i] != be_ref[jnp.maximum(i - 1, 0)])

    @pl.when(jnp.logical_and(used, new_expert))
    def _():
        wg_s[...] = wg_ref[0, 0].astype(BF16)
        wu_s[...] = wu_ref[0, 0].astype(BF16)
        wd_s[...] = wd_ref[0, 0].astype(BF16)

    @pl.when(used)
    def _():
        x = _load_token_tiles(x_ref, 0, MOE_BM).astype(BF16)
        g = _dot(x, wg_s[...])
        u = _dot(x, wu_s[...])
        a = (g * _sigmoid(g) * u).astype(BF16)
        _store_token_tiles(y_ref, 0, _dot(a, wd_s[...]))

    @pl.when(jnp.logical_not(used))
    def _():
        y_ref[...] = jnp.zeros_like(y_ref)


def moe_experts(x_pad, block_e, n_used, layer, wg, wu, wd):
    n_rows = x_pad.shape[0] // ROW_TILE
    rows = pl.BlockSpec((MOE_BM * ROW_TILE, LANES), lambda i, be, nu: (i, 0))
    w_in = pl.BlockSpec((1, 1, D_MODEL, D_FF_EXPERT), lambda i, be, nu: (layer, be[i], 0, 0))
    w_dn = pl.BlockSpec((1, 1, D_FF_EXPERT, D_MODEL), lambda i, be, nu: (layer, be[i], 0, 0))
    grid_spec = pltpu.PrefetchScalarGridSpec(
        num_scalar_prefetch=2,
        grid=(n_rows // MOE_BM,),
        in_specs=[pl.BlockSpec((MOE_BM * ROW_TILE, LANES), lambda i, be, nu: (jnp.minimum(i, nu[0] - 1), 0)),
                  w_in, w_in, w_dn],
        out_specs=rows,
        scratch_shapes=[pltpu.VMEM((D_MODEL, D_FF_EXPERT), BF16), pltpu.VMEM((D_MODEL, D_FF_EXPERT), BF16),
                        pltpu.VMEM((D_FF_EXPERT, D_MODEL), BF16)],
    )
    return pl.pallas_call(
        _moe_kernel,
        grid_spec=grid_spec,
        out_shape=jax.ShapeDtypeStruct((n_rows * ROW_TILE, LANES), F32),
        compiler_params=_cparams(("arbitrary",)),
        name="moe_experts",
    )(block_e, n_used, x_pad, wg, wu, wd)


def _combine_kernel(tm, n_steps, pos0_ref, pos1_ref, pos0_next_ref, pos1_next_ref, route_ref, x1_ref, ypad_hbm,
                    o_ref, ybuf, sems):
    i = pl.program_id(0)
    slot = lax.rem(i, 2)

    def gather(p0_ref, p1_ref, s):
        buf = ybuf.at[s]

        def body(r, carry):
            pltpu.make_async_copy(_token_tile(ypad_hbm, p0_ref[r]), _token_tile(buf, r), sems.at[s]).start(
                priority=0)
            pltpu.make_async_copy(_token_tile(ypad_hbm, p1_ref[r]), _token_tile(buf, tm + r), sems.at[s]).start(
                priority=1)
            return carry

        lax.fori_loop(0, tm, body, 0, unroll=8)

    @pl.when(i == 0)
    def _():
        gather(pos0_ref, pos1_ref, 0)

    @pl.when(i + 1 < n_steps)
    def _():
        gather(pos0_next_ref, pos1_next_ref, 1 - slot)

    buf = ybuf.at[slot]
    pltpu.make_async_copy(ypad_hbm.at[pl.ds(0, 2 * tm * ROW_TILE)], buf, sems.at[slot]).wait()
    g1 = route_ref[:, 2:3]
    g2 = route_ref[:, 3:4]
    o_ref[...] = x1_ref[...] + (g1 * _load_token_tiles(buf, 0, tm) + g2 * _load_token_tiles(buf, tm, tm))


def moe_combine(pos, route, x1, y_pad):
    pos0, pos1 = pos
    T = x1.shape[0]
    tm = min(T, COMBINE_TM)
    n_steps = T // tm
    return pl.pallas_call(
        functools.partial(_combine_kernel, tm, n_steps),
        grid=(n_steps,),
        in_specs=[pl.BlockSpec((tm,), lambda i: (i,), memory_space=pltpu.SMEM)] * 2
        + [pl.BlockSpec((tm,), lambda i: (jnp.minimum(i + 1, n_steps - 1),), memory_space=pltpu.SMEM)] * 2
        + [pl.BlockSpec((tm, LANES), lambda i: (i, 0)), pl.BlockSpec((tm, D_MODEL), lambda i: (i, 0)),
           pl.BlockSpec(memory_space=pl.ANY)],
        out_specs=pl.BlockSpec((tm, D_MODEL), lambda i: (i, 0)),
        out_shape=jax.ShapeDtypeStruct((T, D_MODEL), F32),
        scratch_shapes=[pltpu.VMEM((2, 2 * tm * ROW_TILE, LANES), F32), pltpu.SemaphoreType.DMA((2,))],
        compiler_params=_cparams(("arbitrary",)),
        name="moe_combine",
    )(pos0, pos1, pos0, pos1, route, x1, y_pad)


def hier_moe(groups, counts, layer, wg, wu, wd):
    sizes = [g[0].shape[0] for g in groups]
    n_assign = 2 * sum(sizes)
    n_blocks = (n_assign + N_EXPERTS * (MOE_BM - 1) + MOE_BM - 1) // MOE_BM
    fields = jnp.concatenate([g[3] for g in groups], axis=1)
    experts = fields[0:2].astype(jnp.int32)
    ranks = fields[4:6].astype(jnp.int32)
    cnt = counts[0, :N_EXPERTS].astype(jnp.int32)
    padded = ((cnt + MOE_BM - 1) // MOE_BM) * MOE_BM
    pad_ends = jnp.cumsum(padded).astype(jnp.int32)
    pad_starts = pad_ends - padded
    is_e = experts[:, :, None] == jnp.arange(N_EXPERTS, dtype=jnp.int32)
    pos = ranks + jnp.sum(jnp.where(is_e, pad_starts, 0), axis=-1)
    block_start = jnp.arange(n_blocks, dtype=jnp.int32) * MOE_BM
    block_e = jnp.minimum(jnp.sum(block_start[:, None] >= pad_ends[None, :], axis=1), N_EXPERTS - 1).astype(jnp.int32)
    n_used = (pad_ends[-1:] // MOE_BM).astype(jnp.int32)
    starts = [sum(sizes[:g]) for g in range(len(groups))]
    pos_g = [(pos[0, s:s + n], pos[1, s:s + n]) for s, n in zip(starts, sizes)]
    x_pad = moe_dispatch(pad_ends, pos_g, [g[1] for g in groups], n_blocks * MOE_BM)
    y_pad = moe_experts(x_pad, block_e, n_used, layer, wg, wu, wd)
    return [moe_combine(p, g[2], g[0], y_pad) for g, p in zip(groups, pos_g)]


def _rope_tables(start, n):
    half = D_HEAD // 2
    inv = ROPE_BASE ** (-jnp.arange(half, dtype=F32) / half)

    def trig(pos):
        ang = pos.astype(F32)[:, None] * inv[None, :]
        return jnp.cos(ang), jnp.sin(ang)

    if n % ROPE_BLOCK == 0:
        cb, sb = trig(start + ROPE_BLOCK * jnp.arange(n // ROPE_BLOCK, dtype=jnp.int32))
        co, so = trig(jnp.arange(ROPE_BLOCK, dtype=jnp.int32))
        cos = (cb[:, None] * co[None] - sb[:, None] * so[None]).reshape(n, half)
        sin = (sb[:, None] * co[None] + cb[:, None] * so[None]).reshape(n, half)
    else:
        cos, sin = trig(start + jnp.arange(n, dtype=jnp.int32))
    return jnp.concatenate([cos, cos], axis=-1), jnp.concatenate([-sin, sin], axis=-1)


def _router_weights(w_group, b_group, w_router, b_router):
    w = jnp.zeros((D_MODEL, LANES), F32)
    w = w.at[:, :N_EXPERTS].set(w_router).at[:, ROUTER_GROUP_LANE0:ROUTER_GROUP_LANE0 + N_GROUPS].set(w_group)
    b = jnp.zeros((1, LANES), F32)
    b = b.at[0, :N_EXPERTS].set(b_router).at[0, ROUTER_GROUP_LANE0:ROUTER_GROUP_LANE0 + N_GROUPS].set(b_group)
    return w.astype(BF16), b


def _tile(v, n):
    return jnp.tile(v.astype(F32), n)


def _run_trunk(x, B, L, mem_k, mem_v, ret_state0, band_past, P):
    prompt = band_past is None
    out_tm, out_groups = (OUT_PROJ_TM, OUT_PROJ_TM // OUT_PROJ_GROUP) if prompt else (OUT_PROJ_SAMPLE_TM, 1)
    proj_tm = min(B * L, PROJ_TM)
    scale = D_HEAD ** -0.5
    ones = lambda n: jnp.ones((n * D_HEAD,), F32)
    if prompt:
        cos2, sin2 = _rope_tables(0, L)
        tab_map = lambda i: i % (L // proj_tm)
    else:
        cos2, sin2 = _rope_tables(PAST_LEN, L)
        cos2, sin2 = jnp.tile(cos2, (B, 1)), jnp.tile(sin2, (B, 1))
        tab_map = lambda i: i
    out = {}

    def ffn(layer, mixed, proj, qm_block, w_out, x):
        def run(counts0):
            return out_proj(
                mixed, proj, qm_block, mem_k, mem_v, layer, w_out.astype(BF16), x, P["norm_ffn"][layer],
                *_router_weights(P["w_group"][layer], P["b_group"][layer], P["w_router"][layer],
                                 P["b_router"][layer]), counts0, tm=out_tm, n_groups=out_groups, rows_per_stream=L)
        return (yield run)

    segs0 = (("rope", N_HEADS), ("rope", N_HEADS), ("plain", N_HEADS), ("silu", N_HEADS), ("norm", H_MEM))
    gain0 = jnp.concatenate([ones(N_HEADS), ones(N_HEADS) * scale, ones(2 * N_HEADS),
                             _tile(P["q_norm_mem"][0], H_MEM) * scale])
    proj = norm_proj(x, P["norm_mix"][0], P["w_in_ret"][0].astype(BF16), gain0, segs0, tm=proj_tm,
                     out_dtype=BF16, rope=(cos2, sin2, tab_map))
    mixed, S = retention(proj, ret_state0, P["gn_ret"][0], B=B, L=L, C=RET_CHUNK if prompt else L)
    out["ret_state"] = S
    x = yield from ffn(0, mixed, proj, (4 * MIX_W) // MEM_W, P["w_out_ret"][0], x)

    segs1 = (("norm", N_HEADS), ("norm", N_HEADS), ("plain", N_HEADS), ("norm", H_MEM))
    gain1 = jnp.concatenate([_tile(P["q_norm_att"][0], N_HEADS) * scale, _tile(P["k_norm_att"][0], N_HEADS),
                             ones(N_HEADS), _tile(P["q_norm_mem"][1], H_MEM) * scale])
    w_in = P["w_in_att"][0].astype(BF16)
    if prompt:
        proj = norm_proj(x, P["norm_mix"][1], w_in, gain1, segs1, tm=proj_tm, out_dtype=BF16)
        keep = min(BAND_PAST, L)
        tiles = L // keep
        kv = norm_proj(x, P["norm_mix"][1], w_in[:, MIX_W:3 * MIX_W], gain1[MIX_W:3 * MIX_W], segs1[1:3],
                       tm=keep, out_dtype=F32, n_tiles=B, row_map=lambda i: i * tiles + tiles - 1)
        out["band_k"] = kv[:, :MIX_W].reshape(B, keep, N_HEADS, D_HEAD)
        out["band_v"] = kv[:, MIX_W:].reshape(B, keep, N_HEADS, D_HEAD)
        mixed = band_prompt(proj, P["rel_bias_att"][0], B=B, L=L)
    else:
        proj_f = norm_proj(x, P["norm_mix"][1], w_in, gain1, segs1, tm=proj_tm, out_dtype=F32)
        out["band_k"] = proj_f[:, MIX_W:2 * MIX_W].reshape(B, L, N_HEADS, D_HEAD)
        out["band_v"] = proj_f[:, 2 * MIX_W:3 * MIX_W].reshape(B, L, N_HEADS, D_HEAD)
        proj = proj_f.astype(BF16)
        mixed = band_sample(proj, band_past[0], band_past[1], P["rel_bias_att"][0], B=B, L=L)
    out["y"] = yield from ffn(1, mixed, proj, (3 * MIX_W) // MEM_W, P["w_out_att"][0], x)
    return out


def kernel(x_prompt, x_sample, mem_prompt, state_ret, cache_band_k, cache_band_v, cache_mem_k, cache_mem_v, norm_mix, norm_ffn, norm_mem, w_in_ret, gn_ret, w_out_ret, w_in_att, q_norm_att, k_norm_att, rel_bias_att, w_out_att, w_mem_kv, q_norm_mem, k_norm_mem, w_group, b_group, w_router, b_router, w_e_gate, w_e_up, w_e_down):
    P = {"norm_mix": norm_mix, "norm_ffn": norm_ffn, "w_in_ret": w_in_ret, "gn_ret": gn_ret,
         "w_out_ret": w_out_ret, "w_in_att": w_in_att, "q_norm_att": q_norm_att,
         "k_norm_att": k_norm_att, "rel_bias_att": rel_bias_att, "w_out_att": w_out_att,
         "q_norm_mem": q_norm_mem, "w_group": w_group, "b_group": b_group, "w_router": w_router,
         "b_router": b_router, "w_e_gate": w_e_gate, "w_e_up": w_e_up, "w_e_down": w_e_down}
    B, L, D = x_prompt.shape
    Bd, Ld, _ = x_sample.shape
    depth = norm_mix.shape[0]

    mem_k_p, mem_v_p = [], []
    for i in range(depth):
        gain = jnp.concatenate([_tile(k_norm_mem[i], H_MEM), jnp.ones((MEM_W,), F32)])
        kv = norm_proj(mem_prompt.reshape(B * N_MEM, D), norm_mem[i], w_mem_kv[i].astype(BF16), gain,
                       (("norm", H_MEM), ("plain", H_MEM)), tm=N_MEM, out_dtype=F32)
        mem_k_p.append(kv[:, :MEM_W].reshape(B, N_MEM, H_MEM, D_HEAD))
        mem_v_p.append(kv[:, MEM_W:].reshape(B, N_MEM, H_MEM, D_HEAD))
    mem_k_p = jnp.stack(mem_k_p)
    mem_v_p = jnp.stack(mem_v_p)

    zeros_state = jnp.zeros((B, N_HEADS, D_HEAD, D_HEAD), F32)
    trunks = [_run_trunk(x_prompt.reshape(B * L, D), B, L, mem_k_p, mem_v_p, zeros_state, None, P),
              _run_trunk(x_sample.reshape(Bd * Ld, D), Bd, Ld, cache_mem_k, cache_mem_v, state_ret[0],
                         (cache_band_k[0], cache_band_v[0]), P)]
    pending = [next(t) for t in trunks]
    results = []
    for layer in range(depth):
        counts = jnp.zeros((SUBLANES, LANES), F32)
        groups = []
        for run in pending:
            x1, h2, route, route_t, counts = run(counts)
            groups.append((x1, h2, route, route_t))
        outs = hier_moe(groups, counts, layer, w_e_gate, w_e_up, w_e_down)
        pending = []
        for t, x2 in zip(trunks, outs):
            try:
                pending.append(t.send(x2))
            except StopIteration as done:
                results.append(done.value)
    res_p, res_s = results

    return (res_p["y"].reshape(B, L, D), res_s["y"].reshape(Bd, Ld, D),
            res_p["ret_state"][None], res_s["ret_state"][None],
            res_p["band_k"][None], res_p["band_v"][None], res_s["band_k"][None], res_s["band_v"][None],
            mem_k_p, mem_v_p)
```

```python
import functools

import jax
import jax.numpy as jnp
from jax import lax
from jax.experimental import pallas as pl
from jax.experimental.pallas import tpu as pltpu

F32 = jnp.float32
BF16 = jnp.bfloat16

D_MODEL = 1024
D_HEAD = 128
CHUNK = 64
N_HEADS = 8
H_MEM = 4
N_MEM = 256
MEM_W = H_MEM * D_HEAD
MIX_W = N_HEADS * D_HEAD
N_PREV_CHUNKS = 8
BAND_PAST = N_PREV_CHUNKS * CHUNK
REL_CLIP = 128
N_GROUPS = 4
EXPERTS_PER_GROUP = 8
N_EXPERTS = N_GROUPS * EXPERTS_PER_GROUP
D_FF_EXPERT = 512
ROPE_BASE = 10000.0
EPS = 1e-6
NEG_INF = -1e30
PAST_LEN = 1024

LANES = 128
SUBLANES = 8

ROPE_BLOCK = 128
RET_CHUNK = 256
BAND_TQ = 512
BAND_SUB = 128
BAND_WIN = BAND_PAST + BAND_SUB
BAND_SAMPLE_STREAMS = 4
MOE_BM = 512
DISPATCH_TM = 4096
COMBINE_TM = 1024
PROJ_TM = 1024
OUT_PROJ_TM = 1024
OUT_PROJ_GROUP = 512
OUT_PROJ_SAMPLE_TM = 128
ROUTE_ROWS = 128
ROUTER_GROUP_LANE0 = N_EXPERTS
V7X_VMEM_BYTES = 64 * 1024 * 1024
VMEM_LIMIT = V7X_VMEM_BYTES - 8 * 1024 * 1024


def _cparams(sem):
    return pltpu.CompilerParams(dimension_semantics=sem, vmem_limit_bytes=VMEM_LIMIT)


def _dot(a, b):
    return jnp.dot(a, b, preferred_element_type=F32)


def _dot_nt(a, b):
    return lax.dot_general(a, b, (((1,), (1,)), ((), ())), preferred_element_type=F32)


def _sigmoid(x):
    return 1.0 / (1.0 + jnp.exp(-x))


ROW_TILE = D_MODEL // LANES
assert ROW_TILE == SUBLANES


def _load_token_tiles(ref, t0, n):
    return jnp.concatenate([ref[pl.ds(t0 * ROW_TILE + j, n, stride=ROW_TILE), :] for j in range(ROW_TILE)], axis=1)


def _store_token_tiles(ref, t0, val):
    n = val.shape[0]
    for j in range(ROW_TILE):
        ref[pl.ds(t0 * ROW_TILE + j, n, stride=ROW_TILE), :] = val[:, j * LANES:(j + 1) * LANES]


def _token_tile(ref, t):
    return ref.at[pl.ds(pl.multiple_of(t * ROW_TILE, ROW_TILE), ROW_TILE)]


def _norm_proj_kernel(segs, has_rope, x_ref, g_ref, w_ref, gain_ref, *rest):
    if has_rope:
        cos_ref, sin_ref, o_ref = rest
    else:
        (o_ref,) = rest
    x = x_ref[...]
    ms = jnp.mean(x * x, axis=-1, keepdims=True)
    h = (x * lax.rsqrt(ms + EPS) * g_ref[...]).astype(BF16)
    col = 0
    for kind, n_heads in segs:
        width = n_heads * D_HEAD
        acc = _dot(h, w_ref[:, col:col + width])
        for j in range(n_heads):
            a = acc[:, j * D_HEAD:(j + 1) * D_HEAD]
            c0 = col + j * D_HEAD
            gain = gain_ref[:, c0:c0 + D_HEAD]
            if kind == "rope":
                rot = pltpu.roll(a, D_HEAD // 2, 1)
                out = (a * cos_ref[...] + rot * sin_ref[...]) * gain
            elif kind == "norm":
                out = a * lax.rsqrt(jnp.mean(a * a, axis=-1, keepdims=True) + EPS) * gain
            elif kind == "silu":
                out = a * _sigmoid(a)
            else:
                out = a
            o_ref[:, c0:c0 + D_HEAD] = out.astype(o_ref.dtype)
        col += width


def norm_proj(x, g, w, gain, segs, *, tm, out_dtype, n_tiles=None, row_map=None, rope=None):
    T, D = x.shape
    N = w.shape[1]
    assert sum(n for _, n in segs) * D_HEAD == N
    if n_tiles is None:
        n_tiles = T // tm
    if row_map is None:
        row_map = lambda i: i
    in_specs = [
        pl.BlockSpec((tm, D), lambda i: (row_map(i), 0)),
        pl.BlockSpec((1, D), lambda i: (0, 0)),
        pl.BlockSpec((D, N), lambda i: (0, 0)),
        pl.BlockSpec((1, N), lambda i: (0, 0)),
    ]
    args = [x, g.reshape(1, D), w, gain.reshape(1, N)]
    if rope is not None:
        cos2, sin2, tab_map = rope
        in_specs += [pl.BlockSpec((tm, D_HEAD), lambda i: (tab_map(i), 0)),
                     pl.BlockSpec((tm, D_HEAD), lambda i: (tab_map(i), 0))]
        args += [cos2, sin2]
    return pl.pallas_call(
        functools.partial(_norm_proj_kernel, segs, rope is not None),
        grid=(n_tiles,),
        in_specs=in_specs,
        out_specs=pl.BlockSpec((tm, N), lambda i: (i, 0)),
        out_shape=jax.ShapeDtypeStruct((n_tiles * tm, N), out_dtype),
        compiler_params=_cparams(("arbitrary",)),
        name="norm_proj",
    )(*args)


def _retention_kernel(q_ref, k_ref, v_ref, g_ref, s0_ref, dmask_ref, qdec_ref, kdec_ref, gc_ref,
                      gn_ref, o_ref, s_ref):
    @pl.when(pl.program_id(1) == 0)
    def _():
        s_ref[...] = s0_ref[...]

    for h in range(N_HEADS):
        sl = slice(h * D_HEAD, (h + 1) * D_HEAD)
        q = q_ref[:, sl]
        k = k_ref[:, sl]
        v = v_ref[:, sl]
        S = s_ref[0, h]
        scores = _dot_nt(q, k) * dmask_ref[h]
        intra = _dot(scores.astype(BF16), v)
        cross = _dot(q, S.astype(BF16)) * qdec_ref[:, sl]
        o = intra + cross
        kd = (k.astype(F32) * kdec_ref[:, sl]).astype(BF16)
        s_ref[0, h] = gc_ref[h] * S + lax.dot_general(kd, v, (((0,), (0,)), ((), ())), preferred_element_type=F32)
        mu = jnp.mean(o, axis=-1, keepdims=True)
        d = o - mu
        var = jnp.mean(d * d, axis=-1, keepdims=True)
        y = d * lax.rsqrt(var + EPS) * gn_ref[:, sl] * g_ref[:, sl].astype(F32)
        o_ref[:, sl] = y.astype(o_ref.dtype)


def _retention_tables(C):
    log_gamma = jnp.log1p(-jnp.exp2(-5.0 - jnp.arange(N_HEADS, dtype=F32)))
    idx = jnp.arange(C, dtype=F32)
    diff = idx[:, None] - idx[None, :]
    dmask = jnp.where(diff[None] >= 0,
                      jnp.exp(log_gamma[:, None, None] * jnp.maximum(diff, 0.0)[None]), 0.0)
    q_dec = jnp.exp(log_gamma[None, :] * (idx + 1.0)[:, None])
    k_dec = jnp.exp(log_gamma[None, :] * (C - 1.0 - idx)[:, None])
    g_c = jnp.exp(log_gamma * C)
    rep = lambda t: jnp.repeat(t, D_HEAD, axis=-1)
    return dmask, rep(q_dec), rep(k_dec), rep(g_c[:, None])[:, None, :]


def retention(proj, s0, gn, *, B, L, C):
    nc = L // C
    dmask, qdec, kdec, gc = _retention_tables(C)
    row = lambda col: pl.BlockSpec((C, MIX_W), lambda b, c: (b * nc + c, col))
    const = lambda shape: pl.BlockSpec(shape, lambda b, c: (0,) * len(shape))
    state_spec = pl.BlockSpec((1, N_HEADS, D_HEAD, D_HEAD), lambda b, c: (b, 0, 0, 0))
    return pl.pallas_call(
        _retention_kernel,
        grid=(B, nc),
        in_specs=[row(0), row(1), row(2), row(3), state_spec,
                  const((N_HEADS, C, C)), const((C, MIX_W)), const((C, MIX_W)),
                  const((N_HEADS, 1, D_HEAD)), const((1, MIX_W))],
        out_specs=[pl.BlockSpec((C, MIX_W), lambda b, c: (b * nc + c, 0)), state_spec],
        out_shape=[jax.ShapeDtypeStruct((B * L, MIX_W), BF16),
                   jax.ShapeDtypeStruct((B, N_HEADS, D_HEAD, D_HEAD), F32)],
        compiler_params=_cparams(("arbitrary", "arbitrary")),
        name="retention",
    )(proj, proj, proj, proj, s0, dmask, qdec, kdec, gc, gn.reshape(1, MIX_W))


def _attend_two_piece(q, k_a, k_b, v_a, v_b, bias_a, bias_b):
    s_a = _dot_nt(q, k_a) + bias_a
    s_b = _dot_nt(q, k_b) + bias_b
    m = jnp.maximum(jnp.max(s_a, axis=-1, keepdims=True), jnp.max(s_b, axis=-1, keepdims=True))
    p_a = jnp.exp(s_a - m)
    p_b = jnp.exp(s_b - m)
    denom = jnp.sum(p_a, axis=-1, keepdims=True) + jnp.sum(p_b, axis=-1, keepdims=True)
    o = _dot(p_a.astype(BF16), v_a) + _dot(p_b.astype(BF16), v_b)
    return o / denom


def _band_prompt_kernel(q_ref, kp_ref, kc_ref, vp_ref, vc_ref, bias_ref, o_ref, s_buf, p_buf):
    @pl.when((pl.program_id(0) == 0) & (pl.program_id(1) == 0))
    def _():
        p_buf[...] = jnp.zeros_like(p_buf)

    prev_off = jnp.where(pl.program_id(1) > 0, 0.0, NEG_INF).astype(F32)
    for h in range(N_HEADS):
        sl = slice(h * D_HEAD, (h + 1) * D_HEAD)
        q = q_ref[:, sl]
        s_buf[:, :BAND_TQ] = _dot_nt(q, kp_ref[:, sl]) + prev_off
        s_buf[:, BAND_TQ:] = _dot_nt(q, kc_ref[:, sl])
        denoms = []
        for s in range(BAND_TQ // BAND_SUB):
            r0 = s * BAND_SUB
            sc = s_buf[r0:r0 + BAND_SUB, r0:r0 + BAND_WIN] + bias_ref[h]
            p = jnp.exp(sc - jnp.max(sc, axis=-1, keepdims=True))
            denoms.append(jnp.sum(p, axis=-1, keepdims=True))
            p_buf[r0:r0 + BAND_SUB, r0:r0 + BAND_WIN] = p.astype(BF16)
        o = _dot(p_buf[:, :BAND_TQ], vp_ref[:, sl]) + _dot(p_buf[:, BAND_TQ:], vc_ref[:, sl])
        o_ref[:, sl] = (o / jnp.concatenate(denoms, axis=0)).astype(o_ref.dtype)


def _rel_bias(bias_table, offset, n_rows, width):
    n = n_rows + width - 1
    v = jnp.arange(n, dtype=jnp.int32)
    v = jnp.where(v < width, v, v - n)
    diag = bias_table.astype(F32)[:, jnp.clip(offset - v, -REL_CLIP, REL_CLIP) + REL_CLIP]
    cut = jnp.tile(diag, (1, n_rows))[:, :n_rows * (n - 1)].reshape(diag.shape[0], n_rows, n - 1)
    return cut[:, :, :width]


def _band_prompt_bias(bias_table):
    r = jnp.arange(BAND_SUB, dtype=jnp.int32)[:, None]
    j = jnp.arange(BAND_WIN, dtype=jnp.int32)[None, :]
    q_chunk = r // CHUNK + N_PREV_CHUNKS
    k_chunk = j // CHUNK
    allowed = (k_chunk <= q_chunk) & (k_chunk >= q_chunk - N_PREV_CHUNKS)
    return jnp.where(allowed[None], _rel_bias(bias_table, BAND_PAST, BAND_SUB, BAND_WIN), NEG_INF)


def band_prompt(proj, bias_table, *, B, L):
    nq = L // BAND_TQ
    cur = lambda col: pl.BlockSpec((BAND_TQ, MIX_W), lambda b, i: (b * nq + i, col))
    prev = lambda col: pl.BlockSpec((BAND_TQ, MIX_W), lambda b, i: (b * nq + jnp.maximum(i - 1, 0), col))
    return pl.pallas_call(
        _band_prompt_kernel,
        grid=(B, nq),
        in_specs=[cur(0), prev(1), cur(1), prev(2), cur(2),
                  pl.BlockSpec((N_HEADS, BAND_SUB, BAND_WIN), lambda b, i: (0, 0, 0))],
        out_specs=pl.BlockSpec((BAND_TQ, MIX_W), lambda b, i: (b * nq + i, 0)),
        out_shape=jax.ShapeDtypeStruct((B * L, MIX_W), BF16),
        scratch_shapes=[pltpu.VMEM((BAND_TQ, 2 * BAND_TQ), F32), pltpu.VMEM((BAND_TQ, 2 * BAND_TQ), BF16)],
        compiler_params=_cparams(("arbitrary", "arbitrary")),
        name="band_prompt",
    )(proj, proj, proj, proj, proj, _band_prompt_bias(bias_table))


def _band_sample_kernel(q_ref, kp_ref, kc_ref, vp_ref, vc_ref, bias_p_ref, bias_c_ref, o_ref):
    n_streams = kp_ref.shape[0]
    n_past = kp_ref.shape[1] // N_HEADS
    L = q_ref.shape[0] // n_streams
    for c in range(n_streams):
        rows = slice(c * L, (c + 1) * L)
        for h in range(N_HEADS):
            sl = slice(h * D_HEAD, (h + 1) * D_HEAD)
            kp = kp_ref.at[c][pl.ds(h, n_past, stride=N_HEADS), :].astype(BF16)
            vp = vp_ref.at[c][pl.ds(h, n_past, stride=N_HEADS), :].astype(BF16)
            o = _attend_two_piece(q_ref[rows, sl], kp, kc_ref[rows, sl], vp, vc_ref[rows, sl], bias_p_ref[h],
                                  bias_c_ref[h])
            o_ref[rows, sl] = o.astype(o_ref.dtype)


def band_sample(proj, past_k, past_v, bias_table, *, B, L):
    P = past_k.shape[1]
    qpos = PAST_LEN + jnp.arange(L, dtype=jnp.int32)
    kpos = jnp.concatenate([PAST_LEN - P + jnp.arange(P, dtype=jnp.int32), qpos])
    qc = (qpos // CHUNK)[:, None]
    kc = (kpos // CHUNK)[None, :]
    allowed = (kpos[None, :] >= 0) & (kc <= qc) & (kc >= qc - N_PREV_CHUNKS)
    rel_bias = jnp.concatenate([_rel_bias(bias_table, P, L, P), _rel_bias(bias_table, 0, L, L)], axis=-1)
    bias = jnp.where(allowed[None], rel_bias, NEG_INF)
    S = BAND_SAMPLE_STREAMS
    cur = lambda col: pl.BlockSpec((S * L, MIX_W), lambda b: (b, col))
    past = pl.BlockSpec((S, P * N_HEADS, D_HEAD), lambda b: (b, 0, 0))
    past_k = past_k.reshape(B, P * N_HEADS, D_HEAD)
    past_v = past_v.reshape(B, P * N_HEADS, D_HEAD)
    return pl.pallas_call(
        _band_sample_kernel,
        grid=(B // S,),
        in_specs=[cur(0), past, cur(1), past, cur(2),
                  pl.BlockSpec((N_HEADS, L, P), lambda b: (0, 0, 0)),
                  pl.BlockSpec((N_HEADS, L, L), lambda b: (0, 0, 0))],
        out_specs=pl.BlockSpec((S * L, MIX_W), lambda b: (b, 0)),
        out_shape=jax.ShapeDtypeStruct((B * L, MIX_W), BF16),
        compiler_params=_cparams(("arbitrary",)),
        name="band_sample",
    )(proj, past_k, proj, past_v, proj, bias[:, :, :P], bias[:, :, P:])


def _out_proj_kernel(n_streams, n_groups, o_ref, qm_ref, mk_ref, mv_ref, w_ref, x_ref, gf_ref, wr_ref, br_ref,
                     tri_ref, cnt0_ref, x1_ref, h2_ref, route_ref, route_t_ref, cnt_ref):
    @pl.when(pl.program_id(0) == 0)
    def _():
        cnt_ref[...] = cnt0_ref[...]

    tm = x_ref.shape[0]
    sr = tm // n_streams
    gr = tm // n_groups
    om_heads = []
    for h in range(H_MEM):
        sl = slice(h * D_HEAD, (h + 1) * D_HEAD)
        chunks = []
        for c in range(n_streams):
            mk = mk_ref.at[0, c][pl.ds(h, N_MEM, stride=H_MEM), :].astype(BF16)
            mv = mv_ref.at[0, c][pl.ds(h, N_MEM, stride=H_MEM), :].astype(BF16)
            s = _dot_nt(qm_ref[c * sr:(c + 1) * sr, sl], mk)
            p = jnp.exp(s - jnp.max(s, axis=-1, keepdims=True))
            om = _dot(p.astype(BF16), mv) / jnp.sum(p, axis=-1, keepdims=True)
            chunks.append(om.astype(BF16))
        om_heads.append(chunks[0] if n_streams == 1 else jnp.concatenate(chunks, axis=0))

    rows = min(gr, ROUTE_ROWS)
    lane = lax.broadcasted_iota(jnp.int32, (rows, LANES), 1)
    for g in range(n_groups):
        g0 = g * gr
        feat = jnp.concatenate([o_ref[g0:g0 + gr, :]] + [om[g0:g0 + gr] for om in om_heads], axis=1)
        x1 = x_ref[g0:g0 + gr, :] + _dot(feat, w_ref[...])
        x1_ref[g0:g0 + gr, :] = x1
        hn = x1 * lax.rsqrt(jnp.mean(x1 * x1, axis=-1, keepdims=True) + EPS) * gf_ref[...]
        _store_token_tiles(h2_ref, g0, hn)

        logits = _dot(hn.astype(BF16), wr_ref[...]) + br_ref[...]
        tops = [_route_top2(logits[r0:r0 + rows]) for r0 in range(0, gr, rows)]

        onehot = jnp.concatenate([jnp.where(t[4] | t[5], 1.0, 0.0) for t in tops], axis=0)
        before = cnt_ref[0:1, :] + _dot(tri_ref[...], onehot.astype(BF16))
        cnt_ref[0:1, :] = cnt_ref[0:1, :] + jnp.sum(onehot, axis=0, keepdims=True)
        for k, (i1, i2, w1, w2, oh1, oh2) in enumerate(tops):
            b = before[k * rows:(k + 1) * rows]
            r1 = jnp.sum(jnp.where(oh1, b, 0.0), axis=-1, keepdims=True)
            r2 = jnp.sum(jnp.where(oh2, b, 0.0), axis=-1, keepdims=True)
            route = jnp.where(lane == 0, i1, jnp.where(lane == 1, i2, jnp.where(lane == 2, w1, jnp.where(
                lane == 3, w2, jnp.where(lane == 4, r1, jnp.where(lane == 5, r2, 0.0))))))
            c0 = g0 + k * rows
            route_ref[c0:c0 + rows, :] = route
            route_t_ref[:, c0:c0 + rows] = route.T[:SUBLANES, :]


def _route_top2(logits):
    lane = lax.broadcasted_iota(jnp.int32, logits.shape, 1)
    lane_f = lane.astype(F32)
    far = float(LANES)
    is_group = (lane >= ROUTER_GROUP_LANE0) & (lane < ROUTER_GROUP_LANE0 + N_GROUPS)
    gl = jnp.where(is_group, logits, NEG_INF)
    g_max = jnp.max(gl, axis=-1, keepdims=True)
    g_sel = jnp.min(jnp.where(gl == g_max, lane_f, far), axis=-1, keepdims=True) - float(ROUTER_GROUP_LANE0)
    g_gate = 1.0 / jnp.sum(jnp.exp(gl - g_max), axis=-1, keepdims=True)
    in_group = jnp.right_shift(lane, EXPERTS_PER_GROUP.bit_length() - 1).astype(F32) == g_sel
    el = jnp.where(in_group, logits, NEG_INF)
    m1 = jnp.max(el, axis=-1, keepdims=True)
    i1 = jnp.min(jnp.where(el == m1, lane_f, far), axis=-1, keepdims=True)
    el2 = jnp.where(lane_f == i1, NEG_INF, el)
    m2 = jnp.max(el2, axis=-1, keepdims=True)
    i2 = jnp.min(jnp.where(el2 == m2, lane_f, far), axis=-1, keepdims=True)
    e2 = jnp.exp(m2 - m1)
    p1 = 1.0 / (1.0 + e2)
    return i1, i2, g_gate * p1, g_gate * (e2 * p1), lane_f == i1, lane_f == i2


def out_proj(mixed, proj, qm_block, mem_k, mem_v, layer, w_out, x, g_ffn, w_router, b_router, counts0, *, tm,
             n_groups, rows_per_stream):
    T = x.shape[0]
    n_streams = max(1, tm // rows_per_stream)
    tiles_per_stream = max(1, rows_per_stream // tm)
    gr = tm // n_groups
    assert min(gr, ROUTE_ROWS) % LANES == 0
    row = lambda width, col: pl.BlockSpec((tm, width), lambda i: (i, col))
    const = lambda shape: pl.BlockSpec(shape, lambda i: (0,) * len(shape))
    mem = pl.BlockSpec((1, n_streams, N_MEM * H_MEM, D_HEAD), lambda i: (layer, i // tiles_per_stream, 0, 0))
    mem_k = mem_k.reshape(mem_k.shape[:2] + (N_MEM * H_MEM, D_HEAD))
    mem_v = mem_v.reshape(mem_v.shape[:2] + (N_MEM * H_MEM, D_HEAD))
    tri = (jnp.arange(gr)[None, :] < jnp.arange(gr)[:, None]).astype(BF16)
    return pl.pallas_call(
        functools.partial(_out_proj_kernel, n_streams, n_groups),
        grid=(T // tm,),
        in_specs=[row(MIX_W, 0), row(MEM_W, qm_block), mem, mem, const((MIX_W + MEM_W, D_MODEL)),
                  row(D_MODEL, 0), const((1, D_MODEL)), const((D_MODEL, LANES)), const((1, LANES)),
                  const((gr, gr)), const((SUBLANES, LANES))],
        out_specs=[row(D_MODEL, 0), pl.BlockSpec((tm * ROW_TILE, LANES), lambda i: (i, 0)), row(LANES, 0),
                   pl.BlockSpec((SUBLANES, tm), lambda i: (0, i)), const((SUBLANES, LANES))],
        out_shape=[jax.ShapeDtypeStruct((T, D_MODEL), F32), jax.ShapeDtypeStruct((T * ROW_TILE, LANES), F32),
                   jax.ShapeDtypeStruct((T, LANES), F32), jax.ShapeDtypeStruct((SUBLANES, T), F32),
                   jax.ShapeDtypeStruct((SUBLANES, LANES), F32)],
        compiler_params=_cparams(("arbitrary",)),
        name="out_proj",
    )(mixed, proj, mem_k, mem_v, w_out, x, g_ffn.reshape(1, D_MODEL), w_router, b_router, tri, counts0)


def _dispatch_kernel(tms, steps, pe_ref, pos0_ref, pos1_ref, *rest):
    h2_refs = rest[:len(tms)]
    xpad_hbm, zbuf, sem, zsem = rest[len(tms):]
    i = pl.program_id(0)

    @pl.when(i == 0)
    def _():
        zbuf[...] = jnp.zeros_like(zbuf)

        def block_copy(first_row):
            start = pl.multiple_of(first_row * ROW_TILE, MOE_BM * ROW_TILE)
            return pltpu.make_async_copy(zbuf, xpad_hbm.at[pl.ds(start, MOE_BM * ROW_TILE)], zsem)

        def tail_copy(e):
            return block_copy(pe_ref[e] - MOE_BM)

        def nonempty(e):
            return pe_ref[e] > (pe_ref[e - 1] if e > 0 else 0)

        for e in range(N_EXPERTS):
            pl.when(nonempty(e))(lambda e=e: tail_copy(e).start(priority=e % 2))
        for e in range(N_EXPERTS):
            pl.when(nonempty(e))(lambda e=e: tail_copy(e).wait())

        def fill(b, carry):
            cp = block_copy(b * MOE_BM)
            cp.start()
            cp.wait()
            return carry

        lax.fori_loop(pe_ref[N_EXPERTS - 1] // MOE_BM, xpad_hbm.shape[0] // (MOE_BM * ROW_TILE), fill, 0)

    def scatter_rows(h2_ref, tm):
        def body(r, carry):
            src = _token_tile(h2_ref, r)
            pltpu.make_async_copy(src, _token_tile(xpad_hbm, pos0_ref[r]), sem).start(priority=0)
            pltpu.make_async_copy(src, _token_tile(xpad_hbm, pos1_ref[r]), sem).start(priority=1)
            return carry

        lax.fori_loop(0, tm, body, 0, unroll=8)
        for _ in range(2):
            pltpu.make_async_copy(h2_ref, xpad_hbm.at[pl.ds(0, tm * ROW_TILE)], sem).wait()

    first = 0
    for h2_ref, tm, n in zip(h2_refs, tms, steps):
        pl.when(jnp.logical_and(i >= first, i < first + n))(functools.partial(scatter_rows, h2_ref, tm))
        first += n


def moe_dispatch(pad_ends, pos_groups, h2_groups, n_rows):
    sizes = [h2.shape[0] // ROW_TILE for h2 in h2_groups]
    tms = [min(T, DISPATCH_TM) for T in sizes]
    steps = [T // tm for T, tm in zip(sizes, tms)]
    firsts = [sum(steps[:g]) for g in range(len(steps))]
    pos0, pos1 = (jnp.concatenate([jnp.pad(p[k], (0, n * DISPATCH_TM - p[k].shape[0]))
                                   for p, n in zip(pos_groups, steps)]) for k in range(2))

    def rows(tm, first, n):
        return pl.BlockSpec((tm * ROW_TILE, LANES), lambda i, pe: (jnp.clip(i - first, 0, n - 1), 0))

    grid_spec = pltpu.PrefetchScalarGridSpec(
        num_scalar_prefetch=1,
        grid=(sum(steps),),
        in_specs=[pl.BlockSpec((DISPATCH_TM,), lambda i, pe: (i,), memory_space=pltpu.SMEM)] * 2
        + [rows(tm, first, n) for tm, first, n in zip(tms, firsts, steps)],
        out_specs=pl.BlockSpec(memory_space=pl.ANY),
        scratch_shapes=[pltpu.VMEM((MOE_BM * ROW_TILE, LANES), F32), pltpu.SemaphoreType.DMA(()),
                        pltpu.SemaphoreType.DMA(())],
    )
    return pl.pallas_call(
        functools.partial(_dispatch_kernel, tuple(tms), tuple(steps)),
        grid_spec=grid_spec,
        out_shape=jax.ShapeDtypeStruct((n_rows * ROW_TILE, LANES), F32),
        compiler_params=_cparams(("arbitrary",)),
        name="moe_dispatch",
    )(pad_ends, pos0, pos1, *h2_groups)


def _moe_kernel(be_ref, nu_ref, x_ref, wg_ref, wu_ref, wd_ref, y_ref, wg_s, wu_s, wd_s):
    i = pl.program_id(0)
    used = i < nu_ref[0]
    new_expert = jnp.logical_or(i == 0, be_ref[i] != be_ref[jnp.maximum(i - 1, 0)])

    @pl.when(jnp.logical_and(used, new_expert))
    def _():
        wg_s[...] = wg_ref[0, 0].astype(BF16)
        wu_s[...] = wu_ref[0, 0].astype(BF16)
        wd_s[...] = wd_ref[0, 0].astype(BF16)

    @pl.when(used)
    def _():
        x = _load_token_tiles(x_ref, 0, MOE_BM).astype(BF16)
        g = _dot(x, wg_s[...])
        u = _dot(x, wu_s[...])
        a = (g * _sigmoid(g) * u).astype(BF16)
        _store_token_tiles(y_ref, 0, _dot(a, wd_s[...]))

    @pl.when(jnp.logical_not(used))
    def _():
        y_ref[...] = jnp.zeros_like(y_ref)


def moe_experts(x_pad, block_e, n_used, layer, wg, wu, wd):
    n_rows = x_pad.shape[0] // ROW_TILE
    rows = pl.BlockSpec((MOE_BM * ROW_TILE, LANES), lambda i, be, nu: (i, 0))
    w_in = pl.BlockSpec((1, 1, D_MODEL, D_FF_EXPERT), lambda i, be, nu: (layer, be[i], 0, 0))
    w_dn = pl.BlockSpec((1, 1, D_FF_EXPERT, D_MODEL), lambda i, be, nu: (layer, be[i], 0, 0))
    grid_spec = pltpu.PrefetchScalarGridSpec(
        num_scalar_prefetch=2,
        grid=(n_rows // MOE_BM,),
        in_specs=[pl.BlockSpec((MOE_BM * ROW_TILE, LANES), lambda i, be, nu: (jnp.minimum(i, nu[0] - 1), 0)),
                  w_in, w_in, w_dn],
        out_specs=rows,
        scratch_shapes=[pltpu.VMEM((D_MODEL, D_FF_EXPERT), BF16), pltpu.VMEM((D_MODEL, D_FF_EXPERT), BF16),
                        pltpu.VMEM((D_FF_EXPERT, D_MODEL), BF16)],
    )
    return pl.pallas_call(
        _moe_kernel,
        grid_spec=grid_spec,
        out_shape=jax.ShapeDtypeStruct((n_rows * ROW_TILE, LANES), F32),
        compiler_params=_cparams(("arbitrary",)),
        name="moe_experts",
    )(block_e, n_used, x_pad, wg, wu, wd)


def _combine_kernel(tm, n_steps, pos0_ref, pos1_ref, pos0_next_ref, pos1_next_ref, route_ref, x1_ref, ypad_hbm,
                    o_ref, ybuf, sems):
    i = pl.program_id(0)
    slot = lax.rem(i, 2)

    def gather(p0_ref, p1_ref, s):
        buf = ybuf.at[s]

        def body(r, carry):
            pltpu.make_async_copy(_token_tile(ypad_hbm, p0_ref[r]), _token_tile(buf, r), sems.at[s]).start(
                priority=0)
            pltpu.make_async_copy(_token_tile(ypad_hbm, p1_ref[r]), _token_tile(buf, tm + r), sems.at[s]).start(
                priority=1)
            return carry

        lax.fori_loop(0, tm, body, 0, unroll=8)

    @pl.when(i == 0)
    def _():
        gather(pos0_ref, pos1_ref, 0)

    @pl.when(i + 1 < n_steps)
    def _():
        gather(pos0_next_ref, pos1_next_ref, 1 - slot)

    buf = ybuf.at[slot]
    pltpu.make_async_copy(ypad_hbm.at[pl.ds(0, 2 * tm * ROW_TILE)], buf, sems.at[slot]).wait()
    g1 = route_ref[:, 2:3]
    g2 = route_ref[:, 3:4]
    o_ref[...] = x1_ref[...] + (g1 * _load_token_tiles(buf, 0, tm) + g2 * _load_token_tiles(buf, tm, tm))


def moe_combine(pos, route, x1, y_pad):
    pos0, pos1 = pos
    T = x1.shape[0]
    tm = min(T, COMBINE_TM)
    n_steps = T // tm
    return pl.pallas_call(
        functools.partial(_combine_kernel, tm, n_steps),
        grid=(n_steps,),
        in_specs=[pl.BlockSpec((tm,), lambda i: (i,), memory_space=pltpu.SMEM)] * 2
        + [pl.BlockSpec((tm,), lambda i: (jnp.minimum(i + 1, n_steps - 1),), memory_space=pltpu.SMEM)] * 2
        + [pl.BlockSpec((tm, LANES), lambda i: (i, 0)), pl.BlockSpec((tm, D_MODEL), lambda i: (i, 0)),
           pl.BlockSpec(memory_space=pl.ANY)],
        out_specs=pl.BlockSpec((tm, D_MODEL), lambda i: (i, 0)),
        out_shape=jax.ShapeDtypeStruct((T, D_MODEL), F32),
        scratch_shapes=[pltpu.VMEM((2, 2 * tm * ROW_TILE, LANES), F32), pltpu.SemaphoreType.DMA((2,))],
        compiler_params=_cparams(("arbitrary",)),
        name="moe_combine",
    )(pos0, pos1, pos0, pos1, route, x1, y_pad)


def hier_moe(groups, counts, layer, wg, wu, wd):
    sizes = [g[0].shape[0] for g in groups]
    n_assign = 2 * sum(sizes)
    n_blocks = (n_assign + N_EXPERTS * (MOE_BM - 1) + MOE_BM - 1) // MOE_BM
    fields = jnp.concatenate([g[3] for g in groups], axis=1)
    experts = fields[0:2].astype(jnp.int32)
    ranks = fields[4:6].astype(jnp.int32)
    cnt = counts[0, :N_EXPERTS].astype(jnp.int32)
    padded = ((cnt + MOE_BM - 1) // MOE_BM) * MOE_BM
    pad_ends = jnp.cumsum(padded).astype(jnp.int32)
    pad_starts = pad_ends - padded
    is_e = experts[:, :, None] == jnp.arange(N_EXPERTS, dtype=jnp.int32)
    pos = ranks + jnp.sum(jnp.where(is_e, pad_starts, 0), axis=-1)
    block_start = jnp.arange(n_blocks, dtype=jnp.int32) * MOE_BM
    block_e = jnp.minimum(jnp.sum(block_start[:, None] >= pad_ends[None, :], axis=1), N_EXPERTS - 1).astype(jnp.int32)
    n_used = (pad_ends[-1:] // MOE_BM).astype(jnp.int32)
    starts = [sum(sizes[:g]) for g in range(len(groups))]
    pos_g = [(pos[0, s:s + n], pos[1, s:s + n]) for s, n in zip(starts, sizes)]
    x_pad = moe_dispatch(pad_ends, pos_g, [g[1] for g in groups], n_blocks * MOE_BM)
    y_pad = moe_experts(x_pad, block_e, n_used, layer, wg, wu, wd)
    return [moe_combine(p, g[2], g[0], y_pad) for g, p in zip(groups, pos_g)]


def _rope_tables(start, n):
    half = D_HEAD // 2
    inv = ROPE_BASE ** (-jnp.arange(half, dtype=F32) / half)

    def trig(pos):
        ang = pos.astype(F32)[:, None] * inv[None, :]
        return jnp.cos(ang), jnp.sin(ang)

    if n % ROPE_BLOCK == 0:
        cb, sb = trig(start + ROPE_BLOCK * jnp.arange(n // ROPE_BLOCK, dtype=jnp.int32))
        co, so = trig(jnp.arange(ROPE_BLOCK, dtype=jnp.int32))
        cos = (cb[:, None] * co[None] - sb[:, None] * so[None]).reshape(n, half)
        sin = (sb[:, None] * co[None] + cb[:, None] * so[None]).reshape(n, half)
    else:
        cos, sin = trig(start + jnp.arange(n, dtype=jnp.int32))
    return jnp.concatenate([cos, cos], axis=-1), jnp.concatenate([-sin, sin], axis=-1)


def _router_weights(w_group, b_group, w_router, b_router):
    w = jnp.zeros((D_MODEL, LANES), F32)
    w = w.at[:, :N_EXPERTS].set(w_router).at[:, ROUTER_GROUP_LANE0:ROUTER_GROUP_LANE0 + N_GROUPS].set(w_group)
    b = jnp.zeros((1, LANES), F32)
    b = b.at[0, :N_EXPERTS].set(b_router).at[0, ROUTER_GROUP_LANE0:ROUTER_GROUP_LANE0 + N_GROUPS].set(b_group)
    return w.astype(BF16), b


def _tile(v, n):
    return jnp.tile(v.astype(F32), n)


def _run_trunk(x, B, L, mem_k, mem_v, ret_state0, band_past, P):
    prompt = band_past is None
    out_tm, out_groups = (OUT_PROJ_TM, OUT_PROJ_TM // OUT_PROJ_GROUP) if prompt else (OUT_PROJ_SAMPLE_TM, 1)
    proj_tm = min(B * L, PROJ_TM)
    scale = D_HEAD ** -0.5
    ones = lambda n: jnp.ones((n * D_HEAD,), F32)
    if prompt:
        cos2, sin2 = _rope_tables(0, L)
        tab_map = lambda i: i % (L // proj_tm)
    else:
        cos2, sin2 = _rope_tables(PAST_LEN, L)
        cos2, sin2 = jnp.tile(cos2, (B, 1)), jnp.tile(sin2, (B, 1))
        tab_map = lambda i: i
    out = {}

    def ffn(layer, mixed, proj, qm_block, w_out, x):
        def run(counts0):
            return out_proj(
                mixed, proj, qm_block, mem_k, mem_v, layer, w_out.astype(BF16), x, P["norm_ffn"][layer],
                *_router_weights(P["w_group"][layer], P["b_group"][layer], P["w_router"][layer],
                                 P["b_router"][layer]), counts0, tm=out_tm, n_groups=out_groups, rows_per_stream=L)
        return (yield run)

    segs0 = (("rope", N_HEADS), ("rope", N_HEADS), ("plain", N_HEADS), ("silu", N_HEADS), ("norm", H_MEM))
    gain0 = jnp.concatenate([ones(N_HEADS), ones(N_HEADS) * scale, ones(2 * N_HEADS),
                             _tile(P["q_norm_mem"][0], H_MEM) * scale])
    proj = norm_proj(x, P["norm_mix"][0], P["w_in_ret"][0].astype(BF16), gain0, segs0, tm=proj_tm,
                     out_dtype=BF16, rope=(cos2, sin2, tab_map))
    mixed, S = retention(proj, ret_state0, P["gn_ret"][0], B=B, L=L, C=RET_CHUNK if prompt else L)
    out["ret_state"] = S
    x = yield from ffn(0, mixed, proj, (4 * MIX_W) // MEM_W, P["w_out_ret"][0], x)

    segs1 = (("norm", N_HEADS), ("norm", N_HEADS), ("plain", N_HEADS), ("norm", H_MEM))
    gain1 = jnp.concatenate([_tile(P["q_norm_att"][0], N_HEADS) * scale, _tile(P["k_norm_att"][0], N_HEADS),
                             ones(N_HEADS), _tile(P["q_norm_mem"][1], H_MEM) * scale])
    w_in = P["w_in_att"][0].astype(BF16)
    if prompt:
        proj = norm_proj(x, P["norm_mix"][1], w_in, gain1, segs1, tm=proj_tm, out_dtype=BF16)
        keep = min(BAND_PAST, L)
        tiles = L // keep
        kv = norm_proj(x, P["norm_mix"][1], w_in[:, MIX_W:3 * MIX_W], gain1[MIX_W:3 * MIX_W], segs1[1:3],
                       tm=keep, out_dtype=F32, n_tiles=B, row_map=lambda i: i * tiles + tiles - 1)
        out["band_k"] = kv[:, :MIX_W].reshape(B, keep, N_HEADS, D_HEAD)
        out["band_v"] = kv[:, MIX_W:].reshape(B, keep, N_HEADS, D_HEAD)
        mixed = band_prompt(proj, P["rel_bias_att"][0], B=B, L=L)
    else:
        proj_f = norm_proj(x, P["norm_mix"][1], w_in, gain1, segs1, tm=proj_tm, out_dtype=F32)
        out["band_k"] = proj_f[:, MIX_W:2 * MIX_W].reshape(B, L, N_HEADS, D_HEAD)
        out["band_v"] = proj_f[:, 2 * MIX_W:3 * MIX_W].reshape(B, L, N_HEADS, D_HEAD)
        proj = proj_f.astype(BF16)
        mixed = band_sample(proj, band_past[0], band_past[1], P["rel_bias_att"][0], B=B, L=L)
    out["y"] = yield from ffn(1, mixed, proj, (3 * MIX_W) // MEM_W, P["w_out_att"][0], x)
    return out


def kernel(x_prompt, x_sample, mem_prompt, state_ret, cache_band_k, cache_band_v, cache_mem_k, cache_mem_v, norm_mix, norm_ffn, norm_mem, w_in_ret, gn_ret, w_out_ret, w_in_att, q_norm_att, k_norm_att, rel_bias_att, w_out_att, w_mem_kv, q_norm_mem, k_norm_mem, w_group, b_group, w_router, b_router, w_e_gate, w_e_up, w_e_down):
    P = {"norm_mix": norm_mix, "norm_ffn": norm_ffn, "w_in_ret": w_in_ret, "gn_ret": gn_ret,
         "w_out_ret": w_out_ret, "w_in_att": w_in_att, "q_norm_att": q_norm_att,
         "k_norm_att": k_norm_att, "rel_bias_att": rel_bias_att, "w_out_att": w_out_att,
         "q_norm_mem": q_norm_mem, "w_group": w_group, "b_group": b_group, "w_router": w_router,
         "b_router": b_router, "w_e_gate": w_e_gate, "w_e_up": w_e_up, "w_e_down": w_e_down}
    B, L, D = x_prompt.shape
    Bd, Ld, _ = x_sample.shape
    depth = norm_mix.shape[0]

    mem_k_p, mem_v_p = [], []
    for i in range(depth):
        gain = jnp.concatenate([_tile(k_norm_mem[i], H_MEM), jnp.ones((MEM_W,), F32)])
        kv = norm_proj(mem_prompt.reshape(B * N_MEM, D), norm_mem[i], w_mem_kv[i].astype(BF16), gain,
                       (("norm", H_MEM), ("plain", H_MEM)), tm=N_MEM, out_dtype=F32)
        mem_k_p.append(kv[:, :MEM_W].reshape(B, N_MEM, H_MEM, D_HEAD))
        mem_v_p.append(kv[:, MEM_W:].reshape(B, N_MEM, H_MEM, D_HEAD))
    mem_k_p = jnp.stack(mem_k_p)
    mem_v_p = jnp.stack(mem_v_p)

    zeros_state = jnp.zeros((B, N_HEADS, D_HEAD, D_HEAD), F32)
    trunks = [_run_trunk(x_prompt.reshape(B * L, D), B, L, mem_k_p, mem_v_p, zeros_state, None, P),
              _run_trunk(x_sample.reshape(Bd * Ld, D), Bd, Ld, cache_mem_k, cache_mem_v, state_ret[0],
                         (cache_band_k[0], cache_band_v[0]), P)]
    pending = [next(t) for t in trunks]
    results = []
    for layer in range(depth):
        counts = jnp.zeros((SUBLANES, LANES), F32)
        groups = []
        for run in pending:
            x1, h2, route, route_t, counts = run(counts)
            groups.append((x1, h2, route, route_t))
        outs = hier_moe(groups, counts, layer, w_e_gate, w_e_up, w_e_down)
        pending = []
        for t, x2 in zip(trunks, outs):
            try:
                pending.append(t.send(x2))
            except StopIteration as done:
                results.append(done.value)
    res_p, res_s = results

    return (res_p["y"].reshape(B, L, D), res_s["y"].reshape(Bd, Ld, D),
            res_p["ret_state"][None], res_s["ret_state"][None],
            res_p["band_k"][None], res_p["band_v"][None], res_s["band_k"][None], res_s["band_v"][None],
            mem_k_p, mem_v_p)
```

```python
import functools

import jax
import jax.numpy as jnp
from jax import lax
from jax.experimental import pallas as pl
from jax.experimental.pallas import tpu as pltpu

F32 = jnp.float32
BF16 = jnp.bfloat16

D_MODEL = 1024
D_HEAD = 128
CHUNK = 64
N_HEADS = 8
H_MEM = 4
N_MEM = 256
MEM_W = H_MEM * D_HEAD
MIX_W = N_HEADS * D_HEAD
N_PREV_CHUNKS = 8
BAND_PAST = N_PREV_CHUNKS * CHUNK
REL_CLIP = 128
N_GROUPS = 4
EXPERTS_PER_GROUP = 8
N_EXPERTS = N_GROUPS * EXPERTS_PER_GROUP
D_FF_EXPERT = 512
ROPE_BASE = 10000.0
EPS = 1e-6
NEG_INF = -1e30
PAST_LEN = 1024

LANES = 128
SUBLANES = 8

ROPE_BLOCK = 128
RET_CHUNK = 256
BAND_TQ = 512
BAND_SUB = 128
BAND_WIN = BAND_PAST + BAND_SUB
MOE_BM = 512
DISPATCH_TM = 4096
COMBINE_TM = 1024
COMBINE_PROJ_TM = 512
PROJ_TM = 1024
OUT_PROJ_TM = 1024
OUT_PROJ_GROUP = 512
OUT_PROJ_SAMPLE_TM = 128
ROUTE_ROWS = 128
ROUTER_GROUP_LANE0 = N_EXPERTS
V7X_VMEM_BYTES = 64 * 1024 * 1024
VMEM_LIMIT = V7X_VMEM_BYTES - 8 * 1024 * 1024


def _cparams(sem):
    return pltpu.CompilerParams(dimension_semantics=sem, vmem_limit_bytes=VMEM_LIMIT)


def _dot(a, b):
    return jnp.dot(a, b, preferred_element_type=F32)


def _dot_nt(a, b):
    return lax.dot_general(a, b, (((1,), (1,)), ((), ())), preferred_element_type=F32)


def _sigmoid(x):
    return 1.0 / (1.0 + jnp.exp(-x))


ROW_TILE = D_MODEL // LANES
assert ROW_TILE == SUBLANES


def _load_token_tiles(ref, t0, n):
    return jnp.concatenate([ref[pl.ds(t0 * ROW_TILE + j, n, stride=ROW_TILE), :] for j in range(ROW_TILE)], axis=1)


def _store_token_tiles(ref, t0, val):
    n = val.shape[0]
    for j in range(ROW_TILE):
        ref[pl.ds(t0 * ROW_TILE + j, n, stride=ROW_TILE), :] = val[:, j * LANES:(j + 1) * LANES]


def _token_tile(ref, t):
    return ref.at[pl.ds(pl.multiple_of(t * ROW_TILE, ROW_TILE), ROW_TILE)]


def _norm_proj_kernel(segs, has_rope, x_ref, g_ref, w_ref, gain_ref, *rest):
    if has_rope:
        cos_ref, sin_ref, o_ref = rest
    else:
        (o_ref,) = rest
    _project(segs, x_ref[...], g_ref, w_ref, gain_ref, cos_ref if has_rope else None,
             sin_ref if has_rope else None, o_ref)


def _project(segs, x, g_ref, w_ref, gain_ref, cos_ref, sin_ref, o_ref):
    ms = jnp.mean(x * x, axis=-1, keepdims=True)
    h = (x * lax.rsqrt(ms + EPS) * g_ref[...]).astype(BF16)
    col = 0
    for kind, n_heads in segs:
        width = n_heads * D_HEAD
        acc = _dot(h, w_ref[:, col:col + width])
        for j in range(n_heads):
            a = acc[:, j * D_HEAD:(j + 1) * D_HEAD]
            c0 = col + j * D_HEAD
            gain = gain_ref[:, c0:c0 + D_HEAD]
            if kind == "rope":
                rot = pltpu.roll(a, D_HEAD // 2, 1)
                out = (a * cos_ref[...] + rot * sin_ref[...]) * gain
            elif kind == "norm":
                out = a * lax.rsqrt(jnp.mean(a * a, axis=-1, keepdims=True) + EPS) * gain
            elif kind == "silu":
                out = a * _sigmoid(a)
            else:
                out = a
            o_ref[:, c0:c0 + D_HEAD] = out.astype(o_ref.dtype)
        col += width


def norm_proj(x, g, w, gain, segs, *, tm, out_dtype, n_tiles=None, row_map=None, rope=None):
    T, D = x.shape
    N = w.shape[1]
    assert sum(n for _, n in segs) * D_HEAD == N
    if n_tiles is None:
        n_tiles = T // tm
    if row_map is None:
        row_map = lambda i: i
    in_specs = [
        pl.BlockSpec((tm, D), lambda i: (row_map(i), 0)),
        pl.BlockSpec((1, D), lambda i: (0, 0)),
        pl.BlockSpec((D, N), lambda i: (0, 0)),
        pl.BlockSpec((1, N), lambda i: (0, 0)),
    ]
    args = [x, g.reshape(1, D), w, gain.reshape(1, N)]
    if rope is not None:
        cos2, sin2, tab_map = rope
        in_specs += [pl.BlockSpec((tm, D_HEAD), lambda i: (tab_map(i), 0)),
                     pl.BlockSpec((tm, D_HEAD), lambda i: (tab_map(i), 0))]
        args += [cos2, sin2]
    return pl.pallas_call(
        functools.partial(_norm_proj_kernel, segs, rope is not None),
        grid=(n_tiles,),
        in_specs=in_specs,
        out_specs=pl.BlockSpec((tm, N), lambda i: (i, 0)),
        out_shape=jax.ShapeDtypeStruct((n_tiles * tm, N), out_dtype),
        compiler_params=_cparams(("arbitrary",)),
        name="norm_proj",
    )(*args)


def _retention_kernel(q_ref, k_ref, v_ref, g_ref, s0_ref, dmask_ref, qdec_ref, kdec_ref, gc_ref,
                      gn_ref, o_ref, s_ref):
    @pl.when(pl.program_id(1) == 0)
    def _():
        s_ref[...] = s0_ref[...]

    for h in range(N_HEADS):
        sl = slice(h * D_HEAD, (h + 1) * D_HEAD)
        q = q_ref[:, sl]
        k = k_ref[:, sl]
        v = v_ref[:, sl]
        S = s_ref[0, h]
        scores = _dot_nt(q, k) * dmask_ref[h]
        intra = _dot(scores.astype(BF16), v)
        cross = _dot(q, S.astype(BF16)) * qdec_ref[:, sl]
        o = intra + cross
        kd = (k.astype(F32) * kdec_ref[:, sl]).astype(BF16)
        s_ref[0, h] = gc_ref[h] * S + lax.dot_general(kd, v, (((0,), (0,)), ((), ())), preferred_element_type=F32)
        mu = jnp.mean(o, axis=-1, keepdims=True)
        d = o - mu
        var = jnp.mean(d * d, axis=-1, keepdims=True)
        y = d * lax.rsqrt(var + EPS) * gn_ref[:, sl] * g_ref[:, sl].astype(F32)
        o_ref[:, sl] = y.astype(o_ref.dtype)


def _retention_tables(C):
    log_gamma = jnp.log1p(-jnp.exp2(-5.0 - jnp.arange(N_HEADS, dtype=F32)))
    idx = jnp.arange(C, dtype=F32)
    diff = idx[:, None] - idx[None, :]
    dmask = jnp.where(diff[None] >= 0,
                      jnp.exp(log_gamma[:, None, None] * jnp.maximum(diff, 0.0)[None]), 0.0)
    q_dec = jnp.exp(log_gamma[None, :] * (idx + 1.0)[:, None])
    k_dec = jnp.exp(log_gamma[None, :] * (C - 1.0 - idx)[:, None])
    g_c = jnp.exp(log_gamma * C)
    rep = lambda t: jnp.repeat(t, D_HEAD, axis=-1)
    return dmask, rep(q_dec), rep(k_dec), rep(g_c[:, None])[:, None, :]


def retention(proj, s0, gn, *, B, L, C):
    nc = L // C
    dmask, qdec, kdec, gc = _retention_tables(C)
    row = lambda col: pl.BlockSpec((C, MIX_W), lambda b, c: (b * nc + c, col))
    const = lambda shape: pl.BlockSpec(shape, lambda b, c: (0,) * len(shape))
    state_spec = pl.BlockSpec((1, N_HEADS, D_HEAD, D_HEAD), lambda b, c: (b, 0, 0, 0))
    return pl.pallas_call(
        _retention_kernel,
        grid=(B, nc),
        in_specs=[row(0), row(1), row(2), row(3), state_spec,
                  const((N_HEADS, C, C)), const((C, MIX_W)), const((C, MIX_W)),
                  const((N_HEADS, 1, D_HEAD)), const((1, MIX_W))],
        out_specs=[pl.BlockSpec((C, MIX_W), lambda b, c: (b * nc + c, 0)), state_spec],
        out_shape=[jax.ShapeDtypeStruct((B * L, MIX_W), BF16),
                   jax.ShapeDtypeStruct((B, N_HEADS, D_HEAD, D_HEAD), F32)],
        compiler_params=_cparams(("arbitrary", "arbitrary")),
        name="retention",
    )(proj, proj, proj, proj, s0, dmask, qdec, kdec, gc, gn.reshape(1, MIX_W))


def _attend_two_piece(q, k_a, k_b, v_a, v_b, bias_a, bias_b):
    s_a = _dot_nt(q, k_a) + bias_a
    s_b = _dot_nt(q, k_b) + bias_b
    m = jnp.maximum(jnp.max(s_a, axis=-1, keepdims=True), jnp.max(s_b, axis=-1, keepdims=True))
    p_a = jnp.exp(s_a - m)
    p_b = jnp.exp(s_b - m)
    denom = jnp.sum(p_a, axis=-1, keepdims=True) + jnp.sum(p_b, axis=-1, keepdims=True)
    o = _dot(p_a.astype(BF16), v_a) + _dot(p_b.astype(BF16), v_b)
    return o / denom


def _band_prompt_kernel(q_ref, kp_ref, kc_ref, vp_ref, vc_ref, bias_ref, o_ref, s_buf, p_buf):
    @pl.when((pl.program_id(0) == 0) & (pl.program_id(1) == 0))
    def _():
        p_buf[...] = jnp.zeros_like(p_buf)

    prev_off = jnp.where(pl.program_id(1) > 0, 0.0, NEG_INF).astype(F32)
    for h in range(N_HEADS):
        sl = slice(h * D_HEAD, (h + 1) * D_HEAD)
        q = q_ref[:, sl]
        s_buf[:, :BAND_TQ] = _dot_nt(q, kp_ref[:, sl]) + prev_off
        s_buf[:, BAND_TQ:] = _dot_nt(q, kc_ref[:, sl])
        denoms = []
        for s in range(BAND_TQ // BAND_SUB):
            r0 = s * BAND_SUB
            sc = s_buf[r0:r0 + BAND_SUB, r0:r0 + BAND_WIN] + bias_ref[h]
            p = jnp.exp(sc - jnp.max(sc, axis=-1, keepdims=True))
            denoms.append(jnp.sum(p, axis=-1, keepdims=True))
            p_buf[r0:r0 + BAND_SUB, r0:r0 + BAND_WIN] = p.astype(BF16)
        o = _dot(p_buf[:, :BAND_TQ], vp_ref[:, sl]) + _dot(p_buf[:, BAND_TQ:], vc_ref[:, sl])
        o_ref[:, sl] = (o / jnp.concatenate(denoms, axis=0)).astype(o_ref.dtype)


def _rel_bias(bias_table, offset, n_rows, width):
    n = n_rows + width - 1
    v = jnp.arange(n, dtype=jnp.int32)
    v = jnp.where(v < width, v, v - n)
    diag = bias_table.astype(F32)[:, jnp.clip(offset - v, -REL_CLIP, REL_CLIP) + REL_CLIP]
    cut = jnp.tile(diag, (1, n_rows))[:, :n_rows * (n - 1)].reshape(diag.shape[0], n_rows, n - 1)
    return cut[:, :, :width]


def _band_prompt_bias(bias_table):
    r = jnp.arange(BAND_SUB, dtype=jnp.int32)[:, None]
    j = jnp.arange(BAND_WIN, dtype=jnp.int32)[None, :]
    q_chunk = r // CHUNK + N_PREV_CHUNKS
    k_chunk = j // CHUNK
    allowed = (k_chunk <= q_chunk) & (k_chunk >= q_chunk - N_PREV_CHUNKS)
    return jnp.where(allowed[None], _rel_bias(bias_table, BAND_PAST, BAND_SUB, BAND_WIN), NEG_INF)


def band_prompt(proj, bias_table, *, B, L):
    nq = L // BAND_TQ
    cur = lambda col: pl.BlockSpec((BAND_TQ, MIX_W), lambda b, i: (b * nq + i, col))
    prev = lambda col: pl.BlockSpec((BAND_TQ, MIX_W), lambda b, i: (b * nq + jnp.maximum(i - 1, 0), col))
    return pl.pallas_call(
        _band_prompt_kernel,
        grid=(B, nq),
        in_specs=[cur(0), prev(1), cur(1), prev(2), cur(2),
                  pl.BlockSpec((N_HEADS, BAND_SUB, BAND_WIN), lambda b, i: (0, 0, 0))],
        out_specs=pl.BlockSpec((BAND_TQ, MIX_W), lambda b, i: (b * nq + i, 0)),
        out_shape=jax.ShapeDtypeStruct((B * L, MIX_W), BF16),
        scratch_shapes=[pltpu.VMEM((BAND_TQ, 2 * BAND_TQ), F32), pltpu.VMEM((BAND_TQ, 2 * BAND_TQ), BF16)],
        compiler_params=_cparams(("arbitrary", "arbitrary")),
        name="band_prompt",
    )(proj, proj, proj, proj, proj, _band_prompt_bias(bias_table))


def _band_sample_kernel(q_ref, kp_ref, kc_ref, vp_ref, vc_ref, bias_p_ref, bias_c_ref, o_ref):
    n_past = kp_ref.shape[1] // N_HEADS
    for h in range(N_HEADS):
        sl = slice(h * D_HEAD, (h + 1) * D_HEAD)
        kp = kp_ref.at[0][pl.ds(h, n_past, stride=N_HEADS), :].astype(BF16)
        vp = vp_ref.at[0][pl.ds(h, n_past, stride=N_HEADS), :].astype(BF16)
        o = _attend_two_piece(q_ref[:, sl], kp, kc_ref[:, sl], vp, vc_ref[:, sl], bias_p_ref[h], bias_c_ref[h])
        o_ref[:, sl] = o.astype(o_ref.dtype)


def band_sample(proj, past_k, past_v, bias_table, *, B, L):
    P = past_k.shape[1]
    qpos = PAST_LEN + jnp.arange(L, dtype=jnp.int32)
    kpos = jnp.concatenate([PAST_LEN - P + jnp.arange(P, dtype=jnp.int32), qpos])
    qc = (qpos // CHUNK)[:, None]
    kc = (kpos // CHUNK)[None, :]
    allowed = (kpos[None, :] >= 0) & (kc <= qc) & (kc >= qc - N_PREV_CHUNKS)
    rel_bias = jnp.concatenate([_rel_bias(bias_table, P, L, P), _rel_bias(bias_table, 0, L, L)], axis=-1)
    bias = jnp.where(allowed[None], rel_bias, NEG_INF)
    cur = lambda col: pl.BlockSpec((L, MIX_W), lambda b: (b, col))
    past = pl.BlockSpec((1, P * N_HEADS, D_HEAD), lambda b: (b, 0, 0))
    past_k = past_k.reshape(B, P * N_HEADS, D_HEAD)
    past_v = past_v.reshape(B, P * N_HEADS, D_HEAD)
    return pl.pallas_call(
        _band_sample_kernel,
        grid=(B,),
        in_specs=[cur(0), past, cur(1), past, cur(2),
                  pl.BlockSpec((N_HEADS, L, P), lambda b: (0, 0, 0)),
                  pl.BlockSpec((N_HEADS, L, L), lambda b: (0, 0, 0))],
        out_specs=pl.BlockSpec((L, MIX_W), lambda b: (b, 0)),
        out_shape=jax.ShapeDtypeStruct((B * L, MIX_W), BF16),
        compiler_params=_cparams(("arbitrary",)),
        name="band_sample",
    )(proj, past_k, proj, past_v, proj, bias[:, :, :P], bias[:, :, P:])


def _out_proj_kernel(n_streams, n_groups, o_ref, qm_ref, mk_ref, mv_ref, w_ref, x_ref, gf_ref, wr_ref, br_ref,
                     tri_ref, cnt0_ref, x1_ref, h2_ref, route_ref, route_t_ref, cnt_ref):
    @pl.when(pl.program_id(0) == 0)
    def _():
        cnt_ref[...] = cnt0_ref[...]

    tm = x_ref.shape[0]
    sr = tm // n_streams
    gr = tm // n_groups
    om_heads = []
    for h in range(H_MEM):
        sl = slice(h * D_HEAD, (h + 1) * D_HEAD)
        chunks = []
        for c in range(n_streams):
            mk = mk_ref.at[0, c][pl.ds(h, N_MEM, stride=H_MEM), :].astype(BF16)
            mv = mv_ref.at[0, c][pl.ds(h, N_MEM, stride=H_MEM), :].astype(BF16)
            s = _dot_nt(qm_ref[c * sr:(c + 1) * sr, sl], mk)
            p = jnp.exp(s - jnp.max(s, axis=-1, keepdims=True))
            om = _dot(p.astype(BF16), mv) / jnp.sum(p, axis=-1, keepdims=True)
            chunks.append(om.astype(BF16))
        om_heads.append(chunks[0] if n_streams == 1 else jnp.concatenate(chunks, axis=0))

    rows = min(gr, ROUTE_ROWS)
    lane = lax.broadcasted_iota(jnp.int32, (rows, LANES), 1)
    for g in range(n_groups):
        g0 = g * gr
        feat = jnp.concatenate([o_ref[g0:g0 + gr, :]] + [om[g0:g0 + gr] for om in om_heads], axis=1)
        x1 = x_ref[g0:g0 + gr, :] + _dot(feat, w_ref[...])
        x1_ref[g0:g0 + gr, :] = x1
        hn = x1 * lax.rsqrt(jnp.mean(x1 * x1, axis=-1, keepdims=True) + EPS) * gf_ref[...]
        _store_token_tiles(h2_ref, g0, hn)

        logits = _dot(hn.astype(BF16), wr_ref[...]) + br_ref[...]
        tops = [_route_top2(logits[r0:r0 + rows]) for r0 in range(0, gr, rows)]

        onehot = jnp.concatenate([jnp.where(t[4] | t[5], 1.0, 0.0) for t in tops], axis=0)
        before = cnt_ref[0:1, :] + _dot(tri_ref[...], onehot.astype(BF16))
        cnt_ref[0:1, :] = cnt_ref[0:1, :] + jnp.sum(onehot, axis=0, keepdims=True)
        for k, (i1, i2, w1, w2, oh1, oh2) in enumerate(tops):
            b = before[k * rows:(k + 1) * rows]
            r1 = jnp.sum(jnp.where(oh1, b, 0.0), axis=-1, keepdims=True)
            r2 = jnp.sum(jnp.where(oh2, b, 0.0), axis=-1, keepdims=True)
            route = jnp.where(lane == 0, i1, jnp.where(lane == 1, i2, jnp.where(lane == 2, w1, jnp.where(
                lane == 3, w2, jnp.where(lane == 4, r1, jnp.where(lane == 5, r2, 0.0))))))
            c0 = g0 + k * rows
            route_ref[c0:c0 + rows, :] = route
            route_t_ref[:, c0:c0 + rows] = route.T[:SUBLANES, :]


def _route_top2(logits):
    lane = lax.broadcasted_iota(jnp.int32, logits.shape, 1)
    lane_f = lane.astype(F32)
    far = float(LANES)
    is_group = (lane >= ROUTER_GROUP_LANE0) & (lane < ROUTER_GROUP_LANE0 + N_GROUPS)
    gl = jnp.where(is_group, logits, NEG_INF)
    g_max = jnp.max(gl, axis=-1, keepdims=True)
    g_sel = jnp.min(jnp.where(gl == g_max, lane_f, far), axis=-1, keepdims=True) - float(ROUTER_GROUP_LANE0)
    g_gate = 1.0 / jnp.sum(jnp.exp(gl - g_max), axis=-1, keepdims=True)
    in_group = jnp.right_shift(lane, EXPERTS_PER_GROUP.bit_length() - 1).astype(F32) == g_sel
    el = jnp.where(in_group, logits, NEG_INF)
    m1 = jnp.max(el, axis=-1, keepdims=True)
    i1 = jnp.min(jnp.where(el == m1, lane_f, far), axis=-1, keepdims=True)
    el2 = jnp.where(lane_f == i1, NEG_INF, el)
    m2 = jnp.max(el2, axis=-1, keepdims=True)
    i2 = jnp.min(jnp.where(el2 == m2, lane_f, far), axis=-1, keepdims=True)
    e2 = jnp.exp(m2 - m1)
    p1 = 1.0 / (1.0 + e2)
    return i1, i2, g_gate * p1, g_gate * (e2 * p1), lane_f == i1, lane_f == i2


def out_proj(mixed, proj, qm_block, mem_k, mem_v, layer, w_out, x, g_ffn, w_router, b_router, counts0, *, tm,
             n_groups, rows_per_stream):
    T = x.shape[0]
    n_streams = max(1, tm // rows_per_stream)
    tiles_per_stream = max(1, rows_per_stream // tm)
    gr = tm // n_groups
    assert min(gr, ROUTE_ROWS) % LANES == 0
    row = lambda width, col: pl.BlockSpec((tm, width), lambda i: (i, col))
    const = lambda shape: pl.BlockSpec(shape, lambda i: (0,) * len(shape))
    mem = pl.BlockSpec((1, n_streams, N_MEM * H_MEM, D_HEAD), lambda i: (layer, i // tiles_per_stream, 0, 0))
    mem_k = mem_k.reshape(mem_k.shape[:2] + (N_MEM * H_MEM, D_HEAD))
    mem_v = mem_v.reshape(mem_v.shape[:2] + (N_MEM * H_MEM, D_HEAD))
    tri = (jnp.arange(gr)[None, :] < jnp.arange(gr)[:, None]).astype(BF16)
    return pl.pallas_call(
        functools.partial(_out_proj_kernel, n_streams, n_groups),
        grid=(T // tm,),
        in_specs=[row(MIX_W, 0), row(MEM_W, qm_block), mem, mem, const((MIX_W + MEM_W, D_MODEL)),
                  row(D_MODEL, 0), const((1, D_MODEL)), const((D_MODEL, LANES)), const((1, LANES)),
                  const((gr, gr)), const((SUBLANES, LANES))],
        out_specs=[row(D_MODEL, 0), pl.BlockSpec((tm * ROW_TILE, LANES), lambda i: (i, 0)), row(LANES, 0),
                   pl.BlockSpec((SUBLANES, tm), lambda i: (0, i)), const((SUBLANES, LANES))],
        out_shape=[jax.ShapeDtypeStruct((T, D_MODEL), F32), jax.ShapeDtypeStruct((T * ROW_TILE, LANES), F32),
                   jax.ShapeDtypeStruct((T, LANES), F32), jax.ShapeDtypeStruct((SUBLANES, T), F32),
                   jax.ShapeDtypeStruct((SUBLANES, LANES), F32)],
        compiler_params=_cparams(("arbitrary",)),
        name="out_proj",
    )(mixed, proj, mem_k, mem_v, w_out, x, g_ffn.reshape(1, D_MODEL), w_router, b_router, tri, counts0)


def _dispatch_kernel(tms, steps, pe_ref, pos0_ref, pos1_ref, *rest):
    h2_refs = rest[:len(tms)]
    xpad_hbm, zbuf, sem, zsem = rest[len(tms):]
    i = pl.program_id(0)

    @pl.when(i == 0)
    def _():
        zbuf[...] = jnp.zeros_like(zbuf)

        def block_copy(first_row):
            start = pl.multiple_of(first_row * ROW_TILE, MOE_BM * ROW_TILE)
            return pltpu.make_async_copy(zbuf, xpad_hbm.at[pl.ds(start, MOE_BM * ROW_TILE)], zsem)

        def tail_copy(e):
            return block_copy(pe_ref[e] - MOE_BM)

        def nonempty(e):
            return pe_ref[e] > (pe_ref[e - 1] if e > 0 else 0)

        for e in range(N_EXPERTS):
            pl.when(nonempty(e))(lambda e=e: tail_copy(e).start(priority=e % 2))
        for e in range(N_EXPERTS):
            pl.when(nonempty(e))(lambda e=e: tail_copy(e).wait())

        def fill(b, carry):
            cp = block_copy(b * MOE_BM)
            cp.start()
            cp.wait()
            return carry

        lax.fori_loop(pe_ref[N_EXPERTS - 1] // MOE_BM, xpad_hbm.shape[0] // (MOE_BM * ROW_TILE), fill, 0)

    def scatter_rows(h2_ref, tm):
        def body(r, carry):
            src = _token_tile(h2_ref, r)
            pltpu.make_async_copy(src, _token_tile(xpad_hbm, pos0_ref[r]), sem).start(priority=0)
            pltpu.make_async_copy(src, _token_tile(xpad_hbm, pos1_ref[r]), sem).start(priority=1)
            return carry

        lax.fori_loop(0, tm, body, 0, unroll=8)
        for _ in range(2):
            pltpu.make_async_copy(h2_ref, xpad_hbm.at[pl.ds(0, tm * ROW_TILE)], sem).wait()

    first = 0
    for h2_ref, tm, n in zip(h2_refs, tms, steps):
        pl.when(jnp.logical_and(i >= first, i < first + n))(functools.partial(scatter_rows, h2_ref, tm))
        first += n


def moe_dispatch(pad_ends, pos_groups, h2_groups, n_rows):
    sizes = [h2.shape[0] // ROW_TILE for h2 in h2_groups]
    tms = [min(T, DISPATCH_TM) for T in sizes]
    steps = [T // tm for T, tm in zip(sizes, tms)]
    firsts = [sum(steps[:g]) for g in range(len(steps))]
    pos0, pos1 = (jnp.concatenate([jnp.pad(p[k], (0, n * DISPATCH_TM - p[k].shape[0]))
                                   for p, n in zip(pos_groups, steps)]) for k in range(2))

    def rows(tm, first, n):
        return pl.BlockSpec((tm * ROW_TILE, LANES), lambda i, pe: (jnp.clip(i - first, 0, n - 1), 0))

    grid_spec = pltpu.PrefetchScalarGridSpec(
        num_scalar_prefetch=1,
        grid=(sum(steps),),
        in_specs=[pl.BlockSpec((DISPATCH_TM,), lambda i, pe: (i,), memory_space=pltpu.SMEM)] * 2
        + [rows(tm, first, n) for tm, first, n in zip(tms, firsts, steps)],
        out_specs=pl.BlockSpec(memory_space=pl.ANY),
        scratch_shapes=[pltpu.VMEM((MOE_BM * ROW_TILE, LANES), F32), pltpu.SemaphoreType.DMA(()),
                        pltpu.SemaphoreType.DMA(())],
    )
    return pl.pallas_call(
        functools.partial(_dispatch_kernel, tuple(tms), tuple(steps)),
        grid_spec=grid_spec,
        out_shape=jax.ShapeDtypeStruct((n_rows * ROW_TILE, LANES), F32),
        compiler_params=_cparams(("arbitrary",)),
        name="moe_dispatch",
    )(pad_ends, pos0, pos1, *h2_groups)


def _moe_kernel(be_ref, nu_ref, x_ref, wg_ref, wu_ref, wd_ref, y_ref, wg_s, wu_s, wd_s):
    i = pl.program_id(0)
    used = i < nu_ref[0]
    new_expert = jnp.logical_or(i == 0, be_ref[i] != be_ref[jnp.maximum(i - 1, 0)])

    @pl.when(jnp.logical_and(used, new_expert))
    def _():
        wg_s[...] = wg_ref[0, 0].astype(BF16)
        wu_s[...] = wu_ref[0, 0].astype(BF16)
        wd_s[...] = wd_ref[0, 0].astype(BF16)

    @pl.when(used)
    def _():
        x = _load_token_tiles(x_ref, 0, MOE_BM).astype(BF16)
        g = _dot(x, wg_s[...])
        u = _dot(x, wu_s[...])
        a = (g * _sigmoid(g) * u).astype(BF16)
        _store_token_tiles(y_ref, 0, _dot(a, wd_s[...]))

    @pl.when(jnp.logical_not(used))
    def _():
        y_ref[...] = jnp.zeros_like(y_ref)


def moe_experts(x_pad, block_e, n_used, layer, wg, wu, wd):
    n_rows = x_pad.shape[0] // ROW_TILE
    rows = pl.BlockSpec((MOE_BM * ROW_TILE, LANES), lambda i, be, nu: (i, 0))
    w_in = pl.BlockSpec((1, 1, D_MODEL, D_FF_EXPERT), lambda i, be, nu: (layer, be[i], 0, 0))
    w_dn = pl.BlockSpec((1, 1, D_FF_EXPERT, D_MODEL), lambda i, be, nu: (layer, be[i], 0, 0))
    grid_spec = pltpu.PrefetchScalarGridSpec(
        num_scalar_prefetch=2,
        grid=(n_rows // MOE_BM,),
        in_specs=[pl.BlockSpec((MOE_BM * ROW_TILE, LANES), lambda i, be, nu: (jnp.minimum(i, nu[0] - 1), 0)),
                  w_in, w_in, w_dn],
        out_specs=rows,
        scratch_shapes=[pltpu.VMEM((D_MODEL, D_FF_EXPERT), BF16), pltpu.VMEM((D_MODEL, D_FF_EXPERT), BF16),
                        pltpu.VMEM((D_FF_EXPERT, D_MODEL), BF16)],
    )
    return pl.pallas_call(
        _moe_kernel,
        grid_spec=grid_spec,
        out_shape=jax.ShapeDtypeStruct((n_rows * ROW_TILE, LANES), F32),
        compiler_params=_cparams(("arbitrary",)),
        name="moe_experts",
    )(block_e, n_used, x_pad, wg, wu, wd)


def _combine_kernel(tm, n_steps, pos0_ref, pos1_ref, pos0_next_ref, pos1_next_ref, route_ref, x1_ref, ypad_hbm,
                    o_ref, ybuf, sems):
    i = pl.program_id(0)
    slot = lax.rem(i, 2)

    def gather(p0_ref, p1_ref, s):
        buf = ybuf.at[s]

        def body(r, carry):
            pltpu.make_async_copy(_token_tile(ypad_hbm, p0_ref[r]), _token_tile(buf, r), sems.at[s]).start(
                priority=0)
            pltpu.make_async_copy(_token_tile(ypad_hbm, p1_ref[r]), _token_tile(buf, tm + r), sems.at[s]).start(
                priority=1)
            return carry

        lax.fori_loop(0, tm, body, 0, unroll=8)

    @pl.when(i == 0)
    def _():
        gather(pos0_ref, pos1_ref, 0)

    @pl.when(i + 1 < n_steps)
    def _():
        gather(pos0_next_ref, pos1_next_ref, 1 - slot)

    buf = ybuf.at[slot]
    pltpu.make_async_copy(ypad_hbm.at[pl.ds(0, 2 * tm * ROW_TILE)], buf, sems.at[slot]).wait()
    g1 = route_ref[:, 2:3]
    g2 = route_ref[:, 3:4]
    o_ref[...] = x1_ref[...] + (g1 * _load_token_tiles(buf, 0, tm) + g2 * _load_token_tiles(buf, tm, tm))


def moe_combine(pos, route, x1, y_pad):
    pos0, pos1 = pos
    T = x1.shape[0]
    tm = min(T, COMBINE_TM)
    n_steps = T // tm
    return pl.pallas_call(
        functools.partial(_combine_kernel, tm, n_steps),
        grid=(n_steps,),
        in_specs=[pl.BlockSpec((tm,), lambda i: (i,), memory_space=pltpu.SMEM)] * 2
        + [pl.BlockSpec((tm,), lambda i: (jnp.minimum(i + 1, n_steps - 1),), memory_space=pltpu.SMEM)] * 2
        + [pl.BlockSpec((tm, LANES), lambda i: (i, 0)), pl.BlockSpec((tm, D_MODEL), lambda i: (i, 0)),
           pl.BlockSpec(memory_space=pl.ANY)],
        out_specs=pl.BlockSpec((tm, D_MODEL), lambda i: (i, 0)),
        out_shape=jax.ShapeDtypeStruct((T, D_MODEL), F32),
        scratch_shapes=[pltpu.VMEM((2, 2 * tm * ROW_TILE, LANES), F32), pltpu.SemaphoreType.DMA((2,))],
        compiler_params=_cparams(("arbitrary",)),
        name="moe_combine",
    )(pos0, pos1, pos0, pos1, route, x1, y_pad)


def _combine_proj_kernel(segs, tm, n_steps, pos0_ref, pos1_ref, pos0_next_ref, pos1_next_ref, route_ref, x1_ref,
                         g_ref, w_ref, gain_ref, ypad_hbm, o_ref, x_ref, ybuf, sems):
    i = pl.program_id(0)
    slot = lax.rem(i, 2)

    def gather(p0_ref, p1_ref, s):
        buf = ybuf.at[s]

        def body(r, carry):
            pltpu.make_async_copy(_token_tile(ypad_hbm, p0_ref[r]), _token_tile(buf, r), sems.at[s]).start(
                priority=0)
            pltpu.make_async_copy(_token_tile(ypad_hbm, p1_ref[r]), _token_tile(buf, tm + r), sems.at[s]).start(
                priority=1)
            return carry

        lax.fori_loop(0, tm, body, 0, unroll=8)

    @pl.when(i == 0)
    def _():
        gather(pos0_ref, pos1_ref, 0)

    @pl.when(i + 1 < n_steps)
    def _():
        gather(pos0_next_ref, pos1_next_ref, 1 - slot)

    buf = ybuf.at[slot]
    pltpu.make_async_copy(ypad_hbm.at[pl.ds(0, 2 * tm * ROW_TILE)], buf, sems.at[slot]).wait()
    g1 = route_ref[:, 2:3]
    g2 = route_ref[:, 3:4]
    x = x1_ref[...] + (g1 * _load_token_tiles(buf, 0, tm) + g2 * _load_token_tiles(buf, tm, tm))
    x_ref[...] = x
    _project(segs, x, g_ref, w_ref, gain_ref, None, None, o_ref)


def combine_proj(pending, g, w, gain, segs):
    (pos0, pos1), route, x1, y_pad = pending
    T, D = x1.shape
    N = w.shape[1]
    tm = COMBINE_PROJ_TM
    n_steps = T // tm
    const = lambda shape: pl.BlockSpec(shape, lambda i: (0,) * len(shape))
    proj, x = pl.pallas_call(
        functools.partial(_combine_proj_kernel, segs, tm, n_steps),
        grid=(n_steps,),
        in_specs=[pl.BlockSpec((tm,), lambda i: (i,), memory_space=pltpu.SMEM)] * 2
        + [pl.BlockSpec((tm,), lambda i: (jnp.minimum(i + 1, n_steps - 1),), memory_space=pltpu.SMEM)] * 2
        + [pl.BlockSpec((tm, LANES), lambda i: (i, 0)), pl.BlockSpec((tm, D), lambda i: (i, 0)),
           const((1, D)), const((D, N)), const((1, N)), pl.BlockSpec(memory_space=pl.ANY)],
        out_specs=[pl.BlockSpec((tm, N), lambda i: (i, 0)), pl.BlockSpec((tm, D), lambda i: (i, 0))],
        out_shape=[jax.ShapeDtypeStruct((T, N), BF16), jax.ShapeDtypeStruct((T, D), F32)],
        scratch_shapes=[pltpu.VMEM((2, 2 * tm * ROW_TILE, LANES), F32), pltpu.SemaphoreType.DMA((2,))],
        compiler_params=_cparams(("arbitrary",)),
        name="combine_proj",
    )(pos0, pos1, pos0, pos1, route, x1, g.reshape(1, D), w, gain.reshape(1, N), y_pad)
    return x, proj


def hier_moe(groups, counts, layer, wg, wu, wd, defer=()):
    sizes = [g[0].shape[0] for g in groups]
    n_assign = 2 * sum(sizes)
    n_blocks = (n_assign + N_EXPERTS * (MOE_BM - 1) + MOE_BM - 1) // MOE_BM
    fields = jnp.concatenate([g[3] for g in groups], axis=1)
    experts = fields[0:2].astype(jnp.int32)
    ranks = fields[4:6].astype(jnp.int32)
    cnt = counts[0, :N_EXPERTS].astype(jnp.int32)
    padded = ((cnt + MOE_BM - 1) // MOE_BM) * MOE_BM
    pad_ends = jnp.cumsum(padded).astype(jnp.int32)
    pad_starts = pad_ends - padded
    is_e = experts[:, :, None] == jnp.arange(N_EXPERTS, dtype=jnp.int32)
    pos = ranks + jnp.sum(jnp.where(is_e, pad_starts, 0), axis=-1)
    block_start = jnp.arange(n_blocks, dtype=jnp.int32) * MOE_BM
    block_e = jnp.minimum(jnp.sum(block_start[:, None] >= pad_ends[None, :], axis=1), N_EXPERTS - 1).astype(jnp.int32)
    n_used = (pad_ends[-1:] // MOE_BM).astype(jnp.int32)
    starts = [sum(sizes[:g]) for g in range(len(groups))]
    pos_g = [(pos[0, s:s + n], pos[1, s:s + n]) for s, n in zip(starts, sizes)]
    x_pad = moe_dispatch(pad_ends, pos_g, [g[1] for g in groups], n_blocks * MOE_BM)
    y_pad = moe_experts(x_pad, block_e, n_used, layer, wg, wu, wd)
    return [(p, g[2], g[0], y_pad) if k in defer else moe_combine(p, g[2], g[0], y_pad)
            for k, (g, p) in enumerate(zip(groups, pos_g))]


def _rope_tables(start, n):
    half = D_HEAD // 2
    inv = ROPE_BASE ** (-jnp.arange(half, dtype=F32) / half)

    def trig(pos):
        ang = pos.astype(F32)[:, None] * inv[None, :]
        return jnp.cos(ang), jnp.sin(ang)

    if n % ROPE_BLOCK == 0:
        cb, sb = trig(start + ROPE_BLOCK * jnp.arange(n // ROPE_BLOCK, dtype=jnp.int32))
        co, so = trig(jnp.arange(ROPE_BLOCK, dtype=jnp.int32))
        cos = (cb[:, None] * co[None] - sb[:, None] * so[None]).reshape(n, half)
        sin = (sb[:, None] * co[None] + cb[:, None] * so[None]).reshape(n, half)
    else:
        cos, sin = trig(start + jnp.arange(n, dtype=jnp.int32))
    return jnp.concatenate([cos, cos], axis=-1), jnp.concatenate([-sin, sin], axis=-1)


def _router_weights(w_group, b_group, w_router, b_router):
    w = jnp.zeros((D_MODEL, LANES), F32)
    w = w.at[:, :N_EXPERTS].set(w_router).at[:, ROUTER_GROUP_LANE0:ROUTER_GROUP_LANE0 + N_GROUPS].set(w_group)
    b = jnp.zeros((1, LANES), F32)
    b = b.at[0, :N_EXPERTS].set(b_router).at[0, ROUTER_GROUP_LANE0:ROUTER_GROUP_LANE0 + N_GROUPS].set(b_group)
    return w.astype(BF16), b


def _tile(v, n):
    return jnp.tile(v.astype(F32), n)


def _run_trunk(x, B, L, mem_k, mem_v, ret_state0, band_past, P):
    prompt = band_past is None
    out_tm, out_groups = (OUT_PROJ_TM, OUT_PROJ_TM // OUT_PROJ_GROUP) if prompt else (OUT_PROJ_SAMPLE_TM, 1)
    proj_tm = min(B * L, PROJ_TM)
    scale = D_HEAD ** -0.5
    ones = lambda n: jnp.ones((n * D_HEAD,), F32)
    if prompt:
        cos2, sin2 = _rope_tables(0, L)
        tab_map = lambda i: i % (L // proj_tm)
    else:
        cos2, sin2 = _rope_tables(PAST_LEN, L)
        cos2, sin2 = jnp.tile(cos2, (B, 1)), jnp.tile(sin2, (B, 1))
        tab_map = lambda i: i
    out = {}

    def ffn(layer, mixed, proj, qm_block, w_out, x):
        def run(counts0):
            return out_proj(
                mixed, proj, qm_block, mem_k, mem_v, layer, w_out.astype(BF16), x, P["norm_ffn"][layer],
                *_router_weights(P["w_group"][layer], P["b_group"][layer], P["w_router"][layer],
                                 P["b_router"][layer]), counts0, tm=out_tm, n_groups=out_groups, rows_per_stream=L)
        return (yield run)

    segs0 = (("rope", N_HEADS), ("rope", N_HEADS), ("plain", N_HEADS), ("silu", N_HEADS), ("norm", H_MEM))
    gain0 = jnp.concatenate([ones(N_HEADS), ones(N_HEADS) * scale, ones(2 * N_HEADS),
                             _tile(P["q_norm_mem"][0], H_MEM) * scale])
    proj = norm_proj(x, P["norm_mix"][0], P["w_in_ret"][0].astype(BF16), gain0, segs0, tm=proj_tm,
                     out_dtype=BF16, rope=(cos2, sin2, tab_map))
    mixed, S = retention(proj, ret_state0, P["gn_ret"][0], B=B, L=L, C=RET_CHUNK if prompt else L)
    out["ret_state"] = S
    x = yield from ffn(0, mixed, proj, (4 * MIX_W) // MEM_W, P["w_out_ret"][0], x)

    segs1 = (("norm", N_HEADS), ("norm", N_HEADS), ("plain", N_HEADS), ("norm", H_MEM))
    gain1 = jnp.concatenate([_tile(P["q_norm_att"][0], N_HEADS) * scale, _tile(P["k_norm_att"][0], N_HEADS),
                             ones(N_HEADS), _tile(P["q_norm_mem"][1], H_MEM) * scale])
    w_in = P["w_in_att"][0].astype(BF16)
    if prompt:
        x, proj = combine_proj(x, P["norm_mix"][1], w_in, gain1, segs1)
        keep = min(BAND_PAST, L)
        tiles = L // keep
        kv = norm_proj(x, P["norm_mix"][1], w_in[:, MIX_W:3 * MIX_W], gain1[MIX_W:3 * MIX_W], segs1[1:3],
                       tm=keep, out_dtype=F32, n_tiles=B, row_map=lambda i: i * tiles + tiles - 1)
        out["band_k"] = kv[:, :MIX_W].reshape(B, keep, N_HEADS, D_HEAD)
        out["band_v"] = kv[:, MIX_W:].reshape(B, keep, N_HEADS, D_HEAD)
        mixed = band_prompt(proj, P["rel_bias_att"][0], B=B, L=L)
    else:
        proj_f = norm_proj(x, P["norm_mix"][1], w_in, gain1, segs1, tm=proj_tm, out_dtype=F32)
        out["band_k"] = proj_f[:, MIX_W:2 * MIX_W].reshape(B, L, N_HEADS, D_HEAD)
        out["band_v"] = proj_f[:, 2 * MIX_W:3 * MIX_W].reshape(B, L, N_HEADS, D_HEAD)
        proj = proj_f.astype(BF16)
        mixed = band_sample(proj, band_past[0], band_past[1], P["rel_bias_att"][0], B=B, L=L)
    out["y"] = yield from ffn(1, mixed, proj, (3 * MIX_W) // MEM_W, P["w_out_att"][0], x)
    return out


def kernel(x_prompt, x_sample, mem_prompt, state_ret, cache_band_k, cache_band_v, cache_mem_k, cache_mem_v, norm_mix, norm_ffn, norm_mem, w_in_ret, gn_ret, w_out_ret, w_in_att, q_norm_att, k_norm_att, rel_bias_att, w_out_att, w_mem_kv, q_norm_mem, k_norm_mem, w_group, b_group, w_router, b_router, w_e_gate, w_e_up, w_e_down):
    P = {"norm_mix": norm_mix, "norm_ffn": norm_ffn, "w_in_ret": w_in_ret, "gn_ret": gn_ret,
         "w_out_ret": w_out_ret, "w_in_att": w_in_att, "q_norm_att": q_norm_att,
         "k_norm_att": k_norm_att, "rel_bias_att": rel_bias_att, "w_out_att": w_out_att,
         "q_norm_mem": q_norm_mem, "w_group": w_group, "b_group": b_group, "w_router": w_router,
         "b_router": b_router, "w_e_gate": w_e_gate, "w_e_up": w_e_up, "w_e_down": w_e_down}
    B, L, D = x_prompt.shape
    Bd, Ld, _ = x_sample.shape
    depth = norm_mix.shape[0]

    mem_k_p, mem_v_p = [], []
    for i in range(depth):
        gain = jnp.concatenate([_tile(k_norm_mem[i], H_MEM), jnp.ones((MEM_W,), F32)])
        kv = norm_proj(mem_prompt.reshape(B * N_MEM, D), norm_mem[i], w_mem_kv[i].astype(BF16), gain,
                       (("norm", H_MEM), ("plain", H_MEM)), tm=N_MEM, out_dtype=F32)
        mem_k_p.append(kv[:, :MEM_W].reshape(B, N_MEM, H_MEM, D_HEAD))
        mem_v_p.append(kv[:, MEM_W:].reshape(B, N_MEM, H_MEM, D_HEAD))
    mem_k_p = jnp.stack(mem_k_p)
    mem_v_p = jnp.stack(mem_v_p)

    zeros_state = jnp.zeros((B, N_HEADS, D_HEAD, D_HEAD), F32)
    trunks = [_run_trunk(x_prompt.reshape(B * L, D), B, L, mem_k_p, mem_v_p, zeros_state, None, P),
              _run_trunk(x_sample.reshape(Bd * Ld, D), Bd, Ld, cache_mem_k, cache_mem_v, state_ret[0],
                         (cache_band_k[0], cache_band_v[0]), P)]
    pending = [next(t) for t in trunks]
    results = []
    for layer in range(depth):
        counts = jnp.zeros((SUBLANES, LANES), F32)
        groups = []
        for run in pending:
            x1, h2, route, route_t, counts = run(counts)
            groups.append((x1, h2, route, route_t))
        outs = hier_moe(groups, counts, layer, w_e_gate, w_e_up, w_e_down, defer=(0,) if layer == 0 else ())
        pending = []
        for t, x2 in zip(trunks, outs):
            try:
                pending.append(t.send(x2))
            except StopIteration as done:
                results.append(done.value)
    res_p, res_s = results

    return (res_p["y"].reshape(B, L, D), res_s["y"].reshape(Bd, Ld, D),
            res_p["ret_state"][None], res_s["ret_state"][None],
            res_p["band_k"][None], res_p["band_v"][None], res_s["band_k"][None], res_s["band_v"][None],
            mem_k_p, mem_v_p)
```

```python
import functools

import jax
import jax.numpy as jnp
from jax import lax
from jax.experimental import pallas as pl
from jax.experimental.pallas import tpu as pltpu

F32 = jnp.float32
BF16 = jnp.bfloat16

D_MODEL = 1024
D_HEAD = 128
CHUNK = 64
N_HEADS = 8
H_MEM = 4
N_MEM = 256
MEM_W = H_MEM * D_HEAD
MIX_W = N_HEADS * D_HEAD
N_PREV_CHUNKS = 8
BAND_PAST = N_PREV_CHUNKS * CHUNK
REL_CLIP = 128
N_GROUPS = 4
EXPERTS_PER_GROUP = 8
N_EXPERTS = N_GROUPS * EXPERTS_PER_GROUP
D_FF_EXPERT = 512
ROPE_BASE = 10000.0
EPS = 1e-6
NEG_INF = -1e30
PAST_LEN = 1024

LANES = 128
SUBLANES = 8

ROPE_BLOCK = 128
RET_CHUNK = 256
BAND_TQ = 512
BAND_SUB = 128
BAND_WIN = BAND_PAST + BAND_SUB
MOE_BM = 512
DISPATCH_TM = 4096
COMBINE_TM = 1024
PROJ_TM = 1024
OUT_PROJ_TM = 1024
OUT_PROJ_GROUP = 512
OUT_PROJ_SAMPLE_TM = 128
ROUTE_ROWS = 128
ROUTER_GROUP_LANE0 = N_EXPERTS
V7X_VMEM_BYTES = 64 * 1024 * 1024
VMEM_LIMIT = V7X_VMEM_BYTES - 8 * 1024 * 1024


def _cparams(sem):
    return pltpu.CompilerParams(dimension_semantics=sem, vmem_limit_bytes=VMEM_LIMIT)


def _dot(a, b):
    return jnp.dot(a, b, preferred_element_type=F32)


def _dot_nt(a, b):
    return lax.dot_general(a, b, (((1,), (1,)), ((), ())), preferred_element_type=F32)


def _sigmoid(x):
    return 1.0 / (1.0 + jnp.exp(-x))


ROW_TILE = D_MODEL // LANES
assert ROW_TILE == SUBLANES


def _load_token_tiles(ref, t0, n):
    return jnp.concatenate([ref[pl.ds(t0 * ROW_TILE + j, n, stride=ROW_TILE), :] for j in range(ROW_TILE)], axis=1)


def _store_token_tiles(ref, t0, val):
    n = val.shape[0]
    for j in range(ROW_TILE):
        ref[pl.ds(t0 * ROW_TILE + j, n, stride=ROW_TILE), :] = val[:, j * LANES:(j + 1) * LANES]


def _token_tile(ref, t):
    return ref.at[pl.ds(pl.multiple_of(t * ROW_TILE, ROW_TILE), ROW_TILE)]


def _norm_proj_kernel(segs, has_rope, x_ref, g_ref, w_ref, gain_ref, *rest):
    if has_rope:
        cos_ref, sin_ref, o_ref = rest
    else:
        (o_ref,) = rest
    x = x_ref[...]
    ms = jnp.mean(x * x, axis=-1, keepdims=True)
    h = (x * lax.rsqrt(ms + EPS) * g_ref[...]).astype(BF16)
    col = 0
    for kind, n_heads in segs:
        width = n_heads * D_HEAD
        acc = _dot(h, w_ref[:, col:col + width])
        for j in range(n_heads):
            a = acc[:, j * D_HEAD:(j + 1) * D_HEAD]
            c0 = col + j * D_HEAD
            gain = gain_ref[:, c0:c0 + D_HEAD]
            if kind == "rope":
                rot = pltpu.roll(a, D_HEAD // 2, 1)
                out = (a * cos_ref[...] + rot * sin_ref[...]) * gain
            elif kind == "norm":
                out = a * lax.rsqrt(jnp.mean(a * a, axis=-1, keepdims=True) + EPS) * gain
            elif kind == "silu":
                out = a * _sigmoid(a)
            else:
                out = a
            o_ref[:, c0:c0 + D_HEAD] = out.astype(o_ref.dtype)
        col += width


def norm_proj(x, g, w, gain, segs, *, tm, out_dtype, n_tiles=None, row_map=None, rope=None):
    T, D = x.shape
    N = w.shape[1]
    assert sum(n for _, n in segs) * D_HEAD == N
    if n_tiles is None:
        n_tiles = T // tm
    if row_map is None:
        row_map = lambda i: i
    in_specs = [
        pl.BlockSpec((tm, D), lambda i: (row_map(i), 0)),
        pl.BlockSpec((1, D), lambda i: (0, 0)),
        pl.BlockSpec((D, N), lambda i: (0, 0)),
        pl.BlockSpec((1, N), lambda i: (0, 0)),
    ]
    args = [x, g.reshape(1, D), w, gain.reshape(1, N)]
    if rope is not None:
        cos2, sin2, tab_map = rope
        in_specs += [pl.BlockSpec((tm, D_HEAD), lambda i: (tab_map(i), 0)),
                     pl.BlockSpec((tm, D_HEAD), lambda i: (tab_map(i), 0))]
        args += [cos2, sin2]
    return pl.pallas_call(
        functools.partial(_norm_proj_kernel, segs, rope is not None),
        grid=(n_tiles,),
        in_specs=in_specs,
        out_specs=pl.BlockSpec((tm, N), lambda i: (i, 0)),
        out_shape=jax.ShapeDtypeStruct((n_tiles * tm, N), out_dtype),
        compiler_params=_cparams(("arbitrary",)),
        name="norm_proj",
    )(*args)


def _retention_kernel(q_ref, k_ref, v_ref, g_ref, s0_ref, dmask_ref, qdec_ref, kdec_ref, gc_ref,
                      gn_ref, o_ref, s_ref):
    @pl.when(pl.program_id(1) == 0)
    def _():
        s_ref[...] = s0_ref[...]

    for h in range(N_HEADS):
        sl = slice(h * D_HEAD, (h + 1) * D_HEAD)
        q = q_ref[:, sl]
        k = k_ref[:, sl]
        v = v_ref[:, sl]
        S = s_ref[0, h]
        scores = _dot_nt(q, k) * dmask_ref[h]
        intra = _dot(scores.astype(BF16), v)
        cross = _dot(q, S.astype(BF16)) * qdec_ref[:, sl]
        o = intra + cross
        kd = (k.astype(F32) * kdec_ref[:, sl]).astype(BF16)
        s_ref[0, h] = gc_ref[h] * S + lax.dot_general(kd, v, (((0,), (0,)), ((), ())), preferred_element_type=F32)
        mu = jnp.mean(o, axis=-1, keepdims=True)
        d = o - mu
        var = jnp.mean(d * d, axis=-1, keepdims=True)
        y = d * lax.rsqrt(var + EPS) * gn_ref[:, sl] * g_ref[:, sl].astype(F32)
        o_ref[:, sl] = y.astype(o_ref.dtype)


def _retention_tables(C):
    log_gamma = jnp.log1p(-jnp.exp2(-5.0 - jnp.arange(N_HEADS, dtype=F32)))
    idx = jnp.arange(C, dtype=F32)
    diff = idx[:, None] - idx[None, :]
    dmask = jnp.where(diff[None] >= 0,
                      jnp.exp(log_gamma[:, None, None] * jnp.maximum(diff, 0.0)[None]), 0.0)
    q_dec = jnp.exp(log_gamma[None, :] * (idx + 1.0)[:, None])
    k_dec = jnp.exp(log_gamma[None, :] * (C - 1.0 - idx)[:, None])
    g_c = jnp.exp(log_gamma * C)
    rep = lambda t: jnp.repeat(t, D_HEAD, axis=-1)
    return dmask, rep(q_dec), rep(k_dec), rep(g_c[:, None])[:, None, :]


def retention(proj, s0, gn, *, B, L, C):
    nc = L // C
    dmask, qdec, kdec, gc = _retention_tables(C)
    row = lambda col: pl.BlockSpec((C, MIX_W), lambda b, c: (b * nc + c, col))
    const = lambda shape: pl.BlockSpec(shape, lambda b, c: (0,) * len(shape))
    state_spec = pl.BlockSpec((1, N_HEADS, D_HEAD, D_HEAD), lambda b, c: (b, 0, 0, 0))
    return pl.pallas_call(
        _retention_kernel,
        grid=(B, nc),
        in_specs=[row(0), row(1), row(2), row(3), state_spec,
                  const((N_HEADS, C, C)), const((C, MIX_W)), const((C, MIX_W)),
                  const((N_HEADS, 1, D_HEAD)), const((1, MIX_W))],
        out_specs=[pl.BlockSpec((C, MIX_W), lambda b, c: (b * nc + c, 0)), state_spec],
        out_shape=[jax.ShapeDtypeStruct((B * L, MIX_W), BF16),
                   jax.ShapeDtypeStruct((B, N_HEADS, D_HEAD, D_HEAD), F32)],
        compiler_params=_cparams(("arbitrary", "arbitrary")),
        name="retention",
    )(proj, proj, proj, proj, s0, dmask, qdec, kdec, gc, gn.reshape(1, MIX_W))


def _attend_two_piece(q, k_a, k_b, v_a, v_b, bias_a, bias_b):
    s_a = _dot_nt(q, k_a) + bias_a
    s_b = _dot_nt(q, k_b) + bias_b
    m = jnp.maximum(jnp.max(s_a, axis=-1, keepdims=True), jnp.max(s_b, axis=-1, keepdims=True))
    p_a = jnp.exp(s_a - m)
    p_b = jnp.exp(s_b - m)
    denom = jnp.sum(p_a, axis=-1, keepdims=True) + jnp.sum(p_b, axis=-1, keepdims=True)
    o = _dot(p_a.astype(BF16), v_a) + _dot(p_b.astype(BF16), v_b)
    return o / denom


def _band_prompt_kernel(q_ref, kp_ref, kc_ref, vp_ref, vc_ref, bias_ref, o_ref, s_buf, p_buf):
    @pl.when((pl.program_id(0) == 0) & (pl.program_id(1) == 0))
    def _():
        p_buf[...] = jnp.zeros_like(p_buf)

    prev_off = jnp.where(pl.program_id(1) > 0, 0.0, NEG_INF).astype(F32)
    for h in range(N_HEADS):
        sl = slice(h * D_HEAD, (h + 1) * D_HEAD)
        q = q_ref[:, sl]
        s_buf[:, :BAND_TQ] = _dot_nt(q, kp_ref[:, sl]) + prev_off
        s_buf[:, BAND_TQ:] = _dot_nt(q, kc_ref[:, sl])
        denoms = []
        for s in range(BAND_TQ // BAND_SUB):
            r0 = s * BAND_SUB
            sc = s_buf[r0:r0 + BAND_SUB, r0:r0 + BAND_WIN] + bias_ref[h]
            p = jnp.exp(sc - jnp.max(sc, axis=-1, keepdims=True))
            denoms.append(jnp.sum(p, axis=-1, keepdims=True))
            p_buf[r0:r0 + BAND_SUB, r0:r0 + BAND_WIN] = p.astype(BF16)
        o = _dot(p_buf[:, :BAND_TQ], vp_ref[:, sl]) + _dot(p_buf[:, BAND_TQ:], vc_ref[:, sl])
        o_ref[:, sl] = (o / jnp.concatenate(denoms, axis=0)).astype(o_ref.dtype)


def _rel_bias(bias_table, offset, n_rows, width):
    n = n_rows + width - 1
    v = jnp.arange(n, dtype=jnp.int32)
    v = jnp.where(v < width, v, v - n)
    diag = bias_table.astype(F32)[:, jnp.clip(offset - v, -REL_CLIP, REL_CLIP) + REL_CLIP]
    cut = jnp.tile(diag, (1, n_rows))[:, :n_rows * (n - 1)].reshape(diag.shape[0], n_rows, n - 1)
    return cut[:, :, :width]


def _band_prompt_bias(bias_table):
    r = jnp.arange(BAND_SUB, dtype=jnp.int32)[:, None]
    j = jnp.arange(BAND_WIN, dtype=jnp.int32)[None, :]
    q_chunk = r // CHUNK + N_PREV_CHUNKS
    k_chunk = j // CHUNK
    allowed = (k_chunk <= q_chunk) & (k_chunk >= q_chunk - N_PREV_CHUNKS)
    return jnp.where(allowed[None], _rel_bias(bias_table, BAND_PAST, BAND_SUB, BAND_WIN), NEG_INF)


def band_prompt(proj, bias_table, *, B, L):
    nq = L // BAND_TQ
    cur = lambda col: pl.BlockSpec((BAND_TQ, MIX_W), lambda b, i: (b * nq + i, col))
    prev = lambda col: pl.BlockSpec((BAND_TQ, MIX_W), lambda b, i: (b * nq + jnp.maximum(i - 1, 0), col))
    return pl.pallas_call(
        _band_prompt_kernel,
        grid=(B, nq),
        in_specs=[cur(0), prev(1), cur(1), prev(2), cur(2),
                  pl.BlockSpec((N_HEADS, BAND_SUB, BAND_WIN), lambda b, i: (0, 0, 0))],
        out_specs=pl.BlockSpec((BAND_TQ, MIX_W), lambda b, i: (b * nq + i, 0)),
        out_shape=jax.ShapeDtypeStruct((B * L, MIX_W), BF16),
        scratch_shapes=[pltpu.VMEM((BAND_TQ, 2 * BAND_TQ), F32), pltpu.VMEM((BAND_TQ, 2 * BAND_TQ), BF16)],
        compiler_params=_cparams(("arbitrary", "arbitrary")),
        name="band_prompt",
    )(proj, proj, proj, proj, proj, _band_prompt_bias(bias_table))


def _band_sample_kernel(q_ref, kp_ref, kc_ref, vp_ref, vc_ref, bias_p_ref, bias_c_ref, o_ref):
    n_past = kp_ref.shape[1] // N_HEADS
    for h in range(N_HEADS):
        sl = slice(h * D_HEAD, (h + 1) * D_HEAD)
        kp = kp_ref.at[0][pl.ds(h, n_past, stride=N_HEADS), :].astype(BF16)
        vp = vp_ref.at[0][pl.ds(h, n_past, stride=N_HEADS), :].astype(BF16)
        o = _attend_two_piece(q_ref[:, sl], kp, kc_ref[:, sl], vp, vc_ref[:, sl], bias_p_ref[h], bias_c_ref[h])
        o_ref[:, sl] = o.astype(o_ref.dtype)


def band_sample(proj, past_k, past_v, bias_table, *, B, L):
    P = past_k.shape[1]
    qpos = PAST_LEN + jnp.arange(L, dtype=jnp.int32)
    kpos = jnp.concatenate([PAST_LEN - P + jnp.arange(P, dtype=jnp.int32), qpos])
    qc = (qpos // CHUNK)[:, None]
    kc = (kpos // CHUNK)[None, :]
    allowed = (kpos[None, :] >= 0) & (kc <= qc) & (kc >= qc - N_PREV_CHUNKS)
    rel_bias = jnp.concatenate([_rel_bias(bias_table, P, L, P), _rel_bias(bias_table, 0, L, L)], axis=-1)
    bias = jnp.where(allowed[None], rel_bias, NEG_INF)
    cur = lambda col: pl.BlockSpec((L, MIX_W), lambda b: (b, col))
    past = pl.BlockSpec((1, P * N_HEADS, D_HEAD), lambda b: (b, 0, 0))
    past_k = past_k.reshape(B, P * N_HEADS, D_HEAD)
    past_v = past_v.reshape(B, P * N_HEADS, D_HEAD)
    return pl.pallas_call(
        _band_sample_kernel,
        grid=(B,),
        in_specs=[cur(0), past, cur(1), past, cur(2),
                  pl.BlockSpec((N_HEADS, L, P), lambda b: (0, 0, 0)),
                  pl.BlockSpec((N_HEADS, L, L), lambda b: (0, 0, 0))],
        out_specs=pl.BlockSpec((L, MIX_W), lambda b: (b, 0)),
        out_shape=jax.ShapeDtypeStruct((B * L, MIX_W), BF16),
        compiler_params=_cparams(("arbitrary",)),
        name="band_sample",
    )(proj, past_k, proj, past_v, proj, bias[:, :, :P], bias[:, :, P:])


def _out_proj_kernel(n_streams, n_groups, o_ref, qm_ref, mk_ref, mv_ref, w_ref, x_ref, gf_ref, wr_ref, br_ref,
                     tri_ref, cnt0_ref, x1_ref, h2_ref, route_ref, route_t_ref, cnt_ref):
    @pl.when(pl.program_id(0) == 0)
    def _():
        cnt_ref[...] = cnt0_ref[...]

    tm = x_ref.shape[0]
    sr = tm // n_streams
    gr = tm // n_groups
    om_heads = []
    for h in range(H_MEM):
        sl = slice(h * D_HEAD, (h + 1) * D_HEAD)
        chunks = []
        for c in range(n_streams):
            mk = mk_ref.at[0, c][pl.ds(h, N_MEM, stride=H_MEM), :].astype(BF16)
            mv = mv_ref.at[0, c][pl.ds(h, N_MEM, stride=H_MEM), :].astype(BF16)
            s = _dot_nt(qm_ref[c * sr:(c + 1) * sr, sl], mk)
            p = jnp.exp(s - jnp.max(s, axis=-1, keepdims=True))
            om = _dot(p.astype(BF16), mv) / jnp.sum(p, axis=-1, keepdims=True)
            chunks.append(om.astype(BF16))
        om_heads.append(chunks[0] if n_streams == 1 else jnp.concatenate(chunks, axis=0))

    rows = min(gr, ROUTE_ROWS)
    lane = lax.broadcasted_iota(jnp.int32, (rows, LANES), 1)
    for g in range(n_groups):
        g0 = g * gr
        feat = jnp.concatenate([o_ref[g0:g0 + gr, :]] + [om[g0:g0 + gr] for om in om_heads], axis=1)
        x1 = x_ref[g0:g0 + gr, :] + _dot(feat, w_ref[...])
        x1_ref[g0:g0 + gr, :] = x1
        hn = x1 * lax.rsqrt(jnp.mean(x1 * x1, axis=-1, keepdims=True) + EPS) * gf_ref[...]
        _store_token_tiles(h2_ref, g0, hn)

        logits = _dot(hn.astype(BF16), wr_ref[...]) + br_ref[...]
        tops = [_route_top2(logits[r0:r0 + rows]) for r0 in range(0, gr, rows)]

        onehot = jnp.concatenate([jnp.where(t[4] | t[5], 1.0, 0.0) for t in tops], axis=0)
        before = cnt_ref[0:1, :] + _dot(tri_ref[...], onehot.astype(BF16))
        cnt_ref[0:1, :] = cnt_ref[0:1, :] + jnp.sum(onehot, axis=0, keepdims=True)
        for k, (i1, i2, w1, w2, oh1, oh2) in enumerate(tops):
            b = before[k * rows:(k + 1) * rows]
            r1 = jnp.sum(jnp.where(oh1, b, 0.0), axis=-1, keepdims=True)
            r2 = jnp.sum(jnp.where(oh2, b, 0.0), axis=-1, keepdims=True)
            route = jnp.where(lane == 0, i1, jnp.where(lane == 1, i2, jnp.where(lane == 2, w1, jnp.where(
                lane == 3, w2, jnp.where(lane == 4, r1, jnp.where(lane == 5, r2, 0.0))))))
            c0 = g0 + k * rows
            route_ref[c0:c0 + rows, :] = route
            route_t_ref[:, c0:c0 + rows] = route.T[:SUBLANES, :]


def _route_top2(logits):
    lane = lax.broadcasted_iota(jnp.int32, logits.shape, 1)
    lane_f = lane.astype(F32)
    far = float(LANES)
    is_group = (lane >= ROUTER_GROUP_LANE0) & (lane < ROUTER_GROUP_LANE0 + N_GROUPS)
    gl = jnp.where(is_group, logits, NEG_INF)
    g_max = jnp.max(gl, axis=-1, keepdims=True)
    g_sel = jnp.min(jnp.where(gl == g_max, lane_f, far), axis=-1, keepdims=True) - float(ROUTER_GROUP_LANE0)
    g_gate = 1.0 / jnp.sum(jnp.exp(gl - g_max), axis=-1, keepdims=True)
    in_group = jnp.right_shift(lane, EXPERTS_PER_GROUP.bit_length() - 1).astype(F32) == g_sel
    el = jnp.where(in_group, logits, NEG_INF)
    m1 = jnp.max(el, axis=-1, keepdims=True)
    i1 = jnp.min(jnp.where(el == m1, lane_f, far), axis=-1, keepdims=True)
    el2 = jnp.where(lane_f == i1, NEG_INF, el)
    m2 = jnp.max(el2, axis=-1, keepdims=True)
    i2 = jnp.min(jnp.where(el2 == m2, lane_f, far), axis=-1, keepdims=True)
    e2 = jnp.exp(m2 - m1)
    p1 = 1.0 / (1.0 + e2)
    return i1, i2, g_gate * p1, g_gate * (e2 * p1), lane_f == i1, lane_f == i2


def out_proj(mixed, proj, qm_block, mem_k, mem_v, layer, w_out, x, g_ffn, w_router, b_router, counts0, *, tm,
             n_groups, rows_per_stream):
    T = x.shape[0]
    n_streams = max(1, tm // rows_per_stream)
    tiles_per_stream = max(1, rows_per_stream // tm)
    gr = tm // n_groups
    assert min(gr, ROUTE_ROWS) % LANES == 0
    row = lambda width, col: pl.BlockSpec((tm, width), lambda i: (i, col))
    const = lambda shape: pl.BlockSpec(shape, lambda i: (0,) * len(shape))
    mem = pl.BlockSpec((1, n_streams, N_MEM * H_MEM, D_HEAD), lambda i: (layer, i // tiles_per_stream, 0, 0))
    mem_k = mem_k.reshape(mem_k.shape[:2] + (N_MEM * H_MEM, D_HEAD))
    mem_v = mem_v.reshape(mem_v.shape[:2] + (N_MEM * H_MEM, D_HEAD))
    tri = (jnp.arange(gr)[None, :] < jnp.arange(gr)[:, None]).astype(BF16)
    return pl.pallas_call(
        functools.partial(_out_proj_kernel, n_streams, n_groups),
        grid=(T // tm,),
        in_specs=[row(MIX_W, 0), row(MEM_W, qm_block), mem, mem, const((MIX_W + MEM_W, D_MODEL)),
                  row(D_MODEL, 0), const((1, D_MODEL)), const((D_MODEL, LANES)), const((1, LANES)),
                  const((gr, gr)), const((SUBLANES, LANES))],
        out_specs=[row(D_MODEL, 0), pl.BlockSpec((tm * ROW_TILE, LANES), lambda i: (i, 0)), row(LANES, 0),
                   pl.BlockSpec((SUBLANES, tm), lambda i: (0, i)), const((SUBLANES, LANES))],
        out_shape=[jax.ShapeDtypeStruct((T, D_MODEL), F32), jax.ShapeDtypeStruct((T * ROW_TILE, LANES), F32),
                   jax.ShapeDtypeStruct((T, LANES), F32), jax.ShapeDtypeStruct((SUBLANES, T), F32),
                   jax.ShapeDtypeStruct((SUBLANES, LANES), F32)],
        compiler_params=_cparams(("arbitrary",)),
        name="out_proj",
    )(mixed, proj, mem_k, mem_v, w_out, x, g_ffn.reshape(1, D_MODEL), w_router, b_router, tri, counts0)


def _dispatch_kernel(tms, steps, pe_ref, pos0_ref, pos1_ref, *rest):
    h2_refs = rest[:len(tms)]
    xpad_hbm, zbuf, sem, zsem = rest[len(tms):]
    i = pl.program_id(0)

    @pl.when(i == 0)
    def _():
        zbuf[...] = jnp.zeros_like(zbuf)

        def block_copy(first_row):
            start = pl.multiple_of(first_row * ROW_TILE, MOE_BM * ROW_TILE)
            return pltpu.make_async_copy(zbuf, xpad_hbm.at[pl.ds(start, MOE_BM * ROW_TILE)], zsem)

        def tail_copy(e):
            return block_copy(pe_ref[e] - MOE_BM)

        def nonempty(e):
            return pe_ref[e] > (pe_ref[e - 1] if e > 0 else 0)

        for e in range(N_EXPERTS):
            pl.when(nonempty(e))(lambda e=e: tail_copy(e).start(priority=e % 2))
        for e in range(N_EXPERTS):
            pl.when(nonempty(e))(lambda e=e: tail_copy(e).wait())

        def fill(b, carry):
            cp = block_copy(b * MOE_BM)
            cp.start()
            cp.wait()
            return carry

        lax.fori_loop(pe_ref[N_EXPERTS - 1] // MOE_BM, xpad_hbm.shape[0] // (MOE_BM * ROW_TILE), fill, 0)

    def scatter_rows(h2_ref, tm):
        def body(r, carry):
            src = _token_tile(h2_ref, r)
            pltpu.make_async_copy(src, _token_tile(xpad_hbm, pos0_ref[r]), sem).start(priority=0)
            pltpu.make_async_copy(src, _token_tile(xpad_hbm, pos1_ref[r]), sem).start(priority=1)
            return carry

        lax.fori_loop(0, tm, body, 0, unroll=8)
        for _ in range(2):
            pltpu.make_async_copy(h2_ref, xpad_hbm.at[pl.ds(0, tm * ROW_TILE)], sem).wait()

    first = 0
    for h2_ref, tm, n in zip(h2_refs, tms, steps):
        pl.when(jnp.logical_and(i >= first, i < first + n))(functools.partial(scatter_rows, h2_ref, tm))
        first += n


def moe_dispatch(pad_ends, pos_groups, h2_groups, n_rows):
    sizes = [h2.shape[0] // ROW_TILE for h2 in h2_groups]
    tms = [min(T, DISPATCH_TM) for T in sizes]
    steps = [T // tm for T, tm in zip(sizes, tms)]
    firsts = [sum(steps[:g]) for g in range(len(steps))]
    pos0, pos1 = (jnp.concatenate([jnp.pad(p[k], (0, n * DISPATCH_TM - p[k].shape[0]))
                                   for p, n in zip(pos_groups, steps)]) for k in range(2))

    def rows(tm, first, n):
        return pl.BlockSpec((tm * ROW_TILE, LANES), lambda i, pe: (jnp.clip(i - first, 0, n - 1), 0))

    grid_spec = pltpu.PrefetchScalarGridSpec(
        num_scalar_prefetch=1,
        grid=(sum(steps),),
        in_specs=[pl.BlockSpec((DISPATCH_TM,), lambda i, pe: (i,), memory_space=pltpu.SMEM)] * 2
        + [rows(tm, first, n) for tm, first, n in zip(tms, firsts, steps)],
        out_specs=pl.BlockSpec(memory_space=pl.ANY),
        scratch_shapes=[pltpu.VMEM((MOE_BM * ROW_TILE, LANES), F32), pltpu.SemaphoreType.DMA(()),
                        pltpu.SemaphoreType.DMA(())],
    )
    return pl.pallas_call(
        functools.partial(_dispatch_kernel, tuple(tms), tuple(steps)),
        grid_spec=grid_spec,
        out_shape=jax.ShapeDtypeStruct((n_rows * ROW_TILE, LANES), F32),
        compiler_params=_cparams(("arbitrary",)),
        name="moe_dispatch",
    )(pad_ends, pos0, pos1, *h2_groups)


def _moe_kernel(be_ref, nu_ref, x_ref, wg_ref, wu_ref, wd_ref, y_ref):
    used = pl.program_id(0) < nu_ref[0]

    @pl.when(used)
    def _():
        x = _load_token_tiles(x_ref, 0, MOE_BM).astype(BF16)
        g = _dot(x, wg_ref[0, 0].astype(BF16))
        u = _dot(x, wu_ref[0, 0].astype(BF16))
        a = (g * _sigmoid(g) * u).astype(BF16)
        _store_token_tiles(y_ref, 0, _dot(a, wd_ref[0, 0].astype(BF16)))

    @pl.when(jnp.logical_not(used))
    def _():
        y_ref[...] = jnp.zeros_like(y_ref)


def moe_experts(x_pad, block_e, n_used, layer, wg, wu, wd):
    n_rows = x_pad.shape[0] // ROW_TILE
    rows = pl.BlockSpec((MOE_BM * ROW_TILE, LANES), lambda i, be, nu: (i, 0))
    w_in = pl.BlockSpec((1, 1, D_MODEL, D_FF_EXPERT), lambda i, be, nu: (layer, be[i], 0, 0))
    w_dn = pl.BlockSpec((1, 1, D_FF_EXPERT, D_MODEL), lambda i, be, nu: (layer, be[i], 0, 0))
    grid_spec = pltpu.PrefetchScalarGridSpec(
        num_scalar_prefetch=2,
        grid=(n_rows // MOE_BM,),
        in_specs=[pl.BlockSpec((MOE_BM * ROW_TILE, LANES), lambda i, be, nu: (jnp.minimum(i, nu[0] - 1), 0)),
                  w_in, w_in, w_dn],
        out_specs=rows,
    )
    return pl.pallas_call(
        _moe_kernel,
        grid_spec=grid_spec,
        out_shape=jax.ShapeDtypeStruct((n_rows * ROW_TILE, LANES), F32),
        compiler_params=_cparams(("arbitrary",)),
        name="moe_experts",
    )(block_e, n_used, x_pad, wg, wu, wd)


def _combine_kernel(tm, n_steps, pos0_ref, pos1_ref, pos0_next_ref, pos1_next_ref, route_ref, x1_ref, ypad_hbm,
                    o_ref, ybuf, sems):
    i = pl.program_id(0)
    slot = lax.rem(i, 2)

    def gather(p0_ref, p1_ref, s):
        buf = ybuf.at[s]

        def body(r, carry):
            pltpu.make_async_copy(_token_tile(ypad_hbm, p0_ref[r]), _token_tile(buf, r), sems.at[s]).start(
                priority=0)
            pltpu.make_async_copy(_token_tile(ypad_hbm, p1_ref[r]), _token_tile(buf, tm + r), sems.at[s]).start(
                priority=1)
            return carry

        lax.fori_loop(0, tm, body, 0, unroll=8)

    @pl.when(i == 0)
    def _():
        gather(pos0_ref, pos1_ref, 0)

    @pl.when(i + 1 < n_steps)
    def _():
        gather(pos0_next_ref, pos1_next_ref, 1 - slot)

    buf = ybuf.at[slot]
    pltpu.make_async_copy(ypad_hbm.at[pl.ds(0, 2 * tm * ROW_TILE)], buf, sems.at[slot]).wait()
    g1 = route_ref[:, 2:3]
    g2 = route_ref[:, 3:4]
    o_ref[...] = x1_ref[...] + (g1 * _load_token_tiles(buf, 0, tm) + g2 * _load_token_tiles(buf, tm, tm))


def moe_combine(pos, route, x1, y_pad):
    pos0, pos1 = pos
    T = x1.shape[0]
    tm = min(T, COMBINE_TM)
    n_steps = T // tm
    return pl.pallas_call(
        functools.partial(_combine_kernel, tm, n_steps),
        grid=(n_steps,),
        in_specs=[pl.BlockSpec((tm,), lambda i: (i,), memory_space=pltpu.SMEM)] * 2
        + [pl.BlockSpec((tm,), lambda i: (jnp.minimum(i + 1, n_steps - 1),), memory_space=pltpu.SMEM)] * 2
        + [pl.BlockSpec((tm, LANES), lambda i: (i, 0)), pl.BlockSpec((tm, D_MODEL), lambda i: (i, 0)),
           pl.BlockSpec(memory_space=pl.ANY)],
        out_specs=pl.BlockSpec((tm, D_MODEL), lambda i: (i, 0)),
        out_shape=jax.ShapeDtypeStruct((T, D_MODEL), F32),
        scratch_shapes=[pltpu.VMEM((2, 2 * tm * ROW_TILE, LANES), F32), pltpu.SemaphoreType.DMA((2,))],
        compiler_params=_cparams(("arbitrary",)),
        name="moe_combine",
    )(pos0, pos1, pos0, pos1, route, x1, y_pad)


def hier_moe(groups, counts, layer, wg, wu, wd):
    sizes = [g[0].shape[0] for g in groups]
    n_assign = 2 * sum(sizes)
    n_blocks = (n_assign + N_EXPERTS * (MOE_BM - 1) + MOE_BM - 1) // MOE_BM
    fields = jnp.concatenate([g[3] for g in groups], axis=1)
    experts = fields[0:2].astype(jnp.int32)
    ranks = fields[4:6].astype(jnp.int32)
    cnt = counts[0, :N_EXPERTS].astype(jnp.int32)
    padded = ((cnt + MOE_BM - 1) // MOE_BM) * MOE_BM
    pad_ends = jnp.cumsum(padded).astype(jnp.int32)
    pad_starts = pad_ends - padded
    is_e = experts[:, :, None] == jnp.arange(N_EXPERTS, dtype=jnp.int32)
    pos = ranks + jnp.sum(jnp.where(is_e, pad_starts, 0), axis=-1)
    block_start = jnp.arange(n_blocks, dtype=jnp.int32) * MOE_BM
    block_e = jnp.minimum(jnp.sum(block_start[:, None] >= pad_ends[None, :], axis=1), N_EXPERTS - 1).astype(jnp.int32)
    n_used = (pad_ends[-1:] // MOE_BM).astype(jnp.int32)
    starts = [sum(sizes[:g]) for g in range(len(groups))]
    pos_g = [(pos[0, s:s + n], pos[1, s:s + n]) for s, n in zip(starts, sizes)]
    x_pad = moe_dispatch(pad_ends, pos_g, [g[1] for g in groups], n_blocks * MOE_BM)
    y_pad = moe_experts(x_pad, block_e, n_used, layer, wg, wu, wd)
    return [moe_combine(p, g[2], g[0], y_pad) for g, p in zip(groups, pos_g)]


def _rope_tables(start, n):
    half = D_HEAD // 2
    inv = ROPE_BASE ** (-jnp.arange(half, dtype=F32) / half)

    def trig(pos):
        ang = pos.astype(F32)[:, None] * inv[None, :]
        return jnp.cos(ang), jnp.sin(ang)

    if n % ROPE_BLOCK == 0:
        cb, sb = trig(start + ROPE_BLOCK * jnp.arange(n // ROPE_BLOCK, dtype=jnp.int32))
        co, so = trig(jnp.arange(ROPE_BLOCK, dtype=jnp.int32))
        cos = (cb[:, None] * co[None] - sb[:, None] * so[None]).reshape(n, half)
        sin = (sb[:, None] * co[None] + cb[:, None] * so[None]).reshape(n, half)
    else:
        cos, sin = trig(start + jnp.arange(n, dtype=jnp.int32))
    return jnp.concatenate([cos, cos], axis=-1), jnp.concatenate([-sin, sin], axis=-1)


def _router_weights(w_group, b_group, w_router, b_router):
    w = jnp.zeros((D_MODEL, LANES), F32)
    w = w.at[:, :N_EXPERTS].set(w_router).at[:, ROUTER_GROUP_LANE0:ROUTER_GROUP_LANE0 + N_GROUPS].set(w_group)
    b = jnp.zeros((1, LANES), F32)
    b = b.at[0, :N_EXPERTS].set(b_router).at[0, ROUTER_GROUP_LANE0:ROUTER_GROUP_LANE0 + N_GROUPS].set(b_group)
    return w.astype(BF16), b


def _tile(v, n):
    return jnp.tile(v.astype(F32), n)


def _run_trunk(x, B, L, mem_k, mem_v, ret_state0, band_past, P):
    prompt = band_past is None
    out_tm, out_groups = (OUT_PROJ_TM, OUT_PROJ_TM // OUT_PROJ_GROUP) if prompt else (OUT_PROJ_SAMPLE_TM, 1)
    proj_tm = min(B * L, PROJ_TM)
    scale = D_HEAD ** -0.5
    ones = lambda n: jnp.ones((n * D_HEAD,), F32)
    if prompt:
        cos2, sin2 = _rope_tables(0, L)
        tab_map = lambda i: i % (L // proj_tm)
    else:
        cos2, sin2 = _rope_tables(PAST_LEN, L)
        cos2, sin2 = jnp.tile(cos2, (B, 1)), jnp.tile(sin2, (B, 1))
        tab_map = lambda i: i
    out = {}

    def ffn(layer, mixed, proj, qm_block, w_out, x):
        def run(counts0):
            return out_proj(
                mixed, proj, qm_block, mem_k, mem_v, layer, w_out.astype(BF16), x, P["norm_ffn"][layer],
                *_router_weights(P["w_group"][layer], P["b_group"][layer], P["w_router"][layer],
                                 P["b_router"][layer]), counts0, tm=out_tm, n_groups=out_groups, rows_per_stream=L)
        return (yield run)

    segs0 = (("rope", N_HEADS), ("rope", N_HEADS), ("plain", N_HEADS), ("silu", N_HEADS), ("norm", H_MEM))
    gain0 = jnp.concatenate([ones(N_HEADS), ones(N_HEADS) * scale, ones(2 * N_HEADS),
                             _tile(P["q_norm_mem"][0], H_MEM) * scale])
    proj = norm_proj(x, P["norm_mix"][0], P["w_in_ret"][0].astype(BF16), gain0, segs0, tm=proj_tm,
                     out_dtype=BF16, rope=(cos2, sin2, tab_map))
    mixed, S = retention(proj, ret_state0, P["gn_ret"][0], B=B, L=L, C=RET_CHUNK if prompt else L)
    out["ret_state"] = S
    x = yield from ffn(0, mixed, proj, (4 * MIX_W) // MEM_W, P["w_out_ret"][0], x)

    segs1 = (("norm", N_HEADS), ("norm", N_HEADS), ("plain", N_HEADS), ("norm", H_MEM))
    gain1 = jnp.concatenate([_tile(P["q_norm_att"][0], N_HEADS) * scale, _tile(P["k_norm_att"][0], N_HEADS),
                             ones(N_HEADS), _tile(P["q_norm_mem"][1], H_MEM) * scale])
    w_in = P["w_in_att"][0].astype(BF16)
    if prompt:
        proj = norm_proj(x, P["norm_mix"][1], w_in, gain1, segs1, tm=proj_tm, out_dtype=BF16)
        keep = min(BAND_PAST, L)
        tiles = L // keep
        kv = norm_proj(x, P["norm_mix"][1], w_in[:, MIX_W:3 * MIX_W], gain1[MIX_W:3 * MIX_W], segs1[1:3],
                       tm=keep, out_dtype=F32, n_tiles=B, row_map=lambda i: i * tiles + tiles - 1)
        out["band_k"] = kv[:, :MIX_W].reshape(B, keep, N_HEADS, D_HEAD)
        out["band_v"] = kv[:, MIX_W:].reshape(B, keep, N_HEADS, D_HEAD)
        mixed = band_prompt(proj, P["rel_bias_att"][0], B=B, L=L)
    else:
        proj_f = norm_proj(x, P["norm_mix"][1], w_in, gain1, segs1, tm=proj_tm, out_dtype=F32)
        out["band_k"] = proj_f[:, MIX_W:2 * MIX_W].reshape(B, L, N_HEADS, D_HEAD)
        out["band_v"] = proj_f[:, 2 * MIX_W:3 * MIX_W].reshape(B, L, N_HEADS, D_HEAD)
        proj = proj_f.astype(BF16)
        mixed = band_sample(proj, band_past[0], band_past[1], P["rel_bias_att"][0], B=B, L=L)
    out["y"] = yield from ffn(1, mixed, proj, (3 * MIX_W) // MEM_W, P["w_out_att"][0], x)
    return out


def kernel(x_prompt, x_sample, mem_prompt, state_ret, cache_band_k, cache_band_v, cache_mem_k, cache_mem_v, norm_mix, norm_ffn, norm_mem, w_in_ret, gn_ret, w_out_ret, w_in_att, q_norm_att, k_norm_att, rel_bias_att, w_out_att, w_mem_kv, q_norm_mem, k_norm_mem, w_group, b_group, w_router, b_router, w_e_gate, w_e_up, w_e_down):
    P = {"norm_mix": norm_mix, "norm_ffn": norm_ffn, "w_in_ret": w_in_ret, "gn_ret": gn_ret,
         "w_out_ret": w_out_ret, "w_in_att": w_in_att, "q_norm_att": q_norm_att,
         "k_norm_att": k_norm_att, "rel_bias_att": rel_bias_att, "w_out_att": w_out_att,
         "q_norm_mem": q_norm_mem, "w_group": w_group, "b_group": b_group, "w_router": w_router,
         "b_router": b_router, "w_e_gate": w_e_gate, "w_e_up": w_e_up, "w_e_down": w_e_down}
    B, L, D = x_prompt.shape
    Bd, Ld, _ = x_sample.shape
    depth = norm_mix.shape[0]

    mem_k_p, mem_v_p = [], []
    for i in range(depth):
        gain = jnp.concatenate([_tile(k_norm_mem[i], H_MEM), jnp.ones((MEM_W,), F32)])
        kv = norm_proj(mem_prompt.reshape(B * N_MEM, D), norm_mem[i], w_mem_kv[i].astype(BF16), gain,
                       (("norm", H_MEM), ("plain", H_MEM)), tm=N_MEM, out_dtype=F32)
        mem_k_p.append(kv[:, :MEM_W].reshape(B, N_MEM, H_MEM, D_HEAD))
        mem_v_p.append(kv[:, MEM_W:].reshape(B, N_MEM, H_MEM, D_HEAD))
    mem_k_p = jnp.stack(mem_k_p)
    mem_v_p = jnp.stack(mem_v_p)

    zeros_state = jnp.zeros((B, N_HEADS, D_HEAD, D_HEAD), F32)
    trunks = [_run_trunk(x_prompt.reshape(B * L, D), B, L, mem_k_p, mem_v_p, zeros_state, None, P),
              _run_trunk(x_sample.reshape(Bd * Ld, D), Bd, Ld, cache_mem_k, cache_mem_v, state_ret[0],
                         (cache_band_k[0], cache_band_v[0]), P)]
    pending = [next(t) for t in trunks]
    results = []
    for layer in range(depth):
        counts = jnp.zeros((SUBLANES, LANES), F32)
        groups = []
        for run in pending:
            x1, h2, route, route_t, counts = run(counts)
            groups.append((x1, h2, route, route_t))
        outs = hier_moe(groups, counts, layer, w_e_gate, w_e_up, w_e_down)
        pending = []
        for t, x2 in zip(trunks, outs):
            try:
                pending.append(t.send(x2))
            except StopIteration as done:
                results.append(done.value)
    res_p, res_s = results

    return (res_p["y"].reshape(B, L, D), res_s["y"].reshape(Bd, Ld, D),
            res_p["ret_state"][None], res_s["ret_state"][None],
            res_p["band_k"][None], res_p["band_v"][None], res_s["band_k"][None], res_s["band_v"][None],
            mem_k_p, mem_v_p)
```
